```python
import jax, jax.numpy as jnp
from jax import lax
import numpy as np

D_MODEL = 2048
BATCH = 8
SEQ = 8192
DEPTH = 2

HEAD_DIM = 128
POOL_WINDOWS = (2, 4, 8, 16)
POOL_WIDTH = D_MODEL // 4
POOL_GROUP = POOL_WIDTH // len(POOL_WINDOWS)
SGU_WIDTH = (D_MODEL - POOL_WIDTH) // 2
SGU_HEADS = SGU_WIDTH // HEAD_DIM
CHUNK = 128
CONV_WIDTH = D_MODEL - POOL_WIDTH - SGU_WIDTH
CONV_GROUPS = CONV_WIDTH // HEAD_DIM
CONV_KERNEL = 31
IN_WIDTH = POOL_WIDTH + 2 * SGU_WIDTH + 2 * CONV_WIDTH
D_FF = 4 * D_MODEL
DEEPNORM_ALPHA = (2 * DEPTH) ** 0.25
DEEPNORM_BETA = (8 * DEPTH) ** -0.25
LN_EPS = 1e-5

kernel_name = "hybrid_pool_sgu_conv_deepnorm"


def layer_norm(x, g, b):
    xf = x.astype(jnp.float32)
    mu = jnp.mean(xf, axis=-1, keepdims=True)
    xc = xf - mu
    var = jnp.mean(jnp.square(xc), axis=-1, keepdims=True)
    y = xc * lax.rsqrt(var + LN_EPS)
    return (y * g.astype(jnp.float32) + b.astype(jnp.float32)).astype(x.dtype)


def pool_mixer(a, w_pool, pool_scale):
    bsz, s, _ = a.shape
    cs = jnp.cumsum(a.astype(jnp.float32), axis=1)
    count = jnp.arange(1, s + 1, dtype=jnp.float32)[None, :, None]
    means = []
    for g, win in enumerate(POOL_WINDOWS):
        c = cs[..., g * POOL_GROUP:(g + 1) * POOL_GROUP]
        prev = jnp.pad(c[:, :-win], ((0, 0), (win, 0), (0, 0)))
        means.append((c - prev) / jnp.minimum(count, float(win)))
    pooled = jnp.concatenate(means, axis=-1).astype(a.dtype) - a
    pooled = pooled.reshape(bsz, s, len(POOL_WINDOWS), POOL_GROUP)
    y = jnp.einsum('bsgc,gcd->bsgd', pooled, w_pool).reshape(bsz, s, POOL_WIDTH)
    return y * pool_scale


def sgu_mixer(uv, ln_g, ln_b, w_s, b_s):
    bsz, s, _ = uv.shape
    uv = jax.nn.gelu(uv)
    u, v = jnp.split(uv, 2, axis=-1)
    v = layer_norm(v, ln_g, ln_b)
    vc = v.reshape(bsz, s // CHUNK, CHUNK, SGU_HEADS, HEAD_DIM)
    mask = jnp.tril(jnp.ones((CHUNK, CHUNK), dtype=w_s.dtype))
    mixed = jnp.einsum('hts,bnshc->bnthc', w_s * mask, vc) + b_s.T[None, None, :, :, None]
    return u * mixed.reshape(bsz, s, SGU_WIDTH)


def conv_module(ag, conv_w, conv_b, ln_g, ln_b):
    a, g = jnp.split(ag, 2, axis=-1)
    h = a * jax.nn.sigmoid(g)
    h = lax.conv_general_dilated(
        h, conv_w[:, None, :], window_strides=(1,), padding=[(CONV_KERNEL - 1, 0)],
        dimension_numbers=('NWC', 'WIO', 'NWC'), feature_group_count=CONV_WIDTH) + conv_b
    h = layer_norm(h, ln_g, ln_b)
    return jax.nn.silu(h)


def _fwd_setup_inputs(seed: int = 0) -> dict:
    key = jax.random.key(seed)
    ks = jax.random.split(key, 24)

    def nrm(k, shape, scale):
        return jax.random.normal(k, shape, dtype=jnp.float32) * scale

    L = DEPTH
    return {
        "x": nrm(ks[0], (BATCH, SEQ, D_MODEL), 1.0),
        "w_in": nrm(ks[1], (L, D_MODEL, IN_WIDTH), D_MODEL ** -0.5),
        "b_in": nrm(ks[2], (L, IN_WIDTH), 0.02),
        "w_pool": nrm(ks[3], (L, len(POOL_WINDOWS), POOL_GROUP, POOL_GROUP), POOL_GROUP ** -0.5),
        "pool_scale": 1.0 + nrm(ks[4], (L, POOL_WIDTH), 0.1),
        "sgu_ln_g": 1.0 + nrm(ks[5], (L, SGU_WIDTH), 0.02),
        "sgu_ln_b": nrm(ks[6], (L, SGU_WIDTH), 0.02),
        "sgu_w": nrm(ks[7], (L, SGU_HEADS, CHUNK, CHUNK), CHUNK ** -0.5),
        "sgu_b": 1.0 + nrm(ks[8], (L, SGU_HEADS, CHUNK), 0.02),
        "conv_w": nrm(ks[9], (L, CONV_KERNEL, CONV_WIDTH), CONV_KERNEL ** -0.5),
        "conv_b": nrm(ks[10], (L, CONV_WIDTH), 0.02),
        "conv_ln_g": 1.0 + nrm(ks[11], (L, CONV_WIDTH), 0.02),
        "conv_ln_b": nrm(ks[12], (L, CONV_WIDTH), 0.02),
        "w_out": nrm(ks[13], (L, D_MODEL, D_MODEL), DEEPNORM_BETA * D_MODEL ** -0.5),
        "b_out": nrm(ks[14], (L, D_MODEL), 0.02),
        "ln1_g": 1.0 + nrm(ks[15], (L, D_MODEL), 0.02),
        "ln1_b": nrm(ks[16], (L, D_MODEL), 0.02),
        "w_ff1": nrm(ks[17], (L, D_MODEL, D_FF), D_MODEL ** -0.5),
        "b_ff1": nrm(ks[18], (L, D_FF), 0.02),
        "w_ff2": nrm(ks[19], (L, D_FF, D_MODEL), DEEPNORM_BETA * D_FF ** -0.5),
        "b_ff2": nrm(ks[20], (L, D_MODEL), 0.02),
        "ln2_g": 1.0 + nrm(ks[21], (L, D_MODEL), 0.02),
        "ln2_b": nrm(ks[22], (L, D_MODEL), 0.02),
    }


def _fwd_reference(x, w_in, b_in, w_pool, pool_scale, sgu_ln_g, sgu_ln_b, sgu_w, sgu_b,
              conv_w, conv_b, conv_ln_g, conv_ln_b, w_out, b_out, ln1_g, ln1_b,
              w_ff1, b_ff1, w_ff2, b_ff2, ln2_g, ln2_b):
    for l in range(DEPTH):
        proj = jnp.einsum('bsd,de->bse', x, w_in[l]) + b_in[l]
        p_a = proj[..., :POOL_WIDTH]
        p_b = proj[..., POOL_WIDTH:POOL_WIDTH + 2 * SGU_WIDTH]
        p_c = proj[..., POOL_WIDTH + 2 * SGU_WIDTH:]
        y_a = pool_mixer(p_a, w_pool[l], pool_scale[l])
        y_b = sgu_mixer(p_b, sgu_ln_g[l], sgu_ln_b[l], sgu_w[l], sgu_b[l])
        y_c = conv_module(p_c, conv_w[l], conv_b[l], conv_ln_g[l], conv_ln_b[l])
        mixed = jnp.concatenate([y_a, y_b, y_c], axis=-1)
        mix_out = jnp.einsum('bsd,de->bse', mixed, w_out[l]) + b_out[l]
        x = layer_norm(DEEPNORM_ALPHA * x + mix_out, ln1_g[l], ln1_b[l])
        h = jnp.square(jax.nn.relu(jnp.einsum('bsd,df->bsf', x, w_ff1[l]) + b_ff1[l]))
        ff_out = jnp.einsum('bsf,fd->bsd', h, w_ff2[l]) + b_ff2[l]
        x = layer_norm(DEEPNORM_ALPHA * x + ff_out, ln2_g[l], ln2_b[l])
    return x


import jax as _jax
import jax.numpy as _jnp

TWIN_FORMAT = 'train_step'
FWD_PARAMS = ['x', 'w_in', 'b_in', 'w_pool', 'pool_scale', 'sgu_ln_g', 'sgu_ln_b', 'sgu_w', 'sgu_b', 'conv_w', 'conv_b', 'conv_ln_g', 'conv_ln_b', 'w_out', 'b_out', 'ln1_g', 'ln1_b', 'w_ff1', 'b_ff1', 'w_ff2', 'b_ff2', 'ln2_g', 'ln2_b']
TWIN_WEIGHTS = ['w_in', 'b_in', 'w_pool', 'pool_scale', 'sgu_ln_g', 'sgu_ln_b', 'sgu_w', 'sgu_b', 'conv_w', 'conv_b', 'conv_ln_g', 'conv_ln_b', 'w_out', 'b_out', 'ln1_g', 'ln1_b', 'w_ff1', 'b_ff1', 'w_ff2', 'b_ff2', 'ln2_g', 'ln2_b']
TWIN_DIFF_INPUT = 'x'
TWIN_INPUTS = ['x', 'w_in', 'b_in', 'w_pool', 'pool_scale', 'sgu_ln_g', 'sgu_ln_b', 'sgu_w', 'sgu_b', 'conv_w', 'conv_b', 'conv_ln_g', 'conv_ln_b', 'w_out', 'b_out', 'ln1_g', 'ln1_b', 'w_ff1', 'b_ff1', 'w_ff2', 'b_ff2', 'ln2_g', 'ln2_b', 'loss_target', 'm_w_in', 'm_b_in', 'm_w_pool', 'm_pool_scale', 'm_sgu_ln_g', 'm_sgu_ln_b', 'm_sgu_w', 'm_sgu_b', 'm_conv_w', 'm_conv_b', 'm_conv_ln_g', 'm_conv_ln_b', 'm_w_out', 'm_b_out', 'm_ln1_g', 'm_ln1_b', 'm_w_ff1', 'm_b_ff1', 'm_w_ff2', 'm_b_ff2', 'm_ln2_g', 'm_ln2_b', 'v_w_in', 'v_b_in', 'v_w_pool', 'v_pool_scale', 'v_sgu_ln_g', 'v_sgu_ln_b', 'v_sgu_w', 'v_sgu_b', 'v_conv_w', 'v_conv_b', 'v_conv_ln_g', 'v_conv_ln_b', 'v_w_out', 'v_b_out', 'v_ln1_g', 'v_ln1_b', 'v_w_ff1', 'v_b_ff1', 'v_w_ff2', 'v_b_ff2', 'v_ln2_g', 'v_ln2_b']
TWIN_OUTPUTS = ['loss', 'grad_x', 'grad_w_in', 'grad_b_in', 'grad_w_pool', 'grad_pool_scale', 'grad_sgu_ln_g', 'grad_sgu_ln_b', 'grad_sgu_w', 'grad_sgu_b', 'grad_conv_w', 'grad_conv_b', 'grad_conv_ln_g', 'grad_conv_ln_b', 'grad_w_out', 'grad_b_out', 'grad_ln1_g', 'grad_ln1_b', 'grad_w_ff1', 'grad_b_ff1', 'grad_w_ff2', 'grad_b_ff2', 'grad_ln2_g', 'grad_ln2_b', 'delta_w_in', 'delta_b_in', 'delta_w_pool', 'delta_pool_scale', 'delta_sgu_ln_g', 'delta_sgu_ln_b', 'delta_sgu_w', 'delta_sgu_b', 'delta_conv_w', 'delta_conv_b', 'delta_conv_ln_g', 'delta_conv_ln_b', 'delta_w_out', 'delta_b_out', 'delta_ln1_g', 'delta_ln1_b', 'delta_w_ff1', 'delta_b_ff1', 'delta_w_ff2', 'delta_b_ff2', 'delta_ln2_g', 'delta_ln2_b', 'new_m_w_in', 'new_m_b_in', 'new_m_w_pool', 'new_m_pool_scale', 'new_m_sgu_ln_g', 'new_m_sgu_ln_b', 'new_m_sgu_w', 'new_m_sgu_b', 'new_m_conv_w', 'new_m_conv_b', 'new_m_conv_ln_g', 'new_m_conv_ln_b', 'new_m_w_out', 'new_m_b_out', 'new_m_ln1_g', 'new_m_ln1_b', 'new_m_w_ff1', 'new_m_b_ff1', 'new_m_w_ff2', 'new_m_b_ff2', 'new_m_ln2_g', 'new_m_ln2_b', 'new_v_w_in', 'new_v_b_in', 'new_v_w_pool', 'new_v_pool_scale', 'new_v_sgu_ln_g', 'new_v_sgu_ln_b', 'new_v_sgu_w', 'new_v_sgu_b', 'new_v_conv_w', 'new_v_conv_b', 'new_v_conv_ln_g', 'new_v_conv_ln_b', 'new_v_w_out', 'new_v_b_out', 'new_v_ln1_g', 'new_v_ln1_b', 'new_v_w_ff1', 'new_v_b_ff1', 'new_v_w_ff2', 'new_v_b_ff2', 'new_v_ln2_g', 'new_v_ln2_b']
TWIN_LEAF_KINDS = {'loss': 'loss', 'grad_x': 'grad_x', 'grad_w_in': 'grad_w', 'grad_b_in': 'grad_w', 'grad_w_pool': 'grad_w', 'grad_pool_scale': 'grad_w', 'grad_sgu_ln_g': 'grad_w', 'grad_sgu_ln_b': 'grad_w', 'grad_sgu_w': 'grad_w', 'grad_sgu_b': 'grad_w', 'grad_conv_w': 'grad_w', 'grad_conv_b': 'grad_w', 'grad_conv_ln_g': 'grad_w', 'grad_conv_ln_b': 'grad_w', 'grad_w_out': 'grad_w', 'grad_b_out': 'grad_w', 'grad_ln1_g': 'grad_w', 'grad_ln1_b': 'grad_w', 'grad_w_ff1': 'grad_w', 'grad_b_ff1': 'grad_w', 'grad_w_ff2': 'grad_w', 'grad_b_ff2': 'grad_w', 'grad_ln2_g': 'grad_w', 'grad_ln2_b': 'grad_w', 'delta_w_in': 'delta_w', 'delta_b_in': 'delta_w', 'delta_w_pool': 'delta_w', 'delta_pool_scale': 'delta_w', 'delta_sgu_ln_g': 'delta_w', 'delta_sgu_ln_b': 'delta_w', 'delta_sgu_w': 'delta_w', 'delta_sgu_b': 'delta_w', 'delta_conv_w': 'delta_w', 'delta_conv_b': 'delta_w', 'delta_conv_ln_g': 'delta_w', 'delta_conv_ln_b': 'delta_w', 'delta_w_out': 'delta_w', 'delta_b_out': 'delta_w', 'delta_ln1_g': 'delta_w', 'delta_ln1_b': 'delta_w', 'delta_w_ff1': 'delta_w', 'delta_b_ff1': 'delta_w', 'delta_w_ff2': 'delta_w', 'delta_b_ff2': 'delta_w', 'delta_ln2_g': 'delta_w', 'delta_ln2_b': 'delta_w', 'new_m_w_in': 'new_m', 'new_m_b_in': 'new_m', 'new_m_w_pool': 'new_m', 'new_m_pool_scale': 'new_m', 'new_m_sgu_ln_g': 'new_m', 'new_m_sgu_ln_b': 'new_m', 'new_m_sgu_w': 'new_m', 'new_m_sgu_b': 'new_m', 'new_m_conv_w': 'new_m', 'new_m_conv_b': 'new_m', 'new_m_conv_ln_g': 'new_m', 'new_m_conv_ln_b': 'new_m', 'new_m_w_out': 'new_m', 'new_m_b_out': 'new_m', 'new_m_ln1_g': 'new_m', 'new_m_ln1_b': 'new_m', 'new_m_w_ff1': 'new_m', 'new_m_b_ff1': 'new_m', 'new_m_w_ff2': 'new_m', 'new_m_b_ff2': 'new_m', 'new_m_ln2_g': 'new_m', 'new_m_ln2_b': 'new_m', 'new_v_w_in': 'new_v', 'new_v_b_in': 'new_v', 'new_v_w_pool': 'new_v', 'new_v_pool_scale': 'new_v', 'new_v_sgu_ln_g': 'new_v', 'new_v_sgu_ln_b': 'new_v', 'new_v_sgu_w': 'new_v', 'new_v_sgu_b': 'new_v', 'new_v_conv_w': 'new_v', 'new_v_conv_b': 'new_v', 'new_v_conv_ln_g': 'new_v', 'new_v_conv_ln_b': 'new_v', 'new_v_w_out': 'new_v', 'new_v_b_out': 'new_v', 'new_v_ln1_g': 'new_v', 'new_v_ln1_b': 'new_v', 'new_v_w_ff1': 'new_v', 'new_v_b_ff1': 'new_v', 'new_v_w_ff2': 'new_v', 'new_v_b_ff2': 'new_v', 'new_v_ln2_g': 'new_v', 'new_v_ln2_b': 'new_v'}


def _forward(args):
    return _fwd_reference(*[args[k] for k in FWD_PARAMS])


def _output_shape():
    def fwd():
        inp = _fwd_setup_inputs(0)
        return _fwd_reference(*[inp[k] for k in FWD_PARAMS])
    out = _jax.eval_shape(fwd)
    return out.shape, out.dtype

N_MICROBATCH = 1
ADAM_LR = 0.001
ADAM_B1 = 0.9
ADAM_B2 = 0.999
ADAM_EPS = 1e-08
ADAM_WD = 0.01
ADAM_STEP = 10
PER_EXAMPLE_BATCH_AXIS = {'x': 0, 'loss_target': 0}
SHARED_INPUTS = []
_WEIGHT_DTYPES = {'w_in': _jnp.float32, 'b_in': _jnp.float32, 'w_pool': _jnp.float32, 'pool_scale': _jnp.float32, 'sgu_ln_g': _jnp.float32, 'sgu_ln_b': _jnp.float32, 'sgu_w': _jnp.float32, 'sgu_b': _jnp.float32, 'conv_w': _jnp.float32, 'conv_b': _jnp.float32, 'conv_ln_g': _jnp.float32, 'conv_ln_b': _jnp.float32, 'w_out': _jnp.float32, 'b_out': _jnp.float32, 'ln1_g': _jnp.float32, 'ln1_b': _jnp.float32, 'w_ff1': _jnp.float32, 'b_ff1': _jnp.float32, 'w_ff2': _jnp.float32, 'b_ff2': _jnp.float32, 'ln2_g': _jnp.float32, 'ln2_b': _jnp.float32}
MOMENT_SCALE = {'w_in': 2.915403e-02, 'b_in': 5.037102e-02, 'w_pool': 3.905182e-02, 'pool_scale': 4.040599e-02, 'sgu_ln_g': 2.096268e-02, 'sgu_ln_b': 2.179081e-02, 'sgu_w': 2.073406e-02, 'sgu_b': 2.963329e-02, 'conv_w': 2.951940e-02, 'conv_b': 1.355345e-01, 'conv_ln_g': 5.693436e-02, 'conv_ln_b': 8.068927e-02, 'w_out': 8.730508e-02, 'b_out': 3.508267e-01, 'ln1_g': 8.344329e-01, 'ln1_b': 5.134768e-01, 'w_ff1': 3.023065e-02, 'b_ff1': 6.585925e-02, 'w_ff2': 1.711031e-01, 'b_ff2': 3.563376e-01, 'ln2_g': 2.272579e+01, 'ln2_b': 5.518353e+00}


def _to_microbatches(a, axis):
    t = _jnp.moveaxis(a, axis, 0)
    t = t.reshape((N_MICROBATCH, t.shape[0] // N_MICROBATCH) + t.shape[1:])
    return _jnp.moveaxis(t, 1, axis + 1)


def setup_inputs(seed: int = 0) -> dict:
    inp = _fwd_setup_inputs(seed)
    key = _jax.random.fold_in(_jax.random.key(seed), 7919)
    shape, _ = _output_shape()
    out = dict(inp)
    out["loss_target"] = _jax.random.normal(_jax.random.fold_in(key, 0), shape, _jnp.float32)
    for i, name in enumerate(TWIN_WEIGHTS):
        w = inp[name].astype(_jnp.float32)
        if MOMENT_SCALE is None:
            s = _jnp.sqrt(_jnp.mean(_jnp.square(w)) + 1e-30)
        else:
            s = MOMENT_SCALE[name]
        km, kv = _jax.random.split(_jax.random.fold_in(key, i + 1))
        out[name] = w
        out["m_" + name] = s * _jax.random.normal(km, w.shape, _jnp.float32)
        out["v_" + name] = (s * s) * _jax.random.uniform(kv, w.shape, _jnp.float32, 0.5, 1.5)
    if N_MICROBATCH > 1:
        for name, axis in PER_EXAMPLE_BATCH_AXIS.items():
            out[name] = _to_microbatches(out[name], axis)
    return {'x': out['x'], 'w_in': out['w_in'], 'b_in': out['b_in'], 'w_pool': out['w_pool'], 'pool_scale': out['pool_scale'], 'sgu_ln_g': out['sgu_ln_g'], 'sgu_ln_b': out['sgu_ln_b'], 'sgu_w': out['sgu_w'], 'sgu_b': out['sgu_b'], 'conv_w': out['conv_w'], 'conv_b': out['conv_b'], 'conv_ln_g': out['conv_ln_g'], 'conv_ln_b': out['conv_ln_b'], 'w_out': out['w_out'], 'b_out': out['b_out'], 'ln1_g': out['ln1_g'], 'ln1_b': out['ln1_b'], 'w_ff1': out['w_ff1'], 'b_ff1': out['b_ff1'], 'w_ff2': out['w_ff2'], 'b_ff2': out['b_ff2'], 'ln2_g': out['ln2_g'], 'ln2_b': out['ln2_b'], 'loss_target': out['loss_target'], 'm_w_in': out['m_w_in'], 'm_b_in': out['m_b_in'], 'm_w_pool': out['m_w_pool'], 'm_pool_scale': out['m_pool_scale'], 'm_sgu_ln_g': out['m_sgu_ln_g'], 'm_sgu_ln_b': out['m_sgu_ln_b'], 'm_sgu_w': out['m_sgu_w'], 'm_sgu_b': out['m_sgu_b'], 'm_conv_w': out['m_conv_w'], 'm_conv_b': out['m_conv_b'], 'm_conv_ln_g': out['m_conv_ln_g'], 'm_conv_ln_b': out['m_conv_ln_b'], 'm_w_out': out['m_w_out'], 'm_b_out': out['m_b_out'], 'm_ln1_g': out['m_ln1_g'], 'm_ln1_b': out['m_ln1_b'], 'm_w_ff1': out['m_w_ff1'], 'm_b_ff1': out['m_b_ff1'], 'm_w_ff2': out['m_w_ff2'], 'm_b_ff2': out['m_b_ff2'], 'm_ln2_g': out['m_ln2_g'], 'm_ln2_b': out['m_ln2_b'], 'v_w_in': out['v_w_in'], 'v_b_in': out['v_b_in'], 'v_w_pool': out['v_w_pool'], 'v_pool_scale': out['v_pool_scale'], 'v_sgu_ln_g': out['v_sgu_ln_g'], 'v_sgu_ln_b': out['v_sgu_ln_b'], 'v_sgu_w': out['v_sgu_w'], 'v_sgu_b': out['v_sgu_b'], 'v_conv_w': out['v_conv_w'], 'v_conv_b': out['v_conv_b'], 'v_conv_ln_g': out['v_conv_ln_g'], 'v_conv_ln_b': out['v_conv_ln_b'], 'v_w_out': out['v_w_out'], 'v_b_out': out['v_b_out'], 'v_ln1_g': out['v_ln1_g'], 'v_ln1_b': out['v_ln1_b'], 'v_w_ff1': out['v_w_ff1'], 'v_b_ff1': out['v_b_ff1'], 'v_w_ff2': out['v_w_ff2'], 'v_b_ff2': out['v_b_ff2'], 'v_ln2_g': out['v_ln2_g'], 'v_ln2_b': out['v_ln2_b']}


def _loss(weights, diff, rest, loss_target):
    with _jax.named_scope("forward"):
        args = {**rest, TWIN_DIFF_INPUT: diff, **{k: w.astype(_WEIGHT_DTYPES[k]) for k, w in weights.items()}}
        y = _forward(args)
    with _jax.named_scope("loss_head"):
        err = _jnp.square(y.astype(_jnp.float32) - loss_target)
        return 0.5 * _jnp.sum(_jnp.mean(err, axis=-1)) if err.ndim else 0.5 * err


def _adamw(w, g, m, v):
    m = ADAM_B1 * m + (1.0 - ADAM_B1) * g
    v = ADAM_B2 * v + (1.0 - ADAM_B2) * _jnp.square(g)
    m_hat = m / (1.0 - ADAM_B1 ** ADAM_STEP)
    v_hat = v / (1.0 - ADAM_B2 ** ADAM_STEP)
    delta = -ADAM_LR * (m_hat / (_jnp.sqrt(v_hat) + ADAM_EPS) + ADAM_WD * w)
    return delta, m, v


def reference(x, w_in, b_in, w_pool, pool_scale, sgu_ln_g, sgu_ln_b, sgu_w, sgu_b, conv_w, conv_b, conv_ln_g, conv_ln_b, w_out, b_out, ln1_g, ln1_b, w_ff1, b_ff1, w_ff2, b_ff2, ln2_g, ln2_b, loss_target, m_w_in, m_b_in, m_w_pool, m_pool_scale, m_sgu_ln_g, m_sgu_ln_b, m_sgu_w, m_sgu_b, m_conv_w, m_conv_b, m_conv_ln_g, m_conv_ln_b, m_w_out, m_b_out, m_ln1_g, m_ln1_b, m_w_ff1, m_b_ff1, m_w_ff2, m_b_ff2, m_ln2_g, m_ln2_b, v_w_in, v_b_in, v_w_pool, v_pool_scale, v_sgu_ln_g, v_sgu_ln_b, v_sgu_w, v_sgu_b, v_conv_w, v_conv_b, v_conv_ln_g, v_conv_ln_b, v_w_out, v_b_out, v_ln1_g, v_ln1_b, v_w_ff1, v_b_ff1, v_w_ff2, v_b_ff2, v_ln2_g, v_ln2_b):
    given = dict(x=x, w_in=w_in, b_in=b_in, w_pool=w_pool, pool_scale=pool_scale, sgu_ln_g=sgu_ln_g, sgu_ln_b=sgu_ln_b, sgu_w=sgu_w, sgu_b=sgu_b, conv_w=conv_w, conv_b=conv_b, conv_ln_g=conv_ln_g, conv_ln_b=conv_ln_b, w_out=w_out, b_out=b_out, ln1_g=ln1_g, ln1_b=ln1_b, w_ff1=w_ff1, b_ff1=b_ff1, w_ff2=w_ff2, b_ff2=b_ff2, ln2_g=ln2_g, ln2_b=ln2_b, loss_target=loss_target, m_w_in=m_w_in, m_b_in=m_b_in, m_w_pool=m_w_pool, m_pool_scale=m_pool_scale, m_sgu_ln_g=m_sgu_ln_g, m_sgu_ln_b=m_sgu_ln_b, m_sgu_w=m_sgu_w, m_sgu_b=m_sgu_b, m_conv_w=m_conv_w, m_conv_b=m_conv_b, m_conv_ln_g=m_conv_ln_g, m_conv_ln_b=m_conv_ln_b, m_w_out=m_w_out, m_b_out=m_b_out, m_ln1_g=m_ln1_g, m_ln1_b=m_ln1_b, m_w_ff1=m_w_ff1, m_b_ff1=m_b_ff1, m_w_ff2=m_w_ff2, m_b_ff2=m_b_ff2, m_ln2_g=m_ln2_g, m_ln2_b=m_ln2_b, v_w_in=v_w_in, v_b_in=v_b_in, v_w_pool=v_w_pool, v_pool_scale=v_pool_scale, v_sgu_ln_g=v_sgu_ln_g, v_sgu_ln_b=v_sgu_ln_b, v_sgu_w=v_sgu_w, v_sgu_b=v_sgu_b, v_conv_w=v_conv_w, v_conv_b=v_conv_b, v_conv_ln_g=v_conv_ln_g, v_conv_ln_b=v_conv_ln_b, v_w_out=v_w_out, v_b_out=v_b_out, v_ln1_g=v_ln1_g, v_ln1_b=v_ln1_b, v_w_ff1=v_w_ff1, v_b_ff1=v_b_ff1, v_w_ff2=v_w_ff2, v_b_ff2=v_b_ff2, v_ln2_g=v_ln2_g, v_ln2_b=v_ln2_b)
    weights = {n: given[n] for n in TWIN_WEIGHTS}
    shared = {n: given[n] for n in SHARED_INPUTS}
    per_example = {n: given[n] for n in ['x']}
    grad_fn = _jax.value_and_grad(_loss, argnums=(0, 1))

    def one_microbatch(ex, loss_target):
        ex = dict(ex)
        diff = ex.pop(TWIN_DIFF_INPUT)
        return grad_fn(weights, diff, {**shared, **ex}, loss_target)

    if N_MICROBATCH == 1:
        loss, (grad_w, grad_x) = one_microbatch(per_example, given["loss_target"])
    else:
        def body(carry, xs):
            loss_sum, grad_sum = carry
            l_k, (gw_k, gx_k) = one_microbatch(xs[0], xs[1])
            with _jax.named_scope("update"):
                return (loss_sum + l_k, _jax.tree.map(_jnp.add, grad_sum, gw_k)), gx_k

        init = (_jnp.zeros((), _jnp.float32), _jax.tree.map(_jnp.zeros_like, weights))
        (loss, grad_w), grad_x = _jax.lax.scan(body, init, (per_example, given["loss_target"]))
    with _jax.named_scope("update"):
        delta_w, new_m, new_v = {}, {}, {}
        for n in TWIN_WEIGHTS:
            delta_w[n], new_m[n], new_v[n] = _adamw(weights[n], grad_w[n], given["m_" + n], given["v_" + n])
    return (loss, grad_x, *[grad_w[n] for n in TWIN_WEIGHTS], *[delta_w[n] for n in TWIN_WEIGHTS],
            *[new_m[n] for n in TWIN_WEIGHTS], *[new_v[n] for n in TWIN_WEIGHTS])
```

```python
import functools

import jax
import jax.numpy as jnp
from jax import lax
from jax.experimental import pallas as pl
from jax.experimental.pallas import tpu as pltpu

F32, BF16 = jnp.float32, jnp.bfloat16
S = jax.ShapeDtypeStruct

DEPTH = 2
D_MODEL = 2048
POOL_WINDOWS = (2, 4, 8, 16)
POOL_WIDTH = 512
GROUP = 128
SGU_WIDTH = 768
SGU_HEADS = 6
CONV_WIDTH = 768
CONV_KERNEL = 31
IN_WIDTH = 3584
D_FF = 8192
ALPHA = (2 * DEPTH) ** 0.25
LN_EPS = 1e-5
ADAM_LR, ADAM_B1, ADAM_B2, ADAM_EPS, ADAM_WD, ADAM_STEP = 0.001, 0.9, 0.999, 1e-08, 0.01, 10

N_DEV = 8
HALO = 32
VMEM_LIMIT = 56 << 20
MESH = pl.DeviceIdType.MESH

C_POOL = (0, 512)
C_U = (512, 1280)
C_V = (1280, 2048)
C_CA = (2048, 2816)
C_CG = (2816, 3584)
M_POOL = (0, 512)
M_SGU = (512, 1280)
M_CONV = (1280, 2048)


def _cparams(n_axes):
    return pltpu.CompilerParams(dimension_semantics=("arbitrary",) * n_axes, vmem_limit_bytes=VMEM_LIMIT)


def _for_strips(rows, strip, fn):
    n = rows // strip
    if n == 1:
        fn(0)
        return

    def step(s, carry):
        fn(pl.multiple_of(s * strip, strip))
        return carry

    lax.fori_loop(0, n, step, 0)


def _row_sum(x):
    return jnp.sum(x, axis=0, keepdims=True)


def _ln_stats(r):
    mu = jnp.mean(r, axis=-1, keepdims=True)
    xc = r - mu
    var = jnp.mean(xc * xc, axis=-1, keepdims=True)
    rs = lax.rsqrt(var + LN_EPS)
    return xc * rs, rs


def _ln_bwd(dy, xhat, rs, g):
    gy = dy * g
    m1 = jnp.mean(gy, axis=-1, keepdims=True)
    m2 = jnp.mean(gy * xhat, axis=-1, keepdims=True)
    return rs * (gy - m1 - xhat * m2)


_GELU_C = 0.7978845608028654


def _gelu(x):
    th = jnp.tanh(_GELU_C * (x + 0.044715 * (x * x * x)))
    return 0.5 * x * (1.0 + th), th


def _gelu_grad(x, th):
    return 0.5 * (1.0 + th) + 0.5 * x * (1.0 - th * th) * (_GELU_C * (1.0 + 3.0 * 0.044715 * (x * x)))


_CONTRACT = {"nn": ((1,), (0,)), "nt": ((1,), (1,)), "tn": ((0,), (0,))}


def _mm(name, a, b, dims, tm, tn, tk, *, ins=(), outs, epilogue, j_outer=False):
    if dims == "tn":
        K, M = a.shape
    else:
        M, K = a.shape
    N = b.shape[0] if dims == "nt" else b.shape[1]
    tm, tn, tk = min(tm, M), min(tn, N), min(tk, K)
    assert M % tm == 0 and N % tn == 0 and K % tk == 0, (name, M, N, K, tm, tn, tk)
    nm, nn, nk = M // tm, N // tn, K // tk
    if j_outer:
        grid = (nn, nm, nk)
        ij = lambda g0, g1: (g1, g0)
    else:
        grid = (nm, nn, nk)
        ij = lambda g0, g1: (g0, g1)

    def amap(g0, g1, k):
        i, _ = ij(g0, g1)
        return (k, i) if dims == "tn" else (i, k)

    def bmap(g0, g1, k):
        _, j = ij(g0, g1)
        return (j, k) if dims == "nt" else (k, j)

    def spec(kind):
        if kind == "tile":
            return pl.BlockSpec((tm, tn), lambda g0, g1, k: ij(g0, g1))
        if kind == "row":
            return pl.BlockSpec((1, tn), lambda g0, g1, k: (0, ij(g0, g1)[1]))
        assert kind == "col", kind
        return pl.BlockSpec((tm, 1), lambda g0, g1, k: (ij(g0, g1)[0], 0))

    in_specs = [
        pl.BlockSpec((tk, tm) if dims == "tn" else (tm, tk), amap),
        pl.BlockSpec((tn, tk) if dims == "nt" else (tk, tn), bmap),
    ] + [spec(kind) for _, kind in ins]
    out_specs = [spec(kind) for _, _, kind in outs]
    out_shape = [S(shape, dtype) for shape, dtype, _ in outs]
    n_in, n_out = len(ins), len(outs)
    contract = (_CONTRACT[dims], ((), ()))

    def body(*refs):
        a_ref, b_ref = refs[:2]
        in_refs = refs[2 : 2 + n_in]
        out_refs = refs[2 + n_in : 2 + n_in + n_out]
        acc = refs[2 + n_in + n_out]
        i, _ = ij(pl.program_id(0), pl.program_id(1))
        k = pl.program_id(2)

        @pl.when(k == 0)
        def _():
            acc[...] = jnp.zeros_like(acc)

        acc[...] += lax.dot_general(a_ref[...], b_ref[...], contract, preferred_element_type=F32)

        @pl.when(k == nk - 1)
        def _():
            epilogue(i, acc, in_refs, out_refs)

    res = pl.pallas_call(
        body,
        name=name,
        grid=grid,
        in_specs=in_specs,
        out_specs=out_specs,
        out_shape=out_shape,
        scratch_shapes=[pltpu.VMEM((tm, tn), F32)],
        compiler_params=_cparams(3),
    )(a, b, *[x for x, _ in ins])
    return res


def _row(v):
    return v.reshape(1, -1)


def _mm_bias(name, a, b, dims, bias, tm, tn, tk):
    M = a.shape[0]
    N = b.shape[0] if dims == "nt" else b.shape[1]

    def epilogue(i, acc, ins, outs):
        def strip(r0):
            rows = pl.ds(r0, 128)
            outs[0][rows, :] = acc[rows, :] + ins[0][...]

        _for_strips(acc.shape[0], 128, strip)

    return _mm(name, a, b, dims, tm, tn, tk, ins=[(_row(bias), "row")], outs=[((M, N), F32, "tile")], epilogue=epilogue)[0]


def _mm_relu2(name, a, b, dims, bias, tm, tn, tk):
    M = a.shape[0]
    N = b.shape[0] if dims == "nt" else b.shape[1]

    def epilogue(i, acc, ins, outs):
        def strip(r0):
            rows = pl.ds(r0, 128)
            r = jnp.maximum(acc[rows, :] + ins[0][...], 0.0)
            outs[0][rows, :] = r.astype(BF16)
            outs[1][rows, :] = (r * r).astype(BF16)

        _for_strips(acc.shape[0], 128, strip)

    return _mm(
        name, a, b, dims, tm, tn, tk, ins=[(_row(bias), "row")],
        outs=[((M, N), BF16, "tile"), ((M, N), BF16, "tile")], epilogue=epilogue,
    )


def _mm_ln(name, a, b, bias, res, g, beta, tm, tk):
    M = a.shape[0]
    N = b.shape[1]
    rxh, rg, rb = res

    def epilogue(i, acc, ins, outs):
        bias_r, rxh_r, rg_r, rb_r, g_r, beta_r = ins
        xhat_o, rstd_o, xbf_o = outs

        def strip(r0):
            rows = pl.ds(r0, 64)
            resid = rxh_r[rows, :] * rg_r[...] + rb_r[...]
            r = ALPHA * resid + (acc[rows, :] + bias_r[...])
            xhat, rs = _ln_stats(r)
            xhat_o[rows, :] = xhat
            rstd_o[rows, :] = rs
            xbf_o[rows, :] = (xhat * g_r[...] + beta_r[...]).astype(BF16)

        _for_strips(acc.shape[0], 64, strip)

    return _mm(
        name, a, b, "nn", tm, N, tk,
        ins=[(_row(bias), "row"), (rxh, "tile"), (_row(rg), "row"), (_row(rb), "row"), (_row(g), "row"), (_row(beta), "row")],
        outs=[((M, N), F32, "tile"), ((M, 1), F32, "col"), ((M, N), BF16, "tile")],
        epilogue=epilogue,
    )


def _ln_bwd_strip(dyv, xhat, rs, g, dr_o, drbf_o, dg_o, db_o, dsum_o, rows):
    dr = _ln_bwd(dyv, xhat, rs, g)
    dr_o[rows, :] = dr
    drbf_o[rows, :] = dr.astype(BF16)
    dg_o[...] += _row_sum(dyv * xhat)
    db_o[...] += _row_sum(dyv)
    dsum_o[...] += _row_sum(dr)


def _mm_ln_bwd(name, a, b, resgrad, xhat, rstd, g, tm, tk):
    M = a.shape[0]
    N = b.shape[1]

    def epilogue(i, acc, ins, outs):
        rg_r, xh_r, rs_r, g_r = ins
        dr_o, drbf_o, dg_o, db_o, dsum_o = outs

        @pl.when(i == 0)
        def _():
            dg_o[...] = jnp.zeros_like(dg_o)
            db_o[...] = jnp.zeros_like(db_o)
            dsum_o[...] = jnp.zeros_like(dsum_o)

        def strip(r0):
            rows = pl.ds(r0, 64)
            dyv = acc[rows, :] + ALPHA * rg_r[rows, :]
            _ln_bwd_strip(dyv, xh_r[rows, :], rs_r[rows, :], g_r[...], dr_o, drbf_o, dg_o, db_o, dsum_o, rows)

        _for_strips(acc.shape[0], 64, strip)

    return _mm(
        name, a, b, "nn", tm, N, tk,
        ins=[(resgrad, "tile"), (xhat, "tile"), (rstd, "col"), (_row(g), "row")],
        outs=[((M, N), F32, "tile"), ((M, N), BF16, "tile"), ((1, N), F32, "row"), ((1, N), F32, "row"), ((1, N), F32, "row")],
        epilogue=epilogue,
    )


def _mm_dh(name, a, b, act, tm, tn, tk):
    M = a.shape[0]
    N = b.shape[0]

    def epilogue(i, acc, ins, outs):
        @pl.when(i == 0)
        def _():
            outs[1][...] = jnp.zeros_like(outs[1])

        def strip(r0):
            rows = pl.ds(r0, 128)
            d = acc[rows, :] * (2.0 * ins[0][rows, :].astype(F32))
            outs[0][rows, :] = d.astype(BF16)
            outs[1][...] += _row_sum(d)

        _for_strips(acc.shape[0], 128, strip)

    return _mm(
        name, a, b, "nt", tm, tn, tk, ins=[(act, "tile")],
        outs=[((M, N), BF16, "tile"), ((1, N), F32, "row")], epilogue=epilogue, j_outer=True,
    )


def _mm_plain(name, a, b, dims, tm, tn, tk, res=None):
    M = a.shape[0]
    N = b.shape[0] if dims == "nt" else b.shape[1]

    def epilogue(i, acc, ins, outs):
        def strip(r0):
            rows = pl.ds(r0, 128)
            v = acc[rows, :]
            if res is not None:
                v = v + ALPHA * ins[0][rows, :]
            outs[0][rows, :] = v

        _for_strips(acc.shape[0], 128, strip)

    return _mm(
        name, a, b, dims, tm, tn, tk, ins=[] if res is None else [(res, "tile")],
        outs=[((M, N), F32, "tile")], epilogue=epilogue,
    )[0]


def _mm_wgrad(name, a, b, tm, tk):
    M = a.shape[1]
    N = b.shape[1]

    def epilogue(i, acc, ins, outs):
        def strip(r0):
            rows = pl.ds(r0, 128)
            outs[0][rows, :] = acc[rows, :].astype(BF16)

        _for_strips(acc.shape[0], 128, strip)

    return _mm(name, a, b, "tn", tm, N, tk, outs=[((M, N), BF16, "tile")], epilogue=epilogue)[0]


def _loss_top(xhat, rstd, g, beta, target, tm):
    T, D = xhat.shape
    tm = min(tm, T)
    nt = T // tm

    def body(xh_r, rs_r, g_r, b_r, t_r, dr_o, drbf_o, dg_o, db_o, dsum_o, loss_o, sq_acc):
        i = pl.program_id(0)

        @pl.when(i == 0)
        def _():
            dg_o[...] = jnp.zeros_like(dg_o)
            db_o[...] = jnp.zeros_like(db_o)
            dsum_o[...] = jnp.zeros_like(dsum_o)
            sq_acc[...] = jnp.zeros_like(sq_acc)

        def strip(r0):
            rows = pl.ds(r0, 64)
            xh = xh_r[rows, :]
            err = (xh * g_r[...] + b_r[...]) - t_r[rows, :]
            sq_acc[...] += _row_sum(err * err)
            _ln_bwd_strip(err * (1.0 / D), xh, rs_r[rows, :], g_r[...], dr_o, drbf_o, dg_o, db_o, dsum_o, rows)

        _for_strips(tm, 64, strip)

        @pl.when(i == nt - 1)
        def _():
            total = jnp.sum(sq_acc[...], axis=-1, keepdims=True) * (0.5 / D)
            loss_o[...] = jnp.broadcast_to(total, loss_o.shape)

    tile = pl.BlockSpec((tm, D), lambda i: (i, 0))
    row = pl.BlockSpec((1, D), lambda i: (0, 0))
    return pl.pallas_call(
        body,
        name="loss_top",
        grid=(nt,),
        in_specs=[tile, pl.BlockSpec((tm, 1), lambda i: (i, 0)), row, row, tile],
        out_specs=[tile, tile, row, row, row, pl.BlockSpec((1, 128), lambda i: (0, 0))],
        out_shape=[S((T, D), F32), S((T, D), BF16), S((1, D), F32), S((1, D), F32), S((1, D), F32), S((1, 128), F32)],
        scratch_shapes=[pltpu.VMEM((1, D), F32)],
        compiler_params=_cparams(1),
    )(xhat, rstd, _row(g), _row(beta), target)


def _cols(ref, c):
    return ref[:, c[0] : c[1]]


def _causal_window_sum(e, w):
    s, sh = e, 1
    while sh < w:
        s = s + pltpu.roll(s, sh, axis=0)
        sh *= 2
    return s


def _anticausal_window_sum(d, w):
    n = d.shape[0]
    r, sh = d, 1
    while sh < w:
        r = r + pltpu.roll(r, n - sh, axis=0)
        sh *= 2
    return r


def _with_halo(halo_ref, main_ref, c, keep):
    return jnp.concatenate([_cols(halo_ref, c) * keep, _cols(main_ref, c)], axis=0)


def _pool_counts(tile_index, R, w):
    pos = lax.broadcasted_iota(jnp.int32, (R, 1), 0) + tile_index * R
    return jnp.minimum(pos + 1, w).astype(F32)


def _sgu_mix(wm_ref, vnb):
    return jnp.concatenate(
        [
            jnp.dot(wm_ref[h * GROUP : (h + 1) * GROUP, :], vnb[:, h * GROUP : (h + 1) * GROUP], preferred_element_type=F32)
            for h in range(SGU_HEADS)
        ],
        axis=1,
    )


def _conv_taps(buf, cw_ref, first, rows):
    acc = buf[pl.ds(first, rows), :] * cw_ref[pl.ds(0, 1), :]
    for k in range(1, CONV_KERNEL):
        acc = acc + buf[pl.ds(first + k, rows), :] * cw_ref[pl.ds(k, 1), :]
    return acc


def _mixer_params(p):
    return [p["wp"], p["ps"], p["lg"], p["lb"], p["wm"], p["wmt"], p["bsf"], p["cw"], p["cb"], p["cg"], p["cbeta"]]


def _whole(x):
    return pl.BlockSpec(x.shape, lambda i: (0,) * x.ndim)


def _mixer_fwd(name, proj, p, R):
    T = proj.shape[0]
    R = min(R, T)
    E = R + HALO
    nt = T // R
    hb = R // HALO
    halo_off = HALO - (CONV_KERNEL - 1)

    def body(pm, ph, wp, ps, lg, lb, wm, wmt, bsf, cw, cb, cg, cbeta, out, hbuf):
        i = pl.program_id(0)
        keep = (i > 0).astype(F32)
        a_ext = _with_halo(ph, pm, C_POOL, keep)
        for gi, w in enumerate(POOL_WINDOWS):
            cs = slice(gi * GROUP, (gi + 1) * GROUP)
            e = a_ext[:, cs]
            s = _causal_window_sum(e, w)
            pooled = s[HALO:, :] / _pool_counts(i, R, w) - e[HALO:, :]
            z = jnp.dot(pooled.astype(BF16), wp[cs, :], preferred_element_type=F32)
            out[:, cs] = (z * ps[:, cs]).astype(BF16)
        u, _ = _gelu(_cols(pm, C_U))
        v, _ = _gelu(_cols(pm, C_V))
        vhat, _ = _ln_stats(v)
        vn = vhat * lg[...] + lb[...]
        for c in range(R // GROUP):
            rs = slice(c * GROUP, (c + 1) * GROUP)
            mixed = _sgu_mix(wm, vn[rs, :].astype(BF16)) + bsf[...]
            out[rs, M_SGU[0] : M_SGU[1]] = (u[rs, :] * mixed).astype(BF16)
        hbuf[...] = _with_halo(ph, pm, C_CA, keep) * jax.nn.sigmoid(_with_halo(ph, pm, C_CG, keep))
        conv = _conv_taps(hbuf, cw, halo_off, R) + cb[...]
        chat, _ = _ln_stats(conv)
        cn = chat * cg[...] + cbeta[...]
        out[:, M_CONV[0] : M_CONV[1]] = (cn * jax.nn.sigmoid(cn)).astype(BF16)

    params = _mixer_params(p)
    return pl.pallas_call(
        body,
        name=name,
        grid=(nt,),
        in_specs=[
            pl.BlockSpec((R, IN_WIDTH), lambda i: (i, 0)),
            pl.BlockSpec((HALO, IN_WIDTH), lambda i: (jnp.maximum(i * hb - 1, 0), 0)),
        ]
        + [_whole(x) for x in params],
        out_specs=pl.BlockSpec((R, D_MODEL), lambda i: (i, 0)),
        out_shape=S((T, D_MODEL), BF16),
        scratch_shapes=[pltpu.VMEM((E, CONV_WIDTH), F32)],
        compiler_params=_cparams(1),
    )(proj, proj, *params)


def _mixer_bwd(name, proj, dmix, p, R):
    T = proj.shape[0]
    R = min(R, T)
    E = R + HALO
    nt = T // R
    hb = R // HALO
    halo_off = HALO - (CONV_KERNEL - 1)

    def body(pm, ph, dm, wp, ps, lg, lb, wm, wmt, bsf, cw, cb, cg, cbeta,
             dproj, dwp, dps, dlg, dlb, dwm, dbs, dcw, dcb, dcg, dcbeta, dbin,
             hbuf, dbuf, carry_p, carry_c, dbs_acc):
        step = pl.program_id(0)
        ti = nt - 1 - step
        keep = (ti > 0).astype(F32)

        @pl.when(step == 0)
        def _():
            for r in (dwp, dps, dlg, dlb, dwm, dcw, dcb, dcg, dcbeta, dbin, carry_p, carry_c, dbs_acc):
                r[...] = jnp.zeros_like(r)

        def tail(carry):
            return jnp.concatenate([jnp.zeros((R - HALO, carry.shape[1]), F32), carry], axis=0)

        def head(x):
            return jnp.concatenate([jnp.zeros((HALO, x.shape[1]), F32), x], axis=0)

        a_ext = _with_halo(ph, pm, C_POOL, keep)
        carry_in = carry_p[...]
        for gi, w in enumerate(POOL_WINDOWS):
            cs = slice(gi * GROUP, (gi + 1) * GROUP)
            e = a_ext[:, cs]
            s = _causal_window_sum(e, w)
            cnt = _pool_counts(ti, R, w)
            pooled_b = (s[HALO:, :] / cnt - e[HALO:, :]).astype(BF16)
            wg = wp[cs, :]
            z = jnp.dot(pooled_b, wg, preferred_element_type=F32)
            dya = dm[:, cs]
            dps[:, cs] += _row_sum(dya * z)
            dz_b = (dya * ps[:, cs]).astype(BF16)
            dwp[cs, :] += lax.dot_general(pooled_b, dz_b, (((0,), (0,)), ((), ())), preferred_element_type=F32)
            dpooled = lax.dot_general(dz_b, wg, (((1,), (1,)), ((), ())), preferred_element_type=F32)
            da_ext = _anticausal_window_sum(head(dpooled / cnt), w) - head(dpooled)
            carry_p[:, cs] = da_ext[:HALO, :]
            d_a = da_ext[HALO:, :] + tail(carry_in[:, cs])
            dbin[:, cs] += _row_sum(d_a)
            dproj[:, cs] = d_a.astype(BF16)

        pu = _cols(pm, C_U)
        pv = _cols(pm, C_V)
        u, thu = _gelu(pu)
        v, thv = _gelu(pv)
        vhat, vrs = _ln_stats(v)
        vn = vhat * lg[...] + lb[...]
        dyb = dm[:, M_SGU[0] : M_SGU[1]]
        du_parts, dvn_parts = [], []
        for c in range(R // GROUP):
            rs = slice(c * GROUP, (c + 1) * GROUP)
            vnb = vn[rs, :].astype(BF16)
            mixed = _sgu_mix(wm, vnb) + bsf[...]
            du_parts.append(dyb[rs, :] * mixed)
            dmixed = dyb[rs, :] * u[rs, :]
            dbs_acc[...] += dmixed
            dmb = dmixed.astype(BF16)
            dvn_h = []
            for h in range(SGU_HEADS):
                hs = slice(h * GROUP, (h + 1) * GROUP)
                dwm[hs, :] += lax.dot_general(dmb[:, hs], vnb[:, hs], (((1,), (1,)), ((), ())), preferred_element_type=F32)
                dvn_h.append(jnp.dot(wmt[hs, :], dmb[:, hs], preferred_element_type=F32))
            dvn_parts.append(jnp.concatenate(dvn_h, axis=1))
        du = jnp.concatenate(du_parts, axis=0) if len(du_parts) > 1 else du_parts[0]
        dvn = jnp.concatenate(dvn_parts, axis=0) if len(dvn_parts) > 1 else dvn_parts[0]
        dlg[...] += _row_sum(dvn * vhat)
        dlb[...] += _row_sum(dvn)
        d_pu = du * _gelu_grad(pu, thu)
        d_pv = _ln_bwd(dvn, vhat, vrs, lg[...]) * _gelu_grad(pv, thv)
        dbin[:, C_U[0] : C_U[1]] += _row_sum(d_pu)
        dbin[:, C_V[0] : C_V[1]] += _row_sum(d_pv)
        dproj[:, C_U[0] : C_U[1]] = d_pu.astype(BF16)
        dproj[:, C_V[0] : C_V[1]] = d_pv.astype(BF16)

        sg_ext = jax.nn.sigmoid(_with_halo(ph, pm, C_CG, keep))
        ca_ext = _with_halo(ph, pm, C_CA, keep)
        hbuf[...] = ca_ext * sg_ext
        conv = _conv_taps(hbuf, cw, halo_off, R) + cb[...]
        chat, crs = _ln_stats(conv)
        cn = chat * cg[...] + cbeta[...]
        sc = jax.nn.sigmoid(cn)
        dcn = dm[:, M_CONV[0] : M_CONV[1]] * (sc * (1.0 + cn * (1.0 - sc)))
        dcg[...] += _row_sum(dcn * chat)
        dcbeta[...] += _row_sum(dcn)
        dconv = _ln_bwd(dcn, chat, crs, cg[...])
        dcb[...] += _row_sum(dconv)
        for k in range(CONV_KERNEL):
            dcw[pl.ds(k, 1), :] += _row_sum(dconv * hbuf[pl.ds(halo_off + k, R), :])
        dbuf[pl.ds(0, HALO), :] = jnp.zeros((HALO, CONV_WIDTH), F32)
        dbuf[pl.ds(HALO, R), :] = dconv
        dbuf[pl.ds(HALO + R, HALO), :] = jnp.zeros((HALO, CONV_WIDTH), F32)
        dhc = dbuf[pl.ds(CONV_KERNEL - 1, E), :] * cw[pl.ds(0, 1), :]
        for k in range(1, CONV_KERNEL):
            dhc = dhc + dbuf[pl.ds(CONV_KERNEL - 1 - k, E), :] * cw[pl.ds(k, 1), :]
        dhc_main = dhc[HALO:, :] + tail(carry_c[...])
        carry_c[...] = dhc[:HALO, :]
        sg = sg_ext[HALO:, :]
        d_ca = dhc_main * sg
        d_cg = dhc_main * ca_ext[HALO:, :] * (sg * (1.0 - sg))
        dbin[:, C_CA[0] : C_CA[1]] += _row_sum(d_ca)
        dbin[:, C_CG[0] : C_CG[1]] += _row_sum(d_cg)
        dproj[:, C_CA[0] : C_CA[1]] = d_ca.astype(BF16)
        dproj[:, C_CG[0] : C_CG[1]] = d_cg.astype(BF16)

        @pl.when(step == nt - 1)
        def _():
            row = lax.broadcasted_iota(jnp.int32, (GROUP, GROUP), 0)
            col = lax.broadcasted_iota(jnp.int32, (GROUP, GROUP), 1)
            dbs[...] = jnp.zeros_like(dbs)
            for h in range(SGU_HEADS):
                hs = slice(h * GROUP, (h + 1) * GROUP)
                dwm[hs, :] = jnp.where(row >= col, dwm[hs, :], 0.0)
                dbs[pl.ds(h, 1), :] = _row_sum(dbs_acc[:, hs].T)

    params = _mixer_params(p)
    accs = [
        S((POOL_WIDTH, GROUP), F32), S((1, POOL_WIDTH), F32), S((1, SGU_WIDTH), F32), S((1, SGU_WIDTH), F32),
        S((SGU_WIDTH, GROUP), F32), S((8, GROUP), F32), S((32, CONV_WIDTH), F32), S((1, CONV_WIDTH), F32),
        S((1, CONV_WIDTH), F32), S((1, CONV_WIDTH), F32), S((1, IN_WIDTH), F32),
    ]
    return pl.pallas_call(
        body,
        name=name,
        grid=(nt,),
        in_specs=[
            pl.BlockSpec((R, IN_WIDTH), lambda i: (nt - 1 - i, 0)),
            pl.BlockSpec((HALO, IN_WIDTH), lambda i: (jnp.maximum((nt - 1 - i) * hb - 1, 0), 0)),
            pl.BlockSpec((R, D_MODEL), lambda i: (nt - 1 - i, 0)),
        ]
        + [_whole(x) for x in params],
        out_specs=[pl.BlockSpec((R, IN_WIDTH), lambda i: (nt - 1 - i, 0))] + [_whole(x) for x in accs],
        out_shape=[S((T, IN_WIDTH), BF16)] + accs,
        scratch_shapes=[
            pltpu.VMEM((E, CONV_WIDTH), F32), pltpu.VMEM((E + HALO, CONV_WIDTH), F32),
            pltpu.VMEM((HALO, POOL_WIDTH), F32), pltpu.VMEM((HALO, CONV_WIDTH), F32), pltpu.VMEM((GROUP, SGU_WIDTH), F32),
        ],
        compiler_params=_cparams(1),
    )(proj, proj, dmix, *params)


def _place():
    x, y, c = lax.axis_index("x"), lax.axis_index("y"), lax.axis_index("c")
    return x, y, c


def _lin(p):
    return 4 * p[0] + 2 * p[1] + p[2]


def _flip(p, r):
    return tuple(1 - v if (r >> (2 - ax)) & 1 else v for ax, v in enumerate(p))


_ANY = pl.BlockSpec(memory_space=pl.ANY)


def _all_gather(xs):
    n = len(xs)

    def body(*refs):
        x_refs, o_refs = refs[:n], refs[n : 2 * n]
        send_sems, recv_sems, local_sems = refs[2 * n :]
        x, y, c = _place()
        me, sibling = (x, y, c), (x, y, 1 - c)
        chips = [(1 - x, y), (x, 1 - y), (1 - x, 1 - y)]

        def copy(a, k, block, to, src=None):
            dst = o_refs[a].at[_lin(block)]
            return pltpu.make_async_remote_copy(
                src_ref=dst if src is None else src, dst_ref=dst, send_sem=send_sems.at[a, k], recv_sem=recv_sems.at[a, k],
                device_id=to, device_id_type=MESH,
            )

        mine = [pltpu.make_async_copy(x_refs[a], o_refs[a].at[_lin(me)], local_sems.at[a]) for a in range(n)]
        for m in mine:
            m.start()
        first = []
        for a in range(n):
            first.append(copy(a, 0, me, sibling, src=x_refs[a]))
            first += [copy(a, 1 + j, me, (*chip, c), src=x_refs[a]) for j, chip in enumerate(chips)]
        for cp in first:
            cp.start()
        passed = []
        for a in range(n):
            for j, chip in enumerate(chips):
                copy(a, 1 + j, (*chip, c), me).wait_recv()
                fwd = copy(a, 4 + j, (*chip, c), sibling)
                fwd.start()
                passed.append(fwd)
        for a in range(n):
            copy(a, 0, sibling, me).wait_recv()
            for j, chip in enumerate(chips):
                copy(a, 4 + j, (*chip, 1 - c), me).wait_recv()
        for cp in first + passed:
            cp.wait_send()
        for m in mine:
            m.wait()

    return pl.pallas_call(
        body,
        name="all_gather_weights",
        in_specs=[_ANY] * n,
        out_specs=[_ANY] * n,
        out_shape=[S((N_DEV, *x.shape), x.dtype) for x in xs],
        scratch_shapes=[pltpu.SemaphoreType.DMA((n, 7)), pltpu.SemaphoreType.DMA((n, 7)), pltpu.SemaphoreType.DMA((n,))],
    )(*xs)


def _exchange_slices(gs):
    n = len(gs)

    def body(*refs):
        g_refs, o_refs = refs[:n], refs[n : 2 * n]
        send_sems, recv_sems, local_sems = refs[2 * n :]
        me = _place()
        rows = [g.shape[0] // N_DEV for g in gs]

        def block(a, d):
            return g_refs[a].at[pl.ds(pl.multiple_of(_lin(d) * rows[a], 16), rows[a])]

        def copy(a, r):
            peer = _flip(me, r)
            return pltpu.make_async_remote_copy(
                src_ref=block(a, peer), dst_ref=o_refs[a].at[_lin(me)], send_sem=send_sems.at[a, r - 1],
                recv_sem=recv_sems.at[a, r - 1], device_id=peer, device_id_type=MESH,
            )

        def arrival(a, r):
            peer = _flip(me, r)
            return pltpu.make_async_remote_copy(
                src_ref=block(a, peer), dst_ref=o_refs[a].at[_lin(peer)], send_sem=send_sems.at[a, r - 1],
                recv_sem=recv_sems.at[a, r - 1], device_id=peer, device_id_type=MESH,
            )

        mine = [pltpu.make_async_copy(block(a, me), o_refs[a].at[_lin(me)], local_sems.at[a]) for a in range(n)]
        for m in mine:
            m.start()
        sends = [copy(a, r) for a in range(n) for r in range(1, N_DEV)]
        for cp in sends:
            cp.start()
        for a in range(n):
            for r in range(1, N_DEV):
                arrival(a, r).wait_recv()
        for cp in sends:
            cp.wait_send()
        for m in mine:
            m.wait()

    return pl.pallas_call(
        body,
        name="exchange_grad_slices",
        in_specs=[_ANY] * n,
        out_specs=[_ANY] * n,
        out_shape=[S((N_DEV, g.shape[0] // N_DEV, g.shape[1]), g.dtype) for g in gs],
        scratch_shapes=[pltpu.SemaphoreType.DMA((n, 7)), pltpu.SemaphoreType.DMA((n, 7)), pltpu.SemaphoreType.DMA((n,))],
    )(*gs)


def _all_reduce_small(g):
    rows = g.shape[0]

    def body(g_ref, o_ref, slots, send_sems, recv_sems):
        me = _place()
        slots[_lin(me)] = g_ref[...]

        def copy(r):
            peer = _flip(me, r)
            return pltpu.make_async_remote_copy(
                src_ref=g_ref, dst_ref=slots.at[_lin(me)], send_sem=send_sems.at[r - 1], recv_sem=recv_sems.at[r - 1],
                device_id=peer, device_id_type=MESH,
            )

        def arrival(r):
            peer = _flip(me, r)
            return pltpu.make_async_remote_copy(
                src_ref=g_ref, dst_ref=slots.at[_lin(peer)], send_sem=send_sems.at[r - 1], recv_sem=recv_sems.at[r - 1],
                device_id=peer, device_id_type=MESH,
            )

        sends = [copy(r) for r in range(1, N_DEV)]
        for cp in sends:
            cp.start()
        for r in range(1, N_DEV):
            arrival(r).wait_recv()
        for cp in sends:
            cp.wait_send()
        total = slots[0]
        for d in range(1, N_DEV):
            total = total + slots[d]
        o_ref[...] = total

    vmem = pl.BlockSpec(memory_space=pltpu.VMEM)
    return pl.pallas_call(
        body,
        name="all_reduce_small_grads",
        in_specs=[vmem],
        out_specs=vmem,
        out_shape=S(g.shape, F32),
        scratch_shapes=[pltpu.VMEM((N_DEV, rows, 128), F32), pltpu.SemaphoreType.DMA((7,)), pltpu.SemaphoreType.DMA((7,))],
        compiler_params=pltpu.CompilerParams(vmem_limit_bytes=VMEM_LIMIT),
    )(g)


def _row_tile(rows, want):
    t = min(rows, want)
    while rows % t:
        t //= 2
    return t


def _sum_slots(name, slots):
    _, rows, cols = slots.shape
    tr = _row_tile(rows, 256)

    def body(s_ref, o_ref):
        total = s_ref[0].astype(F32)
        for d in range(1, N_DEV):
            total = total + s_ref[d].astype(F32)
        o_ref[...] = total

    return pl.pallas_call(
        body,
        name=name,
        grid=(rows // tr,),
        in_specs=[pl.BlockSpec((N_DEV, tr, cols), lambda i: (0, i, 0))],
        out_specs=pl.BlockSpec((tr, cols), lambda i: (i, 0)),
        out_shape=S((rows, cols), F32),
        compiler_params=_cparams(1),
    )(slots)


def _adamw(name, w, g, m, v):
    rows, cols = w.shape
    tr = rows if rows * cols * 4 <= (2 << 20) else _row_tile(rows, 1 << ((1 << 18) // cols).bit_length() - 1)

    def body(w_ref, g_ref, m_ref, v_ref, d_ref, nm_ref, nv_ref):
        gv = g_ref[...]
        nm = ADAM_B1 * m_ref[...] + (1.0 - ADAM_B1) * gv
        nv = ADAM_B2 * v_ref[...] + (1.0 - ADAM_B2) * (gv * gv)
        m_hat = nm / (1.0 - ADAM_B1**ADAM_STEP)
        v_hat = nv / (1.0 - ADAM_B2**ADAM_STEP)
        d_ref[...] = -ADAM_LR * (m_hat / (jnp.sqrt(v_hat) + ADAM_EPS) + ADAM_WD * w_ref[...])
        nm_ref[...] = nm
        nv_ref[...] = nv

    blk = pl.BlockSpec((tr, cols), lambda i: (i, 0))
    return pl.pallas_call(
        body,
        name=name,
        grid=(rows // tr,),
        in_specs=[blk] * 4,
        out_specs=[blk] * 3,
        out_shape=[S((rows, cols), F32)] * 3,
        compiler_params=_cparams(1),
    )(w, g, m, v)


_BIG = ("w_in", "w_out", "w_ff1", "w_ff2")
_TRANSPOSED = ("w_in", "w_ff1")
_SMALL = ("b_in", "w_pool", "pool_scale", "sgu_ln_g", "sgu_ln_b", "sgu_w", "sgu_b", "conv_b", "conv_ln_g", "conv_ln_b",
          "b_out", "ln1_g", "ln1_b", "b_ff1", "b_ff2", "ln2_g", "ln2_b")
_WEIGHTS = ("w_in", "b_in", "w_pool", "pool_scale", "sgu_ln_g", "sgu_ln_b", "sgu_w", "sgu_b", "conv_w", "conv_b", "conv_ln_g",
            "conv_ln_b", "w_out", "b_out", "ln1_g", "ln1_b", "w_ff1", "b_ff1", "w_ff2", "b_ff2", "ln2_g", "ln2_b")


def _pack(arrays):
    flat = jnp.concatenate([a.reshape(-1, 128) for a in arrays], axis=0)
    pad = -flat.shape[0] % 8
    return jnp.pad(flat, ((0, pad), (0, 0))) if pad else flat


def _unpack(flat, like):
    out, at = [], 0
    for a in like:
        n = a.size // 128
        out.append(flat[at : at + n].reshape(a.shape))
        at += n
    return out


def kernel(x, w_in, b_in, w_pool, pool_scale, sgu_ln_g, sgu_ln_b, sgu_w, sgu_b, conv_w, conv_b, conv_ln_g, conv_ln_b, w_out, b_out, ln1_g, ln1_b, w_ff1, b_ff1, w_ff2, b_ff2, ln2_g, ln2_b, loss_target, m_w_in, m_b_in, m_w_pool, m_pool_scale, m_sgu_ln_g, m_sgu_ln_b, m_sgu_w, m_sgu_b, m_conv_w, m_conv_b, m_conv_ln_g, m_conv_ln_b, m_w_out, m_b_out, m_ln1_g, m_ln1_b, m_w_ff1, m_b_ff1, m_w_ff2, m_b_ff2, m_ln2_g, m_ln2_b, v_w_in, v_b_in, v_w_pool, v_pool_scale, v_sgu_ln_g, v_sgu_ln_b, v_sgu_w, v_sgu_b, v_conv_w, v_conv_b, v_conv_ln_g, v_conv_ln_b, v_w_out, v_b_out, v_ln1_g, v_ln1_b, v_w_ff1, v_b_ff1, v_w_ff2, v_b_ff2, v_ln2_g, v_ln2_b):
    w = dict(w_in=w_in, b_in=b_in, w_pool=w_pool, pool_scale=pool_scale, sgu_ln_g=sgu_ln_g, sgu_ln_b=sgu_ln_b, sgu_w=sgu_w,
             sgu_b=sgu_b, conv_w=conv_w, conv_b=conv_b, conv_ln_g=conv_ln_g, conv_ln_b=conv_ln_b, w_out=w_out, b_out=b_out,
             ln1_g=ln1_g, ln1_b=ln1_b, w_ff1=w_ff1, b_ff1=b_ff1, w_ff2=w_ff2, b_ff2=b_ff2, ln2_g=ln2_g, ln2_b=ln2_b)
    mom = dict(w_in=m_w_in, b_in=m_b_in, w_pool=m_w_pool, pool_scale=m_pool_scale, sgu_ln_g=m_sgu_ln_g, sgu_ln_b=m_sgu_ln_b,
               sgu_w=m_sgu_w, sgu_b=m_sgu_b, conv_w=m_conv_w, conv_b=m_conv_b, conv_ln_g=m_conv_ln_g, conv_ln_b=m_conv_ln_b,
               w_out=m_w_out, b_out=m_b_out, ln1_g=m_ln1_g, ln1_b=m_ln1_b, w_ff1=m_w_ff1, b_ff1=m_b_ff1, w_ff2=m_w_ff2,
               b_ff2=m_b_ff2, ln2_g=m_ln2_g, ln2_b=m_ln2_b)
    var = dict(w_in=v_w_in, b_in=v_b_in, w_pool=v_w_pool, pool_scale=v_pool_scale, sgu_ln_g=v_sgu_ln_g, sgu_ln_b=v_sgu_ln_b,
               sgu_w=v_sgu_w, sgu_b=v_sgu_b, conv_w=v_conv_w, conv_b=v_conv_b, conv_ln_g=v_conv_ln_g, conv_ln_b=v_conv_ln_b,
               w_out=v_w_out, b_out=v_b_out, ln1_g=v_ln1_g, ln1_b=v_ln1_b, w_ff1=v_w_ff1, b_ff1=v_b_ff1, w_ff2=v_w_ff2,
               b_ff2=v_b_ff2, ln2_g=v_ln2_g, ln2_b=v_ln2_b)
    T = x.shape[1]
    x0 = x.reshape(T, D_MODEL)
    target = loss_target.reshape(T, D_MODEL)
    me_lin = _lin(_place())

    shards = []
    for l in range(DEPTH):
        for name in _BIG:
            s = w[name][l]
            shards.append((s.T if name in _TRANSPOSED else s).astype(BF16))
    conv_shard = jnp.pad(conv_w, ((0, 0), (0, 1), (0, 128 - conv_w.shape[2]))).reshape(DEPTH * 32, 128)
    gathered = _all_gather(shards + [conv_shard])
    full = [{} for _ in range(DEPTH)]
    for l in range(DEPTH):
        for k, name in enumerate(_BIG):
            g = gathered[l * len(_BIG) + k]
            full[l][name] = g.reshape(N_DEV * g.shape[1], g.shape[2])
    conv_cols = conv_w.shape[2]
    conv_full = gathered[-1].reshape(N_DEV, DEPTH, 32, 128)[:, :, :CONV_KERNEL, :conv_cols]
    conv_full = conv_full.transpose(1, 2, 0, 3).reshape(DEPTH, CONV_KERNEL, N_DEV * conv_cols)

    tril = jnp.tril(jnp.ones((GROUP, GROUP), F32))
    prm = []
    for l in range(DEPTH):
        wm = sgu_w[l] * tril
        prm.append(dict(
            wp=w_pool[l].reshape(POOL_WIDTH, GROUP).astype(BF16), ps=_row(pool_scale[l]), lg=_row(sgu_ln_g[l]), lb=_row(sgu_ln_b[l]),
            wm=wm.reshape(SGU_WIDTH, GROUP).astype(BF16), wmt=wm.transpose(0, 2, 1).reshape(SGU_WIDTH, GROUP).astype(BF16),
            bsf=jnp.repeat(sgu_b[l].T, GROUP, axis=1), cw=jnp.pad(conv_full[l], ((0, 1), (0, 0))), cb=_row(conv_b[l]),
            cg=_row(conv_ln_g[l]), cbeta=_row(conv_ln_b[l]),
        ))

    saved = []
    res = (x0, jnp.ones((D_MODEL,), F32), jnp.zeros((D_MODEL,), F32))
    xbf = x0.astype(BF16)
    for l in range(DEPTH):
        f = full[l]
        proj = _mm_bias(f"proj{l}", xbf, f["w_in"], "nt", b_in[l], 1024, 896, 2048)
        mixed = _mixer_fwd(f"mixer_fwd{l}", proj, prm[l], 256)
        xh1, rs1, x1bf = _mm_ln(f"out_ln1_{l}", mixed, f["w_out"], b_out[l], res, ln1_g[l], ln1_b[l], 512, 512)
        act, hsq = _mm_relu2(f"ff1_{l}", x1bf, f["w_ff1"], "nt", b_ff1[l], 1024, 1024, 2048)
        xh2, rs2, x2bf = _mm_ln(f"ff2_ln2_{l}", hsq, f["w_ff2"], b_ff2[l], (xh1, ln1_g[l], ln1_b[l]), ln2_g[l], ln2_b[l], 512, 512)
        saved.append(dict(xin=xbf, proj=proj, mixed=mixed, xh1=xh1, rs1=rs1, x1bf=x1bf, act=act, hsq=hsq, xh2=xh2, rs2=rs2))
        res = (xh2, ln2_g[l], ln2_b[l])
        xbf = x2bf

    top = saved[-1]
    dr2, dr2bf, g_ln2g, g_ln2b, g_bff2, loss_row = _loss_top(top["xh2"], top["rs2"], ln2_g[-1], ln2_b[-1], target, 256)
    loss = lax.psum(loss_row[0, 0], ("x", "y", "c"))
    gbig = [{} for _ in range(DEPTH)]
    gsm = [{} for _ in range(DEPTH)]
    grad_x = None
    for l in reversed(range(DEPTH)):
        f, sv = full[l], saved[l]
        gsm[l].update(ln2_g=g_ln2g, ln2_b=g_ln2b, b_ff2=g_bff2)
        gbig[l]["w_ff2"] = _mm_wgrad(f"gw_ff2_{l}", sv["hsq"], dr2bf, 512, 512)
        dhpre, g_bff1 = _mm_dh(f"dff1_{l}", dr2bf, f["w_ff2"], sv["act"], 1024, 1024, 2048)
        gsm[l]["b_ff1"] = g_bff1
        gbig[l]["w_ff1"] = _mm_wgrad(f"gw_ff1_{l}", dhpre, sv["x1bf"], 512, 512)
        dr1, dr1bf, g_ln1g, g_ln1b, g_bout = _mm_ln_bwd(f"dx1_ln1_{l}", dhpre, f["w_ff1"], dr2, sv["xh1"], sv["rs1"], ln1_g[l], 512, 512)
        gsm[l].update(ln1_g=g_ln1g, ln1_b=g_ln1b, b_out=g_bout)
        gbig[l]["w_out"] = _mm_wgrad(f"gw_out_{l}", sv["mixed"], dr1bf, 512, 512)
        dmix = _mm_plain(f"dmixed{l}", dr1bf, f["w_out"], "nt", 1024, 1024, 2048)
        (dproj, g_wp, g_ps, g_lg, g_lb, g_wm, g_bs, g_cw, g_cb, g_cg, g_cbeta, g_bin) = _mixer_bwd(
            f"mixer_bwd{l}", sv["proj"], dmix, prm[l], 256)
        gsm[l].update(b_in=g_bin, w_pool=g_wp, pool_scale=g_ps, sgu_ln_g=g_lg, sgu_ln_b=g_lb, sgu_w=g_wm, sgu_b=g_bs[:SGU_HEADS],
                      conv_w=g_cw[:CONV_KERNEL], conv_b=g_cb, conv_ln_g=g_cg, conv_ln_b=g_cbeta)
        gbig[l]["w_in"] = _mm_wgrad(f"gw_in_{l}", dproj, sv["xin"], 512, 512)
        if l > 0:
            below = saved[l - 1]
            dr2, dr2bf, g_ln2g, g_ln2b, g_bff2 = _mm_ln_bwd(
                f"dx_ln2_{l}", dproj, f["w_in"], dr1, below["xh2"], below["rs2"], ln2_g[l - 1], 512, 512)
        else:
            grad_x = _mm_plain("dx0", dproj, f["w_in"], "nn", 512, 2048, 512, res=dr1)

    slots = _exchange_slices([gbig[l][name] for l in range(DEPTH) for name in _BIG])
    grads, deltas, new_m, new_v = {}, {}, {}, {}
    for k, name in enumerate(_BIG):
        per_layer = []
        for l in range(DEPTH):
            g = _sum_slots(f"sum_{name}_{l}", slots[l * len(_BIG) + k])
            per_layer.append(g.T if name in _TRANSPOSED else g)
        g = jnp.stack(per_layer)
        shape = w[name].shape
        two_d = (shape[0] * shape[1], shape[2])
        d, nm, nv = _adamw(f"adamw_{name}", w[name].reshape(two_d), g.reshape(two_d), mom[name].reshape(two_d), var[name].reshape(two_d))
        grads[name], deltas[name], new_m[name], new_v[name] = g, d.reshape(shape), nm.reshape(shape), nv.reshape(shape)

    stacked = {name: jnp.stack([gsm[l][name].reshape(w[name].shape[1:]) for l in range(DEPTH)]) for name in _SMALL}
    conv_g_local = jnp.stack([gsm[l]["conv_w"] for l in range(DEPTH)])
    conv_g_pad = jnp.pad(conv_g_local, ((0, 0), (0, 1), (0, 0)))
    packed = _pack([stacked[name] for name in _SMALL] + [conv_g_pad])
    total = _all_reduce_small(packed)
    small_g = _unpack(total, [stacked[name] for name in _SMALL] + [conv_g_pad])
    n_small_rows = sum(stacked[name].size for name in _SMALL) // 128
    assert n_small_rows % 8 == 0
    d, nm, nv = _adamw("adamw_small", _pack([w[name] for name in _SMALL]), total[:n_small_rows],
                       _pack([mom[name] for name in _SMALL]), _pack([var[name] for name in _SMALL]))
    like = [w[name] for name in _SMALL]
    for name, gg, dd, mm_, vv in zip(_SMALL, small_g, _unpack(d, like), _unpack(nm, like), _unpack(nv, like)):
        grads[name], deltas[name], new_m[name], new_v[name] = gg, dd, mm_, vv
    conv_g = lax.dynamic_slice_in_dim(small_g[-1][:, :CONV_KERNEL, :], me_lin * conv_cols, conv_cols, axis=2)
    flat = (DEPTH * CONV_KERNEL, conv_cols)
    d, nm, nv = _adamw("adamw_conv_w", conv_w.reshape(flat), conv_g.reshape(flat), m_conv_w.reshape(flat), v_conv_w.reshape(flat))
    grads["conv_w"], deltas["conv_w"], new_m["conv_w"], new_v["conv_w"] = conv_g, d.reshape(conv_w.shape), nm.reshape(conv_w.shape), nv.reshape(conv_w.shape)

    return (loss, grad_x.reshape(x.shape), *[grads[n] for n in _WEIGHTS], *[deltas[n] for n in _WEIGHTS],
            *[new_m[n] for n in _WEIGHTS], *[new_v[n] for n in _WEIGHTS])
```

```python
import functools

import jax
import jax.numpy as jnp
from jax import lax
from jax.experimental import pallas as pl
from jax.experimental.pallas import tpu as pltpu

F32, BF16 = jnp.float32, jnp.bfloat16
S = jax.ShapeDtypeStruct

DEPTH = 2
D_MODEL = 2048
POOL_WINDOWS = (2, 4, 8, 16)
POOL_WIDTH = 512
GROUP = 128
SGU_WIDTH = 768
SGU_HEADS = 6
CONV_WIDTH = 768
CONV_KERNEL = 31
IN_WIDTH = 3584
D_FF = 8192
ALPHA = (2 * DEPTH) ** 0.25
LN_EPS = 1e-5
ADAM_LR, ADAM_B1, ADAM_B2, ADAM_EPS, ADAM_WD, ADAM_STEP = 0.001, 0.9, 0.999, 1e-08, 0.01, 10

N_DEV = 8
HALO = 32
VMEM_LIMIT = 56 << 20
MESH = pl.DeviceIdType.MESH

C_POOL = (0, 512)
C_U = (512, 1280)
C_V = (1280, 2048)
C_CA = (2048, 2816)
C_CG = (2816, 3584)
M_POOL = (0, 512)
M_SGU = (512, 1280)
M_CONV = (1280, 2048)


def _cparams(n_axes):
    return pltpu.CompilerParams(dimension_semantics=("arbitrary",) * n_axes, vmem_limit_bytes=VMEM_LIMIT)


def _for_strips(rows, strip, fn):
    n = rows // strip
    if n == 1:
        fn(0)
        return

    def step(s, carry):
        fn(pl.multiple_of(s * strip, strip))
        return carry

    lax.fori_loop(0, n, step, 0)


def _row_sum(x):
    return jnp.sum(x, axis=0, keepdims=True)


def _ln_stats(r):
    mu = jnp.mean(r, axis=-1, keepdims=True)
    xc = r - mu
    var = jnp.mean(xc * xc, axis=-1, keepdims=True)
    rs = lax.rsqrt(var + LN_EPS)
    return xc * rs, rs


def _ln_bwd(dy, xhat, rs, g):
    gy = dy * g
    m1 = jnp.mean(gy, axis=-1, keepdims=True)
    m2 = jnp.mean(gy * xhat, axis=-1, keepdims=True)
    return rs * (gy - m1 - xhat * m2)


_GELU_C = 0.7978845608028654


def _gelu(x):
    th = jnp.tanh(_GELU_C * (x + 0.044715 * (x * x * x)))
    return 0.5 * x * (1.0 + th), th


def _gelu_grad(x, th):
    return 0.5 * (1.0 + th) + 0.5 * x * (1.0 - th * th) * (_GELU_C * (1.0 + 3.0 * 0.044715 * (x * x)))


def _place():
    x, y, c = lax.axis_index("x"), lax.axis_index("y"), lax.axis_index("c")
    return x, y, c


def _lin(p):
    return 4 * p[0] + 2 * p[1] + p[2]


def _flip(p, r):
    return tuple(1 - v if (r >> (2 - ax)) & 1 else v for ax, v in enumerate(p))


_ANY = pl.BlockSpec(memory_space=pl.ANY)


class _Exchange:
    def __init__(self, items):
        self.kinds = [kind for kind, _ in items]
        self.ins = [x for _, x in items]
        self.out_shape = [
            S((N_DEV, *x.shape), x.dtype) if kind == "gather" else S((N_DEV, x.shape[0] // N_DEV, x.shape[1]), x.dtype)
            for kind, x in items
        ]
        n = len(items)
        self.scratch = [pltpu.SemaphoreType.DMA((n, 7)), pltpu.SemaphoreType.DMA((n, 7)), pltpu.SemaphoreType.DMA((n,))]
        self.results = None

    def _src(self, in_refs, a, dest):
        if self.kinds[a] == "gather":
            return in_refs[a]
        rows = self.ins[a].shape[0] // N_DEV
        return in_refs[a].at[pl.ds(pl.multiple_of(_lin(dest) * rows, 8), rows)]

    def _copies(self, in_refs, out_refs, sems, with_arrivals):
        send_sems, recv_sems, local_sems = sems
        me = _place()
        local, sends, arrivals = [], [], []
        for a in range(len(self.ins)):
            local.append(pltpu.make_async_copy(self._src(in_refs, a, me), out_refs[a].at[_lin(me)], local_sems.at[a]))
            for r in range(1, N_DEV):
                peer = _flip(me, r)
                for dst, group in ((_lin(me), sends), (_lin(peer), arrivals)):
                    if group is sends or with_arrivals:
                        group.append(pltpu.make_async_remote_copy(
                            src_ref=self._src(in_refs, a, peer), dst_ref=out_refs[a].at[dst], send_sem=send_sems.at[a, r - 1],
                            recv_sem=recv_sems.at[a, r - 1], device_id=peer, device_id_type=MESH,
                        ))
        return local, sends, arrivals

    def start(self, in_refs, out_refs, sems):
        local, sends, _ = self._copies(in_refs, out_refs, sems, False)
        for cp in local + sends:
            cp.start()

    def wait(self, in_refs, out_refs, sems):
        local, sends, arrivals = self._copies(in_refs, out_refs, sems, True)
        for cp in arrivals:
            cp.wait_recv()
        for cp in sends:
            cp.wait_send()
        for cp in local:
            cp.wait()


def _call(name, body, grid, in_specs, out_specs, out_shape, scratch, args, side=None):
    in_specs, out_specs, out_shape, scratch = list(in_specs), list(out_specs), list(out_shape), list(scratch)
    if side is None:
        return pl.pallas_call(
            body, name=name, grid=grid, in_specs=in_specs, out_specs=out_specs, out_shape=out_shape, scratch_shapes=scratch,
            compiler_params=_cparams(len(grid)),
        )(*args)
    n_in, n_out, n_scr = len(in_specs), len(out_specs), len(scratch)
    s_in, s_out = len(side.ins), len(side.out_shape)

    def wrapped(*refs):
        at = 0
        parts = []
        for n in (n_in, s_in, n_out, s_out, n_scr, 3):
            parts.append(refs[at : at + n])
            at += n
        ins, side_ins, outs, side_outs, scr, sems = parts
        pids = [pl.program_id(d) for d in range(len(grid))]
        first = functools.reduce(jnp.logical_and, [p == 0 for p in pids])
        last = functools.reduce(jnp.logical_and, [p == g - 1 for p, g in zip(pids, grid)])

        @pl.when(first)
        def _():
            side.start(side_ins, side_outs, sems)

        body(*ins, *outs, *scr)

        @pl.when(last)
        def _():
            side.wait(side_ins, side_outs, sems)

    res = pl.pallas_call(
        wrapped, name=name, grid=grid, in_specs=in_specs + [_ANY] * s_in, out_specs=out_specs + [_ANY] * s_out,
        out_shape=out_shape + side.out_shape, scratch_shapes=scratch + side.scratch, compiler_params=_cparams(len(grid)),
    )(*args, *side.ins)
    side.results = list(res[n_out:])
    return list(res[:n_out])


_CONTRACT = {"nn": ((1,), (0,)), "nt": ((1,), (1,)), "tn": ((0,), (0,))}


def _mm(name, a, b, dims, tm, tn, tk, *, ins=(), outs, epilogue, j_outer=False, side=None):
    if dims == "tn":
        K, M = a.shape
    else:
        M, K = a.shape
    N = b.shape[0] if dims == "nt" else b.shape[1]
    tm, tn, tk = min(tm, M), min(tn, N), min(tk, K)
    assert M % tm == 0 and N % tn == 0 and K % tk == 0, (name, M, N, K, tm, tn, tk)
    nm, nn, nk = M // tm, N // tn, K // tk
    if j_outer:
        grid = (nn, nm, nk)
        ij = lambda g0, g1: (g1, g0)
    else:
        grid = (nm, nn, nk)
        ij = lambda g0, g1: (g0, g1)

    def amap(g0, g1, k):
        i, _ = ij(g0, g1)
        return (k, i) if dims == "tn" else (i, k)

    def bmap(g0, g1, k):
        _, j = ij(g0, g1)
        return (j, k) if dims == "nt" else (k, j)

    def spec(kind):
        if kind == "tile":
            return pl.BlockSpec((tm, tn), lambda g0, g1, k: ij(g0, g1))
        if kind == "row":
            return pl.BlockSpec((1, tn), lambda g0, g1, k: (0, ij(g0, g1)[1]))
        assert kind == "col", kind
        return pl.BlockSpec((tm, 1), lambda g0, g1, k: (ij(g0, g1)[0], 0))

    in_specs = [
        pl.BlockSpec((tk, tm) if dims == "tn" else (tm, tk), amap),
        pl.BlockSpec((tn, tk) if dims == "nt" else (tk, tn), bmap),
    ] + [spec(kind) for _, kind in ins]
    out_specs = [spec(kind) for _, _, kind in outs]
    out_shape = [S(shape, dtype) for shape, dtype, _ in outs]
    n_in, n_out = len(ins), len(outs)
    contract = (_CONTRACT[dims], ((), ()))

    def body(*refs):
        a_ref, b_ref = refs[:2]
        in_refs = refs[2 : 2 + n_in]
        out_refs = refs[2 + n_in : 2 + n_in + n_out]
        acc = refs[2 + n_in + n_out]
        i, _ = ij(pl.program_id(0), pl.program_id(1))
        k = pl.program_id(2)

        def part():
            return lax.dot_general(a_ref[...], b_ref[...], contract, preferred_element_type=F32)

        @pl.when(k == 0)
        def _():
            acc[...] = part()

        @pl.when(k > 0)
        def _():
            acc[...] += part()

        @pl.when(k == nk - 1)
        def _():
            epilogue(i, acc, in_refs, out_refs)

    return _call(name, body, grid, in_specs, out_specs, out_shape, [pltpu.VMEM((tm, tn), F32)], [a, b, *[x for x, _ in ins]], side)


def _row(v):
    return v.reshape(1, -1)


def _mm_bias(name, a, b, dims, bias, tm, tn, tk, side=None):
    M = a.shape[0]
    N = b.shape[0] if dims == "nt" else b.shape[1]

    def epilogue(i, acc, ins, outs):
        def strip(r0):
            rows = pl.ds(r0, 128)
            outs[0][rows, :] = acc[rows, :] + ins[0][...]

        _for_strips(acc.shape[0], 128, strip)

    return _mm(name, a, b, dims, tm, tn, tk, ins=[(_row(bias), "row")], outs=[((M, N), F32, "tile")], epilogue=epilogue, side=side)[0]


def _mm_relu2(name, a, b, dims, bias, tm, tn, tk, side=None):
    M = a.shape[0]
    N = b.shape[0] if dims == "nt" else b.shape[1]

    def epilogue(i, acc, ins, outs):
        def strip(r0):
            rows = pl.ds(r0, 128)
            r = jnp.maximum(acc[rows, :] + ins[0][...], 0.0)
            outs[0][rows, :] = r.astype(BF16)
            outs[1][rows, :] = (r * r).astype(BF16)

        _for_strips(acc.shape[0], 128, strip)

    return _mm(
        name, a, b, dims, tm, tn, tk, ins=[(_row(bias), "row")],
        outs=[((M, N), BF16, "tile"), ((M, N), BF16, "tile")], epilogue=epilogue, side=side,
    )


def _mm_ln(name, a, b, bias, res, g, beta, tm, tk, side=None):
    M = a.shape[0]
    N = b.shape[1]
    rxh, rg, rb = res

    def epilogue(i, acc, ins, outs):
        bias_r, rxh_r, rg_r, rb_r, g_r, beta_r = ins
        xhat_o, rstd_o, xbf_o = outs

        def strip(r0):
            rows = pl.ds(r0, 64)
            resid = rxh_r[rows, :] * rg_r[...] + rb_r[...]
            r = ALPHA * resid + (acc[rows, :] + bias_r[...])
            xhat, rs = _ln_stats(r)
            xhat_o[rows, :] = xhat
            rstd_o[rows, :] = rs
            xbf_o[rows, :] = (xhat * g_r[...] + beta_r[...]).astype(BF16)

        _for_strips(acc.shape[0], 64, strip)

    return _mm(
        name, a, b, "nn", tm, N, tk,
        ins=[(_row(bias), "row"), (rxh, "tile"), (_row(rg), "row"), (_row(rb), "row"), (_row(g), "row"), (_row(beta), "row")],
        outs=[((M, N), F32, "tile"), ((M, 1), F32, "col"), ((M, N), BF16, "tile")],
        epilogue=epilogue, side=side,
    )


def _ln_bwd_strip(dyv, xhat, rs, g, dr_o, drbf_o, dg_o, db_o, dsum_o, rows):
    dr = _ln_bwd(dyv, xhat, rs, g)
    dr_o[rows, :] = dr
    drbf_o[rows, :] = dr.astype(BF16)
    dg_o[...] += _row_sum(dyv * xhat)
    db_o[...] += _row_sum(dyv)
    dsum_o[...] += _row_sum(dr)


def _mm_ln_bwd(name, a, b, resgrad, xhat, rstd, g, tm, tk, side=None):
    M = a.shape[0]
    N = b.shape[1]

    def epilogue(i, acc, ins, outs):
        rg_r, xh_r, rs_r, g_r = ins
        dr_o, drbf_o, dg_o, db_o, dsum_o = outs

        @pl.when(i == 0)
        def _():
            dg_o[...] = jnp.zeros_like(dg_o)
            db_o[...] = jnp.zeros_like(db_o)
            dsum_o[...] = jnp.zeros_like(dsum_o)

        def strip(r0):
            rows = pl.ds(r0, 64)
            dyv = acc[rows, :] + ALPHA * rg_r[rows, :]
            _ln_bwd_strip(dyv, xh_r[rows, :], rs_r[rows, :], g_r[...], dr_o, drbf_o, dg_o, db_o, dsum_o, rows)

        _for_strips(acc.shape[0], 64, strip)

    return _mm(
        name, a, b, "nn", tm, N, tk,
        ins=[(resgrad, "tile"), (xhat, "tile"), (rstd, "col"), (_row(g), "row")],
        outs=[((M, N), F32, "tile"), ((M, N), BF16, "tile"), ((1, N), F32, "row"), ((1, N), F32, "row"), ((1, N), F32, "row")],
        epilogue=epilogue, side=side,
    )


def _mm_dh(name, a, b, act, tm, tn, tk, side=None):
    M = a.shape[0]
    N = b.shape[0]

    def epilogue(i, acc, ins, outs):
        @pl.when(i == 0)
        def _():
            outs[1][...] = jnp.zeros_like(outs[1])

        def strip(r0):
            rows = pl.ds(r0, 128)
            d = acc[rows, :] * (2.0 * ins[0][rows, :].astype(F32))
            outs[0][rows, :] = d.astype(BF16)
            outs[1][...] += _row_sum(d)

        _for_strips(acc.shape[0], 128, strip)

    return _mm(
        name, a, b, "nt", tm, tn, tk, ins=[(act, "tile")],
        outs=[((M, N), BF16, "tile"), ((1, N), F32, "row")], epilogue=epilogue, j_outer=True, side=side,
    )


def _mm_plain(name, a, b, dims, tm, tn, tk, res=None, side=None):
    M = a.shape[0]
    N = b.shape[0] if dims == "nt" else b.shape[1]

    def epilogue(i, acc, ins, outs):
        def strip(r0):
            rows = pl.ds(r0, 128)
            v = acc[rows, :]
            if res is not None:
                v = v + ALPHA * ins[0][rows, :]
            outs[0][rows, :] = v

        _for_strips(acc.shape[0], 128, strip)

    return _mm(
        name, a, b, dims, tm, tn, tk, ins=[] if res is None else [(res, "tile")],
        outs=[((M, N), F32, "tile")], epilogue=epilogue, side=side,
    )[0]


def _mm_wgrad(name, a, b, tm, tk):
    M = a.shape[1]
    N = b.shape[1]

    def epilogue(i, acc, ins, outs):
        def strip(r0):
            rows = pl.ds(r0, 128)
            outs[0][rows, :] = acc[rows, :].astype(BF16)

        _for_strips(acc.shape[0], 128, strip)

    return _mm(name, a, b, "tn", tm, N, tk, outs=[((M, N), BF16, "tile")], epilogue=epilogue)[0]


def _loss_top(xhat, rstd, g, beta, target, tm):
    T, D = xhat.shape
    tm = min(tm, T)
    nt = T // tm

    def body(xh_r, rs_r, g_r, b_r, t_r, dr_o, drbf_o, dg_o, db_o, dsum_o, loss_o, sq_acc):
        i = pl.program_id(0)

        @pl.when(i == 0)
        def _():
            dg_o[...] = jnp.zeros_like(dg_o)
            db_o[...] = jnp.zeros_like(db_o)
            dsum_o[...] = jnp.zeros_like(dsum_o)
            sq_acc[...] = jnp.zeros_like(sq_acc)

        def strip(r0):
            rows = pl.ds(r0, 64)
            xh = xh_r[rows, :]
            err = (xh * g_r[...] + b_r[...]) - t_r[rows, :]
            sq_acc[...] += _row_sum(err * err)
            _ln_bwd_strip(err * (1.0 / D), xh, rs_r[rows, :], g_r[...], dr_o, drbf_o, dg_o, db_o, dsum_o, rows)

        _for_strips(tm, 64, strip)

        @pl.when(i == nt - 1)
        def _():
            total = jnp.sum(sq_acc[...], axis=-1, keepdims=True) * (0.5 / D)
            loss_o[...] = jnp.broadcast_to(total, loss_o.shape)

    tile = pl.BlockSpec((tm, D), lambda i: (i, 0))
    row = pl.BlockSpec((1, D), lambda i: (0, 0))
    return pl.pallas_call(
        body,
        name="loss_top",
        grid=(nt,),
        in_specs=[tile, pl.BlockSpec((tm, 1), lambda i: (i, 0)), row, row, tile],
        out_specs=[tile, tile, row, row, row, pl.BlockSpec((1, 128), lambda i: (0, 0))],
        out_shape=[S((T, D), F32), S((T, D), BF16), S((1, D), F32), S((1, D), F32), S((1, D), F32), S((1, 128), F32)],
        scratch_shapes=[pltpu.VMEM((1, D), F32)],
        compiler_params=_cparams(1),
    )(xhat, rstd, _row(g), _row(beta), target)


def _cols(ref, c):
    return ref[:, c[0] : c[1]]


def _causal_window_sum(e, w):
    s, sh = e, 1
    while sh < w:
        s = s + pltpu.roll(s, sh, axis=0)
        sh *= 2
    return s


def _anticausal_window_sum(d, w):
    n = d.shape[0]
    r, sh = d, 1
    while sh < w:
        r = r + pltpu.roll(r, n - sh, axis=0)
        sh *= 2
    return r


def _with_halo(halo_ref, main_ref, c, keep):
    return jnp.concatenate([_cols(halo_ref, c) * keep, _cols(main_ref, c)], axis=0)


def _pool_counts(tile_index, R, w):
    pos = lax.broadcasted_iota(jnp.int32, (R, 1), 0) + tile_index * R
    return jnp.minimum(pos + 1, w).astype(F32)


def _sgu_mix(wm_ref, vnb):
    return jnp.concatenate(
        [
            jnp.dot(wm_ref[h * GROUP : (h + 1) * GROUP, :], vnb[:, h * GROUP : (h + 1) * GROUP], preferred_element_type=F32)
            for h in range(SGU_HEADS)
        ],
        axis=1,
    )


def _conv_taps(buf, cw_ref, first, rows):
    acc = buf[pl.ds(first, rows), :] * cw_ref[pl.ds(0, 1), :]
    for k in range(1, CONV_KERNEL):
        acc = acc + buf[pl.ds(first + k, rows), :] * cw_ref[pl.ds(k, 1), :]
    return acc


def _mixer_params(p):
    return [p["wp"], p["ps"], p["lg"], p["lb"], p["wm"], p["wmt"], p["bsf"], p["cw"], p["cb"], p["cg"], p["cbeta"]]


def _whole(x):
    return pl.BlockSpec(x.shape, lambda i: (0,) * x.ndim)


def _mixer_fwd(name, proj, p, R, side=None):
    T = proj.shape[0]
    R = min(R, T)
    E = R + HALO
    nt = T // R
    hb = R // HALO
    halo_off = HALO - (CONV_KERNEL - 1)

    def body(pm, ph, wp, ps, lg, lb, wm, wmt, bsf, cw, cb, cg, cbeta, out, hbuf):
        i = pl.program_id(0)
        keep = (i > 0).astype(F32)
        a_ext = _with_halo(ph, pm, C_POOL, keep)
        for gi, w in enumerate(POOL_WINDOWS):
            cs = slice(gi * GROUP, (gi + 1) * GROUP)
            e = a_ext[:, cs]
            s = _causal_window_sum(e, w)
            pooled = s[HALO:, :] / _pool_counts(i, R, w) - e[HALO:, :]
            z = jnp.dot(pooled.astype(BF16), wp[cs, :], preferred_element_type=F32)
            out[:, cs] = (z * ps[:, cs]).astype(BF16)
        u, _ = _gelu(_cols(pm, C_U))
        v, _ = _gelu(_cols(pm, C_V))
        vhat, _ = _ln_stats(v)
        vn = vhat * lg[...] + lb[...]
        for c in range(R // GROUP):
            rs = slice(c * GROUP, (c + 1) * GROUP)
            mixed = _sgu_mix(wm, vn[rs, :].astype(BF16)) + bsf[...]
            out[rs, M_SGU[0] : M_SGU[1]] = (u[rs, :] * mixed).astype(BF16)
        hbuf[...] = _with_halo(ph, pm, C_CA, keep) * jax.nn.sigmoid(_with_halo(ph, pm, C_CG, keep))
        conv = _conv_taps(hbuf, cw, halo_off, R) + cb[...]
        chat, _ = _ln_stats(conv)
        cn = chat * cg[...] + cbeta[...]
        out[:, M_CONV[0] : M_CONV[1]] = (cn * jax.nn.sigmoid(cn)).astype(BF16)

    params = _mixer_params(p)
    in_specs = [
        pl.BlockSpec((R, IN_WIDTH), lambda i: (i, 0)),
        pl.BlockSpec((HALO, IN_WIDTH), lambda i: (jnp.maximum(i * hb - 1, 0), 0)),
    ] + [_whole(x) for x in params]
    return _call(
        name, body, (nt,), in_specs, [pl.BlockSpec((R, D_MODEL), lambda i: (i, 0))], [S((T, D_MODEL), BF16)],
        [pltpu.VMEM((E, CONV_WIDTH), F32)], [proj, proj, *params], side,
    )[0]


def _mixer_bwd(name, proj, dmix, p, R):
    T = proj.shape[0]
    R = min(R, T)
    E = R + HALO
    nt = T // R
    hb = R // HALO
    halo_off = HALO - (CONV_KERNEL - 1)

    def body(pm, ph, dm, wp, ps, lg, lb, wm, wmt, bsf, cw, cb, cg, cbeta,
             dproj, dwp, dps, dlg, dlb, dwm, dbs, dcw, dcb, dcg, dcbeta, dbin,
             hbuf, dbuf, carry_p, carry_c, dbs_acc):
        step = pl.program_id(0)
        ti = nt - 1 - step
        keep = (ti > 0).astype(F32)

        @pl.when(step == 0)
        def _():
            for r in (dwp, dps, dlg, dlb, dwm, dcw, dcb, dcg, dcbeta, dbin, carry_p, carry_c, dbs_acc):
                r[...] = jnp.zeros_like(r)

        def tail(carry):
            return jnp.concatenate([jnp.zeros((R - HALO, carry.shape[1]), F32), carry], axis=0)

        def head(x):
            return jnp.concatenate([jnp.zeros((HALO, x.shape[1]), F32), x], axis=0)

        a_ext = _with_halo(ph, pm, C_POOL, keep)
        carry_in = carry_p[...]
        for gi, w in enumerate(POOL_WINDOWS):
            cs = slice(gi * GROUP, (gi + 1) * GROUP)
            e = a_ext[:, cs]
            s = _causal_window_sum(e, w)
            cnt = _pool_counts(ti, R, w)
            pooled_b = (s[HALO:, :] / cnt - e[HALO:, :]).astype(BF16)
            wg = wp[cs, :]
            z = jnp.dot(pooled_b, wg, preferred_element_type=F32)
            dya = dm[:, cs]
            dps[:, cs] += _row_sum(dya * z)
            dz_b = (dya * ps[:, cs]).astype(BF16)
            dwp[cs, :] += lax.dot_general(pooled_b, dz_b, (((0,), (0,)), ((), ())), preferred_element_type=F32)
            dpooled = lax.dot_general(dz_b, wg, (((1,), (1,)), ((), ())), preferred_element_type=F32)
            da_ext = _anticausal_window_sum(head(dpooled / cnt), w) - head(dpooled)
            carry_p[:, cs] = da_ext[:HALO, :]
            d_a = da_ext[HALO:, :] + tail(carry_in[:, cs])
            dbin[:, cs] += _row_sum(d_a)
            dproj[:, cs] = d_a.astype(BF16)

        pu = _cols(pm, C_U)
        pv = _cols(pm, C_V)
        u, thu = _gelu(pu)
        v, thv = _gelu(pv)
        vhat, vrs = _ln_stats(v)
        vn = vhat * lg[...] + lb[...]
        dyb = dm[:, M_SGU[0] : M_SGU[1]]
        du_parts, dvn_parts = [], []
        for c in range(R // GROUP):
            rs = slice(c * GROUP, (c + 1) * GROUP)
            vnb = vn[rs, :].astype(BF16)
            mixed = _sgu_mix(wm, vnb) + bsf[...]
            du_parts.append(dyb[rs, :] * mixed)
            dmixed = dyb[rs, :] * u[rs, :]
            dbs_acc[...] += dmixed
            dmb = dmixed.astype(BF16)
            dvn_h = []
            for h in range(SGU_HEADS):
                hs = slice(h * GROUP, (h + 1) * GROUP)
                dwm[hs, :] += lax.dot_general(dmb[:, hs], vnb[:, hs], (((1,), (1,)), ((), ())), preferred_element_type=F32)
                dvn_h.append(jnp.dot(wmt[hs, :], dmb[:, hs], preferred_element_type=F32))
            dvn_parts.append(jnp.concatenate(dvn_h, axis=1))
        du = jnp.concatenate(du_parts, axis=0) if len(du_parts) > 1 else du_parts[0]
        dvn = jnp.concatenate(dvn_parts, axis=0) if len(dvn_parts) > 1 else dvn_parts[0]
        dlg[...] += _row_sum(dvn * vhat)
        dlb[...] += _row_sum(dvn)
        d_pu = du * _gelu_grad(pu, thu)
        d_pv = _ln_bwd(dvn, vhat, vrs, lg[...]) * _gelu_grad(pv, thv)
        dbin[:, C_U[0] : C_U[1]] += _row_sum(d_pu)
        dbin[:, C_V[0] : C_V[1]] += _row_sum(d_pv)
        dproj[:, C_U[0] : C_U[1]] = d_pu.astype(BF16)
        dproj[:, C_V[0] : C_V[1]] = d_pv.astype(BF16)

        sg_ext = jax.nn.sigmoid(_with_halo(ph, pm, C_CG, keep))
        ca_ext = _with_halo(ph, pm, C_CA, keep)
        hbuf[...] = ca_ext * sg_ext
        conv = _conv_taps(hbuf, cw, halo_off, R) + cb[...]
        chat, crs = _ln_stats(conv)
        cn = chat * cg[...] + cbeta[...]
        sc = jax.nn.sigmoid(cn)
        dcn = dm[:, M_CONV[0] : M_CONV[1]] * (sc * (1.0 + cn * (1.0 - sc)))
        dcg[...] += _row_sum(dcn * chat)
        dcbeta[...] += _row_sum(dcn)
        dconv = _ln_bwd(dcn, chat, crs, cg[...])
        dcb[...] += _row_sum(dconv)
        for k in range(CONV_KERNEL):
            dcw[pl.ds(k, 1), :] += _row_sum(dconv * hbuf[pl.ds(halo_off + k, R), :])
        dbuf[pl.ds(0, HALO), :] = jnp.zeros((HALO, CONV_WIDTH), F32)
        dbuf[pl.ds(HALO, R), :] = dconv
        dbuf[pl.ds(HALO + R, HALO), :] = jnp.zeros((HALO, CONV_WIDTH), F32)
        dhc = dbuf[pl.ds(CONV_KERNEL - 1, E), :] * cw[pl.ds(0, 1), :]
        for k in range(1, CONV_KERNEL):
            dhc = dhc + dbuf[pl.ds(CONV_KERNEL - 1 - k, E), :] * cw[pl.ds(k, 1), :]
        dhc_main = dhc[HALO:, :] + tail(carry_c[...])
        carry_c[...] = dhc[:HALO, :]
        sg = sg_ext[HALO:, :]
        d_ca = dhc_main * sg
        d_cg = dhc_main * ca_ext[HALO:, :] * (sg * (1.0 - sg))
        dbin[:, C_CA[0] : C_CA[1]] += _row_sum(d_ca)
        dbin[:, C_CG[0] : C_CG[1]] += _row_sum(d_cg)
        dproj[:, C_CA[0] : C_CA[1]] = d_ca.astype(BF16)
        dproj[:, C_CG[0] : C_CG[1]] = d_cg.astype(BF16)

        @pl.when(step == nt - 1)
        def _():
            row = lax.broadcasted_iota(jnp.int32, (GROUP, GROUP), 0)
            col = lax.broadcasted_iota(jnp.int32, (GROUP, GROUP), 1)
            dbs[...] = jnp.zeros_like(dbs)
            for h in range(SGU_HEADS):
                hs = slice(h * GROUP, (h + 1) * GROUP)
                dwm[hs, :] = jnp.where(row >= col, dwm[hs, :], 0.0)
                dbs[pl.ds(h, 1), :] = _row_sum(dbs_acc[:, hs].T)

    params = _mixer_params(p)
    accs = [
        S((POOL_WIDTH, GROUP), F32), S((1, POOL_WIDTH), F32), S((1, SGU_WIDTH), F32), S((1, SGU_WIDTH), F32),
        S((SGU_WIDTH, GROUP), F32), S((8, GROUP), F32), S((32, CONV_WIDTH), F32), S((1, CONV_WIDTH), F32),
        S((1, CONV_WIDTH), F32), S((1, CONV_WIDTH), F32), S((1, IN_WIDTH), F32),
    ]
    return pl.pallas_call(
        body,
        name=name,
        grid=(nt,),
        in_specs=[
            pl.BlockSpec((R, IN_WIDTH), lambda i: (nt - 1 - i, 0)),
            pl.BlockSpec((HALO, IN_WIDTH), lambda i: (jnp.maximum((nt - 1 - i) * hb - 1, 0), 0)),
            pl.BlockSpec((R, D_MODEL), lambda i: (nt - 1 - i, 0)),
        ]
        + [_whole(x) for x in params],
        out_specs=[pl.BlockSpec((R, IN_WIDTH), lambda i: (nt - 1 - i, 0))] + [_whole(x) for x in accs],
        out_shape=[S((T, IN_WIDTH), BF16)] + accs,
        scratch_shapes=[
            pltpu.VMEM((E, CONV_WIDTH), F32), pltpu.VMEM((E + HALO, CONV_WIDTH), F32),
            pltpu.VMEM((HALO, POOL_WIDTH), F32), pltpu.VMEM((HALO, CONV_WIDTH), F32), pltpu.VMEM((GROUP, SGU_WIDTH), F32),
        ],
        compiler_params=_cparams(1),
    )(proj, proj, dmix, *params)


def _all_gather(xs):
    n = len(xs)

    def body(*refs):
        x_refs, o_refs = refs[:n], refs[n : 2 * n]
        send_sems, recv_sems, local_sems = refs[2 * n :]
        x, y, c = _place()
        me, sibling = (x, y, c), (x, y, 1 - c)
        chips = [(1 - x, y), (x, 1 - y), (1 - x, 1 - y)]

        def copy(a, k, block, to, src=None):
            dst = o_refs[a].at[_lin(block)]
            return pltpu.make_async_remote_copy(
                src_ref=dst if src is None else src, dst_ref=dst, send_sem=send_sems.at[a, k], recv_sem=recv_sems.at[a, k],
                device_id=to, device_id_type=MESH,
            )

        mine = [pltpu.make_async_copy(x_refs[a], o_refs[a].at[_lin(me)], local_sems.at[a]) for a in range(n)]
        for m in mine:
            m.start()
        first = []
        for a in range(n):
            first.append(copy(a, 0, me, sibling, src=x_refs[a]))
            first += [copy(a, 1 + j, me, (*chip, c), src=x_refs[a]) for j, chip in enumerate(chips)]
        for cp in first:
            cp.start()
        passed = []
        for a in range(n):
            for j, chip in enumerate(chips):
                copy(a, 1 + j, (*chip, c), me).wait_recv()
                fwd = copy(a, 4 + j, (*chip, c), sibling)
                fwd.start()
                passed.append(fwd)
        for a in range(n):
            copy(a, 0, sibling, me).wait_recv()
            for j, chip in enumerate(chips):
                copy(a, 4 + j, (*chip, 1 - c), me).wait_recv()
        for cp in first + passed:
            cp.wait_send()
        for m in mine:
            m.wait()

    return pl.pallas_call(
        body,
        name="all_gather_weights",
        in_specs=[_ANY] * n,
        out_specs=[_ANY] * n,
        out_shape=[S((N_DEV, *x.shape), x.dtype) for x in xs],
        scratch_shapes=[pltpu.SemaphoreType.DMA((n, 7)), pltpu.SemaphoreType.DMA((n, 7)), pltpu.SemaphoreType.DMA((n,))],
    )(*xs)


def _row_tile(rows, want):
    t = min(rows, want)
    while rows % t:
        t //= 2
    return t


def _sum_slots(name, slots):
    _, rows, cols = slots.shape
    tr = _row_tile(rows, 256)

    def body(s_ref, o_ref):
        total = s_ref[0].astype(F32)
        for d in range(1, N_DEV):
            total = total + s_ref[d].astype(F32)
        o_ref[...] = total

    return pl.pallas_call(
        body,
        name=name,
        grid=(rows // tr,),
        in_specs=[pl.BlockSpec((N_DEV, tr, cols), lambda i: (0, i, 0))],
        out_specs=pl.BlockSpec((tr, cols), lambda i: (i, 0)),
        out_shape=S((rows, cols), F32),
        compiler_params=_cparams(1),
    )(slots)


def _adamw(name, w, g, m, v):
    rows, cols = w.shape
    tr = rows if rows * cols * 4 <= (2 << 20) else _row_tile(rows, 1 << ((1 << 18) // cols).bit_length() - 1)

    def body(w_ref, g_ref, m_ref, v_ref, d_ref, nm_ref, nv_ref):
        gv = g_ref[...]
        nm = ADAM_B1 * m_ref[...] + (1.0 - ADAM_B1) * gv
        nv = ADAM_B2 * v_ref[...] + (1.0 - ADAM_B2) * (gv * gv)
        m_hat = nm / (1.0 - ADAM_B1**ADAM_STEP)
        v_hat = nv / (1.0 - ADAM_B2**ADAM_STEP)
        d_ref[...] = -ADAM_LR * (m_hat / (jnp.sqrt(v_hat) + ADAM_EPS) + ADAM_WD * w_ref[...])
        nm_ref[...] = nm
        nv_ref[...] = nv

    blk = pl.BlockSpec((tr, cols), lambda i: (i, 0))
    return pl.pallas_call(
        body,
        name=name,
        grid=(rows // tr,),
        in_specs=[blk] * 4,
        out_specs=[blk] * 3,
        out_shape=[S((rows, cols), F32)] * 3,
        compiler_params=_cparams(1),
    )(w, g, m, v)


_BIG = ("w_in", "w_out", "w_ff1", "w_ff2")
_TRANSPOSED = ("w_in", "w_ff1")
_SMALL = ("b_in", "w_pool", "pool_scale", "sgu_ln_g", "sgu_ln_b", "sgu_w", "sgu_b", "conv_b", "conv_ln_g", "conv_ln_b",
          "b_out", "ln1_g", "ln1_b", "b_ff1", "b_ff2", "ln2_g", "ln2_b")
_WEIGHTS = ("w_in", "b_in", "w_pool", "pool_scale", "sgu_ln_g", "sgu_ln_b", "sgu_w", "sgu_b", "conv_w", "conv_b", "conv_ln_g",
            "conv_ln_b", "w_out", "b_out", "ln1_g", "ln1_b", "w_ff1", "b_ff1", "w_ff2", "b_ff2", "ln2_g", "ln2_b")


def _pack(arrays):
    parts = []
    for a in arrays:
        rows = a.reshape(-1, 128)
        parts.append(jnp.pad(rows, ((0, -rows.shape[0] % 8), (0, 0))))
    return jnp.concatenate(parts, axis=0)


def _unpack(flat, like):
    out, at = [], 0
    for a in like:
        n = a.size // 128
        out.append(flat[at : at + n].reshape(a.shape))
        at += n + (-n % 8)
    return out


def _packed_rows(arrays):
    return sum(a.size // 128 + (-(a.size // 128) % 8) for a in arrays)


def kernel(x, w_in, b_in, w_pool, pool_scale, sgu_ln_g, sgu_ln_b, sgu_w, sgu_b, conv_w, conv_b, conv_ln_g, conv_ln_b, w_out, b_out, ln1_g, ln1_b, w_ff1, b_ff1, w_ff2, b_ff2, ln2_g, ln2_b, loss_target, m_w_in, m_b_in, m_w_pool, m_pool_scale, m_sgu_ln_g, m_sgu_ln_b, m_sgu_w, m_sgu_b, m_conv_w, m_conv_b, m_conv_ln_g, m_conv_ln_b, m_w_out, m_b_out, m_ln1_g, m_ln1_b, m_w_ff1, m_b_ff1, m_w_ff2, m_b_ff2, m_ln2_g, m_ln2_b, v_w_in, v_b_in, v_w_pool, v_pool_scale, v_sgu_ln_g, v_sgu_ln_b, v_sgu_w, v_sgu_b, v_conv_w, v_conv_b, v_conv_ln_g, v_conv_ln_b, v_w_out, v_b_out, v_ln1_g, v_ln1_b, v_w_ff1, v_b_ff1, v_w_ff2, v_b_ff2, v_ln2_g, v_ln2_b):
    w = dict(w_in=w_in, b_in=b_in, w_pool=w_pool, pool_scale=pool_scale, sgu_ln_g=sgu_ln_g, sgu_ln_b=sgu_ln_b, sgu_w=sgu_w,
             sgu_b=sgu_b, conv_w=conv_w, conv_b=conv_b, conv_ln_g=conv_ln_g, conv_ln_b=conv_ln_b, w_out=w_out, b_out=b_out,
             ln1_g=ln1_g, ln1_b=ln1_b, w_ff1=w_ff1, b_ff1=b_ff1, w_ff2=w_ff2, b_ff2=b_ff2, ln2_g=ln2_g, ln2_b=ln2_b)
    mom = dict(w_in=m_w_in, b_in=m_b_in, w_pool=m_w_pool, pool_scale=m_pool_scale, sgu_ln_g=m_sgu_ln_g, sgu_ln_b=m_sgu_ln_b,
               sgu_w=m_sgu_w, sgu_b=m_sgu_b, conv_w=m_conv_w, conv_b=m_conv_b, conv_ln_g=m_conv_ln_g, conv_ln_b=m_conv_ln_b,
               w_out=m_w_out, b_out=m_b_out, ln1_g=m_ln1_g, ln1_b=m_ln1_b, w_ff1=m_w_ff1, b_ff1=m_b_ff1, w_ff2=m_w_ff2,
               b_ff2=m_b_ff2, ln2_g=m_ln2_g, ln2_b=m_ln2_b)
    var = dict(w_in=v_w_in, b_in=v_b_in, w_pool=v_w_pool, pool_scale=v_pool_scale, sgu_ln_g=v_sgu_ln_g, sgu_ln_b=v_sgu_ln_b,
               sgu_w=v_sgu_w, sgu_b=v_sgu_b, conv_w=v_conv_w, conv_b=v_conv_b, conv_ln_g=v_conv_ln_g, conv_ln_b=v_conv_ln_b,
               w_out=v_w_out, b_out=v_b_out, ln1_g=v_ln1_g, ln1_b=v_ln1_b, w_ff1=v_w_ff1, b_ff1=v_b_ff1, w_ff2=v_w_ff2,
               b_ff2=v_b_ff2, ln2_g=v_ln2_g, ln2_b=v_ln2_b)
    T = x.shape[1]
    x0 = x.reshape(T, D_MODEL)
    target = loss_target.reshape(T, D_MODEL)
    me_lin = _lin(_place())

    shard = [
        {name: (w[name][l].T if name in _TRANSPOSED else w[name][l]).astype(BF16) for name in _BIG} for l in range(DEPTH)
    ]
    conv_shard = jnp.pad(conv_w, ((0, 0), (0, 1), (0, 128 - conv_w.shape[2]))).reshape(DEPTH * 32, 128)

    def rows_of(g):
        return g.reshape(N_DEV * g.shape[1], g.shape[2])

    first = _all_gather([shard[0]["w_in"], shard[0]["w_out"], conv_shard])
    full = [{} for _ in range(DEPTH)]
    full[0]["w_in"], full[0]["w_out"] = rows_of(first[0]), rows_of(first[1])
    conv_cols = conv_w.shape[2]
    conv_full = first[2].reshape(N_DEV, DEPTH, 32, 128)[:, :, :CONV_KERNEL, :conv_cols]
    conv_full = conv_full.transpose(1, 2, 0, 3).reshape(DEPTH, CONV_KERNEL, N_DEV * conv_cols)

    tril = jnp.tril(jnp.ones((GROUP, GROUP), F32))
    prm = []
    for l in range(DEPTH):
        wm = sgu_w[l] * tril
        prm.append(dict(
            wp=w_pool[l].reshape(POOL_WIDTH, GROUP).astype(BF16), ps=_row(pool_scale[l]), lg=_row(sgu_ln_g[l]), lb=_row(sgu_ln_b[l]),
            wm=wm.reshape(SGU_WIDTH, GROUP).astype(BF16), wmt=wm.transpose(0, 2, 1).reshape(SGU_WIDTH, GROUP).astype(BF16),
            bsf=jnp.repeat(sgu_b[l].T, GROUP, axis=1), cw=jnp.pad(conv_full[l], ((0, 1), (0, 0))), cb=_row(conv_b[l]),
            cg=_row(conv_ln_g[l]), cbeta=_row(conv_ln_b[l]),
        ))

    saved = []
    res = (x0, jnp.ones((D_MODEL,), F32), jnp.zeros((D_MODEL,), F32))
    xbf = x0.astype(BF16)
    riders = {"proj": [(0, "w_ff1")], "mixer": [(0, "w_ff2")], "out": [(1, "w_in"), (1, "w_out")], "ff1": [(1, "w_ff1")],
              "ff2": [(1, "w_ff2")]}

    def gather_on(l, call):
        return _Exchange([("gather", shard[gl][name]) for gl, name in riders[call]]) if l == 0 else None

    def gathered(l, call, ex):
        if ex is not None:
            for (gl, name), g in zip(riders[call], ex.results):
                full[gl][name] = rows_of(g)

    for l in range(DEPTH):
        f = full[l]
        ex = gather_on(l, "proj")
        proj = _mm_bias(f"proj{l}", xbf, f["w_in"], "nt", b_in[l], 1024, 896, 2048, side=ex)
        gathered(l, "proj", ex)
        ex = gather_on(l, "mixer")
        mixed = _mixer_fwd(f"mixer_fwd{l}", proj, prm[l], 256, side=ex)
        gathered(l, "mixer", ex)
        ex = gather_on(l, "out")
        xh1, rs1, x1bf = _mm_ln(f"out_ln1_{l}", mixed, f["w_out"], b_out[l], res, ln1_g[l], ln1_b[l], 512, 1024, side=ex)
        gathered(l, "out", ex)
        ex = gather_on(l, "ff1")
        act, hsq = _mm_relu2(f"ff1_{l}", x1bf, f["w_ff1"], "nt", b_ff1[l], 1024, 1024, 2048, side=ex)
        gathered(l, "ff1", ex)
        ex = gather_on(l, "ff2")
        xh2, rs2, x2bf = _mm_ln(
            f"ff2_ln2_{l}", hsq, f["w_ff2"], b_ff2[l], (xh1, ln1_g[l], ln1_b[l]), ln2_g[l], ln2_b[l], 512, 1024, side=ex)
        gathered(l, "ff2", ex)
        saved.append(dict(xin=xbf, proj=proj, mixed=mixed, xh1=xh1, rs1=rs1, x1bf=x1bf, act=act, hsq=hsq, xh2=xh2, rs2=rs2))
        res = (xh2, ln2_g[l], ln2_b[l])
        xbf = x2bf

    top = saved[-1]
    dr2, dr2bf, g_ln2g, g_ln2b, g_bff2, loss_row = _loss_top(top["xh2"], top["rs2"], ln2_g[-1], ln2_b[-1], target, 256)
    loss = lax.psum(loss_row[0, 0], ("x", "y", "c"))
    slots = [{} for _ in range(DEPTH)]
    gsm = [{} for _ in range(DEPTH)]
    grad_x = small_slots = None

    def stacked_small():
        st = {name: jnp.stack([gsm[gl][name].reshape(w[name].shape[1:]) for gl in range(DEPTH)]) for name in _SMALL}
        conv_g = jnp.pad(jnp.stack([gsm[gl]["conv_w"] for gl in range(DEPTH)]), ((0, 0), (0, 1), (0, 0)))
        return [st[name] for name in _SMALL] + [conv_g]

    for l in reversed(range(DEPTH)):
        f, sv = full[l], saved[l]
        gsm[l].update(ln2_g=g_ln2g, ln2_b=g_ln2b, b_ff2=g_bff2)
        gw = _mm_wgrad(f"gw_ff2_{l}", sv["hsq"], dr2bf, 512, 2048)
        ex = _Exchange([("slices", gw)])
        dhpre, g_bff1 = _mm_dh(f"dff1_{l}", dr2bf, f["w_ff2"], sv["act"], 1024, 1024, 2048, side=ex)
        slots[l]["w_ff2"] = ex.results[0]
        gsm[l]["b_ff1"] = g_bff1
        gw = _mm_wgrad(f"gw_ff1_{l}", dhpre, sv["x1bf"], 512, 2048)
        ex = _Exchange([("slices", gw)])
        dr1, dr1bf, g_ln1g, g_ln1b, g_bout = _mm_ln_bwd(
            f"dx1_ln1_{l}", dhpre, f["w_ff1"], dr2, sv["xh1"], sv["rs1"], ln1_g[l], 512, 1024, side=ex)
        slots[l]["w_ff1"] = ex.results[0]
        gsm[l].update(ln1_g=g_ln1g, ln1_b=g_ln1b, b_out=g_bout)
        gw = _mm_wgrad(f"gw_out_{l}", sv["mixed"], dr1bf, 512, 2048)
        ex = _Exchange([("slices", gw)])
        dmix = _mm_plain(f"dmixed{l}", dr1bf, f["w_out"], "nt", 1024, 1024, 2048, side=ex)
        slots[l]["w_out"] = ex.results[0]
        (dproj, g_wp, g_ps, g_lg, g_lb, g_wm, g_bs, g_cw, g_cb, g_cg, g_cbeta, g_bin) = _mixer_bwd(
            f"mixer_bwd{l}", sv["proj"], dmix, prm[l], 256)
        gsm[l].update(b_in=g_bin, w_pool=g_wp, pool_scale=g_ps, sgu_ln_g=g_lg, sgu_ln_b=g_lb, sgu_w=g_wm, sgu_b=g_bs[:SGU_HEADS],
                      conv_w=g_cw[:CONV_KERNEL], conv_b=g_cb, conv_ln_g=g_cg, conv_ln_b=g_cbeta)
        gw = _mm_wgrad(f"gw_in_{l}", dproj, sv["xin"], 512, 2048)
        if l > 0:
            below = saved[l - 1]
            ex = _Exchange([("slices", gw)])
            dr2, dr2bf, g_ln2g, g_ln2b, g_bff2 = _mm_ln_bwd(
                f"dx_ln2_{l}", dproj, f["w_in"], dr1, below["xh2"], below["rs2"], ln2_g[l - 1], 512, 896, side=ex)
        else:
            small_like = stacked_small()
            ex = _Exchange([("slices", gw), ("gather", _pack(small_like))])
            grad_x = _mm_plain("dx0", dproj, f["w_in"], "nn", 512, 2048, 896, res=dr1, side=ex)
            small_slots = ex.results[1]
        slots[l]["w_in"] = ex.results[0]

    grads, deltas, new_m, new_v = {}, {}, {}, {}
    for name in _BIG:
        per_layer = []
        for l in range(DEPTH):
            g = _sum_slots(f"sum_{name}_{l}", slots[l][name])
            per_layer.append(g.T if name in _TRANSPOSED else g)
        g = jnp.stack(per_layer)
        shape = w[name].shape
        two_d = (shape[0] * shape[1], shape[2])
        d, nm, nv = _adamw(f"adamw_{name}", w[name].reshape(two_d), g.reshape(two_d), mom[name].reshape(two_d), var[name].reshape(two_d))
        grads[name], deltas[name], new_m[name], new_v[name] = g, d.reshape(shape), nm.reshape(shape), nv.reshape(shape)

    total = _sum_slots("sum_small", small_slots)
    small_g = _unpack(total, small_like)
    like = [w[name] for name in _SMALL]
    d, nm, nv = _adamw("adamw_small", _pack(like), total[: _packed_rows(like)],
                       _pack([mom[name] for name in _SMALL]), _pack([var[name] for name in _SMALL]))
    for name, gg, dd, mm_, vv in zip(_SMALL, small_g, _unpack(d, like), _unpack(nm, like), _unpack(nv, like)):
        grads[name], deltas[name], new_m[name], new_v[name] = gg, dd, mm_, vv
    conv_g = lax.dynamic_slice_in_dim(small_g[-1][:, :CONV_KERNEL, :], me_lin * conv_cols, conv_cols, axis=2)
    flat = (DEPTH * CONV_KERNEL, conv_cols)
    d, nm, nv = _adamw("adamw_conv_w", conv_w.reshape(flat), conv_g.reshape(flat), m_conv_w.reshape(flat), v_conv_w.reshape(flat))
    grads["conv_w"], deltas["conv_w"], new_m["conv_w"], new_v["conv_w"] = conv_g, d.reshape(conv_w.shape), nm.reshape(conv_w.shape), nv.reshape(conv_w.shape)

    return (loss, grad_x.reshape(x.shape), *[grads[n] for n in _WEIGHTS], *[deltas[n] for n in _WEIGHTS],
            *[new_m[n] for n in _WEIGHTS], *[new_v[n] for n in _WEIGHTS])
```

```python
import functools

import jax
import jax.numpy as jnp
from jax import lax
from jax.experimental import pallas as pl
from jax.experimental.pallas import tpu as pltpu

F32, BF16 = jnp.float32, jnp.bfloat16
S = jax.ShapeDtypeStruct

DEPTH = 2
D_MODEL = 2048
POOL_WINDOWS = (2, 4, 8, 16)
POOL_WIDTH = 512
GROUP = 128
SGU_WIDTH = 768
SGU_HEADS = 6
CONV_WIDTH = 768
CONV_KERNEL = 31
IN_WIDTH = 3584
D_FF = 8192
ALPHA = (2 * DEPTH) ** 0.25
LN_EPS = 1e-5
ADAM_LR, ADAM_B1, ADAM_B2, ADAM_EPS, ADAM_WD, ADAM_STEP = 0.001, 0.9, 0.999, 1e-08, 0.01, 10

N_DEV = 8
HALO = 32
VMEM_LIMIT = 56 << 20
MESH = pl.DeviceIdType.MESH

C_POOL = (0, 512)
C_U = (512, 1280)
C_V = (1280, 2048)
C_CA = (2048, 2816)
C_CG = (2816, 3584)
M_POOL = (0, 512)
M_SGU = (512, 1280)
M_CONV = (1280, 2048)


def _cparams(n_axes):
    return pltpu.CompilerParams(dimension_semantics=("arbitrary",) * n_axes, vmem_limit_bytes=VMEM_LIMIT)


def _for_strips(rows, strip, fn):
    n = rows // strip
    if n == 1:
        fn(0)
        return

    def step(s, carry):
        fn(pl.multiple_of(s * strip, strip))
        return carry

    lax.fori_loop(0, n, step, 0)


def _row_sum(x):
    return jnp.sum(x, axis=0, keepdims=True)


def _ln_stats(r):
    mu = jnp.mean(r, axis=-1, keepdims=True)
    xc = r - mu
    var = jnp.mean(xc * xc, axis=-1, keepdims=True)
    rs = lax.rsqrt(var + LN_EPS)
    return xc * rs, rs


def _ln_bwd(dy, xhat, rs, g):
    gy = dy * g
    m1 = jnp.mean(gy, axis=-1, keepdims=True)
    m2 = jnp.mean(gy * xhat, axis=-1, keepdims=True)
    return rs * (gy - m1 - xhat * m2)


_GELU_C = 0.7978845608028654


def _gelu(x):
    th = jnp.tanh(_GELU_C * (x + 0.044715 * (x * x * x)))
    return 0.5 * x * (1.0 + th), th


def _gelu_grad(x, th):
    return 0.5 * (1.0 + th) + 0.5 * x * (1.0 - th * th) * (_GELU_C * (1.0 + 3.0 * 0.044715 * (x * x)))


def _place():
    x, y, c = lax.axis_index("x"), lax.axis_index("y"), lax.axis_index("c")
    return x, y, c


def _lin(p):
    return 4 * p[0] + 2 * p[1] + p[2]


def _flip(p, r):
    return tuple(1 - v if (r >> (2 - ax)) & 1 else v for ax, v in enumerate(p))


_ANY = pl.BlockSpec(memory_space=pl.ANY)


class _Exchange:
    def __init__(self, items):
        self.kinds = [item[0] for item in items]
        self.srcs = [item[1] for item in items]
        self.rows = [item[2] if len(item) > 2 else None for item in items]
        handed_on = [item[3] if len(item) > 3 else None for item in items]
        self.out_shape = [
            S((N_DEV, *x.shape), x.dtype) if kind == "gather" else S((N_DEV, x.shape[0] // N_DEV, x.shape[1]), x.dtype)
            for kind, x in zip(self.kinds, self.srcs)
        ]
        n = len(items)
        self.ins = self.srcs + [b for b in handed_on if b is not None]
        self.aliases = {}
        for a, b in enumerate(handed_on):
            if b is not None:
                self.aliases[n + len(self.aliases)] = a
        self.scratch = [pltpu.SemaphoreType.DMA((n, 7)), pltpu.SemaphoreType.DMA((n, 7)), pltpu.SemaphoreType.DMA((n,))]
        self.results = None

    def _src(self, in_refs, a, dest):
        if self.kinds[a] == "gather":
            return in_refs[a] if self.rows[a] is None else in_refs[a].at[pl.ds(*self.rows[a])]
        rows = self.srcs[a].shape[0] // N_DEV
        return in_refs[a].at[pl.ds(pl.multiple_of(_lin(dest) * rows, 8), rows)]

    def _dst(self, out_refs, a, slot):
        return out_refs[a].at[slot] if self.rows[a] is None else out_refs[a].at[slot, pl.ds(*self.rows[a])]

    def _copies(self, in_refs, out_refs, sems, with_arrivals):
        send_sems, recv_sems, local_sems = sems
        me = _place()
        local, sends, arrivals = [], [], []
        for a in range(len(self.srcs)):
            local.append(pltpu.make_async_copy(self._src(in_refs, a, me), self._dst(out_refs, a, _lin(me)), local_sems.at[a]))
            for r in range(1, N_DEV):
                peer = _flip(me, r)
                for slot, group in ((_lin(me), sends), (_lin(peer), arrivals)):
                    if group is sends or with_arrivals:
                        group.append(pltpu.make_async_remote_copy(
                            src_ref=self._src(in_refs, a, peer), dst_ref=self._dst(out_refs, a, slot),
                            send_sem=send_sems.at[a, r - 1], recv_sem=recv_sems.at[a, r - 1], device_id=peer, device_id_type=MESH,
                        ))
        return local, sends, arrivals

    def start(self, in_refs, out_refs, sems):
        local, sends, _ = self._copies(in_refs, out_refs, sems, False)
        for cp in local + sends:
            cp.start()

    def wait(self, in_refs, out_refs, sems):
        local, sends, arrivals = self._copies(in_refs, out_refs, sems, True)
        for cp in arrivals:
            cp.wait_recv()
        for cp in sends:
            cp.wait_send()
        for cp in local:
            cp.wait()


def _call(name, body, grid, in_specs, out_specs, out_shape, scratch, args, side=None):
    in_specs, out_specs, out_shape, scratch = list(in_specs), list(out_specs), list(out_shape), list(scratch)
    if side is None:
        return pl.pallas_call(
            body, name=name, grid=grid, in_specs=in_specs, out_specs=out_specs, out_shape=out_shape, scratch_shapes=scratch,
            compiler_params=_cparams(len(grid)),
        )(*args)
    n_in, n_out, n_scr = len(in_specs), len(out_specs), len(scratch)
    s_in, s_out = len(side.ins), len(side.out_shape)

    def wrapped(*refs):
        at = 0
        parts = []
        for n in (n_in, s_in, n_out, s_out, n_scr, 3):
            parts.append(refs[at : at + n])
            at += n
        ins, side_ins, outs, side_outs, scr, sems = parts
        pids = [pl.program_id(d) for d in range(len(grid))]
        first = functools.reduce(jnp.logical_and, [p == 0 for p in pids])
        last = functools.reduce(jnp.logical_and, [p == g - 1 for p, g in zip(pids, grid)])

        @pl.when(first)
        def _():
            side.start(side_ins, side_outs, sems)

        body(*ins, *outs, *scr)

        @pl.when(last)
        def _():
            side.wait(side_ins, side_outs, sems)

    res = pl.pallas_call(
        wrapped, name=name, grid=grid, in_specs=in_specs + [_ANY] * s_in, out_specs=out_specs + [_ANY] * s_out,
        out_shape=out_shape + side.out_shape, scratch_shapes=scratch + side.scratch, compiler_params=_cparams(len(grid)),
        input_output_aliases={n_in + i: n_out + o for i, o in side.aliases.items()},
    )(*args, *side.ins)
    side.results = list(res[n_out:])
    return list(res[:n_out])


_CONTRACT = {"nn": ((1,), (0,)), "nt": ((1,), (1,)), "tn": ((0,), (0,))}


def _mm(name, a, b, dims, tm, tn, tk, *, ins=(), outs, epilogue, j_outer=False, side=None):
    if dims == "tn":
        K, M = a.shape
    else:
        M, K = a.shape
    N = b.shape[0] if dims == "nt" else b.shape[1]
    tm, tn, tk = min(tm, M), min(tn, N), min(tk, K)
    assert M % tm == 0 and N % tn == 0 and K % tk == 0, (name, M, N, K, tm, tn, tk)
    nm, nn, nk = M // tm, N // tn, K // tk
    if j_outer:
        grid = (nn, nm, nk)
        ij = lambda g0, g1: (g1, g0)
    else:
        grid = (nm, nn, nk)
        ij = lambda g0, g1: (g0, g1)

    def amap(g0, g1, k):
        i, _ = ij(g0, g1)
        return (k, i) if dims == "tn" else (i, k)

    def bmap(g0, g1, k):
        _, j = ij(g0, g1)
        return (j, k) if dims == "nt" else (k, j)

    def spec(kind):
        if kind == "tile":
            return pl.BlockSpec((tm, tn), lambda g0, g1, k: ij(g0, g1))
        if kind == "row":
            return pl.BlockSpec((1, tn), lambda g0, g1, k: (0, ij(g0, g1)[1]))
        assert kind == "col", kind
        return pl.BlockSpec((tm, 1), lambda g0, g1, k: (ij(g0, g1)[0], 0))

    in_specs = [
        pl.BlockSpec((tk, tm) if dims == "tn" else (tm, tk), amap),
        pl.BlockSpec((tn, tk) if dims == "nt" else (tk, tn), bmap),
    ] + [spec(kind) for _, kind in ins]
    out_specs = [spec(kind) for _, _, kind in outs]
    out_shape = [S(shape, dtype) for shape, dtype, _ in outs]
    n_in, n_out = len(ins), len(outs)
    contract = (_CONTRACT[dims], ((), ()))

    def body(*refs):
        a_ref, b_ref = refs[:2]
        in_refs = refs[2 : 2 + n_in]
        out_refs = refs[2 + n_in : 2 + n_in + n_out]
        acc = refs[2 + n_in + n_out]
        i, _ = ij(pl.program_id(0), pl.program_id(1))
        k = pl.program_id(2)

        def part():
            return lax.dot_general(a_ref[...], b_ref[...], contract, preferred_element_type=F32)

        @pl.when(k == 0)
        def _():
            acc[...] = part()

        @pl.when(k > 0)
        def _():
            acc[...] += part()

        @pl.when(k == nk - 1)
        def _():
            epilogue(i, acc, in_refs, out_refs)

    return _call(name, body, grid, in_specs, out_specs, out_shape, [pltpu.VMEM((tm, tn), F32)], [a, b, *[x for x, _ in ins]], side)


def _row(v):
    return v.reshape(1, -1)


def _mm_bias(name, a, b, dims, bias, tm, tn, tk, side=None):
    M = a.shape[0]
    N = b.shape[0] if dims == "nt" else b.shape[1]

    def epilogue(i, acc, ins, outs):
        def strip(r0):
            rows = pl.ds(r0, 128)
            outs[0][rows, :] = acc[rows, :] + ins[0][...]

        _for_strips(acc.shape[0], 128, strip)

    return _mm(name, a, b, dims, tm, tn, tk, ins=[(_row(bias), "row")], outs=[((M, N), F32, "tile")], epilogue=epilogue, side=side)[0]


def _mm_relu2(name, a, b, dims, bias, tm, tn, tk, side=None):
    M = a.shape[0]
    N = b.shape[0] if dims == "nt" else b.shape[1]

    def epilogue(i, acc, ins, outs):
        def strip(r0):
            rows = pl.ds(r0, 128)
            r = jnp.maximum(acc[rows, :] + ins[0][...], 0.0)
            outs[0][rows, :] = r.astype(BF16)
            outs[1][rows, :] = (r * r).astype(BF16)

        _for_strips(acc.shape[0], 128, strip)

    return _mm(
        name, a, b, dims, tm, tn, tk, ins=[(_row(bias), "row")],
        outs=[((M, N), BF16, "tile"), ((M, N), BF16, "tile")], epilogue=epilogue, side=side,
    )


def _mm_ln(name, a, b, bias, res, g, beta, tm, tk, side=None):
    M = a.shape[0]
    N = b.shape[1]
    rxh, rg, rb = res

    def epilogue(i, acc, ins, outs):
        bias_r, rxh_r, rg_r, rb_r, g_r, beta_r = ins
        xhat_o, rstd_o, xbf_o = outs

        def strip(r0):
            rows = pl.ds(r0, 64)
            resid = rxh_r[rows, :] * rg_r[...] + rb_r[...]
            r = ALPHA * resid + (acc[rows, :] + bias_r[...])
            xhat, rs = _ln_stats(r)
            xhat_o[rows, :] = xhat
            rstd_o[rows, :] = rs
            xbf_o[rows, :] = (xhat * g_r[...] + beta_r[...]).astype(BF16)

        _for_strips(acc.shape[0], 64, strip)

    return _mm(
        name, a, b, "nn", tm, N, tk,
        ins=[(_row(bias), "row"), (rxh, "tile"), (_row(rg), "row"), (_row(rb), "row"), (_row(g), "row"), (_row(beta), "row")],
        outs=[((M, N), F32, "tile"), ((M, 1), F32, "col"), ((M, N), BF16, "tile")],
        epilogue=epilogue, side=side,
    )


def _ln_bwd_strip(dyv, xhat, rs, g, dr_o, drbf_o, dg_o, db_o, dsum_o, rows):
    dr = _ln_bwd(dyv, xhat, rs, g)
    dr_o[rows, :] = dr
    drbf_o[rows, :] = dr.astype(BF16)
    dg_o[...] += _row_sum(dyv * xhat)
    db_o[...] += _row_sum(dyv)
    dsum_o[...] += _row_sum(dr)


def _mm_ln_bwd(name, a, b, resgrad, xhat, rstd, g, tm, tk, side=None):
    M = a.shape[0]
    N = b.shape[1]

    def epilogue(i, acc, ins, outs):
        rg_r, xh_r, rs_r, g_r = ins
        dr_o, drbf_o, dg_o, db_o, dsum_o = outs

        @pl.when(i == 0)
        def _():
            dg_o[...] = jnp.zeros_like(dg_o)
            db_o[...] = jnp.zeros_like(db_o)
            dsum_o[...] = jnp.zeros_like(dsum_o)

        def strip(r0):
            rows = pl.ds(r0, 64)
            dyv = acc[rows, :] + ALPHA * rg_r[rows, :]
            _ln_bwd_strip(dyv, xh_r[rows, :], rs_r[rows, :], g_r[...], dr_o, drbf_o, dg_o, db_o, dsum_o, rows)

        _for_strips(acc.shape[0], 64, strip)

    return _mm(
        name, a, b, "nn", tm, N, tk,
        ins=[(resgrad, "tile"), (xhat, "tile"), (rstd, "col"), (_row(g), "row")],
        outs=[((M, N), F32, "tile"), ((M, N), BF16, "tile"), ((1, N), F32, "row"), ((1, N), F32, "row"), ((1, N), F32, "row")],
        epilogue=epilogue, side=side,
    )


def _mm_dh(name, a, b, act, tm, tn, tk, side=None):
    M = a.shape[0]
    N = b.shape[0]

    def epilogue(i, acc, ins, outs):
        @pl.when(i == 0)
        def _():
            outs[1][...] = jnp.zeros_like(outs[1])

        def strip(r0):
            rows = pl.ds(r0, 128)
            d = acc[rows, :] * (2.0 * ins[0][rows, :].astype(F32))
            outs[0][rows, :] = d.astype(BF16)
            outs[1][...] += _row_sum(d)

        _for_strips(acc.shape[0], 128, strip)

    return _mm(
        name, a, b, "nt", tm, tn, tk, ins=[(act, "tile")],
        outs=[((M, N), BF16, "tile"), ((1, N), F32, "row")], epilogue=epilogue, j_outer=True, side=side,
    )


def _mm_plain(name, a, b, dims, tm, tn, tk, res=None, side=None):
    M = a.shape[0]
    N = b.shape[0] if dims == "nt" else b.shape[1]

    def epilogue(i, acc, ins, outs):
        def strip(r0):
            rows = pl.ds(r0, 128)
            v = acc[rows, :]
            if res is not None:
                v = v + ALPHA * ins[0][rows, :]
            outs[0][rows, :] = v

        _for_strips(acc.shape[0], 128, strip)

    return _mm(
        name, a, b, dims, tm, tn, tk, ins=[] if res is None else [(res, "tile")],
        outs=[((M, N), F32, "tile")], epilogue=epilogue, side=side,
    )[0]


def _mm_wgrad(name, a, b, tm, tk, side=None):
    M = a.shape[1]
    N = b.shape[1]

    def epilogue(i, acc, ins, outs):
        def strip(r0):
            rows = pl.ds(r0, 128)
            outs[0][rows, :] = acc[rows, :].astype(BF16)

        _for_strips(acc.shape[0], 128, strip)

    return _mm(name, a, b, "tn", tm, N, tk, outs=[((M, N), BF16, "tile")], epilogue=epilogue, side=side)[0]


def _loss_top(xhat, rstd, g, beta, target, tm):
    T, D = xhat.shape
    tm = min(tm, T)
    nt = T // tm

    def body(xh_r, rs_r, g_r, b_r, t_r, dr_o, drbf_o, dg_o, db_o, dsum_o, loss_o, sq_acc):
        i = pl.program_id(0)

        @pl.when(i == 0)
        def _():
            dg_o[...] = jnp.zeros_like(dg_o)
            db_o[...] = jnp.zeros_like(db_o)
            dsum_o[...] = jnp.zeros_like(dsum_o)
            sq_acc[...] = jnp.zeros_like(sq_acc)

        def strip(r0):
            rows = pl.ds(r0, 64)
            xh = xh_r[rows, :]
            err = (xh * g_r[...] + b_r[...]) - t_r[rows, :]
            sq_acc[...] += _row_sum(err * err)
            _ln_bwd_strip(err * (1.0 / D), xh, rs_r[rows, :], g_r[...], dr_o, drbf_o, dg_o, db_o, dsum_o, rows)

        _for_strips(tm, 64, strip)

        @pl.when(i == nt - 1)
        def _():
            total = jnp.sum(sq_acc[...], axis=-1, keepdims=True) * (0.5 / D)
            loss_o[...] = jnp.broadcast_to(total, loss_o.shape)

    tile = pl.BlockSpec((tm, D), lambda i: (i, 0))
    row = pl.BlockSpec((1, D), lambda i: (0, 0))
    return pl.pallas_call(
        body,
        name="loss_top",
        grid=(nt,),
        in_specs=[tile, pl.BlockSpec((tm, 1), lambda i: (i, 0)), row, row, tile],
        out_specs=[tile, tile, row, row, row, pl.BlockSpec((1, 128), lambda i: (0, 0))],
        out_shape=[S((T, D), F32), S((T, D), BF16), S((1, D), F32), S((1, D), F32), S((1, D), F32), S((1, 128), F32)],
        scratch_shapes=[pltpu.VMEM((1, D), F32)],
        compiler_params=_cparams(1),
    )(xhat, rstd, _row(g), _row(beta), target)


def _cols(ref, c):
    return ref[:, c[0] : c[1]]


def _causal_window_sum(e, w):
    s, sh = e, 1
    while sh < w:
        s = s + pltpu.roll(s, sh, axis=0)
        sh *= 2
    return s


def _anticausal_window_sum(d, w):
    n = d.shape[0]
    r, sh = d, 1
    while sh < w:
        r = r + pltpu.roll(r, n - sh, axis=0)
        sh *= 2
    return r


def _with_halo(halo_ref, main_ref, c, keep):
    return jnp.concatenate([_cols(halo_ref, c) * keep, _cols(main_ref, c)], axis=0)


def _pool_counts(tile_index, R, w):
    pos = lax.broadcasted_iota(jnp.int32, (R, 1), 0) + tile_index * R
    return jnp.minimum(pos + 1, w).astype(F32)


def _sgu_mix(wm_ref, vnb):
    return jnp.concatenate(
        [
            jnp.dot(wm_ref[h * GROUP : (h + 1) * GROUP, :], vnb[:, h * GROUP : (h + 1) * GROUP], preferred_element_type=F32)
            for h in range(SGU_HEADS)
        ],
        axis=1,
    )


def _conv_taps(buf, cw_ref, first, rows):
    acc = buf[pl.ds(first, rows), :] * cw_ref[pl.ds(0, 1), :]
    for k in range(1, CONV_KERNEL):
        acc = acc + buf[pl.ds(first + k, rows), :] * cw_ref[pl.ds(k, 1), :]
    return acc


def _mixer_params(p):
    return [p["wp"], p["ps"], p["lg"], p["lb"], p["wm"], p["wmt"], p["bsf"], p["cw"], p["cb"], p["cg"], p["cbeta"]]


def _whole(x):
    return pl.BlockSpec(x.shape, lambda i: (0,) * x.ndim)


def _mixer_fwd(name, proj, p, R, side=None):
    T = proj.shape[0]
    R = min(R, T)
    E = R + HALO
    nt = T // R
    hb = R // HALO
    halo_off = HALO - (CONV_KERNEL - 1)

    def body(pm, ph, wp, ps, lg, lb, wm, wmt, bsf, cw, cb, cg, cbeta, out, hbuf):
        i = pl.program_id(0)
        keep = (i > 0).astype(F32)
        a_ext = _with_halo(ph, pm, C_POOL, keep)
        for gi, w in enumerate(POOL_WINDOWS):
            cs = slice(gi * GROUP, (gi + 1) * GROUP)
            e = a_ext[:, cs]
            s = _causal_window_sum(e, w)
            pooled = s[HALO:, :] / _pool_counts(i, R, w) - e[HALO:, :]
            z = jnp.dot(pooled.astype(BF16), wp[cs, :], preferred_element_type=F32)
            out[:, cs] = (z * ps[:, cs]).astype(BF16)
        u, _ = _gelu(_cols(pm, C_U))
        v, _ = _gelu(_cols(pm, C_V))
        vhat, _ = _ln_stats(v)
        vn = vhat * lg[...] + lb[...]
        for c in range(R // GROUP):
            rs = slice(c * GROUP, (c + 1) * GROUP)
            mixed = _sgu_mix(wm, vn[rs, :].astype(BF16)) + bsf[...]
            out[rs, M_SGU[0] : M_SGU[1]] = (u[rs, :] * mixed).astype(BF16)
        hbuf[...] = _with_halo(ph, pm, C_CA, keep) * jax.nn.sigmoid(_with_halo(ph, pm, C_CG, keep))
        conv = _conv_taps(hbuf, cw, halo_off, R) + cb[...]
        chat, _ = _ln_stats(conv)
        cn = chat * cg[...] + cbeta[...]
        out[:, M_CONV[0] : M_CONV[1]] = (cn * jax.nn.sigmoid(cn)).astype(BF16)

    params = _mixer_params(p)
    in_specs = [
        pl.BlockSpec((R, IN_WIDTH), lambda i: (i, 0)),
        pl.BlockSpec((HALO, IN_WIDTH), lambda i: (jnp.maximum(i * hb - 1, 0), 0)),
    ] + [_whole(x) for x in params]
    return _call(
        name, body, (nt,), in_specs, [pl.BlockSpec((R, D_MODEL), lambda i: (i, 0))], [S((T, D_MODEL), BF16)],
        [pltpu.VMEM((E, CONV_WIDTH), F32)], [proj, proj, *params], side,
    )[0]


def _mixer_bwd(name, proj, dmix, p, R):
    T = proj.shape[0]
    R = min(R, T)
    E = R + HALO
    nt = T // R
    hb = R // HALO
    halo_off = HALO - (CONV_KERNEL - 1)

    def body(pm, ph, dm, wp, ps, lg, lb, wm, wmt, bsf, cw, cb, cg, cbeta,
             dproj, dwp, dps, dlg, dlb, dwm, dbs, dcw, dcb, dcg, dcbeta, dbin,
             hbuf, dbuf, carry_p, carry_c, dbs_acc):
        step = pl.program_id(0)
        ti = nt - 1 - step
        keep = (ti > 0).astype(F32)

        @pl.when(step == 0)
        def _():
            for r in (dwp, dps, dlg, dlb, dwm, dcw, dcb, dcg, dcbeta, dbin, carry_p, carry_c, dbs_acc):
                r[...] = jnp.zeros_like(r)

        def tail(carry):
            return jnp.concatenate([jnp.zeros((R - HALO, carry.shape[1]), F32), carry], axis=0)

        def head(x):
            return jnp.concatenate([jnp.zeros((HALO, x.shape[1]), F32), x], axis=0)

        a_ext = _with_halo(ph, pm, C_POOL, keep)
        carry_in = carry_p[...]
        for gi, w in enumerate(POOL_WINDOWS):
            cs = slice(gi * GROUP, (gi + 1) * GROUP)
            e = a_ext[:, cs]
            s = _causal_window_sum(e, w)
            cnt = _pool_counts(ti, R, w)
            pooled_b = (s[HALO:, :] / cnt - e[HALO:, :]).astype(BF16)
            wg = wp[cs, :]
            z = jnp.dot(pooled_b, wg, preferred_element_type=F32)
            dya = dm[:, cs]
            dps[:, cs] += _row_sum(dya * z)
            dz_b = (dya * ps[:, cs]).astype(BF16)
            dwp[cs, :] += lax.dot_general(pooled_b, dz_b, (((0,), (0,)), ((), ())), preferred_element_type=F32)
            dpooled = lax.dot_general(dz_b, wg, (((1,), (1,)), ((), ())), preferred_element_type=F32)
            da_ext = _anticausal_window_sum(head(dpooled / cnt), w) - head(dpooled)
            carry_p[:, cs] = da_ext[:HALO, :]
            d_a = da_ext[HALO:, :] + tail(carry_in[:, cs])
            dbin[:, cs] += _row_sum(d_a)
            dproj[:, cs] = d_a.astype(BF16)

        pu = _cols(pm, C_U)
        pv = _cols(pm, C_V)
        u, thu = _gelu(pu)
        v, thv = _gelu(pv)
        vhat, vrs = _ln_stats(v)
        vn = vhat * lg[...] + lb[...]
        dyb = dm[:, M_SGU[0] : M_SGU[1]]
        du_parts, dvn_parts = [], []
        for c in range(R // GROUP):
            rs = slice(c * GROUP, (c + 1) * GROUP)
            vnb = vn[rs, :].astype(BF16)
            mixed = _sgu_mix(wm, vnb) + bsf[...]
            du_parts.append(dyb[rs, :] * mixed)
            dmixed = dyb[rs, :] * u[rs, :]
            dbs_acc[...] += dmixed
            dmb = dmixed.astype(BF16)
            dvn_h = []
            for h in range(SGU_HEADS):
                hs = slice(h * GROUP, (h + 1) * GROUP)
                dwm[hs, :] += lax.dot_general(dmb[:, hs], vnb[:, hs], (((1,), (1,)), ((), ())), preferred_element_type=F32)
                dvn_h.append(jnp.dot(wmt[hs, :], dmb[:, hs], preferred_element_type=F32))
            dvn_parts.append(jnp.concatenate(dvn_h, axis=1))
        du = jnp.concatenate(du_parts, axis=0) if len(du_parts) > 1 else du_parts[0]
        dvn = jnp.concatenate(dvn_parts, axis=0) if len(dvn_parts) > 1 else dvn_parts[0]
        dlg[...] += _row_sum(dvn * vhat)
        dlb[...] += _row_sum(dvn)
        d_pu = du * _gelu_grad(pu, thu)
        d_pv = _ln_bwd(dvn, vhat, vrs, lg[...]) * _gelu_grad(pv, thv)
        dbin[:, C_U[0] : C_U[1]] += _row_sum(d_pu)
        dbin[:, C_V[0] : C_V[1]] += _row_sum(d_pv)
        dproj[:, C_U[0] : C_U[1]] = d_pu.astype(BF16)
        dproj[:, C_V[0] : C_V[1]] = d_pv.astype(BF16)

        sg_ext = jax.nn.sigmoid(_with_halo(ph, pm, C_CG, keep))
        ca_ext = _with_halo(ph, pm, C_CA, keep)
        hbuf[...] = ca_ext * sg_ext
        conv = _conv_taps(hbuf, cw, halo_off, R) + cb[...]
        chat, crs = _ln_stats(conv)
        cn = chat * cg[...] + cbeta[...]
        sc = jax.nn.sigmoid(cn)
        dcn = dm[:, M_CONV[0] : M_CONV[1]] * (sc * (1.0 + cn * (1.0 - sc)))
        dcg[...] += _row_sum(dcn * chat)
        dcbeta[...] += _row_sum(dcn)
        dconv = _ln_bwd(dcn, chat, crs, cg[...])
        dcb[...] += _row_sum(dconv)
        for k in range(CONV_KERNEL):
            dcw[pl.ds(k, 1), :] += _row_sum(dconv * hbuf[pl.ds(halo_off + k, R), :])
        dbuf[pl.ds(0, HALO), :] = jnp.zeros((HALO, CONV_WIDTH), F32)
        dbuf[pl.ds(HALO, R), :] = dconv
        dbuf[pl.ds(HALO + R, HALO), :] = jnp.zeros((HALO, CONV_WIDTH), F32)
        dhc = dbuf[pl.ds(CONV_KERNEL - 1, E), :] * cw[pl.ds(0, 1), :]
        for k in range(1, CONV_KERNEL):
            dhc = dhc + dbuf[pl.ds(CONV_KERNEL - 1 - k, E), :] * cw[pl.ds(k, 1), :]
        dhc_main = dhc[HALO:, :] + tail(carry_c[...])
        carry_c[...] = dhc[:HALO, :]
        sg = sg_ext[HALO:, :]
        d_ca = dhc_main * sg
        d_cg = dhc_main * ca_ext[HALO:, :] * (sg * (1.0 - sg))
        dbin[:, C_CA[0] : C_CA[1]] += _row_sum(d_ca)
        dbin[:, C_CG[0] : C_CG[1]] += _row_sum(d_cg)
        dproj[:, C_CA[0] : C_CA[1]] = d_ca.astype(BF16)
        dproj[:, C_CG[0] : C_CG[1]] = d_cg.astype(BF16)

        @pl.when(step == nt - 1)
        def _():
            row = lax.broadcasted_iota(jnp.int32, (GROUP, GROUP), 0)
            col = lax.broadcasted_iota(jnp.int32, (GROUP, GROUP), 1)
            dbs[...] = jnp.zeros_like(dbs)
            for h in range(SGU_HEADS):
                hs = slice(h * GROUP, (h + 1) * GROUP)
                dwm[hs, :] = jnp.where(row >= col, dwm[hs, :], 0.0)
                dbs[pl.ds(h, 1), :] = _row_sum(dbs_acc[:, hs].T)

    params = _mixer_params(p)
    accs = [
        S((POOL_WIDTH, GROUP), F32), S((1, POOL_WIDTH), F32), S((1, SGU_WIDTH), F32), S((1, SGU_WIDTH), F32),
        S((SGU_WIDTH, GROUP), F32), S((8, GROUP), F32), S((32, CONV_WIDTH), F32), S((1, CONV_WIDTH), F32),
        S((1, CONV_WIDTH), F32), S((1, CONV_WIDTH), F32), S((1, IN_WIDTH), F32),
    ]
    return pl.pallas_call(
        body,
        name=name,
        grid=(nt,),
        in_specs=[
            pl.BlockSpec((R, IN_WIDTH), lambda i: (nt - 1 - i, 0)),
            pl.BlockSpec((HALO, IN_WIDTH), lambda i: (jnp.maximum((nt - 1 - i) * hb - 1, 0), 0)),
            pl.BlockSpec((R, D_MODEL), lambda i: (nt - 1 - i, 0)),
        ]
        + [_whole(x) for x in params],
        out_specs=[pl.BlockSpec((R, IN_WIDTH), lambda i: (nt - 1 - i, 0))] + [_whole(x) for x in accs],
        out_shape=[S((T, IN_WIDTH), BF16)] + accs,
        scratch_shapes=[
            pltpu.VMEM((E, CONV_WIDTH), F32), pltpu.VMEM((E + HALO, CONV_WIDTH), F32),
            pltpu.VMEM((HALO, POOL_WIDTH), F32), pltpu.VMEM((HALO, CONV_WIDTH), F32), pltpu.VMEM((GROUP, SGU_WIDTH), F32),
        ],
        compiler_params=_cparams(1),
    )(proj, proj, dmix, *params)


def _all_gather(xs):
    n = len(xs)

    def body(*refs):
        x_refs, o_refs = refs[:n], refs[n : 2 * n]
        send_sems, recv_sems, local_sems = refs[2 * n :]
        x, y, c = _place()
        me, sibling = (x, y, c), (x, y, 1 - c)
        chips = [(1 - x, y), (x, 1 - y), (1 - x, 1 - y)]

        def copy(a, k, block, to, src=None):
            dst = o_refs[a].at[_lin(block)]
            return pltpu.make_async_remote_copy(
                src_ref=dst if src is None else src, dst_ref=dst, send_sem=send_sems.at[a, k], recv_sem=recv_sems.at[a, k],
                device_id=to, device_id_type=MESH,
            )

        mine = [pltpu.make_async_copy(x_refs[a], o_refs[a].at[_lin(me)], local_sems.at[a]) for a in range(n)]
        for m in mine:
            m.start()
        first = []
        for a in range(n):
            first.append(copy(a, 0, me, sibling, src=x_refs[a]))
            first += [copy(a, 1 + j, me, (*chip, c), src=x_refs[a]) for j, chip in enumerate(chips)]
        for cp in first:
            cp.start()
        passed = []
        for a in range(n):
            for j, chip in enumerate(chips):
                copy(a, 1 + j, (*chip, c), me).wait_recv()
                fwd = copy(a, 4 + j, (*chip, c), sibling)
                fwd.start()
                passed.append(fwd)
        for a in range(n):
            copy(a, 0, sibling, me).wait_recv()
            for j, chip in enumerate(chips):
                copy(a, 4 + j, (*chip, 1 - c), me).wait_recv()
        for cp in first + passed:
            cp.wait_send()
        for m in mine:
            m.wait()

    return pl.pallas_call(
        body,
        name="all_gather_weights",
        in_specs=[_ANY] * n,
        out_specs=[_ANY] * n,
        out_shape=[S((N_DEV, *x.shape), x.dtype) for x in xs],
        scratch_shapes=[pltpu.SemaphoreType.DMA((n, 7)), pltpu.SemaphoreType.DMA((n, 7)), pltpu.SemaphoreType.DMA((n,))],
    )(*xs)


def _row_tile(rows, want):
    return next(t for t in range(min(rows, want) // 8 * 8, 0, -8) if rows % t == 0)


def _sum_slots(name, slots):
    _, rows, cols = slots.shape
    tr = _row_tile(rows, (4 << 20) // (N_DEV * cols * slots.dtype.itemsize))

    def body(s_ref, o_ref):
        total = s_ref[0].astype(F32)
        for d in range(1, N_DEV):
            total = total + s_ref[d].astype(F32)
        o_ref[...] = total

    return pl.pallas_call(
        body,
        name=name,
        grid=(rows // tr,),
        in_specs=[pl.BlockSpec((N_DEV, tr, cols), lambda i: (0, i, 0))],
        out_specs=pl.BlockSpec((tr, cols), lambda i: (i, 0)),
        out_shape=S((rows, cols), F32),
        compiler_params=_cparams(1),
    )(slots)


def _adamw(name, w, g, m, v):
    rows, cols = w.shape
    tr = rows if rows * cols * 4 <= (2 << 20) else _row_tile(rows, 1 << ((1 << 18) // cols).bit_length() - 1)

    def body(w_ref, g_ref, m_ref, v_ref, d_ref, nm_ref, nv_ref):
        gv = g_ref[...]
        nm = ADAM_B1 * m_ref[...] + (1.0 - ADAM_B1) * gv
        nv = ADAM_B2 * v_ref[...] + (1.0 - ADAM_B2) * (gv * gv)
        m_hat = nm / (1.0 - ADAM_B1**ADAM_STEP)
        v_hat = nv / (1.0 - ADAM_B2**ADAM_STEP)
        d_ref[...] = -ADAM_LR * (m_hat / (jnp.sqrt(v_hat) + ADAM_EPS) + ADAM_WD * w_ref[...])
        nm_ref[...] = nm
        nv_ref[...] = nv

    blk = pl.BlockSpec((tr, cols), lambda i: (i, 0))
    return pl.pallas_call(
        body,
        name=name,
        grid=(rows // tr,),
        in_specs=[blk] * 4,
        out_specs=[blk] * 3,
        out_shape=[S((rows, cols), F32)] * 3,
        compiler_params=_cparams(1),
    )(w, g, m, v)


_BIG = ("w_in", "w_out", "w_ff1", "w_ff2")
_TRANSPOSED = ("w_in", "w_ff1")
_SMALL = ("b_in", "w_pool", "pool_scale", "sgu_ln_g", "sgu_ln_b", "sgu_w", "sgu_b", "conv_b", "conv_ln_g", "conv_ln_b",
          "b_out", "ln1_g", "ln1_b", "b_ff1", "b_ff2", "ln2_g", "ln2_b")
_WEIGHTS = ("w_in", "b_in", "w_pool", "pool_scale", "sgu_ln_g", "sgu_ln_b", "sgu_w", "sgu_b", "conv_w", "conv_b", "conv_ln_g",
            "conv_ln_b", "w_out", "b_out", "ln1_g", "ln1_b", "w_ff1", "b_ff1", "w_ff2", "b_ff2", "ln2_g", "ln2_b")


def _pack(arrays):
    parts = []
    for a in arrays:
        rows = a.reshape(-1, 128)
        parts.append(jnp.pad(rows, ((0, -rows.shape[0] % 8), (0, 0))))
    return jnp.concatenate(parts, axis=0)


def _unpack(flat, like):
    out, at = [], 0
    for a in like:
        n = a.size // 128
        out.append(flat[at : at + n].reshape(a.shape))
        at += n + (-n % 8)
    return out


def _packed_rows(arrays):
    return sum(a.size // 128 + (-(a.size // 128) % 8) for a in arrays)


def kernel(x, w_in, b_in, w_pool, pool_scale, sgu_ln_g, sgu_ln_b, sgu_w, sgu_b, conv_w, conv_b, conv_ln_g, conv_ln_b, w_out, b_out, ln1_g, ln1_b, w_ff1, b_ff1, w_ff2, b_ff2, ln2_g, ln2_b, loss_target, m_w_in, m_b_in, m_w_pool, m_pool_scale, m_sgu_ln_g, m_sgu_ln_b, m_sgu_w, m_sgu_b, m_conv_w, m_conv_b, m_conv_ln_g, m_conv_ln_b, m_w_out, m_b_out, m_ln1_g, m_ln1_b, m_w_ff1, m_b_ff1, m_w_ff2, m_b_ff2, m_ln2_g, m_ln2_b, v_w_in, v_b_in, v_w_pool, v_pool_scale, v_sgu_ln_g, v_sgu_ln_b, v_sgu_w, v_sgu_b, v_conv_w, v_conv_b, v_conv_ln_g, v_conv_ln_b, v_w_out, v_b_out, v_ln1_g, v_ln1_b, v_w_ff1, v_b_ff1, v_w_ff2, v_b_ff2, v_ln2_g, v_ln2_b):
    w = dict(w_in=w_in, b_in=b_in, w_pool=w_pool, pool_scale=pool_scale, sgu_ln_g=sgu_ln_g, sgu_ln_b=sgu_ln_b, sgu_w=sgu_w,
             sgu_b=sgu_b, conv_w=conv_w, conv_b=conv_b, conv_ln_g=conv_ln_g, conv_ln_b=conv_ln_b, w_out=w_out, b_out=b_out,
             ln1_g=ln1_g, ln1_b=ln1_b, w_ff1=w_ff1, b_ff1=b_ff1, w_ff2=w_ff2, b_ff2=b_ff2, ln2_g=ln2_g, ln2_b=ln2_b)
    mom = dict(w_in=m_w_in, b_in=m_b_in, w_pool=m_w_pool, pool_scale=m_pool_scale, sgu_ln_g=m_sgu_ln_g, sgu_ln_b=m_sgu_ln_b,
               sgu_w=m_sgu_w, sgu_b=m_sgu_b, conv_w=m_conv_w, conv_b=m_conv_b, conv_ln_g=m_conv_ln_g, conv_ln_b=m_conv_ln_b,
               w_out=m_w_out, b_out=m_b_out, ln1_g=m_ln1_g, ln1_b=m_ln1_b, w_ff1=m_w_ff1, b_ff1=m_b_ff1, w_ff2=m_w_ff2,
               b_ff2=m_b_ff2, ln2_g=m_ln2_g, ln2_b=m_ln2_b)
    var = dict(w_in=v_w_in, b_in=v_b_in, w_pool=v_w_pool, pool_scale=v_pool_scale, sgu_ln_g=v_sgu_ln_g, sgu_ln_b=v_sgu_ln_b,
               sgu_w=v_sgu_w, sgu_b=v_sgu_b, conv_w=v_conv_w, conv_b=v_conv_b, conv_ln_g=v_conv_ln_g, conv_ln_b=v_conv_ln_b,
               w_out=v_w_out, b_out=v_b_out, ln1_g=v_ln1_g, ln1_b=v_ln1_b, w_ff1=v_w_ff1, b_ff1=v_b_ff1, w_ff2=v_w_ff2,
               b_ff2=v_b_ff2, ln2_g=v_ln2_g, ln2_b=v_ln2_b)
    T = x.shape[1]
    x0 = x.reshape(T, D_MODEL)
    target = loss_target.reshape(T, D_MODEL)
    me_lin = _lin(_place())

    shard = [
        {name: (w[name][l].T if name in _TRANSPOSED else w[name][l]).astype(BF16) for name in _BIG} for l in range(DEPTH)
    ]
    conv_shard = jnp.pad(conv_w, ((0, 0), (0, 1), (0, 128 - conv_w.shape[2]))).reshape(DEPTH * 32, 128)

    def rows_of(g):
        return g.reshape(N_DEV * g.shape[1], g.shape[2])

    first = _all_gather([shard[0]["w_in"], shard[0]["w_out"], conv_shard])
    full = [{} for _ in range(DEPTH)]
    full[0]["w_in"], full[0]["w_out"] = rows_of(first[0]), rows_of(first[1])
    conv_cols = conv_w.shape[2]
    conv_full = first[2].reshape(N_DEV, DEPTH, 32, 128)[:, :, :CONV_KERNEL, :conv_cols]
    conv_full = conv_full.transpose(1, 2, 0, 3).reshape(DEPTH, CONV_KERNEL, N_DEV * conv_cols)

    tril = jnp.tril(jnp.ones((GROUP, GROUP), F32))
    prm = []
    for l in range(DEPTH):
        wm = sgu_w[l] * tril
        prm.append(dict(
            wp=w_pool[l].reshape(POOL_WIDTH, GROUP).astype(BF16), ps=_row(pool_scale[l]), lg=_row(sgu_ln_g[l]), lb=_row(sgu_ln_b[l]),
            wm=wm.reshape(SGU_WIDTH, GROUP).astype(BF16), wmt=wm.transpose(0, 2, 1).reshape(SGU_WIDTH, GROUP).astype(BF16),
            bsf=jnp.repeat(sgu_b[l].T, GROUP, axis=1), cw=jnp.pad(conv_full[l], ((0, 1), (0, 0))), cb=_row(conv_b[l]),
            cg=_row(conv_ln_g[l]), cbeta=_row(conv_ln_b[l]),
        ))

    saved = []
    res = (x0, jnp.ones((D_MODEL,), F32), jnp.zeros((D_MODEL,), F32))
    xbf = x0.astype(BF16)
    n_ff = shard[0]["w_ff1"].shape[0]
    thirds = [(0, n_ff // 48 * 16), (n_ff // 48 * 16, n_ff // 48 * 16), (n_ff // 48 * 32, n_ff - n_ff // 48 * 32)]

    def ff1_part(l, k, into):
        return _Exchange([("gather", shard[l]["w_ff1"], thirds[k], into)])

    for l in range(DEPTH):
        f = full[l]
        ex = ff1_part(l, 0, None)
        proj = _mm_bias(f"proj{l}", xbf, f["w_in"], "nt", b_in[l], 1024, 896, 2048, side=ex)
        ex = ff1_part(l, 1, ex.results[0])
        mixed = _mixer_fwd(f"mixer_fwd{l}", proj, prm[l], 256, side=ex)
        ex = ff1_part(l, 2, ex.results[0])
        xh1, rs1, x1bf = _mm_ln(f"out_ln1_{l}", mixed, f["w_out"], b_out[l], res, ln1_g[l], ln1_b[l], 512, 1024, side=ex)
        f["w_ff1"] = rows_of(ex.results[0])
        ex = _Exchange([("gather", shard[l]["w_ff2"])])
        act, hsq = _mm_relu2(f"ff1_{l}", x1bf, f["w_ff1"], "nt", b_ff1[l], 1024, 1024, 2048, side=ex)
        f["w_ff2"] = rows_of(ex.results[0])
        ex = _Exchange([("gather", shard[l + 1]["w_in"]), ("gather", shard[l + 1]["w_out"])]) if l + 1 < DEPTH else None
        xh2, rs2, x2bf = _mm_ln(
            f"ff2_ln2_{l}", hsq, f["w_ff2"], b_ff2[l], (xh1, ln1_g[l], ln1_b[l]), ln2_g[l], ln2_b[l], 512, 1024, side=ex)
        if ex is not None:
            full[l + 1]["w_in"], full[l + 1]["w_out"] = rows_of(ex.results[0]), rows_of(ex.results[1])
        saved.append(dict(xin=xbf, proj=proj, mixed=mixed, xh1=xh1, rs1=rs1, x1bf=x1bf, act=act, hsq=hsq, xh2=xh2, rs2=rs2))
        res = (xh2, ln2_g[l], ln2_b[l])
        xbf = x2bf

    top = saved[-1]
    dr2, dr2bf, g_ln2g, g_ln2b, g_bff2, loss_row = _loss_top(top["xh2"], top["rs2"], ln2_g[-1], ln2_b[-1], target, 256)
    loss = lax.psum(loss_row[0, 0], ("x", "y", "c"))
    slots = [{} for _ in range(DEPTH)]
    gsm = [{} for _ in range(DEPTH)]
    grad_x = small_slots = None

    def stacked_small():
        st = {name: jnp.stack([gsm[gl][name].reshape(w[name].shape[1:]) for gl in range(DEPTH)]) for name in _SMALL}
        conv_g = jnp.pad(jnp.stack([gsm[gl]["conv_w"] for gl in range(DEPTH)]), ((0, 0), (0, 1), (0, 0)))
        return [st[name] for name in _SMALL] + [conv_g]

    for l in reversed(range(DEPTH)):
        f, sv = full[l], saved[l]
        gsm[l].update(ln2_g=g_ln2g, ln2_b=g_ln2b, b_ff2=g_bff2)
        gw = _mm_wgrad(f"gw_ff2_{l}", sv["hsq"], dr2bf, 512, 2048)
        ex = _Exchange([("slices", gw)])
        dhpre, g_bff1 = _mm_dh(f"dff1_{l}", dr2bf, f["w_ff2"], sv["act"], 1024, 1024, 2048, side=ex)
        slots[l]["w_ff2"] = ex.results[0]
        gsm[l]["b_ff1"] = g_bff1
        gw = _mm_wgrad(f"gw_ff1_{l}", dhpre, sv["x1bf"], 512, 2048)
        ex = _Exchange([("slices", gw)])
        dr1, dr1bf, g_ln1g, g_ln1b, g_bout = _mm_ln_bwd(
            f"dx1_ln1_{l}", dhpre, f["w_ff1"], dr2, sv["xh1"], sv["rs1"], ln1_g[l], 512, 1024, side=ex)
        slots[l]["w_ff1"] = ex.results[0]
        gsm[l].update(ln1_g=g_ln1g, ln1_b=g_ln1b, b_out=g_bout)
        gw = _mm_wgrad(f"gw_out_{l}", sv["mixed"], dr1bf, 512, 2048)
        ex = _Exchange([("slices", gw)])
        dmix = _mm_plain(f"dmixed{l}", dr1bf, f["w_out"], "nt", 1024, 1024, 2048, side=ex)
        slots[l]["w_out"] = ex.results[0]
        (dproj, g_wp, g_ps, g_lg, g_lb, g_wm, g_bs, g_cw, g_cb, g_cg, g_cbeta, g_bin) = _mixer_bwd(
            f"mixer_bwd{l}", sv["proj"], dmix, prm[l], 256)
        gsm[l].update(b_in=g_bin, w_pool=g_wp, pool_scale=g_ps, sgu_ln_g=g_lg, sgu_ln_b=g_lb, sgu_w=g_wm, sgu_b=g_bs[:SGU_HEADS],
                      conv_w=g_cw[:CONV_KERNEL], conv_b=g_cb, conv_ln_g=g_cg, conv_ln_b=g_cbeta)
        if l > 0:
            gw = _mm_wgrad(f"gw_in_{l}", dproj, sv["xin"], 512, 2048)
            below = saved[l - 1]
            ex = _Exchange([("slices", gw)])
            dr2, dr2bf, g_ln2g, g_ln2b, g_bff2 = _mm_ln_bwd(
                f"dx_ln2_{l}", dproj, f["w_in"], dr1, below["xh2"], below["rs2"], ln2_g[l - 1], 512, 896, side=ex)
        else:
            small_like = stacked_small()
            ex = _Exchange([("gather", _pack(small_like))])
            gw = _mm_wgrad(f"gw_in_{l}", dproj, sv["xin"], 512, 2048, side=ex)
            small_slots = ex.results[0]
            ex = _Exchange([("slices", gw)])
            grad_x = _mm_plain("dx0", dproj, f["w_in"], "nn", 512, 2048, 896, res=dr1, side=ex)
        slots[l]["w_in"] = ex.results[0]

    grads, deltas, new_m, new_v = {}, {}, {}, {}
    for name in _BIG:
        per_layer = []
        for l in range(DEPTH):
            g = _sum_slots(f"sum_{name}_{l}", slots[l][name])
            per_layer.append(g.T if name in _TRANSPOSED else g)
        g = jnp.stack(per_layer)
        shape = w[name].shape
        two_d = (shape[0] * shape[1], shape[2])
        d, nm, nv = _adamw(f"adamw_{name}", w[name].reshape(two_d), g.reshape(two_d), mom[name].reshape(two_d), var[name].reshape(two_d))
        grads[name], deltas[name], new_m[name], new_v[name] = g, d.reshape(shape), nm.reshape(shape), nv.reshape(shape)

    total = _sum_slots("sum_small", small_slots)
    small_g = _unpack(total, small_like)
    like = [w[name] for name in _SMALL]
    d, nm, nv = _adamw("adamw_small", _pack(like), total[: _packed_rows(like)],
                       _pack([mom[name] for name in _SMALL]), _pack([var[name] for name in _SMALL]))
    for name, gg, dd, mm_, vv in zip(_SMALL, small_g, _unpack(d, like), _unpack(nm, like), _unpack(nv, like)):
        grads[name], deltas[name], new_m[name], new_v[name] = gg, dd, mm_, vv
    conv_g = lax.dynamic_slice_in_dim(small_g[-1][:, :CONV_KERNEL, :], me_lin * conv_cols, conv_cols, axis=2)
    flat = (DEPTH * CONV_KERNEL, conv_cols)
    d, nm, nv = _adamw("adamw_conv_w", conv_w.reshape(flat), conv_g.reshape(flat), m_conv_w.reshape(flat), v_conv_w.reshape(flat))
    grads["conv_w"], deltas["conv_w"], new_m["conv_w"], new_v["conv_w"] = conv_g, d.reshape(conv_w.shape), nm.reshape(conv_w.shape), nv.reshape(conv_w.shape)

    return (loss, grad_x.reshape(x.shape), *[grads[n] for n in _WEIGHTS], *[deltas[n] for n in _WEIGHTS],
            *[new_m[n] for n in _WEIGHTS], *[new_v[n] for n in _WEIGHTS])
```

```python
import functools

import jax
import jax.numpy as jnp
from jax import lax
from jax.experimental import pallas as pl
from jax.experimental.pallas import tpu as pltpu

F32, BF16 = jnp.float32, jnp.bfloat16
S = jax.ShapeDtypeStruct

DEPTH = 2
D_MODEL = 2048
POOL_WINDOWS = (2, 4, 8, 16)
POOL_WIDTH = 512
GROUP = 128
SGU_WIDTH = 768
SGU_HEADS = 6
CONV_WIDTH = 768
CONV_KERNEL = 31
IN_WIDTH = 3584
D_FF = 8192
ALPHA = (2 * DEPTH) ** 0.25
LN_EPS = 1e-5
ADAM_LR, ADAM_B1, ADAM_B2, ADAM_EPS, ADAM_WD, ADAM_STEP = 0.001, 0.9, 0.999, 1e-08, 0.01, 10

N_DEV = 8
HALO = 32
VMEM_LIMIT = 56 << 20
MESH = pl.DeviceIdType.MESH

C_POOL = (0, 512)
C_U = (512, 1280)
C_V = (1280, 2048)
C_CA = (2048, 2816)
C_CG = (2816, 3584)
M_POOL = (0, 512)
M_SGU = (512, 1280)
M_CONV = (1280, 2048)


def _cparams(n_axes):
    return pltpu.CompilerParams(dimension_semantics=("arbitrary",) * n_axes, vmem_limit_bytes=VMEM_LIMIT)


def _for_strips(rows, strip, fn):
    n = rows // strip
    if n == 1:
        fn(0)
        return

    def step(s, carry):
        fn(pl.multiple_of(s * strip, strip))
        return carry

    lax.fori_loop(0, n, step, 0)


def _row_sum(x):
    return jnp.sum(x, axis=0, keepdims=True)


def _ln_stats(r):
    mu = jnp.mean(r, axis=-1, keepdims=True)
    xc = r - mu
    var = jnp.mean(xc * xc, axis=-1, keepdims=True)
    rs = lax.rsqrt(var + LN_EPS)
    return xc * rs, rs


def _ln_bwd(dy, xhat, rs, g):
    gy = dy * g
    m1 = jnp.mean(gy, axis=-1, keepdims=True)
    m2 = jnp.mean(gy * xhat, axis=-1, keepdims=True)
    return rs * (gy - m1 - xhat * m2)


_GELU_C = 0.7978845608028654


def _gelu(x):
    th = jnp.tanh(_GELU_C * (x + 0.044715 * (x * x * x)))
    return 0.5 * x * (1.0 + th), th


def _gelu_grad(x, th):
    return 0.5 * (1.0 + th) + 0.5 * x * (1.0 - th * th) * (_GELU_C * (1.0 + 3.0 * 0.044715 * (x * x)))


def _place():
    x, y, c = lax.axis_index("x"), lax.axis_index("y"), lax.axis_index("c")
    return x, y, c


def _lin(p):
    return 4 * p[0] + 2 * p[1] + p[2]


def _flip(p, r):
    return tuple(1 - v if (r >> (2 - ax)) & 1 else v for ax, v in enumerate(p))


_ANY = pl.BlockSpec(memory_space=pl.ANY)


class _Exchange:
    def __init__(self, items):
        self.kinds = [item[0] for item in items]
        self.srcs = [item[1] for item in items]
        self.rows = [item[2] if len(item) > 2 else None for item in items]
        handed_on = [item[3] if len(item) > 3 else None for item in items]
        self.out_shape = [
            S((N_DEV, *x.shape), x.dtype) if kind == "gather" else S((N_DEV, x.shape[0] // N_DEV, x.shape[1]), x.dtype)
            for kind, x in zip(self.kinds, self.srcs)
        ]
        n = len(items)
        self.ins = self.srcs + [b for b in handed_on if b is not None]
        self.aliases = {}
        for a, b in enumerate(handed_on):
            if b is not None:
                self.aliases[n + len(self.aliases)] = a
        self.scratch = [pltpu.SemaphoreType.DMA((n, 7)), pltpu.SemaphoreType.DMA((n, 7)), pltpu.SemaphoreType.DMA((n,))]
        self.results = None

    def _src(self, in_refs, a, dest):
        if self.kinds[a] == "gather":
            return in_refs[a] if self.rows[a] is None else in_refs[a].at[pl.ds(*self.rows[a])]
        rows = self.srcs[a].shape[0] // N_DEV
        return in_refs[a].at[pl.ds(pl.multiple_of(_lin(dest) * rows, 8), rows)]

    def _dst(self, out_refs, a, slot):
        return out_refs[a].at[slot] if self.rows[a] is None else out_refs[a].at[slot, pl.ds(*self.rows[a])]

    def _copies(self, in_refs, out_refs, sems, with_arrivals):
        send_sems, recv_sems, local_sems = sems
        me = _place()
        local, sends, arrivals = [], [], []
        for a in range(len(self.srcs)):
            local.append(pltpu.make_async_copy(self._src(in_refs, a, me), self._dst(out_refs, a, _lin(me)), local_sems.at[a]))
            for r in range(1, N_DEV):
                peer = _flip(me, r)
                for slot, group in ((_lin(me), sends), (_lin(peer), arrivals)):
                    if group is sends or with_arrivals:
                        group.append(pltpu.make_async_remote_copy(
                            src_ref=self._src(in_refs, a, peer), dst_ref=self._dst(out_refs, a, slot),
                            send_sem=send_sems.at[a, r - 1], recv_sem=recv_sems.at[a, r - 1], device_id=peer, device_id_type=MESH,
                        ))
        return local, sends, arrivals

    def start(self, in_refs, out_refs, sems):
        local, sends, _ = self._copies(in_refs, out_refs, sems, False)
        for cp in local + sends:
            cp.start()

    def wait(self, in_refs, out_refs, sems):
        local, sends, arrivals = self._copies(in_refs, out_refs, sems, True)
        for cp in arrivals:
            cp.wait_recv()
        for cp in sends:
            cp.wait_send()
        for cp in local:
            cp.wait()


def _call(name, body, grid, in_specs, out_specs, out_shape, scratch, args, side=None):
    in_specs, out_specs, out_shape, scratch = list(in_specs), list(out_specs), list(out_shape), list(scratch)
    if side is None:
        return pl.pallas_call(
            body, name=name, grid=grid, in_specs=in_specs, out_specs=out_specs, out_shape=out_shape, scratch_shapes=scratch,
            compiler_params=_cparams(len(grid)),
        )(*args)
    n_in, n_out, n_scr = len(in_specs), len(out_specs), len(scratch)
    s_in, s_out = len(side.ins), len(side.out_shape)

    def wrapped(*refs):
        at = 0
        parts = []
        for n in (n_in, s_in, n_out, s_out, n_scr, 3):
            parts.append(refs[at : at + n])
            at += n
        ins, side_ins, outs, side_outs, scr, sems = parts
        pids = [pl.program_id(d) for d in range(len(grid))]
        first = functools.reduce(jnp.logical_and, [p == 0 for p in pids])
        last = functools.reduce(jnp.logical_and, [p == g - 1 for p, g in zip(pids, grid)])

        @pl.when(first)
        def _():
            side.start(side_ins, side_outs, sems)

        body(*ins, *outs, *scr)

        @pl.when(last)
        def _():
            side.wait(side_ins, side_outs, sems)

    res = pl.pallas_call(
        wrapped, name=name, grid=grid, in_specs=in_specs + [_ANY] * s_in, out_specs=out_specs + [_ANY] * s_out,
        out_shape=out_shape + side.out_shape, scratch_shapes=scratch + side.scratch, compiler_params=_cparams(len(grid)),
        input_output_aliases={n_in + i: n_out + o for i, o in side.aliases.items()},
    )(*args, *side.ins)
    side.results = list(res[n_out:])
    return list(res[:n_out])


_CONTRACT = {"nn": ((1,), (0,)), "nt": ((1,), (1,)), "tn": ((0,), (0,))}


def _mm(name, a, b, dims, tm, tn, tk, *, ins=(), outs, epilogue, j_outer=False, side=None):
    if dims == "tn":
        K, M = a.shape
    else:
        M, K = a.shape
    N = b.shape[0] if dims == "nt" else b.shape[1]
    tm, tn, tk = min(tm, M), min(tn, N), min(tk, K)
    assert M % tm == 0 and N % tn == 0 and K % tk == 0, (name, M, N, K, tm, tn, tk)
    nm, nn, nk = M // tm, N // tn, K // tk
    if j_outer:
        grid = (nn, nm, nk)
        ij = lambda g0, g1: (g1, g0)
    else:
        grid = (nm, nn, nk)
        ij = lambda g0, g1: (g0, g1)

    def amap(g0, g1, k):
        i, _ = ij(g0, g1)
        return (k, i) if dims == "tn" else (i, k)

    def bmap(g0, g1, k):
        _, j = ij(g0, g1)
        return (j, k) if dims == "nt" else (k, j)

    def spec(kind):
        if kind == "tile":
            return pl.BlockSpec((tm, tn), lambda g0, g1, k: ij(g0, g1))
        if kind == "row":
            return pl.BlockSpec((1, tn), lambda g0, g1, k: (0, ij(g0, g1)[1]))
        assert kind == "col", kind
        return pl.BlockSpec((tm, 1), lambda g0, g1, k: (ij(g0, g1)[0], 0))

    in_specs = [
        pl.BlockSpec((tk, tm) if dims == "tn" else (tm, tk), amap),
        pl.BlockSpec((tn, tk) if dims == "nt" else (tk, tn), bmap),
    ] + [spec(kind) for _, kind in ins]
    out_specs = [spec(kind) for _, _, kind in outs]
    out_shape = [S(shape, dtype) for shape, dtype, _ in outs]
    n_in, n_out = len(ins), len(outs)
    contract = (_CONTRACT[dims], ((), ()))

    def body(*refs):
        a_ref, b_ref = refs[:2]
        in_refs = refs[2 : 2 + n_in]
        out_refs = refs[2 + n_in : 2 + n_in + n_out]
        acc = refs[2 + n_in + n_out]
        i, _ = ij(pl.program_id(0), pl.program_id(1))
        k = pl.program_id(2)

        def part():
            return lax.dot_general(a_ref[...], b_ref[...], contract, preferred_element_type=F32)

        @pl.when(k == 0)
        def _():
            acc[...] = part()

        @pl.when(k > 0)
        def _():
            acc[...] += part()

        @pl.when(k == nk - 1)
        def _():
            epilogue(i, acc, in_refs, out_refs)

    return _call(name, body, grid, in_specs, out_specs, out_shape, [pltpu.VMEM((tm, tn), F32)], [a, b, *[x for x, _ in ins]], side)


def _row(v):
    return v.reshape(1, -1)


def _mm_bias(name, a, b, dims, bias, tm, tn, tk, side=None):
    M = a.shape[0]
    N = b.shape[0] if dims == "nt" else b.shape[1]

    def epilogue(i, acc, ins, outs):
        def strip(r0):
            rows = pl.ds(r0, 128)
            outs[0][rows, :] = acc[rows, :] + ins[0][...]

        _for_strips(acc.shape[0], 128, strip)

    return _mm(name, a, b, dims, tm, tn, tk, ins=[(_row(bias), "row")], outs=[((M, N), F32, "tile")], epilogue=epilogue, side=side)[0]


def _mm_relu2(name, a, b, dims, bias, tm, tn, tk, side=None):
    M = a.shape[0]
    N = b.shape[0] if dims == "nt" else b.shape[1]

    def epilogue(i, acc, ins, outs):
        def strip(r0):
            rows = pl.ds(r0, 128)
            r = jnp.maximum(acc[rows, :] + ins[0][...], 0.0)
            outs[0][rows, :] = r.astype(BF16)
            outs[1][rows, :] = (r * r).astype(BF16)

        _for_strips(acc.shape[0], 128, strip)

    return _mm(
        name, a, b, dims, tm, tn, tk, ins=[(_row(bias), "row")],
        outs=[((M, N), BF16, "tile"), ((M, N), BF16, "tile")], epilogue=epilogue, side=side,
    )


def _mm_ln(name, a, b, bias, res, g, beta, tm, tk, side=None):
    M = a.shape[0]
    N = b.shape[1]
    rxh, rg, rb = res

    def epilogue(i, acc, ins, outs):
        bias_r, rxh_r, rg_r, rb_r, g_r, beta_r = ins
        xhat_o, rstd_o, xbf_o = outs

        def strip(r0):
            rows = pl.ds(r0, 64)
            resid = rxh_r[rows, :] * rg_r[...] + rb_r[...]
            r = ALPHA * resid + (acc[rows, :] + bias_r[...])
            xhat, rs = _ln_stats(r)
            xhat_o[rows, :] = xhat
            rstd_o[rows, :] = rs
            xbf_o[rows, :] = (xhat * g_r[...] + beta_r[...]).astype(BF16)

        _for_strips(acc.shape[0], 64, strip)

    return _mm(
        name, a, b, "nn", tm, N, tk,
        ins=[(_row(bias), "row"), (rxh, "tile"), (_row(rg), "row"), (_row(rb), "row"), (_row(g), "row"), (_row(beta), "row")],
        outs=[((M, N), F32, "tile"), ((M, 1), F32, "col"), ((M, N), BF16, "tile")],
        epilogue=epilogue, side=side,
    )


def _ln_bwd_strip(dyv, xhat, rs, g, dr_o, drbf_o, dg_o, db_o, dsum_o, rows):
    dr = _ln_bwd(dyv, xhat, rs, g)
    dr_o[rows, :] = dr
    drbf_o[rows, :] = dr.astype(BF16)
    dg_o[...] += _row_sum(dyv * xhat)
    db_o[...] += _row_sum(dyv)
    dsum_o[...] += _row_sum(dr)


def _mm_ln_bwd(name, a, b, resgrad, xhat, rstd, g, tm, tk, side=None):
    M = a.shape[0]
    N = b.shape[1]

    def epilogue(i, acc, ins, outs):
        rg_r, xh_r, rs_r, g_r = ins
        dr_o, drbf_o, dg_o, db_o, dsum_o = outs

        @pl.when(i == 0)
        def _():
            dg_o[...] = jnp.zeros_like(dg_o)
            db_o[...] = jnp.zeros_like(db_o)
            dsum_o[...] = jnp.zeros_like(dsum_o)

        def strip(r0):
            rows = pl.ds(r0, 64)
            dyv = acc[rows, :] + ALPHA * rg_r[rows, :]
            _ln_bwd_strip(dyv, xh_r[rows, :], rs_r[rows, :], g_r[...], dr_o, drbf_o, dg_o, db_o, dsum_o, rows)

        _for_strips(acc.shape[0], 64, strip)

    return _mm(
        name, a, b, "nn", tm, N, tk,
        ins=[(resgrad, "tile"), (xhat, "tile"), (rstd, "col"), (_row(g), "row")],
        outs=[((M, N), F32, "tile"), ((M, N), BF16, "tile"), ((1, N), F32, "row"), ((1, N), F32, "row"), ((1, N), F32, "row")],
        epilogue=epilogue, side=side,
    )


def _mm_dh(name, a, b, act, tm, tn, tk, side=None):
    M = a.shape[0]
    N = b.shape[0]

    def epilogue(i, acc, ins, outs):
        @pl.when(i == 0)
        def _():
            outs[1][...] = jnp.zeros_like(outs[1])

        def strip(r0):
            rows = pl.ds(r0, 128)
            d = acc[rows, :] * (2.0 * ins[0][rows, :].astype(F32))
            outs[0][rows, :] = d.astype(BF16)
            outs[1][...] += _row_sum(d)

        _for_strips(acc.shape[0], 128, strip)

    return _mm(
        name, a, b, "nt", tm, tn, tk, ins=[(act, "tile")],
        outs=[((M, N), BF16, "tile"), ((1, N), F32, "row")], epilogue=epilogue, j_outer=True, side=side,
    )


def _mm_plain(name, a, b, dims, tm, tn, tk, res=None, side=None):
    M = a.shape[0]
    N = b.shape[0] if dims == "nt" else b.shape[1]

    def epilogue(i, acc, ins, outs):
        def strip(r0):
            rows = pl.ds(r0, 128)
            v = acc[rows, :]
            if res is not None:
                v = v + ALPHA * ins[0][rows, :]
            outs[0][rows, :] = v

        _for_strips(acc.shape[0], 128, strip)

    return _mm(
        name, a, b, dims, tm, tn, tk, ins=[] if res is None else [(res, "tile")],
        outs=[((M, N), F32, "tile")], epilogue=epilogue, side=side,
    )[0]


def _mm_wgrad(name, a, b, tm, tk, side=None):
    M = a.shape[1]
    N = b.shape[1]

    def epilogue(i, acc, ins, outs):
        def strip(r0):
            rows = pl.ds(r0, 128)
            outs[0][rows, :] = acc[rows, :].astype(BF16)

        _for_strips(acc.shape[0], 128, strip)

    return _mm(name, a, b, "tn", tm, N, tk, outs=[((M, N), BF16, "tile")], epilogue=epilogue, side=side)[0]


def _loss_top(xhat, rstd, g, beta, target, tm):
    T, D = xhat.shape
    tm = min(tm, T)
    nt = T // tm

    def body(xh_r, rs_r, g_r, b_r, t_r, dr_o, drbf_o, dg_o, db_o, dsum_o, loss_o, sq_acc):
        i = pl.program_id(0)

        @pl.when(i == 0)
        def _():
            dg_o[...] = jnp.zeros_like(dg_o)
            db_o[...] = jnp.zeros_like(db_o)
            dsum_o[...] = jnp.zeros_like(dsum_o)
            sq_acc[...] = jnp.zeros_like(sq_acc)

        def strip(r0):
            rows = pl.ds(r0, 64)
            xh = xh_r[rows, :]
            err = (xh * g_r[...] + b_r[...]) - t_r[rows, :]
            sq_acc[...] += _row_sum(err * err)
            _ln_bwd_strip(err * (1.0 / D), xh, rs_r[rows, :], g_r[...], dr_o, drbf_o, dg_o, db_o, dsum_o, rows)

        _for_strips(tm, 64, strip)

        @pl.when(i == nt - 1)
        def _():
            total = jnp.sum(sq_acc[...], axis=-1, keepdims=True) * (0.5 / D)
            loss_o[...] = jnp.broadcast_to(total, loss_o.shape)

    tile = pl.BlockSpec((tm, D), lambda i: (i, 0))
    row = pl.BlockSpec((1, D), lambda i: (0, 0))
    return pl.pallas_call(
        body,
        name="loss_top",
        grid=(nt,),
        in_specs=[tile, pl.BlockSpec((tm, 1), lambda i: (i, 0)), row, row, tile],
        out_specs=[tile, tile, row, row, row, pl.BlockSpec((1, 128), lambda i: (0, 0))],
        out_shape=[S((T, D), F32), S((T, D), BF16), S((1, D), F32), S((1, D), F32), S((1, D), F32), S((1, 128), F32)],
        scratch_shapes=[pltpu.VMEM((1, D), F32)],
        compiler_params=_cparams(1),
    )(xhat, rstd, _row(g), _row(beta), target)


def _cols(ref, c):
    return ref[:, c[0] : c[1]]


def _causal_window_sum(e, w):
    s, sh = e, 1
    while sh < w:
        s = s + pltpu.roll(s, sh, axis=0)
        sh *= 2
    return s


def _anticausal_window_sum(d, w):
    n = d.shape[0]
    r, sh = d, 1
    while sh < w:
        r = r + pltpu.roll(r, n - sh, axis=0)
        sh *= 2
    return r


def _with_halo(halo_ref, main_ref, c, keep):
    return jnp.concatenate([_cols(halo_ref, c) * keep, _cols(main_ref, c)], axis=0)


def _pool_counts(tile_index, R, w):
    pos = lax.broadcasted_iota(jnp.int32, (R, 1), 0) + tile_index * R
    return jnp.minimum(pos + 1, w).astype(F32)


def _sgu_mix(wm_ref, vnb):
    return jnp.concatenate(
        [
            jnp.dot(wm_ref[h * GROUP : (h + 1) * GROUP, :], vnb[:, h * GROUP : (h + 1) * GROUP], preferred_element_type=F32)
            for h in range(SGU_HEADS)
        ],
        axis=1,
    )


CONV_HALVES = (slice(0, CONV_WIDTH // 2), slice(CONV_WIDTH // 2, CONV_WIDTH))
TAP_STRIP = 32


def _build_shifts(shf, src, cols, rows):
    n = rows - 8
    for r in range(1, 8):
        shf[r - 1, pl.ds(0, n), :] = src[pl.ds(r, n), cols]


def _shifted(shf, src, cols, offset, start, size):
    q, r = divmod(offset, 8)
    rows = pl.ds(pl.multiple_of(start + 8 * q, 8), size)
    return src[rows, cols] if r == 0 else shf[r - 1, rows, :]


def _conv_taps(shf, src, cw8, cols, offsets, n_rows, out, bias=None):
    width = cols.stop - cols.start

    def strip(s, carry):
        r0 = pl.multiple_of(s * TAP_STRIP, TAP_STRIP)
        acc = jnp.zeros((TAP_STRIP, width), F32)
        for k, o in enumerate(offsets):
            wk = cw8[pl.ds(8 * k, 8), cols]
            acc = acc + _shifted(shf, src, cols, o, r0, TAP_STRIP) * jnp.concatenate([wk] * (TAP_STRIP // 8), axis=0)
        if bias is not None:
            acc = acc + bias[:, cols]
        out[pl.ds(r0, TAP_STRIP), cols] = acc
        return carry

    lax.fori_loop(0, n_rows // TAP_STRIP, strip, 0)


def _conv_weight_grad(shf, src, dsrc, d_first, cols, offsets, n_rows, dcw):
    width = cols.stop - cols.start
    for k, o in enumerate(offsets):

        def strip(s, acc8, o=o):
            r0 = pl.multiple_of(s * TAP_STRIP, TAP_STRIP)
            p = _shifted(shf, src, cols, o, r0, TAP_STRIP) * dsrc[pl.ds(pl.multiple_of(d_first + r0, 8), TAP_STRIP), cols]
            for j in range(TAP_STRIP // 8):
                acc8 = acc8 + p[8 * j : 8 * j + 8, :]
            return acc8

        acc8 = lax.fori_loop(0, n_rows // TAP_STRIP, strip, jnp.zeros((8, width), F32))
        dcw[pl.ds(k, 1), cols] += _row_sum(acc8)


def _mixer_params(p):
    return [p["wp"], p["ps"], p["lg"], p["lb"], p["wm"], p["wmt"], p["bsf"], p["cw8"], p["cb"], p["cg"], p["cbeta"]]


def _whole(x):
    return pl.BlockSpec(x.shape, lambda i: (0,) * x.ndim)


def _mixer_fwd(name, proj, p, R, side=None):
    T = proj.shape[0]
    R = min(R, T)
    E = R + HALO
    nt = T // R
    hb = R // HALO
    tap_offsets = [HALO - (CONV_KERNEL - 1) + k for k in range(CONV_KERNEL)]

    def body(pm, ph, wp, ps, lg, lb, wm, wmt, bsf, cw8, cb, cg, cbeta, out, hbuf, shf, convbuf):
        i = pl.program_id(0)
        keep = (i > 0).astype(F32)
        a_ext = _with_halo(ph, pm, C_POOL, keep)
        for gi, w in enumerate(POOL_WINDOWS):
            cs = slice(gi * GROUP, (gi + 1) * GROUP)
            e = a_ext[:, cs]
            s = _causal_window_sum(e, w)
            pooled = s[HALO:, :] / _pool_counts(i, R, w) - e[HALO:, :]
            z = jnp.dot(pooled.astype(BF16), wp[cs, :], preferred_element_type=F32)
            out[:, cs] = (z * ps[:, cs]).astype(BF16)
        u, _ = _gelu(_cols(pm, C_U))
        v, _ = _gelu(_cols(pm, C_V))
        vhat, _ = _ln_stats(v)
        vn = vhat * lg[...] + lb[...]
        for c in range(R // GROUP):
            rs = slice(c * GROUP, (c + 1) * GROUP)
            mixed = _sgu_mix(wm, vn[rs, :].astype(BF16)) + bsf[...]
            out[rs, M_SGU[0] : M_SGU[1]] = (u[rs, :] * mixed).astype(BF16)
        hbuf[...] = _with_halo(ph, pm, C_CA, keep) * jax.nn.sigmoid(_with_halo(ph, pm, C_CG, keep))
        for cols in CONV_HALVES:
            _build_shifts(shf, hbuf, cols, E)
            _conv_taps(shf, hbuf, cw8, cols, tap_offsets, R, convbuf, bias=cb)
        chat, _ = _ln_stats(convbuf[...])
        cn = chat * cg[...] + cbeta[...]
        out[:, M_CONV[0] : M_CONV[1]] = (cn * jax.nn.sigmoid(cn)).astype(BF16)

    params = _mixer_params(p)
    in_specs = [
        pl.BlockSpec((R, IN_WIDTH), lambda i: (i, 0)),
        pl.BlockSpec((HALO, IN_WIDTH), lambda i: (jnp.maximum(i * hb - 1, 0), 0)),
    ] + [_whole(x) for x in params]
    scratch = [pltpu.VMEM((E, CONV_WIDTH), F32), pltpu.VMEM((7, E, CONV_WIDTH // 2), F32), pltpu.VMEM((R, CONV_WIDTH), F32)]
    return _call(
        name, body, (nt,), in_specs, [pl.BlockSpec((R, D_MODEL), lambda i: (i, 0))], [S((T, D_MODEL), BF16)],
        scratch, [proj, proj, *params], side,
    )[0]


def _mixer_bwd(name, proj, dmix, p, R):
    T = proj.shape[0]
    R = min(R, T)
    E = R + HALO
    nt = T // R
    hb = R // HALO
    tap_offsets = [HALO - (CONV_KERNEL - 1) + k for k in range(CONV_KERNEL)]
    back_offsets = [HALO - o for o in tap_offsets]

    def body(pm, ph, dm, wp, ps, lg, lb, wm, wmt, bsf, cw8, cb, cg, cbeta,
             dproj, dwp, dps, dlg, dlb, dwm, dbs, dcw, dcb, dcg, dcbeta, dbin,
             hbuf, dbuf, carry_p, carry_c, dbs_acc, shf, convbuf, dhcbuf):
        step = pl.program_id(0)
        ti = nt - 1 - step
        keep = (ti > 0).astype(F32)

        @pl.when(step == 0)
        def _():
            for r in (dwp, dps, dlg, dlb, dwm, dcw, dcb, dcg, dcbeta, dbin, carry_p, carry_c, dbs_acc):
                r[...] = jnp.zeros_like(r)

        def tail(carry):
            return jnp.concatenate([jnp.zeros((R - HALO, carry.shape[1]), F32), carry], axis=0)

        def head(x):
            return jnp.concatenate([jnp.zeros((HALO, x.shape[1]), F32), x], axis=0)

        a_ext = _with_halo(ph, pm, C_POOL, keep)
        carry_in = carry_p[...]
        for gi, w in enumerate(POOL_WINDOWS):
            cs = slice(gi * GROUP, (gi + 1) * GROUP)
            e = a_ext[:, cs]
            s = _causal_window_sum(e, w)
            cnt = _pool_counts(ti, R, w)
            pooled_b = (s[HALO:, :] / cnt - e[HALO:, :]).astype(BF16)
            wg = wp[cs, :]
            z = jnp.dot(pooled_b, wg, preferred_element_type=F32)
            dya = dm[:, cs]
            dps[:, cs] += _row_sum(dya * z)
            dz_b = (dya * ps[:, cs]).astype(BF16)
            dwp[cs, :] += lax.dot_general(pooled_b, dz_b, (((0,), (0,)), ((), ())), preferred_element_type=F32)
            dpooled = lax.dot_general(dz_b, wg, (((1,), (1,)), ((), ())), preferred_element_type=F32)
            da_ext = _anticausal_window_sum(head(dpooled / cnt), w) - head(dpooled)
            carry_p[:, cs] = da_ext[:HALO, :]
            d_a = da_ext[HALO:, :] + tail(carry_in[:, cs])
            dbin[:, cs] += _row_sum(d_a)
            dproj[:, cs] = d_a.astype(BF16)

        pu = _cols(pm, C_U)
        pv = _cols(pm, C_V)
        u, thu = _gelu(pu)
        v, thv = _gelu(pv)
        vhat, vrs = _ln_stats(v)
        vn = vhat * lg[...] + lb[...]
        dyb = dm[:, M_SGU[0] : M_SGU[1]]
        du_parts, dvn_parts = [], []
        for c in range(R // GROUP):
            rs = slice(c * GROUP, (c + 1) * GROUP)
            vnb = vn[rs, :].astype(BF16)
            mixed = _sgu_mix(wm, vnb) + bsf[...]
            du_parts.append(dyb[rs, :] * mixed)
            dmixed = dyb[rs, :] * u[rs, :]
            dbs_acc[...] += dmixed
            dmb = dmixed.astype(BF16)
            dvn_h = []
            for h in range(SGU_HEADS):
                hs = slice(h * GROUP, (h + 1) * GROUP)
                dwm[hs, :] += lax.dot_general(dmb[:, hs], vnb[:, hs], (((1,), (1,)), ((), ())), preferred_element_type=F32)
                dvn_h.append(jnp.dot(wmt[hs, :], dmb[:, hs], preferred_element_type=F32))
            dvn_parts.append(jnp.concatenate(dvn_h, axis=1))
        du = jnp.concatenate(du_parts, axis=0) if len(du_parts) > 1 else du_parts[0]
        dvn = jnp.concatenate(dvn_parts, axis=0) if len(dvn_parts) > 1 else dvn_parts[0]
        dlg[...] += _row_sum(dvn * vhat)
        dlb[...] += _row_sum(dvn)
        d_pu = du * _gelu_grad(pu, thu)
        d_pv = _ln_bwd(dvn, vhat, vrs, lg[...]) * _gelu_grad(pv, thv)
        dbin[:, C_U[0] : C_U[1]] += _row_sum(d_pu)
        dbin[:, C_V[0] : C_V[1]] += _row_sum(d_pv)
        dproj[:, C_U[0] : C_U[1]] = d_pu.astype(BF16)
        dproj[:, C_V[0] : C_V[1]] = d_pv.astype(BF16)

        sg_ext = jax.nn.sigmoid(_with_halo(ph, pm, C_CG, keep))
        ca_ext = _with_halo(ph, pm, C_CA, keep)
        hbuf[...] = ca_ext * sg_ext
        for cols in CONV_HALVES:
            _build_shifts(shf, hbuf, cols, E)
            _conv_taps(shf, hbuf, cw8, cols, tap_offsets, R, convbuf, bias=cb)
        chat, crs = _ln_stats(convbuf[...])
        cn = chat * cg[...] + cbeta[...]
        sc = jax.nn.sigmoid(cn)
        dcn = dm[:, M_CONV[0] : M_CONV[1]] * (sc * (1.0 + cn * (1.0 - sc)))
        dcg[...] += _row_sum(dcn * chat)
        dcbeta[...] += _row_sum(dcn)
        dconv = _ln_bwd(dcn, chat, crs, cg[...])
        dcb[...] += _row_sum(dconv)
        dbuf[pl.ds(0, HALO), :] = jnp.zeros((HALO, CONV_WIDTH), F32)
        dbuf[pl.ds(HALO, R), :] = dconv
        dbuf[pl.ds(HALO + R, HALO), :] = jnp.zeros((HALO, CONV_WIDTH), F32)
        for cols in CONV_HALVES:
            _build_shifts(shf, hbuf, cols, E)
            _conv_weight_grad(shf, hbuf, dbuf, HALO, cols, tap_offsets, R, dcw)
            _build_shifts(shf, dbuf, cols, E + HALO)
            _conv_taps(shf, dbuf, cw8, cols, back_offsets, E, dhcbuf)
        dhc_main = dhcbuf[pl.ds(HALO, R), :] + tail(carry_c[...])
        carry_c[...] = dhcbuf[pl.ds(0, HALO), :]
        sg = sg_ext[HALO:, :]
        d_ca = dhc_main * sg
        d_cg = dhc_main * ca_ext[HALO:, :] * (sg * (1.0 - sg))
        dbin[:, C_CA[0] : C_CA[1]] += _row_sum(d_ca)
        dbin[:, C_CG[0] : C_CG[1]] += _row_sum(d_cg)
        dproj[:, C_CA[0] : C_CA[1]] = d_ca.astype(BF16)
        dproj[:, C_CG[0] : C_CG[1]] = d_cg.astype(BF16)

        @pl.when(step == nt - 1)
        def _():
            row = lax.broadcasted_iota(jnp.int32, (GROUP, GROUP), 0)
            col = lax.broadcasted_iota(jnp.int32, (GROUP, GROUP), 1)
            dbs[...] = jnp.zeros_like(dbs)
            for h in range(SGU_HEADS):
                hs = slice(h * GROUP, (h + 1) * GROUP)
                dwm[hs, :] = jnp.where(row >= col, dwm[hs, :], 0.0)
                dbs[pl.ds(h, 1), :] = _row_sum(dbs_acc[:, hs].T)

    params = _mixer_params(p)
    accs = [
        S((POOL_WIDTH, GROUP), F32), S((1, POOL_WIDTH), F32), S((1, SGU_WIDTH), F32), S((1, SGU_WIDTH), F32),
        S((SGU_WIDTH, GROUP), F32), S((8, GROUP), F32), S((32, CONV_WIDTH), F32), S((1, CONV_WIDTH), F32),
        S((1, CONV_WIDTH), F32), S((1, CONV_WIDTH), F32), S((1, IN_WIDTH), F32),
    ]
    return pl.pallas_call(
        body,
        name=name,
        grid=(nt,),
        in_specs=[
            pl.BlockSpec((R, IN_WIDTH), lambda i: (nt - 1 - i, 0)),
            pl.BlockSpec((HALO, IN_WIDTH), lambda i: (jnp.maximum((nt - 1 - i) * hb - 1, 0), 0)),
            pl.BlockSpec((R, D_MODEL), lambda i: (nt - 1 - i, 0)),
        ]
        + [_whole(x) for x in params],
        out_specs=[pl.BlockSpec((R, IN_WIDTH), lambda i: (nt - 1 - i, 0))] + [_whole(x) for x in accs],
        out_shape=[S((T, IN_WIDTH), BF16)] + accs,
        scratch_shapes=[
            pltpu.VMEM((E, CONV_WIDTH), F32), pltpu.VMEM((E + HALO, CONV_WIDTH), F32),
            pltpu.VMEM((HALO, POOL_WIDTH), F32), pltpu.VMEM((HALO, CONV_WIDTH), F32), pltpu.VMEM((GROUP, SGU_WIDTH), F32),
            pltpu.VMEM((7, E + HALO, CONV_WIDTH // 2), F32), pltpu.VMEM((R, CONV_WIDTH), F32), pltpu.VMEM((E, CONV_WIDTH), F32),
        ],
        compiler_params=_cparams(1),
    )(proj, proj, dmix, *params)


def _all_gather(xs):
    n = len(xs)

    def body(*refs):
        x_refs, o_refs = refs[:n], refs[n : 2 * n]
        send_sems, recv_sems, local_sems = refs[2 * n :]
        x, y, c = _place()
        me, sibling = (x, y, c), (x, y, 1 - c)
        chips = [(1 - x, y), (x, 1 - y), (1 - x, 1 - y)]

        def copy(a, k, block, to, src=None):
            dst = o_refs[a].at[_lin(block)]
            return pltpu.make_async_remote_copy(
                src_ref=dst if src is None else src, dst_ref=dst, send_sem=send_sems.at[a, k], recv_sem=recv_sems.at[a, k],
                device_id=to, device_id_type=MESH,
            )

        mine = [pltpu.make_async_copy(x_refs[a], o_refs[a].at[_lin(me)], local_sems.at[a]) for a in range(n)]
        for m in mine:
            m.start()
        first = []
        for a in range(n):
            first.append(copy(a, 0, me, sibling, src=x_refs[a]))
            first += [copy(a, 1 + j, me, (*chip, c), src=x_refs[a]) for j, chip in enumerate(chips)]
        for cp in first:
            cp.start()
        passed = []
        for a in range(n):
            for j, chip in enumerate(chips):
                copy(a, 1 + j, (*chip, c), me).wait_recv()
                fwd = copy(a, 4 + j, (*chip, c), sibling)
                fwd.start()
                passed.append(fwd)
        for a in range(n):
            copy(a, 0, sibling, me).wait_recv()
            for j, chip in enumerate(chips):
                copy(a, 4 + j, (*chip, 1 - c), me).wait_recv()
        for cp in first + passed:
            cp.wait_send()
        for m in mine:
            m.wait()

    return pl.pallas_call(
        body,
        name="all_gather_weights",
        in_specs=[_ANY] * n,
        out_specs=[_ANY] * n,
        out_shape=[S((N_DEV, *x.shape), x.dtype) for x in xs],
        scratch_shapes=[pltpu.SemaphoreType.DMA((n, 7)), pltpu.SemaphoreType.DMA((n, 7)), pltpu.SemaphoreType.DMA((n,))],
    )(*xs)


def _row_tile(rows, want):
    return next(t for t in range(min(rows, want) // 8 * 8, 0, -8) if rows % t == 0)


def _sum_slots(name, slots):
    _, rows, cols = slots.shape
    tr = _row_tile(rows, (4 << 20) // (N_DEV * cols * slots.dtype.itemsize))

    def body(s_ref, o_ref):
        total = s_ref[0].astype(F32)
        for d in range(1, N_DEV):
            total = total + s_ref[d].astype(F32)
        o_ref[...] = total

    return pl.pallas_call(
        body,
        name=name,
        grid=(rows // tr,),
        in_specs=[pl.BlockSpec((N_DEV, tr, cols), lambda i: (0, i, 0))],
        out_specs=pl.BlockSpec((tr, cols), lambda i: (i, 0)),
        out_shape=S((rows, cols), F32),
        compiler_params=_cparams(1),
    )(slots)


def _adamw(name, w, g, m, v):
    rows, cols = w.shape
    tr = rows if rows * cols * 4 <= (2 << 20) else _row_tile(rows, 1 << ((1 << 18) // cols).bit_length() - 1)

    def body(w_ref, g_ref, m_ref, v_ref, d_ref, nm_ref, nv_ref):
        gv = g_ref[...]
        nm = ADAM_B1 * m_ref[...] + (1.0 - ADAM_B1) * gv
        nv = ADAM_B2 * v_ref[...] + (1.0 - ADAM_B2) * (gv * gv)
        m_hat = nm / (1.0 - ADAM_B1**ADAM_STEP)
        v_hat = nv / (1.0 - ADAM_B2**ADAM_STEP)
        d_ref[...] = -ADAM_LR * (m_hat / (jnp.sqrt(v_hat) + ADAM_EPS) + ADAM_WD * w_ref[...])
        nm_ref[...] = nm
        nv_ref[...] = nv

    blk = pl.BlockSpec((tr, cols), lambda i: (i, 0))
    return pl.pallas_call(
        body,
        name=name,
        grid=(rows // tr,),
        in_specs=[blk] * 4,
        out_specs=[blk] * 3,
        out_shape=[S((rows, cols), F32)] * 3,
        compiler_params=_cparams(1),
    )(w, g, m, v)


_BIG = ("w_in", "w_out", "w_ff1", "w_ff2")
_TRANSPOSED = ("w_in", "w_ff1")
_SMALL = ("b_in", "w_pool", "pool_scale", "sgu_ln_g", "sgu_ln_b", "sgu_w", "sgu_b", "conv_b", "conv_ln_g", "conv_ln_b",
          "b_out", "ln1_g", "ln1_b", "b_ff1", "b_ff2", "ln2_g", "ln2_b")
_WEIGHTS = ("w_in", "b_in", "w_pool", "pool_scale", "sgu_ln_g", "sgu_ln_b", "sgu_w", "sgu_b", "conv_w", "conv_b", "conv_ln_g",
            "conv_ln_b", "w_out", "b_out", "ln1_g", "ln1_b", "w_ff1", "b_ff1", "w_ff2", "b_ff2", "ln2_g", "ln2_b")


def _pack(arrays):
    parts = []
    for a in arrays:
        rows = a.reshape(-1, 128)
        parts.append(jnp.pad(rows, ((0, -rows.shape[0] % 8), (0, 0))))
    return jnp.concatenate(parts, axis=0)


def _unpack(flat, like):
    out, at = [], 0
    for a in like:
        n = a.size // 128
        out.append(flat[at : at + n].reshape(a.shape))
        at += n + (-n % 8)
    return out


def _packed_rows(arrays):
    return sum(a.size // 128 + (-(a.size // 128) % 8) for a in arrays)


def kernel(x, w_in, b_in, w_pool, pool_scale, sgu_ln_g, sgu_ln_b, sgu_w, sgu_b, conv_w, conv_b, conv_ln_g, conv_ln_b, w_out, b_out, ln1_g, ln1_b, w_ff1, b_ff1, w_ff2, b_ff2, ln2_g, ln2_b, loss_target, m_w_in, m_b_in, m_w_pool, m_pool_scale, m_sgu_ln_g, m_sgu_ln_b, m_sgu_w, m_sgu_b, m_conv_w, m_conv_b, m_conv_ln_g, m_conv_ln_b, m_w_out, m_b_out, m_ln1_g, m_ln1_b, m_w_ff1, m_b_ff1, m_w_ff2, m_b_ff2, m_ln2_g, m_ln2_b, v_w_in, v_b_in, v_w_pool, v_pool_scale, v_sgu_ln_g, v_sgu_ln_b, v_sgu_w, v_sgu_b, v_conv_w, v_conv_b, v_conv_ln_g, v_conv_ln_b, v_w_out, v_b_out, v_ln1_g, v_ln1_b, v_w_ff1, v_b_ff1, v_w_ff2, v_b_ff2, v_ln2_g, v_ln2_b):
    w = dict(w_in=w_in, b_in=b_in, w_pool=w_pool, pool_scale=pool_scale, sgu_ln_g=sgu_ln_g, sgu_ln_b=sgu_ln_b, sgu_w=sgu_w,
             sgu_b=sgu_b, conv_w=conv_w, conv_b=conv_b, conv_ln_g=conv_ln_g, conv_ln_b=conv_ln_b, w_out=w_out, b_out=b_out,
             ln1_g=ln1_g, ln1_b=ln1_b, w_ff1=w_ff1, b_ff1=b_ff1, w_ff2=w_ff2, b_ff2=b_ff2, ln2_g=ln2_g, ln2_b=ln2_b)
    mom = dict(w_in=m_w_in, b_in=m_b_in, w_pool=m_w_pool, pool_scale=m_pool_scale, sgu_ln_g=m_sgu_ln_g, sgu_ln_b=m_sgu_ln_b,
               sgu_w=m_sgu_w, sgu_b=m_sgu_b, conv_w=m_conv_w, conv_b=m_conv_b, conv_ln_g=m_conv_ln_g, conv_ln_b=m_conv_ln_b,
               w_out=m_w_out, b_out=m_b_out, ln1_g=m_ln1_g, ln1_b=m_ln1_b, w_ff1=m_w_ff1, b_ff1=m_b_ff1, w_ff2=m_w_ff2,
               b_ff2=m_b_ff2, ln2_g=m_ln2_g, ln2_b=m_ln2_b)
    var = dict(w_in=v_w_in, b_in=v_b_in, w_pool=v_w_pool, pool_scale=v_pool_scale, sgu_ln_g=v_sgu_ln_g, sgu_ln_b=v_sgu_ln_b,
               sgu_w=v_sgu_w, sgu_b=v_sgu_b, conv_w=v_conv_w, conv_b=v_conv_b, conv_ln_g=v_conv_ln_g, conv_ln_b=v_conv_ln_b,
               w_out=v_w_out, b_out=v_b_out, ln1_g=v_ln1_g, ln1_b=v_ln1_b, w_ff1=v_w_ff1, b_ff1=v_b_ff1, w_ff2=v_w_ff2,
               b_ff2=v_b_ff2, ln2_g=v_ln2_g, ln2_b=v_ln2_b)
    T = x.shape[1]
    x0 = x.reshape(T, D_MODEL)
    target = loss_target.reshape(T, D_MODEL)
    me_lin = _lin(_place())

    shard = [
        {name: (w[name][l].T if name in _TRANSPOSED else w[name][l]).astype(BF16) for name in _BIG} for l in range(DEPTH)
    ]
    conv_shard = jnp.pad(conv_w, ((0, 0), (0, 1), (0, 128 - conv_w.shape[2]))).reshape(DEPTH * 32, 128)

    def rows_of(g):
        return g.reshape(N_DEV * g.shape[1], g.shape[2])

    first = _all_gather([shard[0]["w_in"], shard[0]["w_out"], conv_shard])
    full = [{} for _ in range(DEPTH)]
    full[0]["w_in"], full[0]["w_out"] = rows_of(first[0]), rows_of(first[1])
    conv_cols = conv_w.shape[2]
    conv_full = first[2].reshape(N_DEV, DEPTH, 32, 128)[:, :, :CONV_KERNEL, :conv_cols]
    conv_full = conv_full.transpose(1, 2, 0, 3).reshape(DEPTH, CONV_KERNEL, N_DEV * conv_cols)

    tril = jnp.tril(jnp.ones((GROUP, GROUP), F32))
    prm = []
    for l in range(DEPTH):
        wm = sgu_w[l] * tril
        prm.append(dict(
            wp=w_pool[l].reshape(POOL_WIDTH, GROUP).astype(BF16), ps=_row(pool_scale[l]), lg=_row(sgu_ln_g[l]), lb=_row(sgu_ln_b[l]),
            wm=wm.reshape(SGU_WIDTH, GROUP).astype(BF16), wmt=wm.transpose(0, 2, 1).reshape(SGU_WIDTH, GROUP).astype(BF16),
            bsf=jnp.repeat(sgu_b[l].T, GROUP, axis=1), cw8=jnp.repeat(jnp.pad(conv_full[l], ((0, 1), (0, 0))), 8, axis=0), cb=_row(conv_b[l]),
            cg=_row(conv_ln_g[l]), cbeta=_row(conv_ln_b[l]),
        ))

    saved = []
    res = (x0, jnp.ones((D_MODEL,), F32), jnp.zeros((D_MODEL,), F32))
    xbf = x0.astype(BF16)
    n_ff = shard[0]["w_ff1"].shape[0]
    thirds = [(0, n_ff // 48 * 16), (n_ff // 48 * 16, n_ff // 48 * 16), (n_ff // 48 * 32, n_ff - n_ff // 48 * 32)]

    def ff1_part(l, k, into):
        return _Exchange([("gather", shard[l]["w_ff1"], thirds[k], into)])

    for l in range(DEPTH):
        f = full[l]
        ex = ff1_part(l, 0, None)
        proj = _mm_bias(f"proj{l}", xbf, f["w_in"], "nt", b_in[l], 1024, 896, 2048, side=ex)
        ex = ff1_part(l, 1, ex.results[0])
        mixed = _mixer_fwd(f"mixer_fwd{l}", proj, prm[l], 256, side=ex)
        ex = ff1_part(l, 2, ex.results[0])
        xh1, rs1, x1bf = _mm_ln(f"out_ln1_{l}", mixed, f["w_out"], b_out[l], res, ln1_g[l], ln1_b[l], 512, 1024, side=ex)
        f["w_ff1"] = rows_of(ex.results[0])
        ex = _Exchange([("gather", shard[l]["w_ff2"])])
        act, hsq = _mm_relu2(f"ff1_{l}", x1bf, f["w_ff1"], "nt", b_ff1[l], 1024, 1024, 2048, side=ex)
        f["w_ff2"] = rows_of(ex.results[0])
        ex = _Exchange([("gather", shard[l + 1]["w_in"]), ("gather", shard[l + 1]["w_out"])]) if l + 1 < DEPTH else None
        xh2, rs2, x2bf = _mm_ln(
            f"ff2_ln2_{l}", hsq, f["w_ff2"], b_ff2[l], (xh1, ln1_g[l], ln1_b[l]), ln2_g[l], ln2_b[l], 512, 1024, side=ex)
        if ex is not None:
            full[l + 1]["w_in"], full[l + 1]["w_out"] = rows_of(ex.results[0]), rows_of(ex.results[1])
        saved.append(dict(xin=xbf, proj=proj, mixed=mixed, xh1=xh1, rs1=rs1, x1bf=x1bf, act=act, hsq=hsq, xh2=xh2, rs2=rs2))
        res = (xh2, ln2_g[l], ln2_b[l])
        xbf = x2bf

    top = saved[-1]
    dr2, dr2bf, g_ln2g, g_ln2b, g_bff2, loss_row = _loss_top(top["xh2"], top["rs2"], ln2_g[-1], ln2_b[-1], target, 256)
    loss = lax.psum(loss_row[0, 0], ("x", "y", "c"))
    slots = [{} for _ in range(DEPTH)]
    gsm = [{} for _ in range(DEPTH)]
    grad_x = small_slots = None

    def stacked_small():
        st = {name: jnp.stack([gsm[gl][name].reshape(w[name].shape[1:]) for gl in range(DEPTH)]) for name in _SMALL}
        conv_g = jnp.pad(jnp.stack([gsm[gl]["conv_w"] for gl in range(DEPTH)]), ((0, 0), (0, 1), (0, 0)))
        return [st[name] for name in _SMALL] + [conv_g]

    for l in reversed(range(DEPTH)):
        f, sv = full[l], saved[l]
        gsm[l].update(ln2_g=g_ln2g, ln2_b=g_ln2b, b_ff2=g_bff2)
        gw = _mm_wgrad(f"gw_ff2_{l}", sv["hsq"], dr2bf, 512, 2048)
        ex = _Exchange([("slices", gw)])
        dhpre, g_bff1 = _mm_dh(f"dff1_{l}", dr2bf, f["w_ff2"], sv["act"], 1024, 1024, 2048, side=ex)
        slots[l]["w_ff2"] = ex.results[0]
        gsm[l]["b_ff1"] = g_bff1
        gw = _mm_wgrad(f"gw_ff1_{l}", dhpre, sv["x1bf"], 512, 2048)
        ex = _Exchange([("slices", gw)])
        dr1, dr1bf, g_ln1g, g_ln1b, g_bout = _mm_ln_bwd(
            f"dx1_ln1_{l}", dhpre, f["w_ff1"], dr2, sv["xh1"], sv["rs1"], ln1_g[l], 512, 1024, side=ex)
        slots[l]["w_ff1"] = ex.results[0]
        gsm[l].update(ln1_g=g_ln1g, ln1_b=g_ln1b, b_out=g_bout)
        gw = _mm_wgrad(f"gw_out_{l}", sv["mixed"], dr1bf, 512, 2048)
        ex = _Exchange([("slices", gw)])
        dmix = _mm_plain(f"dmixed{l}", dr1bf, f["w_out"], "nt", 1024, 1024, 2048, side=ex)
        slots[l]["w_out"] = ex.results[0]
        (dproj, g_wp, g_ps, g_lg, g_lb, g_wm, g_bs, g_cw, g_cb, g_cg, g_cbeta, g_bin) = _mixer_bwd(
            f"mixer_bwd{l}", sv["proj"], dmix, prm[l], 256)
        gsm[l].update(b_in=g_bin, w_pool=g_wp, pool_scale=g_ps, sgu_ln_g=g_lg, sgu_ln_b=g_lb, sgu_w=g_wm, sgu_b=g_bs[:SGU_HEADS],
                      conv_w=g_cw[:CONV_KERNEL], conv_b=g_cb, conv_ln_g=g_cg, conv_ln_b=g_cbeta)
        if l > 0:
            gw = _mm_wgrad(f"gw_in_{l}", dproj, sv["xin"], 512, 2048)
            below = saved[l - 1]
            ex = _Exchange([("slices", gw)])
            dr2, dr2bf, g_ln2g, g_ln2b, g_bff2 = _mm_ln_bwd(
                f"dx_ln2_{l}", dproj, f["w_in"], dr1, below["xh2"], below["rs2"], ln2_g[l - 1], 512, 896, side=ex)
        else:
            small_like = stacked_small()
            ex = _Exchange([("gather", _pack(small_like))])
            gw = _mm_wgrad(f"gw_in_{l}", dproj, sv["xin"], 512, 2048, side=ex)
            small_slots = ex.results[0]
            ex = _Exchange([("slices", gw)])
            grad_x = _mm_plain("dx0", dproj, f["w_in"], "nn", 512, 2048, 896, res=dr1, side=ex)
        slots[l]["w_in"] = ex.results[0]

    grads, deltas, new_m, new_v = {}, {}, {}, {}
    for name in _BIG:
        per_layer = []
        for l in range(DEPTH):
            g = _sum_slots(f"sum_{name}_{l}", slots[l][name])
            per_layer.append(g.T if name in _TRANSPOSED else g)
        g = jnp.stack(per_layer)
        shape = w[name].shape
        two_d = (shape[0] * shape[1], shape[2])
        d, nm, nv = _adamw(f"adamw_{name}", w[name].reshape(two_d), g.reshape(two_d), mom[name].reshape(two_d), var[name].reshape(two_d))
        grads[name], deltas[name], new_m[name], new_v[name] = g, d.reshape(shape), nm.reshape(shape), nv.reshape(shape)

    total = _sum_slots("sum_small", small_slots)
    small_g = _unpack(total, small_like)
    like = [w[name] for name in _SMALL]
    d, nm, nv = _adamw("adamw_small", _pack(like), total[: _packed_rows(like)],
                       _pack([mom[name] for name in _SMALL]), _pack([var[name] for name in _SMALL]))
    for name, gg, dd, mm_, vv in zip(_SMALL, small_g, _unpack(d, like), _unpack(nm, like), _unpack(nv, like)):
        grads[name], deltas[name], new_m[name], new_v[name] = gg, dd, mm_, vv
    conv_g = lax.dynamic_slice_in_dim(small_g[-1][:, :CONV_KERNEL, :], me_lin * conv_cols, conv_cols, axis=2)
    flat = (DEPTH * CONV_KERNEL, conv_cols)
    d, nm, nv = _adamw("adamw_conv_w", conv_w.reshape(flat), conv_g.reshape(flat), m_conv_w.reshape(flat), v_conv_w.reshape(flat))
    grads["conv_w"], deltas["conv_w"], new_m["conv_w"], new_v["conv_w"] = conv_g, d.reshape(conv_w.shape), nm.reshape(conv_w.shape), nv.reshape(conv_w.shape)

    return (loss, grad_x.reshape(x.shape), *[grads[n] for n in _WEIGHTS], *[deltas[n] for n in _WEIGHTS],
            *[new_m[n] for n in _WEIGHTS], *[new_v[n] for n in _WEIGHTS])
```

```python
import functools

import jax
import jax.numpy as jnp
from jax import lax
from jax.experimental import pallas as pl
from jax.experimental.pallas import tpu as pltpu

F32, BF16 = jnp.float32, jnp.bfloat16
S = jax.ShapeDtypeStruct

DEPTH = 2
D_MODEL = 2048
POOL_WINDOWS = (2, 4, 8, 16)
POOL_WIDTH = 512
GROUP = 128
SGU_WIDTH = 768
SGU_HEADS = 6
CONV_WIDTH = 768
CONV_KERNEL = 31
IN_WIDTH = 3584
D_FF = 8192
ALPHA = (2 * DEPTH) ** 0.25
LN_EPS = 1e-5
ADAM_LR, ADAM_B1, ADAM_B2, ADAM_EPS, ADAM_WD, ADAM_STEP = 0.001, 0.9, 0.999, 1e-08, 0.01, 10

N_DEV = 8
LN_STRIP = 32
LN_UNROLL = 4
HALO = 32
VMEM_LIMIT = 56 << 20
MESH = pl.DeviceIdType.MESH

C_POOL = (0, 512)
C_U = (512, 1280)
C_V = (1280, 2048)
C_CA = (2048, 2816)
C_CG = (2816, 3584)
M_POOL = (0, 512)
M_SGU = (512, 1280)
M_CONV = (1280, 2048)


def _cparams(n_axes):
    return pltpu.CompilerParams(dimension_semantics=("arbitrary",) * n_axes, vmem_limit_bytes=VMEM_LIMIT)


def _for_strips(rows, strip, fn, unroll=1):
    n = rows // strip
    if n == 1:
        fn(0)
        return

    def step(s, carry):
        fn(pl.multiple_of(s * strip, strip))
        return carry

    lax.fori_loop(0, n, step, 0, unroll=unroll)


def _row_sum(x):
    return jnp.sum(x, axis=0, keepdims=True)


def _ln_stats(r):
    mu = jnp.mean(r, axis=-1, keepdims=True)
    xc = r - mu
    var = jnp.mean(xc * xc, axis=-1, keepdims=True)
    rs = lax.rsqrt(var + LN_EPS)
    return xc * rs, rs


def _ln_bwd(dy, xhat, rs, g):
    gy = dy * g
    m1 = jnp.mean(gy, axis=-1, keepdims=True)
    m2 = jnp.mean(gy * xhat, axis=-1, keepdims=True)
    return rs * (gy - m1 - xhat * m2)


_GELU_C = 0.7978845608028654


def _gelu(x):
    th = jnp.tanh(_GELU_C * (x + 0.044715 * (x * x * x)))
    return 0.5 * x * (1.0 + th), th


def _gelu_grad(x, th):
    return 0.5 * (1.0 + th) + 0.5 * x * (1.0 - th * th) * (_GELU_C * (1.0 + 3.0 * 0.044715 * (x * x)))


def _place():
    x, y, c = lax.axis_index("x"), lax.axis_index("y"), lax.axis_index("c")
    return x, y, c


def _lin(p):
    return 4 * p[0] + 2 * p[1] + p[2]


def _flip(p, r):
    return tuple(1 - v if (r >> (2 - ax)) & 1 else v for ax, v in enumerate(p))


_ANY = pl.BlockSpec(memory_space=pl.ANY)


class _Exchange:
    def __init__(self, items):
        self.kinds = [item[0] for item in items]
        self.srcs = [item[1] for item in items]
        self.rows = [item[2] if len(item) > 2 else None for item in items]
        handed_on = [item[3] if len(item) > 3 else None for item in items]
        self.out_shape = [
            S((N_DEV, *x.shape), x.dtype) if kind == "gather" else S((N_DEV, x.shape[0] // N_DEV, x.shape[1]), x.dtype)
            for kind, x in zip(self.kinds, self.srcs)
        ]
        n = len(items)
        self.ins = self.srcs + [b for b in handed_on if b is not None]
        self.aliases = {}
        for a, b in enumerate(handed_on):
            if b is not None:
                self.aliases[n + len(self.aliases)] = a
        self.scratch = [pltpu.SemaphoreType.DMA((n, 7)), pltpu.SemaphoreType.DMA((n, 7)), pltpu.SemaphoreType.DMA((n,))]
        self.results = None

    def _src(self, in_refs, a, dest):
        if self.kinds[a] == "gather":
            return in_refs[a] if self.rows[a] is None else in_refs[a].at[pl.ds(*self.rows[a])]
        rows = self.srcs[a].shape[0] // N_DEV
        return in_refs[a].at[pl.ds(pl.multiple_of(_lin(dest) * rows, 8), rows)]

    def _dst(self, out_refs, a, slot):
        return out_refs[a].at[slot] if self.rows[a] is None else out_refs[a].at[slot, pl.ds(*self.rows[a])]

    def _copies(self, in_refs, out_refs, sems, with_arrivals):
        send_sems, recv_sems, local_sems = sems
        me = _place()
        local, sends, arrivals = [], [], []
        for a in range(len(self.srcs)):
            local.append(pltpu.make_async_copy(self._src(in_refs, a, me), self._dst(out_refs, a, _lin(me)), local_sems.at[a]))
            for r in range(1, N_DEV):
                peer = _flip(me, r)
                for slot, group in ((_lin(me), sends), (_lin(peer), arrivals)):
                    if group is sends or with_arrivals:
                        group.append(pltpu.make_async_remote_copy(
                            src_ref=self._src(in_refs, a, peer), dst_ref=self._dst(out_refs, a, slot),
                            send_sem=send_sems.at[a, r - 1], recv_sem=recv_sems.at[a, r - 1], device_id=peer, device_id_type=MESH,
                        ))
        return local, sends, arrivals

    def start(self, in_refs, out_refs, sems):
        local, sends, _ = self._copies(in_refs, out_refs, sems, False)
        for cp in local + sends:
            cp.start()

    def wait(self, in_refs, out_refs, sems):
        local, sends, arrivals = self._copies(in_refs, out_refs, sems, True)
        for cp in arrivals:
            cp.wait_recv()
        for cp in sends:
            cp.wait_send()
        for cp in local:
            cp.wait()


def _call(name, body, grid, in_specs, out_specs, out_shape, scratch, args, side=None):
    in_specs, out_specs, out_shape, scratch = list(in_specs), list(out_specs), list(out_shape), list(scratch)
    if side is None:
        return pl.pallas_call(
            body, name=name, grid=grid, in_specs=in_specs, out_specs=out_specs, out_shape=out_shape, scratch_shapes=scratch,
            compiler_params=_cparams(len(grid)),
        )(*args)
    n_in, n_out, n_scr = len(in_specs), len(out_specs), len(scratch)
    s_in, s_out = len(side.ins), len(side.out_shape)

    def wrapped(*refs):
        at = 0
        parts = []
        for n in (n_in, s_in, n_out, s_out, n_scr, 3):
            parts.append(refs[at : at + n])
            at += n
        ins, side_ins, outs, side_outs, scr, sems = parts
        pids = [pl.program_id(d) for d in range(len(grid))]
        first = functools.reduce(jnp.logical_and, [p == 0 for p in pids])
        last = functools.reduce(jnp.logical_and, [p == g - 1 for p, g in zip(pids, grid)])

        @pl.when(first)
        def _():
            side.start(side_ins, side_outs, sems)

        body(*ins, *outs, *scr)

        @pl.when(last)
        def _():
            side.wait(side_ins, side_outs, sems)

    res = pl.pallas_call(
        wrapped, name=name, grid=grid, in_specs=in_specs + [_ANY] * s_in, out_specs=out_specs + [_ANY] * s_out,
        out_shape=out_shape + side.out_shape, scratch_shapes=scratch + side.scratch, compiler_params=_cparams(len(grid)),
        input_output_aliases={n_in + i: n_out + o for i, o in side.aliases.items()},
    )(*args, *side.ins)
    side.results = list(res[n_out:])
    return list(res[:n_out])


_CONTRACT = {"nn": ((1,), (0,)), "nt": ((1,), (1,)), "tn": ((0,), (0,))}


def _mm(name, a, b, dims, tm, tn, tk, *, ins=(), outs, epilogue, j_outer=False, side=None):
    if dims == "tn":
        K, M = a.shape
    else:
        M, K = a.shape
    N = b.shape[0] if dims == "nt" else b.shape[1]
    tm, tn, tk = min(tm, M), min(tn, N), min(tk, K)
    assert M % tm == 0 and N % tn == 0 and K % tk == 0, (name, M, N, K, tm, tn, tk)
    nm, nn, nk = M // tm, N // tn, K // tk
    if j_outer:
        grid = (nn, nm, nk)
        ij = lambda g0, g1: (g1, g0)
    else:
        grid = (nm, nn, nk)
        ij = lambda g0, g1: (g0, g1)

    def amap(g0, g1, k):
        i, _ = ij(g0, g1)
        return (k, i) if dims == "tn" else (i, k)

    def bmap(g0, g1, k):
        _, j = ij(g0, g1)
        return (j, k) if dims == "nt" else (k, j)

    def spec(kind):
        if kind == "tile":
            return pl.BlockSpec((tm, tn), lambda g0, g1, k: ij(g0, g1))
        if kind == "row":
            return pl.BlockSpec((1, tn), lambda g0, g1, k: (0, ij(g0, g1)[1]))
        assert kind == "col", kind
        return pl.BlockSpec((tm, 1), lambda g0, g1, k: (ij(g0, g1)[0], 0))

    in_specs = [
        pl.BlockSpec((tk, tm) if dims == "tn" else (tm, tk), amap),
        pl.BlockSpec((tn, tk) if dims == "nt" else (tk, tn), bmap),
    ] + [spec(kind) for _, kind in ins]
    out_specs = [spec(kind) for _, _, kind in outs]
    out_shape = [S(shape, dtype) for shape, dtype, _ in outs]
    n_in, n_out = len(ins), len(outs)
    contract = (_CONTRACT[dims], ((), ()))

    def body(*refs):
        a_ref, b_ref = refs[:2]
        in_refs = refs[2 : 2 + n_in]
        out_refs = refs[2 + n_in : 2 + n_in + n_out]
        acc = refs[2 + n_in + n_out]
        i, _ = ij(pl.program_id(0), pl.program_id(1))
        k = pl.program_id(2)

        def part():
            return lax.dot_general(a_ref[...], b_ref[...], contract, preferred_element_type=F32)

        @pl.when(k == 0)
        def _():
            acc[...] = part()

        @pl.when(k > 0)
        def _():
            acc[...] += part()

        @pl.when(k == nk - 1)
        def _():
            epilogue(i, acc, in_refs, out_refs)

    return _call(name, body, grid, in_specs, out_specs, out_shape, [pltpu.VMEM((tm, tn), F32)], [a, b, *[x for x, _ in ins]], side)


def _row(v):
    return v.reshape(1, -1)


def _mm_bias(name, a, b, dims, bias, tm, tn, tk, side=None):
    M = a.shape[0]
    N = b.shape[0] if dims == "nt" else b.shape[1]

    def epilogue(i, acc, ins, outs):
        def strip(r0):
            rows = pl.ds(r0, 128)
            outs[0][rows, :] = acc[rows, :] + ins[0][...]

        _for_strips(acc.shape[0], 128, strip)

    return _mm(name, a, b, dims, tm, tn, tk, ins=[(_row(bias), "row")], outs=[((M, N), F32, "tile")], epilogue=epilogue, side=side)[0]


def _mm_relu2(name, a, b, dims, bias, tm, tn, tk, side=None):
    M = a.shape[0]
    N = b.shape[0] if dims == "nt" else b.shape[1]

    def epilogue(i, acc, ins, outs):
        def strip(r0):
            rows = pl.ds(r0, 128)
            r = jnp.maximum(acc[rows, :] + ins[0][...], 0.0)
            outs[0][rows, :] = r.astype(BF16)
            outs[1][rows, :] = (r * r).astype(BF16)

        _for_strips(acc.shape[0], 128, strip)

    return _mm(
        name, a, b, dims, tm, tn, tk, ins=[(_row(bias), "row")],
        outs=[((M, N), BF16, "tile"), ((M, N), BF16, "tile")], epilogue=epilogue, side=side,
    )


def _mm_ln(name, a, b, bias, res, g, beta, tm, tk, side=None):
    M = a.shape[0]
    N = b.shape[1]
    rxh, rg, rb = res

    def epilogue(i, acc, ins, outs):
        bias_r, rxh_r, rg_r, rb_r, g_r, beta_r = ins
        xhat_o, rstd_o, xbf_o = outs

        def strip(r0):
            rows = pl.ds(r0, LN_STRIP)
            resid = rxh_r[rows, :] * rg_r[...] + rb_r[...]
            r = ALPHA * resid + (acc[rows, :] + bias_r[...])
            xhat, rs = _ln_stats(r)
            xhat_o[rows, :] = xhat
            rstd_o[rows, :] = rs
            xbf_o[rows, :] = (xhat * g_r[...] + beta_r[...]).astype(BF16)

        _for_strips(acc.shape[0], LN_STRIP, strip, unroll=LN_UNROLL)

    return _mm(
        name, a, b, "nn", tm, N, tk,
        ins=[(_row(bias), "row"), (rxh, "tile"), (_row(rg), "row"), (_row(rb), "row"), (_row(g), "row"), (_row(beta), "row")],
        outs=[((M, N), F32, "tile"), ((M, 1), F32, "col"), ((M, N), BF16, "tile")],
        epilogue=epilogue, side=side,
    )


def _ln_bwd_strip(dyv, xhat, rs, g, dr_o, drbf_o, dg_o, db_o, dsum_o, rows):
    dr = _ln_bwd(dyv, xhat, rs, g)
    dr_o[rows, :] = dr
    drbf_o[rows, :] = dr.astype(BF16)
    dg_o[...] += _row_sum(dyv * xhat)
    db_o[...] += _row_sum(dyv)
    dsum_o[...] += _row_sum(dr)


def _mm_ln_bwd(name, a, b, resgrad, xhat, rstd, g, tm, tk, side=None):
    M = a.shape[0]
    N = b.shape[1]

    def epilogue(i, acc, ins, outs):
        rg_r, xh_r, rs_r, g_r = ins
        dr_o, drbf_o, dg_o, db_o, dsum_o = outs

        @pl.when(i == 0)
        def _():
            dg_o[...] = jnp.zeros_like(dg_o)
            db_o[...] = jnp.zeros_like(db_o)
            dsum_o[...] = jnp.zeros_like(dsum_o)

        def strip(r0):
            rows = pl.ds(r0, LN_STRIP)
            dyv = acc[rows, :] + ALPHA * rg_r[rows, :]
            _ln_bwd_strip(dyv, xh_r[rows, :], rs_r[rows, :], g_r[...], dr_o, drbf_o, dg_o, db_o, dsum_o, rows)

        _for_strips(acc.shape[0], LN_STRIP, strip, unroll=LN_UNROLL)

    return _mm(
        name, a, b, "nn", tm, N, tk,
        ins=[(resgrad, "tile"), (xhat, "tile"), (rstd, "col"), (_row(g), "row")],
        outs=[((M, N), F32, "tile"), ((M, N), BF16, "tile"), ((1, N), F32, "row"), ((1, N), F32, "row"), ((1, N), F32, "row")],
        epilogue=epilogue, side=side,
    )


def _mm_dh(name, a, b, act, tm, tn, tk, side=None):
    M = a.shape[0]
    N = b.shape[0]

    def epilogue(i, acc, ins, outs):
        @pl.when(i == 0)
        def _():
            outs[1][...] = jnp.zeros_like(outs[1])

        def strip(r0):
            rows = pl.ds(r0, 128)
            d = acc[rows, :] * (2.0 * ins[0][rows, :].astype(F32))
            outs[0][rows, :] = d.astype(BF16)
            outs[1][...] += _row_sum(d)

        _for_strips(acc.shape[0], 128, strip)

    return _mm(
        name, a, b, "nt", tm, tn, tk, ins=[(act, "tile")],
        outs=[((M, N), BF16, "tile"), ((1, N), F32, "row")], epilogue=epilogue, j_outer=True, side=side,
    )


def _mm_plain(name, a, b, dims, tm, tn, tk, res=None, side=None):
    M = a.shape[0]
    N = b.shape[0] if dims == "nt" else b.shape[1]

    def epilogue(i, acc, ins, outs):
        def strip(r0):
            rows = pl.ds(r0, 128)
            v = acc[rows, :]
            if res is not None:
                v = v + ALPHA * ins[0][rows, :]
            outs[0][rows, :] = v

        _for_strips(acc.shape[0], 128, strip)

    return _mm(
        name, a, b, dims, tm, tn, tk, ins=[] if res is None else [(res, "tile")],
        outs=[((M, N), F32, "tile")], epilogue=epilogue, side=side,
    )[0]


def _mm_wgrad(name, a, b, tm, tk, side=None):
    M = a.shape[1]
    N = b.shape[1]

    def epilogue(i, acc, ins, outs):
        def strip(r0):
            rows = pl.ds(r0, 128)
            outs[0][rows, :] = acc[rows, :].astype(BF16)

        _for_strips(acc.shape[0], 128, strip)

    return _mm(name, a, b, "tn", tm, N, tk, outs=[((M, N), BF16, "tile")], epilogue=epilogue, side=side)[0]


def _loss_top(xhat, rstd, g, beta, target, tm):
    T, D = xhat.shape
    tm = min(tm, T)
    nt = T // tm

    def body(xh_r, rs_r, g_r, b_r, t_r, dr_o, drbf_o, dg_o, db_o, dsum_o, loss_o, sq_acc):
        i = pl.program_id(0)

        @pl.when(i == 0)
        def _():
            dg_o[...] = jnp.zeros_like(dg_o)
            db_o[...] = jnp.zeros_like(db_o)
            dsum_o[...] = jnp.zeros_like(dsum_o)
            sq_acc[...] = jnp.zeros_like(sq_acc)

        def strip(r0):
            rows = pl.ds(r0, LN_STRIP)
            xh = xh_r[rows, :]
            err = (xh * g_r[...] + b_r[...]) - t_r[rows, :]
            sq_acc[...] += _row_sum(err * err)
            _ln_bwd_strip(err * (1.0 / D), xh, rs_r[rows, :], g_r[...], dr_o, drbf_o, dg_o, db_o, dsum_o, rows)

        _for_strips(tm, LN_STRIP, strip, unroll=LN_UNROLL)

        @pl.when(i == nt - 1)
        def _():
            total = jnp.sum(sq_acc[...], axis=-1, keepdims=True) * (0.5 / D)
            loss_o[...] = jnp.broadcast_to(total, loss_o.shape)

    tile = pl.BlockSpec((tm, D), lambda i: (i, 0))
    row = pl.BlockSpec((1, D), lambda i: (0, 0))
    return pl.pallas_call(
        body,
        name="loss_top",
        grid=(nt,),
        in_specs=[tile, pl.BlockSpec((tm, 1), lambda i: (i, 0)), row, row, tile],
        out_specs=[tile, tile, row, row, row, pl.BlockSpec((1, 128), lambda i: (0, 0))],
        out_shape=[S((T, D), F32), S((T, D), BF16), S((1, D), F32), S((1, D), F32), S((1, D), F32), S((1, 128), F32)],
        scratch_shapes=[pltpu.VMEM((1, D), F32)],
        compiler_params=_cparams(1),
    )(xhat, rstd, _row(g), _row(beta), target)


def _cols(ref, c):
    return ref[:, c[0] : c[1]]


def _causal_window_sum(e, w):
    s, sh = e, 1
    while sh < w:
        s = s + pltpu.roll(s, sh, axis=0)
        sh *= 2
    return s


def _anticausal_window_sum(d, w):
    n = d.shape[0]
    r, sh = d, 1
    while sh < w:
        r = r + pltpu.roll(r, n - sh, axis=0)
        sh *= 2
    return r


def _with_halo(halo_ref, main_ref, c, keep):
    return jnp.concatenate([_cols(halo_ref, c) * keep, _cols(main_ref, c)], axis=0)


def _pool_counts(tile_index, R, w):
    pos = lax.broadcasted_iota(jnp.int32, (R, 1), 0) + tile_index * R
    return jnp.minimum(pos + 1, w).astype(F32)


def _sgu_mix(wm_ref, vnb):
    return jnp.concatenate(
        [
            jnp.dot(wm_ref[h * GROUP : (h + 1) * GROUP, :], vnb[:, h * GROUP : (h + 1) * GROUP], preferred_element_type=F32)
            for h in range(SGU_HEADS)
        ],
        axis=1,
    )


CONV_HALVES = (slice(0, CONV_WIDTH // 2), slice(CONV_WIDTH // 2, CONV_WIDTH))
TAP_STRIP = 32
TAP_GROUP = 4


def _build_shifts(shf, src, cols, rows):
    n = rows - 8
    for r in range(1, 8):
        shf[r - 1, pl.ds(0, n), :] = src[pl.ds(r, n), cols]


def _shifted(shf, src, cols, offset, start, size):
    q, r = divmod(offset, 8)
    rows = pl.ds(pl.multiple_of(start + 8 * q, 8), size)
    return src[rows, cols] if r == 0 else shf[r - 1, rows, :]


def _conv_taps(shf, src, cw8, cols, offsets, n_rows, out, bias=None):
    width = cols.stop - cols.start

    def strip(s, carry):
        r0 = pl.multiple_of(s * TAP_STRIP, TAP_STRIP)
        acc = jnp.zeros((TAP_STRIP, width), F32)
        for k, o in enumerate(offsets):
            wk = cw8[pl.ds(8 * k, 8), cols]
            acc = acc + _shifted(shf, src, cols, o, r0, TAP_STRIP) * jnp.concatenate([wk] * (TAP_STRIP // 8), axis=0)
        if bias is not None:
            acc = acc + bias[:, cols]
        out[pl.ds(r0, TAP_STRIP), cols] = acc
        return carry

    lax.fori_loop(0, n_rows // TAP_STRIP, strip, 0)


def _conv_weight_grad(shf, src, dsrc, d_first, cols, offsets, n_rows, dcw):
    width = cols.stop - cols.start
    for k0 in range(0, len(offsets), TAP_GROUP):
        group = offsets[k0 : k0 + TAP_GROUP]

        def strip(s, accs, group=group):
            r0 = pl.multiple_of(s * TAP_STRIP, TAP_STRIP)
            d = dsrc[pl.ds(pl.multiple_of(d_first + r0, 8), TAP_STRIP), cols]
            out = []
            for acc8, o in zip(accs, group):
                p = _shifted(shf, src, cols, o, r0, TAP_STRIP) * d
                for j in range(TAP_STRIP // 8):
                    acc8 = acc8 + p[8 * j : 8 * j + 8, :]
                out.append(acc8)
            return tuple(out)

        accs = lax.fori_loop(0, n_rows // TAP_STRIP, strip, tuple(jnp.zeros((8, width), F32) for _ in group))
        for j, acc8 in enumerate(accs):
            dcw[pl.ds(k0 + j, 1), cols] += _row_sum(acc8)


def _mixer_params(p):
    return [p["wp"], p["ps"], p["lg"], p["lb"], p["wm"], p["wmt"], p["bsf"], p["cw8"], p["cb"], p["cg"], p["cbeta"]]


def _whole(x):
    return pl.BlockSpec(x.shape, lambda i: (0,) * x.ndim)


def _mixer_fwd(name, proj, p, R, side=None):
    T = proj.shape[0]
    R = min(R, T)
    E = R + HALO
    nt = T // R
    hb = R // HALO
    tap_offsets = [HALO - (CONV_KERNEL - 1) + k for k in range(CONV_KERNEL)]

    def body(pm, ph, wp, ps, lg, lb, wm, wmt, bsf, cw8, cb, cg, cbeta, out, hbuf, shf, convbuf):
        i = pl.program_id(0)
        keep = (i > 0).astype(F32)
        a_ext = _with_halo(ph, pm, C_POOL, keep)
        for gi, w in enumerate(POOL_WINDOWS):
            cs = slice(gi * GROUP, (gi + 1) * GROUP)
            e = a_ext[:, cs]
            s = _causal_window_sum(e, w)
            pooled = s[HALO:, :] / _pool_counts(i, R, w) - e[HALO:, :]
            z = jnp.dot(pooled.astype(BF16), wp[cs, :], preferred_element_type=F32)
            out[:, cs] = (z * ps[:, cs]).astype(BF16)
        u, _ = _gelu(_cols(pm, C_U))
        v, _ = _gelu(_cols(pm, C_V))
        vhat, _ = _ln_stats(v)
        vn = vhat * lg[...] + lb[...]
        for c in range(R // GROUP):
            rs = slice(c * GROUP, (c + 1) * GROUP)
            mixed = _sgu_mix(wm, vn[rs, :].astype(BF16)) + bsf[...]
            out[rs, M_SGU[0] : M_SGU[1]] = (u[rs, :] * mixed).astype(BF16)
        hbuf[...] = _with_halo(ph, pm, C_CA, keep) * jax.nn.sigmoid(_with_halo(ph, pm, C_CG, keep))
        for cols in CONV_HALVES:
            _build_shifts(shf, hbuf, cols, E)
            _conv_taps(shf, hbuf, cw8, cols, tap_offsets, R, convbuf, bias=cb)
        chat, _ = _ln_stats(convbuf[...])
        cn = chat * cg[...] + cbeta[...]
        out[:, M_CONV[0] : M_CONV[1]] = (cn * jax.nn.sigmoid(cn)).astype(BF16)

    params = _mixer_params(p)
    in_specs = [
        pl.BlockSpec((R, IN_WIDTH), lambda i: (i, 0)),
        pl.BlockSpec((HALO, IN_WIDTH), lambda i: (jnp.maximum(i * hb - 1, 0), 0)),
    ] + [_whole(x) for x in params]
    scratch = [pltpu.VMEM((E, CONV_WIDTH), F32), pltpu.VMEM((7, E, CONV_WIDTH // 2), F32), pltpu.VMEM((R, CONV_WIDTH), F32)]
    return _call(
        name, body, (nt,), in_specs, [pl.BlockSpec((R, D_MODEL), lambda i: (i, 0))], [S((T, D_MODEL), BF16)],
        scratch, [proj, proj, *params], side,
    )[0]


def _mixer_bwd(name, proj, dmix, p, R):
    T = proj.shape[0]
    R = min(R, T)
    E = R + HALO
    nt = T // R
    hb = R // HALO
    tap_offsets = [HALO - (CONV_KERNEL - 1) + k for k in range(CONV_KERNEL)]
    back_offsets = [HALO - o for o in tap_offsets]

    def body(pm, ph, dm, wp, ps, lg, lb, wm, wmt, bsf, cw8, cb, cg, cbeta,
             dproj, dwp, dps, dlg, dlb, dwm, dbs, dcw, dcb, dcg, dcbeta, dbin,
             hbuf, dbuf, carry_p, carry_c, dbs_acc, shf, convbuf, dhcbuf):
        step = pl.program_id(0)
        ti = nt - 1 - step
        keep = (ti > 0).astype(F32)

        @pl.when(step == 0)
        def _():
            for r in (dwp, dps, dlg, dlb, dwm, dcw, dcb, dcg, dcbeta, dbin, carry_p, carry_c, dbs_acc):
                r[...] = jnp.zeros_like(r)

        def tail(carry):
            return jnp.concatenate([jnp.zeros((R - HALO, carry.shape[1]), F32), carry], axis=0)

        def head(x):
            return jnp.concatenate([jnp.zeros((HALO, x.shape[1]), F32), x], axis=0)

        a_ext = _with_halo(ph, pm, C_POOL, keep)
        carry_in = carry_p[...]
        for gi, w in enumerate(POOL_WINDOWS):
            cs = slice(gi * GROUP, (gi + 1) * GROUP)
            e = a_ext[:, cs]
            s = _causal_window_sum(e, w)
            cnt = _pool_counts(ti, R, w)
            pooled_b = (s[HALO:, :] / cnt - e[HALO:, :]).astype(BF16)
            wg = wp[cs, :]
            z = jnp.dot(pooled_b, wg, preferred_element_type=F32)
            dya = dm[:, cs]
            dps[:, cs] += _row_sum(dya * z)
            dz_b = (dya * ps[:, cs]).astype(BF16)
            dwp[cs, :] += lax.dot_general(pooled_b, dz_b, (((0,), (0,)), ((), ())), preferred_element_type=F32)
            dpooled = lax.dot_general(dz_b, wg, (((1,), (1,)), ((), ())), preferred_element_type=F32)
            da_ext = _anticausal_window_sum(head(dpooled / cnt), w) - head(dpooled)
            carry_p[:, cs] = da_ext[:HALO, :]
            d_a = da_ext[HALO:, :] + tail(carry_in[:, cs])
            dbin[:, cs] += _row_sum(d_a)
            dproj[:, cs] = d_a.astype(BF16)

        pu = _cols(pm, C_U)
        pv = _cols(pm, C_V)
        u, thu = _gelu(pu)
        v, thv = _gelu(pv)
        vhat, vrs = _ln_stats(v)
        vn = vhat * lg[...] + lb[...]
        dyb = dm[:, M_SGU[0] : M_SGU[1]]
        du_parts, dvn_parts = [], []
        for c in range(R // GROUP):
            rs = slice(c * GROUP, (c + 1) * GROUP)
            vnb = vn[rs, :].astype(BF16)
            mixed = _sgu_mix(wm, vnb) + bsf[...]
            du_parts.append(dyb[rs, :] * mixed)
            dmixed = dyb[rs, :] * u[rs, :]
            dbs_acc[...] += dmixed
            dmb = dmixed.astype(BF16)
            dvn_h = []
            for h in range(SGU_HEADS):
                hs = slice(h * GROUP, (h + 1) * GROUP)
                dwm[hs, :] += lax.dot_general(dmb[:, hs], vnb[:, hs], (((1,), (1,)), ((), ())), preferred_element_type=F32)
                dvn_h.append(jnp.dot(wmt[hs, :], dmb[:, hs], preferred_element_type=F32))
            dvn_parts.append(jnp.concatenate(dvn_h, axis=1))
        du = jnp.concatenate(du_parts, axis=0) if len(du_parts) > 1 else du_parts[0]
        dvn = jnp.concatenate(dvn_parts, axis=0) if len(dvn_parts) > 1 else dvn_parts[0]
        dlg[...] += _row_sum(dvn * vhat)
        dlb[...] += _row_sum(dvn)
        d_pu = du * _gelu_grad(pu, thu)
        d_pv = _ln_bwd(dvn, vhat, vrs, lg[...]) * _gelu_grad(pv, thv)
        dbin[:, C_U[0] : C_U[1]] += _row_sum(d_pu)
        dbin[:, C_V[0] : C_V[1]] += _row_sum(d_pv)
        dproj[:, C_U[0] : C_U[1]] = d_pu.astype(BF16)
        dproj[:, C_V[0] : C_V[1]] = d_pv.astype(BF16)

        sg_ext = jax.nn.sigmoid(_with_halo(ph, pm, C_CG, keep))
        ca_ext = _with_halo(ph, pm, C_CA, keep)
        hbuf[...] = ca_ext * sg_ext
        for cols in CONV_HALVES:
            _build_shifts(shf, hbuf, cols, E)
            _conv_taps(shf, hbuf, cw8, cols, tap_offsets, R, convbuf, bias=cb)
        chat, crs = _ln_stats(convbuf[...])
        cn = chat * cg[...] + cbeta[...]
        sc = jax.nn.sigmoid(cn)
        dcn = dm[:, M_CONV[0] : M_CONV[1]] * (sc * (1.0 + cn * (1.0 - sc)))
        dcg[...] += _row_sum(dcn * chat)
        dcbeta[...] += _row_sum(dcn)
        dconv = _ln_bwd(dcn, chat, crs, cg[...])
        dcb[...] += _row_sum(dconv)
        dbuf[pl.ds(0, HALO), :] = jnp.zeros((HALO, CONV_WIDTH), F32)
        dbuf[pl.ds(HALO, R), :] = dconv
        dbuf[pl.ds(HALO + R, HALO), :] = jnp.zeros((HALO, CONV_WIDTH), F32)
        for cols in CONV_HALVES:
            _build_shifts(shf, hbuf, cols, E)
            _conv_weight_grad(shf, hbuf, dbuf, HALO, cols, tap_offsets, R, dcw)
            _build_shifts(shf, dbuf, cols, E + HALO)
            _conv_taps(shf, dbuf, cw8, cols, back_offsets, E, dhcbuf)
        dhc_main = dhcbuf[pl.ds(HALO, R), :] + tail(carry_c[...])
        carry_c[...] = dhcbuf[pl.ds(0, HALO), :]
        sg = sg_ext[HALO:, :]
        d_ca = dhc_main * sg
        d_cg = dhc_main * ca_ext[HALO:, :] * (sg * (1.0 - sg))
        dbin[:, C_CA[0] : C_CA[1]] += _row_sum(d_ca)
        dbin[:, C_CG[0] : C_CG[1]] += _row_sum(d_cg)
        dproj[:, C_CA[0] : C_CA[1]] = d_ca.astype(BF16)
        dproj[:, C_CG[0] : C_CG[1]] = d_cg.astype(BF16)

        @pl.when(step == nt - 1)
        def _():
            row = lax.broadcasted_iota(jnp.int32, (GROUP, GROUP), 0)
            col = lax.broadcasted_iota(jnp.int32, (GROUP, GROUP), 1)
            dbs[...] = jnp.zeros_like(dbs)
            for h in range(SGU_HEADS):
                hs = slice(h * GROUP, (h + 1) * GROUP)
                dwm[hs, :] = jnp.where(row >= col, dwm[hs, :], 0.0)
                dbs[pl.ds(h, 1), :] = _row_sum(dbs_acc[:, hs].T)

    params = _mixer_params(p)
    accs = [
        S((POOL_WIDTH, GROUP), F32), S((1, POOL_WIDTH), F32), S((1, SGU_WIDTH), F32), S((1, SGU_WIDTH), F32),
        S((SGU_WIDTH, GROUP), F32), S((8, GROUP), F32), S((32, CONV_WIDTH), F32), S((1, CONV_WIDTH), F32),
        S((1, CONV_WIDTH), F32), S((1, CONV_WIDTH), F32), S((1, IN_WIDTH), F32),
    ]
    return pl.pallas_call(
        body,
        name=name,
        grid=(nt,),
        in_specs=[
            pl.BlockSpec((R, IN_WIDTH), lambda i: (nt - 1 - i, 0)),
            pl.BlockSpec((HALO, IN_WIDTH), lambda i: (jnp.maximum((nt - 1 - i) * hb - 1, 0), 0)),
            pl.BlockSpec((R, D_MODEL), lambda i: (nt - 1 - i, 0)),
        ]
        + [_whole(x) for x in params],
        out_specs=[pl.BlockSpec((R, IN_WIDTH), lambda i: (nt - 1 - i, 0))] + [_whole(x) for x in accs],
        out_shape=[S((T, IN_WIDTH), BF16)] + accs,
        scratch_shapes=[
            pltpu.VMEM((E, CONV_WIDTH), F32), pltpu.VMEM((E + HALO, CONV_WIDTH), F32),
            pltpu.VMEM((HALO, POOL_WIDTH), F32), pltpu.VMEM((HALO, CONV_WIDTH), F32), pltpu.VMEM((GROUP, SGU_WIDTH), F32),
            pltpu.VMEM((7, E + HALO, CONV_WIDTH // 2), F32), pltpu.VMEM((R, CONV_WIDTH), F32), pltpu.VMEM((E, CONV_WIDTH), F32),
        ],
        compiler_params=_cparams(1),
    )(proj, proj, dmix, *params)


def _all_gather(xs):
    n = len(xs)

    def body(*refs):
        x_refs, o_refs = refs[:n], refs[n : 2 * n]
        send_sems, recv_sems, local_sems = refs[2 * n :]
        x, y, c = _place()
        me, sibling = (x, y, c), (x, y, 1 - c)
        chips = [(1 - x, y), (x, 1 - y), (1 - x, 1 - y)]

        def copy(a, k, block, to, src=None):
            dst = o_refs[a].at[_lin(block)]
            return pltpu.make_async_remote_copy(
                src_ref=dst if src is None else src, dst_ref=dst, send_sem=send_sems.at[a, k], recv_sem=recv_sems.at[a, k],
                device_id=to, device_id_type=MESH,
            )

        mine = [pltpu.make_async_copy(x_refs[a], o_refs[a].at[_lin(me)], local_sems.at[a]) for a in range(n)]
        for m in mine:
            m.start()
        first = []
        for a in range(n):
            first.append(copy(a, 0, me, sibling, src=x_refs[a]))
            first += [copy(a, 1 + j, me, (*chip, c), src=x_refs[a]) for j, chip in enumerate(chips)]
        for cp in first:
            cp.start()
        passed = []
        for a in range(n):
            for j, chip in enumerate(chips):
                copy(a, 1 + j, (*chip, c), me).wait_recv()
                fwd = copy(a, 4 + j, (*chip, c), sibling)
                fwd.start()
                passed.append(fwd)
        for a in range(n):
            copy(a, 0, sibling, me).wait_recv()
            for j, chip in enumerate(chips):
                copy(a, 4 + j, (*chip, 1 - c), me).wait_recv()
        for cp in first + passed:
            cp.wait_send()
        for m in mine:
            m.wait()

    return pl.pallas_call(
        body,
        name="all_gather_weights",
        in_specs=[_ANY] * n,
        out_specs=[_ANY] * n,
        out_shape=[S((N_DEV, *x.shape), x.dtype) for x in xs],
        scratch_shapes=[pltpu.SemaphoreType.DMA((n, 7)), pltpu.SemaphoreType.DMA((n, 7)), pltpu.SemaphoreType.DMA((n,))],
    )(*xs)


def _row_tile(rows, want):
    return next(t for t in range(min(rows, want) // 8 * 8, 0, -8) if rows % t == 0)


def _sum_slots(name, slots):
    _, rows, cols = slots.shape
    tr = _row_tile(rows, (4 << 20) // (N_DEV * cols * slots.dtype.itemsize))

    def body(s_ref, o_ref):
        total = s_ref[0].astype(F32)
        for d in range(1, N_DEV):
            total = total + s_ref[d].astype(F32)
        o_ref[...] = total

    return pl.pallas_call(
        body,
        name=name,
        grid=(rows // tr,),
        in_specs=[pl.BlockSpec((N_DEV, tr, cols), lambda i: (0, i, 0))],
        out_specs=pl.BlockSpec((tr, cols), lambda i: (i, 0)),
        out_shape=S((rows, cols), F32),
        compiler_params=_cparams(1),
    )(slots)


def _adamw(name, w, g, m, v):
    rows, cols = w.shape
    tr = rows if rows * cols * 4 <= (2 << 20) else _row_tile(rows, 1 << ((1 << 18) // cols).bit_length() - 1)

    def body(w_ref, g_ref, m_ref, v_ref, d_ref, nm_ref, nv_ref):
        gv = g_ref[...]
        nm = ADAM_B1 * m_ref[...] + (1.0 - ADAM_B1) * gv
        nv = ADAM_B2 * v_ref[...] + (1.0 - ADAM_B2) * (gv * gv)
        m_hat = nm / (1.0 - ADAM_B1**ADAM_STEP)
        v_hat = nv / (1.0 - ADAM_B2**ADAM_STEP)
        d_ref[...] = -ADAM_LR * (m_hat / (jnp.sqrt(v_hat) + ADAM_EPS) + ADAM_WD * w_ref[...])
        nm_ref[...] = nm
        nv_ref[...] = nv

    blk = pl.BlockSpec((tr, cols), lambda i: (i, 0))
    return pl.pallas_call(
        body,
        name=name,
        grid=(rows // tr,),
        in_specs=[blk] * 4,
        out_specs=[blk] * 3,
        out_shape=[S((rows, cols), F32)] * 3,
        compiler_params=_cparams(1),
    )(w, g, m, v)


_BIG = ("w_in", "w_out", "w_ff1", "w_ff2")
_TRANSPOSED = ("w_in", "w_ff1")
_SMALL = ("b_in", "w_pool", "pool_scale", "sgu_ln_g", "sgu_ln_b", "sgu_w", "sgu_b", "conv_b", "conv_ln_g", "conv_ln_b",
          "b_out", "ln1_g", "ln1_b", "b_ff1", "b_ff2", "ln2_g", "ln2_b")
_WEIGHTS = ("w_in", "b_in", "w_pool", "pool_scale", "sgu_ln_g", "sgu_ln_b", "sgu_w", "sgu_b", "conv_w", "conv_b", "conv_ln_g",
            "conv_ln_b", "w_out", "b_out", "ln1_g", "ln1_b", "w_ff1", "b_ff1", "w_ff2", "b_ff2", "ln2_g", "ln2_b")


def _pack(arrays):
    parts = []
    for a in arrays:
        rows = a.reshape(-1, 128)
        parts.append(jnp.pad(rows, ((0, -rows.shape[0] % 8), (0, 0))))
    return jnp.concatenate(parts, axis=0)


def _unpack(flat, like):
    out, at = [], 0
    for a in like:
        n = a.size // 128
        out.append(flat[at : at + n].reshape(a.shape))
        at += n + (-n % 8)
    return out


def _packed_rows(arrays):
    return sum(a.size // 128 + (-(a.size // 128) % 8) for a in arrays)


def kernel(x, w_in, b_in, w_pool, pool_scale, sgu_ln_g, sgu_ln_b, sgu_w, sgu_b, conv_w, conv_b, conv_ln_g, conv_ln_b, w_out, b_out, ln1_g, ln1_b, w_ff1, b_ff1, w_ff2, b_ff2, ln2_g, ln2_b, loss_target, m_w_in, m_b_in, m_w_pool, m_pool_scale, m_sgu_ln_g, m_sgu_ln_b, m_sgu_w, m_sgu_b, m_conv_w, m_conv_b, m_conv_ln_g, m_conv_ln_b, m_w_out, m_b_out, m_ln1_g, m_ln1_b, m_w_ff1, m_b_ff1, m_w_ff2, m_b_ff2, m_ln2_g, m_ln2_b, v_w_in, v_b_in, v_w_pool, v_pool_scale, v_sgu_ln_g, v_sgu_ln_b, v_sgu_w, v_sgu_b, v_conv_w, v_conv_b, v_conv_ln_g, v_conv_ln_b, v_w_out, v_b_out, v_ln1_g, v_ln1_b, v_w_ff1, v_b_ff1, v_w_ff2, v_b_ff2, v_ln2_g, v_ln2_b):
    w = dict(w_in=w_in, b_in=b_in, w_pool=w_pool, pool_scale=pool_scale, sgu_ln_g=sgu_ln_g, sgu_ln_b=sgu_ln_b, sgu_w=sgu_w,
             sgu_b=sgu_b, conv_w=conv_w, conv_b=conv_b, conv_ln_g=conv_ln_g, conv_ln_b=conv_ln_b, w_out=w_out, b_out=b_out,
             ln1_g=ln1_g, ln1_b=ln1_b, w_ff1=w_ff1, b_ff1=b_ff1, w_ff2=w_ff2, b_ff2=b_ff2, ln2_g=ln2_g, ln2_b=ln2_b)
    mom = dict(w_in=m_w_in, b_in=m_b_in, w_pool=m_w_pool, pool_scale=m_pool_scale, sgu_ln_g=m_sgu_ln_g, sgu_ln_b=m_sgu_ln_b,
               sgu_w=m_sgu_w, sgu_b=m_sgu_b, conv_w=m_conv_w, conv_b=m_conv_b, conv_ln_g=m_conv_ln_g, conv_ln_b=m_conv_ln_b,
               w_out=m_w_out, b_out=m_b_out, ln1_g=m_ln1_g, ln1_b=m_ln1_b, w_ff1=m_w_ff1, b_ff1=m_b_ff1, w_ff2=m_w_ff2,
               b_ff2=m_b_ff2, ln2_g=m_ln2_g, ln2_b=m_ln2_b)
    var = dict(w_in=v_w_in, b_in=v_b_in, w_pool=v_w_pool, pool_scale=v_pool_scale, sgu_ln_g=v_sgu_ln_g, sgu_ln_b=v_sgu_ln_b,
               sgu_w=v_sgu_w, sgu_b=v_sgu_b, conv_w=v_conv_w, conv_b=v_conv_b, conv_ln_g=v_conv_ln_g, conv_ln_b=v_conv_ln_b,
               w_out=v_w_out, b_out=v_b_out, ln1_g=v_ln1_g, ln1_b=v_ln1_b, w_ff1=v_w_ff1, b_ff1=v_b_ff1, w_ff2=v_w_ff2,
               b_ff2=v_b_ff2, ln2_g=v_ln2_g, ln2_b=v_ln2_b)
    T = x.shape[1]
    x0 = x.reshape(T, D_MODEL)
    target = loss_target.reshape(T, D_MODEL)
    me_lin = _lin(_place())

    shard = [
        {name: (w[name][l].T if name in _TRANSPOSED else w[name][l]).astype(BF16) for name in _BIG} for l in range(DEPTH)
    ]
    conv_shard = jnp.pad(conv_w, ((0, 0), (0, 1), (0, 128 - conv_w.shape[2]))).reshape(DEPTH * 32, 128)

    def rows_of(g):
        return g.reshape(N_DEV * g.shape[1], g.shape[2])

    first = _all_gather([shard[0]["w_in"], shard[0]["w_out"], conv_shard])
    full = [{} for _ in range(DEPTH)]
    full[0]["w_in"], full[0]["w_out"] = rows_of(first[0]), rows_of(first[1])
    conv_cols = conv_w.shape[2]
    conv_full = first[2].reshape(N_DEV, DEPTH, 32, 128)[:, :, :CONV_KERNEL, :conv_cols]
    conv_full = conv_full.transpose(1, 2, 0, 3).reshape(DEPTH, CONV_KERNEL, N_DEV * conv_cols)

    tril = jnp.tril(jnp.ones((GROUP, GROUP), F32))
    prm = []
    for l in range(DEPTH):
        wm = sgu_w[l] * tril
        prm.append(dict(
            wp=w_pool[l].reshape(POOL_WIDTH, GROUP).astype(BF16), ps=_row(pool_scale[l]), lg=_row(sgu_ln_g[l]), lb=_row(sgu_ln_b[l]),
            wm=wm.reshape(SGU_WIDTH, GROUP).astype(BF16), wmt=wm.transpose(0, 2, 1).reshape(SGU_WIDTH, GROUP).astype(BF16),
            bsf=jnp.repeat(sgu_b[l].T, GROUP, axis=1), cw8=jnp.repeat(jnp.pad(conv_full[l], ((0, 1), (0, 0))), 8, axis=0), cb=_row(conv_b[l]),
            cg=_row(conv_ln_g[l]), cbeta=_row(conv_ln_b[l]),
        ))

    saved = []
    res = (x0, jnp.ones((D_MODEL,), F32), jnp.zeros((D_MODEL,), F32))
    xbf = x0.astype(BF16)
    n_ff = shard[0]["w_ff1"].shape[0]
    thirds = [(0, n_ff // 48 * 16), (n_ff // 48 * 16, n_ff // 48 * 16), (n_ff // 48 * 32, n_ff - n_ff // 48 * 32)]

    def ff1_part(l, k, into):
        return _Exchange([("gather", shard[l]["w_ff1"], thirds[k], into)])

    for l in range(DEPTH):
        f = full[l]
        ex = ff1_part(l, 0, None)
        proj = _mm_bias(f"proj{l}", xbf, f["w_in"], "nt", b_in[l], 1024, 896, 2048, side=ex)
        ex = ff1_part(l, 1, ex.results[0])
        mixed = _mixer_fwd(f"mixer_fwd{l}", proj, prm[l], 256, side=ex)
        ex = ff1_part(l, 2, ex.results[0])
        xh1, rs1, x1bf = _mm_ln(f"out_ln1_{l}", mixed, f["w_out"], b_out[l], res, ln1_g[l], ln1_b[l], 512, 1024, side=ex)
        f["w_ff1"] = rows_of(ex.results[0])
        ex = _Exchange([("gather", shard[l]["w_ff2"])])
        act, hsq = _mm_relu2(f"ff1_{l}", x1bf, f["w_ff1"], "nt", b_ff1[l], 1024, 1024, 2048, side=ex)
        f["w_ff2"] = rows_of(ex.results[0])
        ex = _Exchange([("gather", shard[l + 1]["w_in"]), ("gather", shard[l + 1]["w_out"])]) if l + 1 < DEPTH else None
        xh2, rs2, x2bf = _mm_ln(
            f"ff2_ln2_{l}", hsq, f["w_ff2"], b_ff2[l], (xh1, ln1_g[l], ln1_b[l]), ln2_g[l], ln2_b[l], 512, 1024, side=ex)
        if ex is not None:
            full[l + 1]["w_in"], full[l + 1]["w_out"] = rows_of(ex.results[0]), rows_of(ex.results[1])
        saved.append(dict(xin=xbf, proj=proj, mixed=mixed, xh1=xh1, rs1=rs1, x1bf=x1bf, act=act, hsq=hsq, xh2=xh2, rs2=rs2))
        res = (xh2, ln2_g[l], ln2_b[l])
        xbf = x2bf

    top = saved[-1]
    dr2, dr2bf, g_ln2g, g_ln2b, g_bff2, loss_row = _loss_top(top["xh2"], top["rs2"], ln2_g[-1], ln2_b[-1], target, 256)
    loss = lax.psum(loss_row[0, 0], ("x", "y", "c"))
    slots = [{} for _ in range(DEPTH)]
    gsm = [{} for _ in range(DEPTH)]
    grad_x = small_slots = None

    def stacked_small():
        st = {name: jnp.stack([gsm[gl][name].reshape(w[name].shape[1:]) for gl in range(DEPTH)]) for name in _SMALL}
        conv_g = jnp.pad(jnp.stack([gsm[gl]["conv_w"] for gl in range(DEPTH)]), ((0, 0), (0, 1), (0, 0)))
        return [st[name] for name in _SMALL] + [conv_g]

    for l in reversed(range(DEPTH)):
        f, sv = full[l], saved[l]
        gsm[l].update(ln2_g=g_ln2g, ln2_b=g_ln2b, b_ff2=g_bff2)
        gw = _mm_wgrad(f"gw_ff2_{l}", sv["hsq"], dr2bf, 512, 2048)
        ex = _Exchange([("slices", gw)])
        dhpre, g_bff1 = _mm_dh(f"dff1_{l}", dr2bf, f["w_ff2"], sv["act"], 1024, 1024, 2048, side=ex)
        slots[l]["w_ff2"] = ex.results[0]
        gsm[l]["b_ff1"] = g_bff1
        gw = _mm_wgrad(f"gw_ff1_{l}", dhpre, sv["x1bf"], 512, 2048)
        ex = _Exchange([("slices", gw)])
        dr1, dr1bf, g_ln1g, g_ln1b, g_bout = _mm_ln_bwd(
            f"dx1_ln1_{l}", dhpre, f["w_ff1"], dr2, sv["xh1"], sv["rs1"], ln1_g[l], 512, 1024, side=ex)
        slots[l]["w_ff1"] = ex.results[0]
        gsm[l].update(ln1_g=g_ln1g, ln1_b=g_ln1b, b_out=g_bout)
        gw = _mm_wgrad(f"gw_out_{l}", sv["mixed"], dr1bf, 512, 2048)
        ex = _Exchange([("slices", gw)])
        dmix = _mm_plain(f"dmixed{l}", dr1bf, f["w_out"], "nt", 1024, 1024, 2048, side=ex)
        slots[l]["w_out"] = ex.results[0]
        (dproj, g_wp, g_ps, g_lg, g_lb, g_wm, g_bs, g_cw, g_cb, g_cg, g_cbeta, g_bin) = _mixer_bwd(
            f"mixer_bwd{l}", sv["proj"], dmix, prm[l], 256)
        gsm[l].update(b_in=g_bin, w_pool=g_wp, pool_scale=g_ps, sgu_ln_g=g_lg, sgu_ln_b=g_lb, sgu_w=g_wm, sgu_b=g_bs[:SGU_HEADS],
                      conv_w=g_cw[:CONV_KERNEL], conv_b=g_cb, conv_ln_g=g_cg, conv_ln_b=g_cbeta)
        if l > 0:
            gw = _mm_wgrad(f"gw_in_{l}", dproj, sv["xin"], 512, 2048)
            below = saved[l - 1]
            ex = _Exchange([("slices", gw)])
            dr2, dr2bf, g_ln2g, g_ln2b, g_bff2 = _mm_ln_bwd(
                f"dx_ln2_{l}", dproj, f["w_in"], dr1, below["xh2"], below["rs2"], ln2_g[l - 1], 512, 896, side=ex)
        else:
            small_like = stacked_small()
            ex = _Exchange([("gather", _pack(small_like))])
            gw = _mm_wgrad(f"gw_in_{l}", dproj, sv["xin"], 512, 2048, side=ex)
            small_slots = ex.results[0]
            ex = _Exchange([("slices", gw)])
            grad_x = _mm_plain("dx0", dproj, f["w_in"], "nn", 512, 2048, 896, res=dr1, side=ex)
        slots[l]["w_in"] = ex.results[0]

    grads, deltas, new_m, new_v = {}, {}, {}, {}
    for name in _BIG:
        per_layer = []
        for l in range(DEPTH):
            g = _sum_slots(f"sum_{name}_{l}", slots[l][name])
            per_layer.append(g.T if name in _TRANSPOSED else g)
        g = jnp.stack(per_layer)
        shape = w[name].shape
        two_d = (shape[0] * shape[1], shape[2])
        d, nm, nv = _adamw(f"adamw_{name}", w[name].reshape(two_d), g.reshape(two_d), mom[name].reshape(two_d), var[name].reshape(two_d))
        grads[name], deltas[name], new_m[name], new_v[name] = g, d.reshape(shape), nm.reshape(shape), nv.reshape(shape)

    total = _sum_slots("sum_small", small_slots)
    small_g = _unpack(total, small_like)
    like = [w[name] for name in _SMALL]
    d, nm, nv = _adamw("adamw_small", _pack(like), total[: _packed_rows(like)],
                       _pack([mom[name] for name in _SMALL]), _pack([var[name] for name in _SMALL]))
    for name, gg, dd, mm_, vv in zip(_SMALL, small_g, _unpack(d, like), _unpack(nm, like), _unpack(nv, like)):
        grads[name], deltas[name], new_m[name], new_v[name] = gg, dd, mm_, vv
    conv_g = lax.dynamic_slice_in_dim(small_g[-1][:, :CONV_KERNEL, :], me_lin * conv_cols, conv_cols, axis=2)
    flat = (DEPTH * CONV_KERNEL, conv_cols)
    d, nm, nv = _adamw("adamw_conv_w", conv_w.reshape(flat), conv_g.reshape(flat), m_conv_w.reshape(flat), v_conv_w.reshape(flat))
    grads["conv_w"], deltas["conv_w"], new_m["conv_w"], new_v["conv_w"] = conv_g, d.reshape(conv_w.shape), nm.reshape(conv_w.shape), nv.reshape(conv_w.shape)

    return (loss, grad_x.reshape(x.shape), *[grads[n] for n in _WEIGHTS], *[deltas[n] for n in _WEIGHTS],
            *[new_m[n] for n in _WEIGHTS], *[new_v[n] for n in _WEIGHTS])
```

```python
import functools

import jax
import jax.numpy as jnp
from jax import lax
from jax.experimental import pallas as pl
from jax.experimental.pallas import tpu as pltpu

F32, BF16 = jnp.float32, jnp.bfloat16
S = jax.ShapeDtypeStruct

DEPTH = 2
D_MODEL = 2048
POOL_WINDOWS = (2, 4, 8, 16)
POOL_WIDTH = 512
GROUP = 128
SGU_WIDTH = 768
SGU_HEADS = 6
CONV_WIDTH = 768
CONV_KERNEL = 31
IN_WIDTH = 3584
D_FF = 8192
ALPHA = (2 * DEPTH) ** 0.25
LN_EPS = 1e-5
ADAM_LR, ADAM_B1, ADAM_B2, ADAM_EPS, ADAM_WD, ADAM_STEP = 0.001, 0.9, 0.999, 1e-08, 0.01, 10

N_DEV = 8
LN_STRIP = 32
LN_UNROLL = 4
HALO = 32
VMEM_LIMIT = 56 << 20
MESH = pl.DeviceIdType.MESH

C_POOL = (0, 512)
C_U = (512, 1280)
C_V = (1280, 2048)
C_CA = (2048, 2816)
C_CG = (2816, 3584)
M_POOL = (0, 512)
M_SGU = (512, 1280)
M_CONV = (1280, 2048)


def _cparams(n_axes):
    return pltpu.CompilerParams(dimension_semantics=("arbitrary",) * n_axes, vmem_limit_bytes=VMEM_LIMIT)


def _for_strips(rows, strip, fn, unroll=1):
    n = rows // strip
    if n == 1:
        fn(0)
        return

    def step(s, carry):
        fn(pl.multiple_of(s * strip, strip))
        return carry

    lax.fori_loop(0, n, step, 0, unroll=unroll)


def _row_sum(x):
    return jnp.sum(x, axis=0, keepdims=True)


def _ln_stats(r):
    mu = jnp.mean(r, axis=-1, keepdims=True)
    xc = r - mu
    var = jnp.mean(xc * xc, axis=-1, keepdims=True)
    rs = lax.rsqrt(var + LN_EPS)
    return xc * rs, rs


def _ln_bwd(dy, xhat, rs, g):
    gy = dy * g
    m1 = jnp.mean(gy, axis=-1, keepdims=True)
    m2 = jnp.mean(gy * xhat, axis=-1, keepdims=True)
    return rs * (gy - m1 - xhat * m2)


_GELU_C = 0.7978845608028654


def _gelu(x):
    th = jnp.tanh(_GELU_C * (x + 0.044715 * (x * x * x)))
    return 0.5 * x * (1.0 + th), th


def _gelu_grad(x, th):
    return 0.5 * (1.0 + th) + 0.5 * x * (1.0 - th * th) * (_GELU_C * (1.0 + 3.0 * 0.044715 * (x * x)))


def _place():
    x, y, c = lax.axis_index("x"), lax.axis_index("y"), lax.axis_index("c")
    return x, y, c


def _lin(p):
    return 4 * p[0] + 2 * p[1] + p[2]


def _flip(p, r):
    return tuple(1 - v if (r >> (2 - ax)) & 1 else v for ax, v in enumerate(p))


_ANY = pl.BlockSpec(memory_space=pl.ANY)


class _Exchange:
    def __init__(self, items):
        self.kinds = [item[0] for item in items]
        self.srcs = [item[1] for item in items]
        self.rows = [item[2] if len(item) > 2 else None for item in items]
        handed_on = [item[3] if len(item) > 3 else None for item in items]
        self.out_shape = [
            S((N_DEV, *x.shape), x.dtype) if kind == "gather" else S((N_DEV, x.shape[0] // N_DEV, x.shape[1]), x.dtype)
            for kind, x in zip(self.kinds, self.srcs)
        ]
        n = len(items)
        self.ins = self.srcs + [b for b in handed_on if b is not None]
        self.aliases = {}
        for a, b in enumerate(handed_on):
            if b is not None:
                self.aliases[n + len(self.aliases)] = a
        self.scratch = [pltpu.SemaphoreType.DMA((n, 7)), pltpu.SemaphoreType.DMA((n, 7)), pltpu.SemaphoreType.DMA((n,))]
        self.results = None

    def _src(self, in_refs, a, dest):
        if self.kinds[a] == "gather":
            return in_refs[a] if self.rows[a] is None else in_refs[a].at[pl.ds(*self.rows[a])]
        rows = self.srcs[a].shape[0] // N_DEV
        return in_refs[a].at[pl.ds(pl.multiple_of(_lin(dest) * rows, 8), rows)]

    def _dst(self, out_refs, a, slot):
        return out_refs[a].at[slot] if self.rows[a] is None else out_refs[a].at[slot, pl.ds(*self.rows[a])]

    def _copies(self, in_refs, out_refs, sems, with_arrivals):
        send_sems, recv_sems, local_sems = sems
        me = _place()
        local, sends, arrivals = [], [], []
        for a in range(len(self.srcs)):
            local.append(pltpu.make_async_copy(self._src(in_refs, a, me), self._dst(out_refs, a, _lin(me)), local_sems.at[a]))
            for r in range(1, N_DEV):
                peer = _flip(me, r)
                for slot, group in ((_lin(me), sends), (_lin(peer), arrivals)):
                    if group is sends or with_arrivals:
                        group.append(pltpu.make_async_remote_copy(
                            src_ref=self._src(in_refs, a, peer), dst_ref=self._dst(out_refs, a, slot),
                            send_sem=send_sems.at[a, r - 1], recv_sem=recv_sems.at[a, r - 1], device_id=peer, device_id_type=MESH,
                        ))
        return local, sends, arrivals

    def start(self, in_refs, out_refs, sems):
        local, sends, _ = self._copies(in_refs, out_refs, sems, False)
        for cp in local + sends:
            cp.start()

    def wait(self, in_refs, out_refs, sems):
        local, sends, arrivals = self._copies(in_refs, out_refs, sems, True)
        for cp in arrivals:
            cp.wait_recv()
        for cp in sends:
            cp.wait_send()
        for cp in local:
            cp.wait()


def _call(name, body, grid, in_specs, out_specs, out_shape, scratch, args, side=None):
    in_specs, out_specs, out_shape, scratch = list(in_specs), list(out_specs), list(out_shape), list(scratch)
    if side is None:
        return pl.pallas_call(
            body, name=name, grid=grid, in_specs=in_specs, out_specs=out_specs, out_shape=out_shape, scratch_shapes=scratch,
            compiler_params=_cparams(len(grid)),
        )(*args)
    n_in, n_out, n_scr = len(in_specs), len(out_specs), len(scratch)
    s_in, s_out = len(side.ins), len(side.out_shape)

    def wrapped(*refs):
        at = 0
        parts = []
        for n in (n_in, s_in, n_out, s_out, n_scr, 3):
            parts.append(refs[at : at + n])
            at += n
        ins, side_ins, outs, side_outs, scr, sems = parts
        pids = [pl.program_id(d) for d in range(len(grid))]
        first = functools.reduce(jnp.logical_and, [p == 0 for p in pids])
        last = functools.reduce(jnp.logical_and, [p == g - 1 for p, g in zip(pids, grid)])

        @pl.when(first)
        def _():
            side.start(side_ins, side_outs, sems)

        body(*ins, *outs, *scr)

        @pl.when(last)
        def _():
            side.wait(side_ins, side_outs, sems)

    res = pl.pallas_call(
        wrapped, name=name, grid=grid, in_specs=in_specs + [_ANY] * s_in, out_specs=out_specs + [_ANY] * s_out,
        out_shape=out_shape + side.out_shape, scratch_shapes=scratch + side.scratch, compiler_params=_cparams(len(grid)),
        input_output_aliases={n_in + i: n_out + o for i, o in side.aliases.items()},
    )(*args, *side.ins)
    side.results = list(res[n_out:])
    return list(res[:n_out])


_CONTRACT = {"nn": ((1,), (0,)), "nt": ((1,), (1,)), "tn": ((0,), (0,))}


def _mm(name, a, b, dims, tm, tn, tk, *, ins=(), outs, epilogue, j_outer=False, side=None):
    if dims == "tn":
        K, M = a.shape
    else:
        M, K = a.shape
    N = b.shape[0] if dims == "nt" else b.shape[1]
    tm, tn, tk = min(tm, M), min(tn, N), min(tk, K)
    assert M % tm == 0 and N % tn == 0 and K % tk == 0, (name, M, N, K, tm, tn, tk)
    nm, nn, nk = M // tm, N // tn, K // tk
    if j_outer:
        grid = (nn, nm, nk)
        ij = lambda g0, g1: (g1, g0)
    else:
        grid = (nm, nn, nk)
        ij = lambda g0, g1: (g0, g1)

    def amap(g0, g1, k):
        i, _ = ij(g0, g1)
        return (k, i) if dims == "tn" else (i, k)

    def bmap(g0, g1, k):
        _, j = ij(g0, g1)
        return (j, k) if dims == "nt" else (k, j)

    def spec(kind):
        if kind == "tile":
            return pl.BlockSpec((tm, tn), lambda g0, g1, k: ij(g0, g1))
        if kind == "row":
            return pl.BlockSpec((1, tn), lambda g0, g1, k: (0, ij(g0, g1)[1]))
        assert kind == "col", kind
        return pl.BlockSpec((tm, 1), lambda g0, g1, k: (ij(g0, g1)[0], 0))

    in_specs = [
        pl.BlockSpec((tk, tm) if dims == "tn" else (tm, tk), amap),
        pl.BlockSpec((tn, tk) if dims == "nt" else (tk, tn), bmap),
    ] + [spec(kind) for _, kind in ins]
    out_specs = [spec(kind) for _, _, kind in outs]
    out_shape = [S(shape, dtype) for shape, dtype, _ in outs]
    n_in, n_out = len(ins), len(outs)
    contract = (_CONTRACT[dims], ((), ()))

    def body(*refs):
        a_ref, b_ref = refs[:2]
        in_refs = refs[2 : 2 + n_in]
        out_refs = refs[2 + n_in : 2 + n_in + n_out]
        acc = refs[2 + n_in + n_out]
        i, _ = ij(pl.program_id(0), pl.program_id(1))
        k = pl.program_id(2)

        def part():
            return lax.dot_general(a_ref[...], b_ref[...], contract, preferred_element_type=F32)

        @pl.when(k == 0)
        def _():
            acc[...] = part()

        @pl.when(k > 0)
        def _():
            acc[...] += part()

        @pl.when(k == nk - 1)
        def _():
            epilogue(i, acc, in_refs, out_refs)

    return _call(name, body, grid, in_specs, out_specs, out_shape, [pltpu.VMEM((tm, tn), F32)], [a, b, *[x for x, _ in ins]], side)


def _row(v):
    return v.reshape(1, -1)


def _mm_bias(name, a, b, dims, bias, tm, tn, tk, side=None):
    M = a.shape[0]
    N = b.shape[0] if dims == "nt" else b.shape[1]

    def epilogue(i, acc, ins, outs):
        def strip(r0):
            rows = pl.ds(r0, 128)
            outs[0][rows, :] = acc[rows, :] + ins[0][...]

        _for_strips(acc.shape[0], 128, strip)

    return _mm(name, a, b, dims, tm, tn, tk, ins=[(_row(bias), "row")], outs=[((M, N), F32, "tile")], epilogue=epilogue, side=side)[0]


def _mm_relu2(name, a, b, dims, bias, tm, tn, tk, side=None):
    M = a.shape[0]
    N = b.shape[0] if dims == "nt" else b.shape[1]

    def epilogue(i, acc, ins, outs):
        def strip(r0):
            rows = pl.ds(r0, 128)
            r = jnp.maximum(acc[rows, :] + ins[0][...], 0.0)
            outs[0][rows, :] = r.astype(BF16)
            outs[1][rows, :] = (r * r).astype(BF16)

        _for_strips(acc.shape[0], 128, strip)

    return _mm(
        name, a, b, dims, tm, tn, tk, ins=[(_row(bias), "row")],
        outs=[((M, N), BF16, "tile"), ((M, N), BF16, "tile")], epilogue=epilogue, side=side,
    )


def _mm_lagged(name, a, b, tm, tk, *, ins, outs, strip_fn, init_fn=None, side=None):
    M, K = a.shape
    N = b.shape[1]
    tm, tk = min(tm, M), min(tk, K)
    assert M % tm == 0 and K % tk == 0, (name, M, K, tm, tk)
    nm, nk = M // tm, K // tk
    assert nk >= 2, (name, nk)
    parts = nk // 2
    rows_p = tm // parts
    assert rows_p % LN_STRIP == 0, (name, rows_p)

    def part_index(i, k):
        return jnp.maximum((i - 1) * parts + jnp.minimum(k, parts - 1), 0)

    def spec(kind):
        if kind == "tile":
            return pl.BlockSpec((rows_p, N), lambda i, k: (part_index(i, k), 0))
        if kind == "row":
            return pl.BlockSpec((1, N), lambda i, k: (0, 0))
        assert kind == "col", kind
        return pl.BlockSpec((rows_p, 1), lambda i, k: (part_index(i, k), 0))

    in_specs = [
        pl.BlockSpec((tm, tk), lambda i, k: (jnp.minimum(i, nm - 1), k)),
        pl.BlockSpec((tk, N), lambda i, k: (jnp.where(i < nm, k, nk - 1), 0)),
    ] + [spec(kind) for _, kind in ins]
    n_in, n_out = len(ins), len(outs)

    def body(*refs):
        a_ref, b_ref = refs[:2]
        in_refs = refs[2 : 2 + n_in]
        out_refs = refs[2 + n_in : 2 + n_in + n_out]
        acc, fin = refs[2 + n_in + n_out :]
        i, k = pl.program_id(0), pl.program_id(1)

        def part():
            return jnp.dot(a_ref[...], b_ref[...], preferred_element_type=F32)

        def epilogue_part():
            base = k * rows_p
            for s in range(rows_p // LN_STRIP):
                acc_rows = fin[pl.ds(pl.multiple_of(base + s * LN_STRIP, LN_STRIP), LN_STRIP), :]
                strip_fn(acc_rows, in_refs, out_refs, pl.ds(s * LN_STRIP, LN_STRIP))

        has_dot = i < nm
        has_epilogue = jnp.logical_and(i > 0, k < parts)
        no_epilogue = jnp.logical_not(has_epilogue)
        last = k == nk - 1
        both = jnp.logical_and(has_dot, has_epilogue)
        alone = jnp.logical_and(has_dot, no_epilogue)

        if init_fn is not None:
            @pl.when(jnp.logical_and(i == 1, k == 0))
            def _():
                init_fn(out_refs)

        @pl.when(jnp.logical_and(alone, k == 0))
        def _():
            acc[...] = part()

        @pl.when(jnp.logical_and(alone, jnp.logical_and(k > 0, jnp.logical_not(last))))
        def _():
            acc[...] += part()

        @pl.when(jnp.logical_and(has_dot, last))
        def _():
            fin[...] = acc[...] + part()

        @pl.when(jnp.logical_and(both, k == 0))
        def _():
            acc[...] = part()
            epilogue_part()

        if parts > 1:
            @pl.when(jnp.logical_and(both, k > 0))
            def _():
                acc[...] += part()
                epilogue_part()

        @pl.when(jnp.logical_and(jnp.logical_not(has_dot), has_epilogue))
        def _():
            epilogue_part()

    return _call(
        name, body, (nm + 1, nk), in_specs, [spec(kind) for _, _, kind in outs], [S(shape, dtype) for shape, dtype, _ in outs],
        [pltpu.VMEM((tm, N), F32), pltpu.VMEM((tm, N), F32)], [a, b, *[x for x, _ in ins]], side,
    )


def _mm_ln(name, a, b, bias, res, g, beta, tm, tk, side=None):
    M = a.shape[0]
    N = b.shape[1]
    rxh, rg, rb = res

    def strip(acc_rows, ins, outs, rows):
        bias_r, rxh_r, rg_r, rb_r, g_r, beta_r = ins
        xhat_o, rstd_o, xbf_o = outs
        resid = rxh_r[rows, :] * rg_r[...] + rb_r[...]
        r = ALPHA * resid + (acc_rows + bias_r[...])
        xhat, rs = _ln_stats(r)
        xhat_o[rows, :] = xhat
        rstd_o[rows, :] = rs
        xbf_o[rows, :] = (xhat * g_r[...] + beta_r[...]).astype(BF16)

    return _mm_lagged(
        name, a, b, tm, tk,
        ins=[(_row(bias), "row"), (rxh, "tile"), (_row(rg), "row"), (_row(rb), "row"), (_row(g), "row"), (_row(beta), "row")],
        outs=[((M, N), F32, "tile"), ((M, 1), F32, "col"), ((M, N), BF16, "tile")],
        strip_fn=strip, side=side,
    )


def _ln_bwd_strip(dyv, xhat, rs, g, dr_o, drbf_o, dg_o, db_o, dsum_o, rows):
    dr = _ln_bwd(dyv, xhat, rs, g)
    dr_o[rows, :] = dr
    drbf_o[rows, :] = dr.astype(BF16)
    dg_o[...] += _row_sum(dyv * xhat)
    db_o[...] += _row_sum(dyv)
    dsum_o[...] += _row_sum(dr)


def _mm_ln_bwd(name, a, b, resgrad, xhat, rstd, g, tm, tk, side=None):
    M = a.shape[0]
    N = b.shape[1]

    def init(outs):
        for o in outs[2:]:
            o[...] = jnp.zeros_like(o)

    def strip(acc_rows, ins, outs, rows):
        rg_r, xh_r, rs_r, g_r = ins
        dyv = acc_rows + ALPHA * rg_r[rows, :]
        _ln_bwd_strip(dyv, xh_r[rows, :], rs_r[rows, :], g_r[...], *outs, rows)

    return _mm_lagged(
        name, a, b, tm, tk,
        ins=[(resgrad, "tile"), (xhat, "tile"), (rstd, "col"), (_row(g), "row")],
        outs=[((M, N), F32, "tile"), ((M, N), BF16, "tile"), ((1, N), F32, "row"), ((1, N), F32, "row"), ((1, N), F32, "row")],
        strip_fn=strip, init_fn=init, side=side,
    )


def _mm_dh(name, a, b, act, tm, tn, tk, side=None):
    M = a.shape[0]
    N = b.shape[0]

    def epilogue(i, acc, ins, outs):
        @pl.when(i == 0)
        def _():
            outs[1][...] = jnp.zeros_like(outs[1])

        def strip(r0):
            rows = pl.ds(r0, 128)
            d = acc[rows, :] * (2.0 * ins[0][rows, :].astype(F32))
            outs[0][rows, :] = d.astype(BF16)
            outs[1][...] += _row_sum(d)

        _for_strips(acc.shape[0], 128, strip)

    return _mm(
        name, a, b, "nt", tm, tn, tk, ins=[(act, "tile")],
        outs=[((M, N), BF16, "tile"), ((1, N), F32, "row")], epilogue=epilogue, j_outer=True, side=side,
    )


def _mm_plain(name, a, b, dims, tm, tn, tk, side=None):
    M = a.shape[0]
    N = b.shape[0] if dims == "nt" else b.shape[1]

    def epilogue(i, acc, ins, outs):
        def strip(r0):
            rows = pl.ds(r0, 128)
            outs[0][rows, :] = acc[rows, :]

        _for_strips(acc.shape[0], 128, strip)

    return _mm(name, a, b, dims, tm, tn, tk, outs=[((M, N), F32, "tile")], epilogue=epilogue, side=side)[0]


def _mm_res(name, a, b, res, tm, tk, side=None):
    def strip(acc_rows, ins, outs, rows):
        outs[0][rows, :] = acc_rows + ALPHA * ins[0][rows, :]

    return _mm_lagged(
        name, a, b, tm, tk, ins=[(res, "tile")], outs=[((a.shape[0], b.shape[1]), F32, "tile")], strip_fn=strip, side=side,
    )[0]


def _mm_wgrad(name, a, b, tm, tk, side=None):
    M = a.shape[1]
    N = b.shape[1]

    def epilogue(i, acc, ins, outs):
        def strip(r0):
            rows = pl.ds(r0, 128)
            outs[0][rows, :] = acc[rows, :].astype(BF16)

        _for_strips(acc.shape[0], 128, strip)

    return _mm(name, a, b, "tn", tm, N, tk, outs=[((M, N), BF16, "tile")], epilogue=epilogue, side=side)[0]


def _loss_top(xhat, rstd, g, beta, target, tm):
    T, D = xhat.shape
    tm = min(tm, T)
    nt = T // tm

    def body(xh_r, rs_r, g_r, b_r, t_r, dr_o, drbf_o, dg_o, db_o, dsum_o, loss_o, sq_acc):
        i = pl.program_id(0)

        @pl.when(i == 0)
        def _():
            dg_o[...] = jnp.zeros_like(dg_o)
            db_o[...] = jnp.zeros_like(db_o)
            dsum_o[...] = jnp.zeros_like(dsum_o)
            sq_acc[...] = jnp.zeros_like(sq_acc)

        def strip(r0):
            rows = pl.ds(r0, LN_STRIP)
            xh = xh_r[rows, :]
            err = (xh * g_r[...] + b_r[...]) - t_r[rows, :]
            sq_acc[...] += _row_sum(err * err)
            _ln_bwd_strip(err * (1.0 / D), xh, rs_r[rows, :], g_r[...], dr_o, drbf_o, dg_o, db_o, dsum_o, rows)

        _for_strips(tm, LN_STRIP, strip, unroll=LN_UNROLL)

        @pl.when(i == nt - 1)
        def _():
            total = jnp.sum(sq_acc[...], axis=-1, keepdims=True) * (0.5 / D)
            loss_o[...] = jnp.broadcast_to(total, loss_o.shape)

    tile = pl.BlockSpec((tm, D), lambda i: (i, 0))
    row = pl.BlockSpec((1, D), lambda i: (0, 0))
    return pl.pallas_call(
        body,
        name="loss_top",
        grid=(nt,),
        in_specs=[tile, pl.BlockSpec((tm, 1), lambda i: (i, 0)), row, row, tile],
        out_specs=[tile, tile, row, row, row, pl.BlockSpec((1, 128), lambda i: (0, 0))],
        out_shape=[S((T, D), F32), S((T, D), BF16), S((1, D), F32), S((1, D), F32), S((1, D), F32), S((1, 128), F32)],
        scratch_shapes=[pltpu.VMEM((1, D), F32)],
        compiler_params=_cparams(1),
    )(xhat, rstd, _row(g), _row(beta), target)


def _cols(ref, c):
    return ref[:, c[0] : c[1]]


def _causal_window_sum(e, w):
    s, sh = e, 1
    while sh < w:
        s = s + pltpu.roll(s, sh, axis=0)
        sh *= 2
    return s


def _anticausal_window_sum(d, w):
    n = d.shape[0]
    r, sh = d, 1
    while sh < w:
        r = r + pltpu.roll(r, n - sh, axis=0)
        sh *= 2
    return r


def _with_halo(halo_ref, main_ref, c, keep):
    return jnp.concatenate([_cols(halo_ref, c) * keep, _cols(main_ref, c)], axis=0)


def _pool_counts(tile_index, R, w):
    pos = lax.broadcasted_iota(jnp.int32, (R, 1), 0) + tile_index * R
    return jnp.minimum(pos + 1, w).astype(F32)


def _sgu_mix(wm_ref, vnb):
    return jnp.concatenate(
        [
            jnp.dot(wm_ref[h * GROUP : (h + 1) * GROUP, :], vnb[:, h * GROUP : (h + 1) * GROUP], preferred_element_type=F32)
            for h in range(SGU_HEADS)
        ],
        axis=1,
    )


CONV_HALVES = (slice(0, CONV_WIDTH // 2), slice(CONV_WIDTH // 2, CONV_WIDTH))
TAP_STRIP = 32
TAP_GROUP = 4


def _build_shifts(shf, src, cols, rows):
    n = rows - 8
    for r in range(1, 8):
        shf[r - 1, pl.ds(0, n), :] = src[pl.ds(r, n), cols]


def _shifted(shf, src, cols, offset, start, size):
    q, r = divmod(offset, 8)
    rows = pl.ds(pl.multiple_of(start + 8 * q, 8), size)
    return src[rows, cols] if r == 0 else shf[r - 1, rows, :]


def _conv_taps(shf, src, cw8, cols, offsets, n_rows, out, bias=None):
    width = cols.stop - cols.start

    def strip(s, carry):
        r0 = pl.multiple_of(s * TAP_STRIP, TAP_STRIP)
        acc = jnp.zeros((TAP_STRIP, width), F32)
        for k, o in enumerate(offsets):
            wk = cw8[pl.ds(8 * k, 8), cols]
            acc = acc + _shifted(shf, src, cols, o, r0, TAP_STRIP) * jnp.concatenate([wk] * (TAP_STRIP // 8), axis=0)
        if bias is not None:
            acc = acc + bias[:, cols]
        out[pl.ds(r0, TAP_STRIP), cols] = acc
        return carry

    lax.fori_loop(0, n_rows // TAP_STRIP, strip, 0)


def _conv_weight_grad(shf, src, dsrc, d_first, cols, offsets, n_rows, dcw):
    width = cols.stop - cols.start
    for k0 in range(0, len(offsets), TAP_GROUP):
        group = offsets[k0 : k0 + TAP_GROUP]

        def strip(s, accs, group=group):
            r0 = pl.multiple_of(s * TAP_STRIP, TAP_STRIP)
            d = dsrc[pl.ds(pl.multiple_of(d_first + r0, 8), TAP_STRIP), cols]
            out = []
            for acc8, o in zip(accs, group):
                p = _shifted(shf, src, cols, o, r0, TAP_STRIP) * d
                for j in range(TAP_STRIP // 8):
                    acc8 = acc8 + p[8 * j : 8 * j + 8, :]
                out.append(acc8)
            return tuple(out)

        accs = lax.fori_loop(0, n_rows // TAP_STRIP, strip, tuple(jnp.zeros((8, width), F32) for _ in group))
        for j, acc8 in enumerate(accs):
            dcw[pl.ds(k0 + j, 1), cols] += _row_sum(acc8)


def _mixer_params(p):
    return [p["wp"], p["ps"], p["lg"], p["lb"], p["wm"], p["wmt"], p["bsf"], p["cw8"], p["cb"], p["cg"], p["cbeta"]]


def _whole(x):
    return pl.BlockSpec(x.shape, lambda i: (0,) * x.ndim)


def _mixer_fwd(name, proj, p, R, side=None):
    T = proj.shape[0]
    R = min(R, T)
    E = R + HALO
    nt = T // R
    hb = R // HALO
    tap_offsets = [HALO - (CONV_KERNEL - 1) + k for k in range(CONV_KERNEL)]

    def body(pm, ph, wp, ps, lg, lb, wm, wmt, bsf, cw8, cb, cg, cbeta, out, hbuf, shf, convbuf):
        i = pl.program_id(0)
        keep = (i > 0).astype(F32)
        a_ext = _with_halo(ph, pm, C_POOL, keep)
        for gi, w in enumerate(POOL_WINDOWS):
            cs = slice(gi * GROUP, (gi + 1) * GROUP)
            e = a_ext[:, cs]
            s = _causal_window_sum(e, w)
            pooled = s[HALO:, :] / _pool_counts(i, R, w) - e[HALO:, :]
            z = jnp.dot(pooled.astype(BF16), wp[cs, :], preferred_element_type=F32)
            out[:, cs] = (z * ps[:, cs]).astype(BF16)
        u, _ = _gelu(_cols(pm, C_U))
        v, _ = _gelu(_cols(pm, C_V))
        vhat, _ = _ln_stats(v)
        vn = vhat * lg[...] + lb[...]
        for c in range(R // GROUP):
            rs = slice(c * GROUP, (c + 1) * GROUP)
            mixed = _sgu_mix(wm, vn[rs, :].astype(BF16)) + bsf[...]
            out[rs, M_SGU[0] : M_SGU[1]] = (u[rs, :] * mixed).astype(BF16)
        hbuf[...] = _with_halo(ph, pm, C_CA, keep) * jax.nn.sigmoid(_with_halo(ph, pm, C_CG, keep))
        for cols in CONV_HALVES:
            _build_shifts(shf, hbuf, cols, E)
            _conv_taps(shf, hbuf, cw8, cols, tap_offsets, R, convbuf, bias=cb)
        chat, _ = _ln_stats(convbuf[...])
        cn = chat * cg[...] + cbeta[...]
        out[:, M_CONV[0] : M_CONV[1]] = (cn * jax.nn.sigmoid(cn)).astype(BF16)

    params = _mixer_params(p)
    in_specs = [
        pl.BlockSpec((R, IN_WIDTH), lambda i: (i, 0)),
        pl.BlockSpec((HALO, IN_WIDTH), lambda i: (jnp.maximum(i * hb - 1, 0), 0)),
    ] + [_whole(x) for x in params]
    scratch = [pltpu.VMEM((E, CONV_WIDTH), F32), pltpu.VMEM((7, E, CONV_WIDTH // 2), F32), pltpu.VMEM((R, CONV_WIDTH), F32)]
    return _call(
        name, body, (nt,), in_specs, [pl.BlockSpec((R, D_MODEL), lambda i: (i, 0))], [S((T, D_MODEL), BF16)],
        scratch, [proj, proj, *params], side,
    )[0]


def _mixer_bwd(name, proj, dmix, p, R):
    T = proj.shape[0]
    R = min(R, T)
    E = R + HALO
    nt = T // R
    hb = R // HALO
    tap_offsets = [HALO - (CONV_KERNEL - 1) + k for k in range(CONV_KERNEL)]
    back_offsets = [HALO - o for o in tap_offsets]

    def body(pm, ph, dm, wp, ps, lg, lb, wm, wmt, bsf, cw8, cb, cg, cbeta,
             dproj, dwp, dps, dlg, dlb, dwm, dbs, dcw, dcb, dcg, dcbeta, dbin,
             hbuf, dbuf, carry_p, carry_c, dbs_acc, shf, convbuf, dhcbuf):
        step = pl.program_id(0)
        ti = nt - 1 - step
        keep = (ti > 0).astype(F32)

        @pl.when(step == 0)
        def _():
            for r in (dwp, dps, dlg, dlb, dwm, dcw, dcb, dcg, dcbeta, dbin, carry_p, carry_c, dbs_acc):
                r[...] = jnp.zeros_like(r)

        def tail(carry):
            return jnp.concatenate([jnp.zeros((R - HALO, carry.shape[1]), F32), carry], axis=0)

        def head(x):
            return jnp.concatenate([jnp.zeros((HALO, x.shape[1]), F32), x], axis=0)

        a_ext = _with_halo(ph, pm, C_POOL, keep)
        carry_in = carry_p[...]
        for gi, w in enumerate(POOL_WINDOWS):
            cs = slice(gi * GROUP, (gi + 1) * GROUP)
            e = a_ext[:, cs]
            s = _causal_window_sum(e, w)
            cnt = _pool_counts(ti, R, w)
            pooled_b = (s[HALO:, :] / cnt - e[HALO:, :]).astype(BF16)
            wg = wp[cs, :]
            z = jnp.dot(pooled_b, wg, preferred_element_type=F32)
            dya = dm[:, cs]
            dps[:, cs] += _row_sum(dya * z)
            dz_b = (dya * ps[:, cs]).astype(BF16)
            dwp[cs, :] += lax.dot_general(pooled_b, dz_b, (((0,), (0,)), ((), ())), preferred_element_type=F32)
            dpooled = lax.dot_general(dz_b, wg, (((1,), (1,)), ((), ())), preferred_element_type=F32)
            da_ext = _anticausal_window_sum(head(dpooled / cnt), w) - head(dpooled)
            carry_p[:, cs] = da_ext[:HALO, :]
            d_a = da_ext[HALO:, :] + tail(carry_in[:, cs])
            dbin[:, cs] += _row_sum(d_a)
            dproj[:, cs] = d_a.astype(BF16)

        pu = _cols(pm, C_U)
        pv = _cols(pm, C_V)
        u, thu = _gelu(pu)
        v, thv = _gelu(pv)
        vhat, vrs = _ln_stats(v)
        vn = vhat * lg[...] + lb[...]
        dyb = dm[:, M_SGU[0] : M_SGU[1]]
        du_parts, dvn_parts = [], []
        for c in range(R // GROUP):
            rs = slice(c * GROUP, (c + 1) * GROUP)
            vnb = vn[rs, :].astype(BF16)
            mixed = _sgu_mix(wm, vnb) + bsf[...]
            du_parts.append(dyb[rs, :] * mixed)
            dmixed = dyb[rs, :] * u[rs, :]
            dbs_acc[...] += dmixed
            dmb = dmixed.astype(BF16)
            dvn_h = []
            for h in range(SGU_HEADS):
                hs = slice(h * GROUP, (h + 1) * GROUP)
                dwm[hs, :] += lax.dot_general(dmb[:, hs], vnb[:, hs], (((1,), (1,)), ((), ())), preferred_element_type=F32)
                dvn_h.append(jnp.dot(wmt[hs, :], dmb[:, hs], preferred_element_type=F32))
            dvn_parts.append(jnp.concatenate(dvn_h, axis=1))
        du = jnp.concatenate(du_parts, axis=0) if len(du_parts) > 1 else du_parts[0]
        dvn = jnp.concatenate(dvn_parts, axis=0) if len(dvn_parts) > 1 else dvn_parts[0]
        dlg[...] += _row_sum(dvn * vhat)
        dlb[...] += _row_sum(dvn)
        d_pu = du * _gelu_grad(pu, thu)
        d_pv = _ln_bwd(dvn, vhat, vrs, lg[...]) * _gelu_grad(pv, thv)
        dbin[:, C_U[0] : C_U[1]] += _row_sum(d_pu)
        dbin[:, C_V[0] : C_V[1]] += _row_sum(d_pv)
        dproj[:, C_U[0] : C_U[1]] = d_pu.astype(BF16)
        dproj[:, C_V[0] : C_V[1]] = d_pv.astype(BF16)

        sg_ext = jax.nn.sigmoid(_with_halo(ph, pm, C_CG, keep))
        ca_ext = _with_halo(ph, pm, C_CA, keep)
        hbuf[...] = ca_ext * sg_ext
        for cols in CONV_HALVES:
            _build_shifts(shf, hbuf, cols, E)
            _conv_taps(shf, hbuf, cw8, cols, tap_offsets, R, convbuf, bias=cb)
        chat, crs = _ln_stats(convbuf[...])
        cn = chat * cg[...] + cbeta[...]
        sc = jax.nn.sigmoid(cn)
        dcn = dm[:, M_CONV[0] : M_CONV[1]] * (sc * (1.0 + cn * (1.0 - sc)))
        dcg[...] += _row_sum(dcn * chat)
        dcbeta[...] += _row_sum(dcn)
        dconv = _ln_bwd(dcn, chat, crs, cg[...])
        dcb[...] += _row_sum(dconv)
        dbuf[pl.ds(0, HALO), :] = jnp.zeros((HALO, CONV_WIDTH), F32)
        dbuf[pl.ds(HALO, R), :] = dconv
        dbuf[pl.ds(HALO + R, HALO), :] = jnp.zeros((HALO, CONV_WIDTH), F32)
        for cols in CONV_HALVES:
            _build_shifts(shf, hbuf, cols, E)
            _conv_weight_grad(shf, hbuf, dbuf, HALO, cols, tap_offsets, R, dcw)
            _build_shifts(shf, dbuf, cols, E + HALO)
            _conv_taps(shf, dbuf, cw8, cols, back_offsets, E, dhcbuf)
        dhc_main = dhcbuf[pl.ds(HALO, R), :] + tail(carry_c[...])
        carry_c[...] = dhcbuf[pl.ds(0, HALO), :]
        sg = sg_ext[HALO:, :]
        d_ca = dhc_main * sg
        d_cg = dhc_main * ca_ext[HALO:, :] * (sg * (1.0 - sg))
        dbin[:, C_CA[0] : C_CA[1]] += _row_sum(d_ca)
        dbin[:, C_CG[0] : C_CG[1]] += _row_sum(d_cg)
        dproj[:, C_CA[0] : C_CA[1]] = d_ca.astype(BF16)
        dproj[:, C_CG[0] : C_CG[1]] = d_cg.astype(BF16)

        @pl.when(step == nt - 1)
        def _():
            row = lax.broadcasted_iota(jnp.int32, (GROUP, GROUP), 0)
            col = lax.broadcasted_iota(jnp.int32, (GROUP, GROUP), 1)
            dbs[...] = jnp.zeros_like(dbs)
            for h in range(SGU_HEADS):
                hs = slice(h * GROUP, (h + 1) * GROUP)
                dwm[hs, :] = jnp.where(row >= col, dwm[hs, :], 0.0)
                dbs[pl.ds(h, 1), :] = _row_sum(dbs_acc[:, hs].T)

    params = _mixer_params(p)
    accs = [
        S((POOL_WIDTH, GROUP), F32), S((1, POOL_WIDTH), F32), S((1, SGU_WIDTH), F32), S((1, SGU_WIDTH), F32),
        S((SGU_WIDTH, GROUP), F32), S((8, GROUP), F32), S((32, CONV_WIDTH), F32), S((1, CONV_WIDTH), F32),
        S((1, CONV_WIDTH), F32), S((1, CONV_WIDTH), F32), S((1, IN_WIDTH), F32),
    ]
    return pl.pallas_call(
        body,
        name=name,
        grid=(nt,),
        in_specs=[
            pl.BlockSpec((R, IN_WIDTH), lambda i: (nt - 1 - i, 0)),
            pl.BlockSpec((HALO, IN_WIDTH), lambda i: (jnp.maximum((nt - 1 - i) * hb - 1, 0), 0)),
            pl.BlockSpec((R, D_MODEL), lambda i: (nt - 1 - i, 0)),
        ]
        + [_whole(x) for x in params],
        out_specs=[pl.BlockSpec((R, IN_WIDTH), lambda i: (nt - 1 - i, 0))] + [_whole(x) for x in accs],
        out_shape=[S((T, IN_WIDTH), BF16)] + accs,
        scratch_shapes=[
            pltpu.VMEM((E, CONV_WIDTH), F32), pltpu.VMEM((E + HALO, CONV_WIDTH), F32),
            pltpu.VMEM((HALO, POOL_WIDTH), F32), pltpu.VMEM((HALO, CONV_WIDTH), F32), pltpu.VMEM((GROUP, SGU_WIDTH), F32),
            pltpu.VMEM((7, E + HALO, CONV_WIDTH // 2), F32), pltpu.VMEM((R, CONV_WIDTH), F32), pltpu.VMEM((E, CONV_WIDTH), F32),
        ],
        compiler_params=_cparams(1),
    )(proj, proj, dmix, *params)


def _all_gather(xs):
    n = len(xs)

    def body(*refs):
        x_refs, o_refs = refs[:n], refs[n : 2 * n]
        send_sems, recv_sems, local_sems = refs[2 * n :]
        x, y, c = _place()
        me, sibling = (x, y, c), (x, y, 1 - c)
        chips = [(1 - x, y), (x, 1 - y), (1 - x, 1 - y)]

        def copy(a, k, block, to, src=None):
            dst = o_refs[a].at[_lin(block)]
            return pltpu.make_async_remote_copy(
                src_ref=dst if src is None else src, dst_ref=dst, send_sem=send_sems.at[a, k], recv_sem=recv_sems.at[a, k],
                device_id=to, device_id_type=MESH,
            )

        mine = [pltpu.make_async_copy(x_refs[a], o_refs[a].at[_lin(me)], local_sems.at[a]) for a in range(n)]
        for m in mine:
            m.start()
        first = []
        for a in range(n):
            first.append(copy(a, 0, me, sibling, src=x_refs[a]))
            first += [copy(a, 1 + j, me, (*chip, c), src=x_refs[a]) for j, chip in enumerate(chips)]
        for cp in first:
            cp.start()
        passed = []
        for a in range(n):
            for j, chip in enumerate(chips):
                copy(a, 1 + j, (*chip, c), me).wait_recv()
                fwd = copy(a, 4 + j, (*chip, c), sibling)
                fwd.start()
                passed.append(fwd)
        for a in range(n):
            copy(a, 0, sibling, me).wait_recv()
            for j, chip in enumerate(chips):
                copy(a, 4 + j, (*chip, 1 - c), me).wait_recv()
        for cp in first + passed:
            cp.wait_send()
        for m in mine:
            m.wait()

    return pl.pallas_call(
        body,
        name="all_gather_weights",
        in_specs=[_ANY] * n,
        out_specs=[_ANY] * n,
        out_shape=[S((N_DEV, *x.shape), x.dtype) for x in xs],
        scratch_shapes=[pltpu.SemaphoreType.DMA((n, 7)), pltpu.SemaphoreType.DMA((n, 7)), pltpu.SemaphoreType.DMA((n,))],
    )(*xs)


def _row_tile(rows, want):
    return next(t for t in range(min(rows, want) // 8 * 8, 0, -8) if rows % t == 0)


def _sum_slots(name, slots):
    _, rows, cols = slots.shape
    tr = _row_tile(rows, (4 << 20) // (N_DEV * cols * slots.dtype.itemsize))

    def body(s_ref, o_ref):
        total = s_ref[0].astype(F32)
        for d in range(1, N_DEV):
            total = total + s_ref[d].astype(F32)
        o_ref[...] = total

    return pl.pallas_call(
        body,
        name=name,
        grid=(rows // tr,),
        in_specs=[pl.BlockSpec((N_DEV, tr, cols), lambda i: (0, i, 0))],
        out_specs=pl.BlockSpec((tr, cols), lambda i: (i, 0)),
        out_shape=S((rows, cols), F32),
        compiler_params=_cparams(1),
    )(slots)


def _adamw(name, w, g, m, v):
    rows, cols = w.shape
    tr = rows if rows * cols * 4 <= (2 << 20) else _row_tile(rows, 1 << ((1 << 18) // cols).bit_length() - 1)

    def body(w_ref, g_ref, m_ref, v_ref, d_ref, nm_ref, nv_ref):
        gv = g_ref[...]
        nm = ADAM_B1 * m_ref[...] + (1.0 - ADAM_B1) * gv
        nv = ADAM_B2 * v_ref[...] + (1.0 - ADAM_B2) * (gv * gv)
        m_hat = nm / (1.0 - ADAM_B1**ADAM_STEP)
        v_hat = nv / (1.0 - ADAM_B2**ADAM_STEP)
        d_ref[...] = -ADAM_LR * (m_hat / (jnp.sqrt(v_hat) + ADAM_EPS) + ADAM_WD * w_ref[...])
        nm_ref[...] = nm
        nv_ref[...] = nv

    blk = pl.BlockSpec((tr, cols), lambda i: (i, 0))
    return pl.pallas_call(
        body,
        name=name,
        grid=(rows // tr,),
        in_specs=[blk] * 4,
        out_specs=[blk] * 3,
        out_shape=[S((rows, cols), F32)] * 3,
        compiler_params=_cparams(1),
    )(w, g, m, v)


_BIG = ("w_in", "w_out", "w_ff1", "w_ff2")
_TRANSPOSED = ("w_in", "w_ff1")
_SMALL = ("b_in", "w_pool", "pool_scale", "sgu_ln_g", "sgu_ln_b", "sgu_w", "sgu_b", "conv_b", "conv_ln_g", "conv_ln_b",
          "b_out", "ln1_g", "ln1_b", "b_ff1", "b_ff2", "ln2_g", "ln2_b")
_WEIGHTS = ("w_in", "b_in", "w_pool", "pool_scale", "sgu_ln_g", "sgu_ln_b", "sgu_w", "sgu_b", "conv_w", "conv_b", "conv_ln_g",
            "conv_ln_b", "w_out", "b_out", "ln1_g", "ln1_b", "w_ff1", "b_ff1", "w_ff2", "b_ff2", "ln2_g", "ln2_b")


def _pack(arrays):
    parts = []
    for a in arrays:
        rows = a.reshape(-1, 128)
        parts.append(jnp.pad(rows, ((0, -rows.shape[0] % 8), (0, 0))))
    return jnp.concatenate(parts, axis=0)


def _unpack(flat, like):
    out, at = [], 0
    for a in like:
        n = a.size // 128
        out.append(flat[at : at + n].reshape(a.shape))
        at += n + (-n % 8)
    return out


def _packed_rows(arrays):
    return sum(a.size // 128 + (-(a.size // 128) % 8) for a in arrays)


def kernel(x, w_in, b_in, w_pool, pool_scale, sgu_ln_g, sgu_ln_b, sgu_w, sgu_b, conv_w, conv_b, conv_ln_g, conv_ln_b, w_out, b_out, ln1_g, ln1_b, w_ff1, b_ff1, w_ff2, b_ff2, ln2_g, ln2_b, loss_target, m_w_in, m_b_in, m_w_pool, m_pool_scale, m_sgu_ln_g, m_sgu_ln_b, m_sgu_w, m_sgu_b, m_conv_w, m_conv_b, m_conv_ln_g, m_conv_ln_b, m_w_out, m_b_out, m_ln1_g, m_ln1_b, m_w_ff1, m_b_ff1, m_w_ff2, m_b_ff2, m_ln2_g, m_ln2_b, v_w_in, v_b_in, v_w_pool, v_pool_scale, v_sgu_ln_g, v_sgu_ln_b, v_sgu_w, v_sgu_b, v_conv_w, v_conv_b, v_conv_ln_g, v_conv_ln_b, v_w_out, v_b_out, v_ln1_g, v_ln1_b, v_w_ff1, v_b_ff1, v_w_ff2, v_b_ff2, v_ln2_g, v_ln2_b):
    w = dict(w_in=w_in, b_in=b_in, w_pool=w_pool, pool_scale=pool_scale, sgu_ln_g=sgu_ln_g, sgu_ln_b=sgu_ln_b, sgu_w=sgu_w,
             sgu_b=sgu_b, conv_w=conv_w, conv_b=conv_b, conv_ln_g=conv_ln_g, conv_ln_b=conv_ln_b, w_out=w_out, b_out=b_out,
             ln1_g=ln1_g, ln1_b=ln1_b, w_ff1=w_ff1, b_ff1=b_ff1, w_ff2=w_ff2, b_ff2=b_ff2, ln2_g=ln2_g, ln2_b=ln2_b)
    mom = dict(w_in=m_w_in, b_in=m_b_in, w_pool=m_w_pool, pool_scale=m_pool_scale, sgu_ln_g=m_sgu_ln_g, sgu_ln_b=m_sgu_ln_b,
               sgu_w=m_sgu_w, sgu_b=m_sgu_b, conv_w=m_conv_w, conv_b=m_conv_b, conv_ln_g=m_conv_ln_g, conv_ln_b=m_conv_ln_b,
               w_out=m_w_out, b_out=m_b_out, ln1_g=m_ln1_g, ln1_b=m_ln1_b, w_ff1=m_w_ff1, b_ff1=m_b_ff1, w_ff2=m_w_ff2,
               b_ff2=m_b_ff2, ln2_g=m_ln2_g, ln2_b=m_ln2_b)
    var = dict(w_in=v_w_in, b_in=v_b_in, w_pool=v_w_pool, pool_scale=v_pool_scale, sgu_ln_g=v_sgu_ln_g, sgu_ln_b=v_sgu_ln_b,
               sgu_w=v_sgu_w, sgu_b=v_sgu_b, conv_w=v_conv_w, conv_b=v_conv_b, conv_ln_g=v_conv_ln_g, conv_ln_b=v_conv_ln_b,
               w_out=v_w_out, b_out=v_b_out, ln1_g=v_ln1_g, ln1_b=v_ln1_b, w_ff1=v_w_ff1, b_ff1=v_b_ff1, w_ff2=v_w_ff2,
               b_ff2=v_b_ff2, ln2_g=v_ln2_g, ln2_b=v_ln2_b)
    T = x.shape[1]
    x0 = x.reshape(T, D_MODEL)
    target = loss_target.reshape(T, D_MODEL)
    me_lin = _lin(_place())

    shard = [
        {name: (w[name][l].T if name in _TRANSPOSED else w[name][l]).astype(BF16) for name in _BIG} for l in range(DEPTH)
    ]
    conv_shard = jnp.pad(conv_w, ((0, 0), (0, 1), (0, 128 - conv_w.shape[2]))).reshape(DEPTH * 32, 128)

    def rows_of(g):
        return g.reshape(N_DEV * g.shape[1], g.shape[2])

    first = _all_gather([shard[0]["w_in"], shard[0]["w_out"], conv_shard])
    full = [{} for _ in range(DEPTH)]
    full[0]["w_in"], full[0]["w_out"] = rows_of(first[0]), rows_of(first[1])
    conv_cols = conv_w.shape[2]
    conv_full = first[2].reshape(N_DEV, DEPTH, 32, 128)[:, :, :CONV_KERNEL, :conv_cols]
    conv_full = conv_full.transpose(1, 2, 0, 3).reshape(DEPTH, CONV_KERNEL, N_DEV * conv_cols)

    tril = jnp.tril(jnp.ones((GROUP, GROUP), F32))
    prm = []
    for l in range(DEPTH):
        wm = sgu_w[l] * tril
        prm.append(dict(
            wp=w_pool[l].reshape(POOL_WIDTH, GROUP).astype(BF16), ps=_row(pool_scale[l]), lg=_row(sgu_ln_g[l]), lb=_row(sgu_ln_b[l]),
            wm=wm.reshape(SGU_WIDTH, GROUP).astype(BF16), wmt=wm.transpose(0, 2, 1).reshape(SGU_WIDTH, GROUP).astype(BF16),
            bsf=jnp.repeat(sgu_b[l].T, GROUP, axis=1), cw8=jnp.repeat(jnp.pad(conv_full[l], ((0, 1), (0, 0))), 8, axis=0), cb=_row(conv_b[l]),
            cg=_row(conv_ln_g[l]), cbeta=_row(conv_ln_b[l]),
        ))

    saved = []
    res = (x0, jnp.ones((D_MODEL,), F32), jnp.zeros((D_MODEL,), F32))
    xbf = x0.astype(BF16)
    n_ff = shard[0]["w_ff1"].shape[0]
    thirds = [(0, n_ff // 48 * 16), (n_ff // 48 * 16, n_ff // 48 * 16), (n_ff // 48 * 32, n_ff - n_ff // 48 * 32)]

    def ff1_part(l, k, into):
        return _Exchange([("gather", shard[l]["w_ff1"], thirds[k], into)])

    for l in range(DEPTH):
        f = full[l]
        ex = ff1_part(l, 0, None)
        proj = _mm_bias(f"proj{l}", xbf, f["w_in"], "nt", b_in[l], 1024, 896, 2048, side=ex)
        ex = ff1_part(l, 1, ex.results[0])
        mixed = _mixer_fwd(f"mixer_fwd{l}", proj, prm[l], 256, side=ex)
        ex = ff1_part(l, 2, ex.results[0])
        xh1, rs1, x1bf = _mm_ln(f"out_ln1_{l}", mixed, f["w_out"], b_out[l], res, ln1_g[l], ln1_b[l], 512, 512, side=ex)
        f["w_ff1"] = rows_of(ex.results[0])
        ex = _Exchange([("gather", shard[l]["w_ff2"])])
        act, hsq = _mm_relu2(f"ff1_{l}", x1bf, f["w_ff1"], "nt", b_ff1[l], 1024, 1024, 2048, side=ex)
        f["w_ff2"] = rows_of(ex.results[0])
        ex = _Exchange([("gather", shard[l + 1]["w_in"]), ("gather", shard[l + 1]["w_out"])]) if l + 1 < DEPTH else None
        xh2, rs2, x2bf = _mm_ln(
            f"ff2_ln2_{l}", hsq, f["w_ff2"], b_ff2[l], (xh1, ln1_g[l], ln1_b[l]), ln2_g[l], ln2_b[l], 512, 1024, side=ex)
        if ex is not None:
            full[l + 1]["w_in"], full[l + 1]["w_out"] = rows_of(ex.results[0]), rows_of(ex.results[1])
        saved.append(dict(xin=xbf, proj=proj, mixed=mixed, xh1=xh1, rs1=rs1, x1bf=x1bf, act=act, hsq=hsq, xh2=xh2, rs2=rs2))
        res = (xh2, ln2_g[l], ln2_b[l])
        xbf = x2bf

    top = saved[-1]
    dr2, dr2bf, g_ln2g, g_ln2b, g_bff2, loss_row = _loss_top(top["xh2"], top["rs2"], ln2_g[-1], ln2_b[-1], target, 256)
    loss = lax.psum(loss_row[0, 0], ("x", "y", "c"))
    slots = [{} for _ in range(DEPTH)]
    gsm = [{} for _ in range(DEPTH)]
    grad_x = small_slots = None

    def stacked_small():
        st = {name: jnp.stack([gsm[gl][name].reshape(w[name].shape[1:]) for gl in range(DEPTH)]) for name in _SMALL}
        conv_g = jnp.pad(jnp.stack([gsm[gl]["conv_w"] for gl in range(DEPTH)]), ((0, 0), (0, 1), (0, 0)))
        return [st[name] for name in _SMALL] + [conv_g]

    for l in reversed(range(DEPTH)):
        f, sv = full[l], saved[l]
        gsm[l].update(ln2_g=g_ln2g, ln2_b=g_ln2b, b_ff2=g_bff2)
        gw = _mm_wgrad(f"gw_ff2_{l}", sv["hsq"], dr2bf, 512, 2048)
        ex = _Exchange([("slices", gw)])
        dhpre, g_bff1 = _mm_dh(f"dff1_{l}", dr2bf, f["w_ff2"], sv["act"], 1024, 1024, 2048, side=ex)
        slots[l]["w_ff2"] = ex.results[0]
        gsm[l]["b_ff1"] = g_bff1
        gw = _mm_wgrad(f"gw_ff1_{l}", dhpre, sv["x1bf"], 512, 2048)
        ex = _Exchange([("slices", gw)])
        dr1, dr1bf, g_ln1g, g_ln1b, g_bout = _mm_ln_bwd(
            f"dx1_ln1_{l}", dhpre, f["w_ff1"], dr2, sv["xh1"], sv["rs1"], ln1_g[l], 512, 1024, side=ex)
        slots[l]["w_ff1"] = ex.results[0]
        gsm[l].update(ln1_g=g_ln1g, ln1_b=g_ln1b, b_out=g_bout)
        gw = _mm_wgrad(f"gw_out_{l}", sv["mixed"], dr1bf, 512, 2048)
        ex = _Exchange([("slices", gw)])
        dmix = _mm_plain(f"dmixed{l}", dr1bf, f["w_out"], "nt", 1024, 1024, 2048, side=ex)
        slots[l]["w_out"] = ex.results[0]
        (dproj, g_wp, g_ps, g_lg, g_lb, g_wm, g_bs, g_cw, g_cb, g_cg, g_cbeta, g_bin) = _mixer_bwd(
            f"mixer_bwd{l}", sv["proj"], dmix, prm[l], 256)
        gsm[l].update(b_in=g_bin, w_pool=g_wp, pool_scale=g_ps, sgu_ln_g=g_lg, sgu_ln_b=g_lb, sgu_w=g_wm, sgu_b=g_bs[:SGU_HEADS],
                      conv_w=g_cw[:CONV_KERNEL], conv_b=g_cb, conv_ln_g=g_cg, conv_ln_b=g_cbeta)
        if l > 0:
            gw = _mm_wgrad(f"gw_in_{l}", dproj, sv["xin"], 512, 2048)
            below = saved[l - 1]
            ex = _Exchange([("slices", gw)])
            dr2, dr2bf, g_ln2g, g_ln2b, g_bff2 = _mm_ln_bwd(
                f"dx_ln2_{l}", dproj, f["w_in"], dr1, below["xh2"], below["rs2"], ln2_g[l - 1], 512, 896, side=ex)
        else:
            small_like = stacked_small()
            ex = _Exchange([("gather", _pack(small_like))])
            gw = _mm_wgrad(f"gw_in_{l}", dproj, sv["xin"], 512, 2048, side=ex)
            small_slots = ex.results[0]
            ex = _Exchange([("slices", gw)])
            grad_x = _mm_res("dx0", dproj, f["w_in"], dr1, 512, 896, side=ex)
        slots[l]["w_in"] = ex.results[0]

    grads, deltas, new_m, new_v = {}, {}, {}, {}
    for name in _BIG:
        per_layer = []
        for l in range(DEPTH):
            g = _sum_slots(f"sum_{name}_{l}", slots[l][name])
            per_layer.append(g.T if name in _TRANSPOSED else g)
        g = jnp.stack(per_layer)
        shape = w[name].shape
        two_d = (shape[0] * shape[1], shape[2])
        d, nm, nv = _adamw(f"adamw_{name}", w[name].reshape(two_d), g.reshape(two_d), mom[name].reshape(two_d), var[name].reshape(two_d))
        grads[name], deltas[name], new_m[name], new_v[name] = g, d.reshape(shape), nm.reshape(shape), nv.reshape(shape)

    total = _sum_slots("sum_small", small_slots)
    small_g = _unpack(total, small_like)
    like = [w[name] for name in _SMALL]
    d, nm, nv = _adamw("adamw_small", _pack(like), total[: _packed_rows(like)],
                       _pack([mom[name] for name in _SMALL]), _pack([var[name] for name in _SMALL]))
    for name, gg, dd, mm_, vv in zip(_SMALL, small_g, _unpack(d, like), _unpack(nm, like), _unpack(nv, like)):
        grads[name], deltas[name], new_m[name], new_v[name] = gg, dd, mm_, vv
    conv_g = lax.dynamic_slice_in_dim(small_g[-1][:, :CONV_KERNEL, :], me_lin * conv_cols, conv_cols, axis=2)
    flat = (DEPTH * CONV_KERNEL, conv_cols)
    d, nm, nv = _adamw("adamw_conv_w", conv_w.reshape(flat), conv_g.reshape(flat), m_conv_w.reshape(flat), v_conv_w.reshape(flat))
    grads["conv_w"], deltas["conv_w"], new_m["conv_w"], new_v["conv_w"] = conv_g, d.reshape(conv_w.shape), nm.reshape(conv_w.shape), nv.reshape(conv_w.shape)

    return (loss, grad_x.reshape(x.shape), *[grads[n] for n in _WEIGHTS], *[deltas[n] for n in _WEIGHTS],
            *[new_m[n] for n in _WEIGHTS], *[new_v[n] for n in _WEIGHTS])
```

```python
import functools

import jax
import jax.numpy as jnp
from jax import lax
from jax.experimental import pallas as pl
from jax.experimental.pallas import tpu as pltpu

F32, BF16 = jnp.float32, jnp.bfloat16
S = jax.ShapeDtypeStruct

DEPTH = 2
D_MODEL = 2048
POOL_WINDOWS = (2, 4, 8, 16)
POOL_WIDTH = 512
GROUP = 128
SGU_WIDTH = 768
SGU_HEADS = 6
CONV_WIDTH = 768
CONV_KERNEL = 31
IN_WIDTH = 3584
D_FF = 8192
ALPHA = (2 * DEPTH) ** 0.25
LN_EPS = 1e-5
ADAM_LR, ADAM_B1, ADAM_B2, ADAM_EPS, ADAM_WD, ADAM_STEP = 0.001, 0.9, 0.999, 1e-08, 0.01, 10

N_DEV = 8
LN_STRIP = 32
LN_UNROLL = 4
HALO = 32
VMEM_LIMIT = 56 << 20
MESH = pl.DeviceIdType.MESH

C_POOL = (0, 512)
C_U = (512, 1280)
C_V = (1280, 2048)
C_CA = (2048, 2816)
C_CG = (2816, 3584)
M_POOL = (0, 512)
M_SGU = (512, 1280)
M_CONV = (1280, 2048)


def _cparams(n_axes):
    return pltpu.CompilerParams(dimension_semantics=("arbitrary",) * n_axes, vmem_limit_bytes=VMEM_LIMIT)


def _for_strips(rows, strip, fn, unroll=1):
    n = rows // strip
    if n == 1:
        fn(0)
        return

    def step(s, carry):
        fn(pl.multiple_of(s * strip, strip))
        return carry

    lax.fori_loop(0, n, step, 0, unroll=unroll)


def _row_sum(x):
    return jnp.sum(x, axis=0, keepdims=True)


def _ln_stats(r):
    mu = jnp.mean(r, axis=-1, keepdims=True)
    xc = r - mu
    var = jnp.mean(xc * xc, axis=-1, keepdims=True)
    rs = lax.rsqrt(var + LN_EPS)
    return xc * rs, rs


def _ln_bwd(dy, xhat, rs, g):
    gy = dy * g
    m1 = jnp.mean(gy, axis=-1, keepdims=True)
    m2 = jnp.mean(gy * xhat, axis=-1, keepdims=True)
    return rs * (gy - m1 - xhat * m2)


_GELU_C = 0.7978845608028654


def _gelu(x):
    th = jnp.tanh(_GELU_C * (x + 0.044715 * (x * x * x)))
    return 0.5 * x * (1.0 + th), th


def _gelu_grad(x, th):
    return 0.5 * (1.0 + th) + 0.5 * x * (1.0 - th * th) * (_GELU_C * (1.0 + 3.0 * 0.044715 * (x * x)))


def _place():
    x, y, c = lax.axis_index("x"), lax.axis_index("y"), lax.axis_index("c")
    return x, y, c


def _lin(p):
    return 4 * p[0] + 2 * p[1] + p[2]


def _flip(p, r):
    return tuple(1 - v if (r >> (2 - ax)) & 1 else v for ax, v in enumerate(p))


_ANY = pl.BlockSpec(memory_space=pl.ANY)


class _Exchange:
    def __init__(self, items):
        self.kinds = [item[0] for item in items]
        self.srcs = [item[1] for item in items]
        self.rows = [item[2] if len(item) > 2 else None for item in items]
        handed_on = [item[3] if len(item) > 3 else None for item in items]
        self.out_shape = [
            S((N_DEV, *x.shape), x.dtype) if kind == "gather" else S((N_DEV, x.shape[0] // N_DEV, x.shape[1]), x.dtype)
            for kind, x in zip(self.kinds, self.srcs)
        ]
        n = len(items)
        self.ins = self.srcs + [b for b in handed_on if b is not None]
        self.aliases = {}
        for a, b in enumerate(handed_on):
            if b is not None:
                self.aliases[n + len(self.aliases)] = a
        self.scratch = [pltpu.SemaphoreType.DMA((n, 7)), pltpu.SemaphoreType.DMA((n, 7)), pltpu.SemaphoreType.DMA((n,))]
        self.results = None

    def _src(self, in_refs, a, dest):
        if self.kinds[a] == "gather":
            return in_refs[a] if self.rows[a] is None else in_refs[a].at[pl.ds(*self.rows[a])]
        rows = self.srcs[a].shape[0] // N_DEV
        return in_refs[a].at[pl.ds(pl.multiple_of(_lin(dest) * rows, 8), rows)]

    def _dst(self, out_refs, a, slot):
        return out_refs[a].at[slot] if self.rows[a] is None else out_refs[a].at[slot, pl.ds(*self.rows[a])]

    def _copies(self, in_refs, out_refs, sems, with_arrivals):
        send_sems, recv_sems, local_sems = sems
        me = _place()
        local, sends, arrivals = [], [], []
        for a in range(len(self.srcs)):
            local.append(pltpu.make_async_copy(self._src(in_refs, a, me), self._dst(out_refs, a, _lin(me)), local_sems.at[a]))
            for r in range(1, N_DEV):
                peer = _flip(me, r)
                for slot, group in ((_lin(me), sends), (_lin(peer), arrivals)):
                    if group is sends or with_arrivals:
                        group.append(pltpu.make_async_remote_copy(
                            src_ref=self._src(in_refs, a, peer), dst_ref=self._dst(out_refs, a, slot),
                            send_sem=send_sems.at[a, r - 1], recv_sem=recv_sems.at[a, r - 1], device_id=peer, device_id_type=MESH,
                        ))
        return local, sends, arrivals

    def start(self, in_refs, out_refs, sems):
        local, sends, _ = self._copies(in_refs, out_refs, sems, False)
        for cp in local + sends:
            cp.start()

    def wait(self, in_refs, out_refs, sems):
        local, sends, arrivals = self._copies(in_refs, out_refs, sems, True)
        for cp in arrivals:
            cp.wait_recv()
        for cp in sends:
            cp.wait_send()
        for cp in local:
            cp.wait()


def _call(name, body, grid, in_specs, out_specs, out_shape, scratch, args, side=None):
    in_specs, out_specs, out_shape, scratch = list(in_specs), list(out_specs), list(out_shape), list(scratch)
    if side is None:
        return pl.pallas_call(
            body, name=name, grid=grid, in_specs=in_specs, out_specs=out_specs, out_shape=out_shape, scratch_shapes=scratch,
            compiler_params=_cparams(len(grid)),
        )(*args)
    n_in, n_out, n_scr = len(in_specs), len(out_specs), len(scratch)
    s_in, s_out = len(side.ins), len(side.out_shape)

    def wrapped(*refs):
        at = 0
        parts = []
        for n in (n_in, s_in, n_out, s_out, n_scr, 3):
            parts.append(refs[at : at + n])
            at += n
        ins, side_ins, outs, side_outs, scr, sems = parts
        pids = [pl.program_id(d) for d in range(len(grid))]
        first = functools.reduce(jnp.logical_and, [p == 0 for p in pids])
        last = functools.reduce(jnp.logical_and, [p == g - 1 for p, g in zip(pids, grid)])

        @pl.when(first)
        def _():
            side.start(side_ins, side_outs, sems)

        body(*ins, *outs, *scr)

        @pl.when(last)
        def _():
            side.wait(side_ins, side_outs, sems)

    res = pl.pallas_call(
        wrapped, name=name, grid=grid, in_specs=in_specs + [_ANY] * s_in, out_specs=out_specs + [_ANY] * s_out,
        out_shape=out_shape + side.out_shape, scratch_shapes=scratch + side.scratch, compiler_params=_cparams(len(grid)),
        input_output_aliases={n_in + i: n_out + o for i, o in side.aliases.items()},
    )(*args, *side.ins)
    side.results = list(res[n_out:])
    return list(res[:n_out])


_CONTRACT = {"nn": ((1,), (0,)), "nt": ((1,), (1,)), "tn": ((0,), (0,))}


def _mm(name, a, b, dims, tm, tn, tk, *, ins=(), outs, epilogue, j_outer=False, side=None):
    if dims == "tn":
        K, M = a.shape
    else:
        M, K = a.shape
    N = b.shape[0] if dims == "nt" else b.shape[1]
    tm, tn, tk = min(tm, M), min(tn, N), min(tk, K)
    assert M % tm == 0 and N % tn == 0 and K % tk == 0, (name, M, N, K, tm, tn, tk)
    nm, nn, nk = M // tm, N // tn, K // tk
    if j_outer:
        grid = (nn, nm, nk)
        ij = lambda g0, g1: (g1, g0)
    else:
        grid = (nm, nn, nk)
        ij = lambda g0, g1: (g0, g1)

    def amap(g0, g1, k):
        i, _ = ij(g0, g1)
        return (k, i) if dims == "tn" else (i, k)

    def bmap(g0, g1, k):
        _, j = ij(g0, g1)
        return (j, k) if dims == "nt" else (k, j)

    def spec(kind):
        if kind == "tile":
            return pl.BlockSpec((tm, tn), lambda g0, g1, k: ij(g0, g1))
        if kind == "row":
            return pl.BlockSpec((1, tn), lambda g0, g1, k: (0, ij(g0, g1)[1]))
        assert kind == "col", kind
        return pl.BlockSpec((tm, 1), lambda g0, g1, k: (ij(g0, g1)[0], 0))

    in_specs = [
        pl.BlockSpec((tk, tm) if dims == "tn" else (tm, tk), amap),
        pl.BlockSpec((tn, tk) if dims == "nt" else (tk, tn), bmap),
    ] + [spec(kind) for _, kind in ins]
    out_specs = [spec(kind) for _, _, kind in outs]
    out_shape = [S(shape, dtype) for shape, dtype, _ in outs]
    n_in, n_out = len(ins), len(outs)
    contract = (_CONTRACT[dims], ((), ()))

    def body(*refs):
        a_ref, b_ref = refs[:2]
        in_refs = refs[2 : 2 + n_in]
        out_refs = refs[2 + n_in : 2 + n_in + n_out]
        acc = refs[2 + n_in + n_out]
        i, _ = ij(pl.program_id(0), pl.program_id(1))
        k = pl.program_id(2)

        def part():
            return lax.dot_general(a_ref[...], b_ref[...], contract, preferred_element_type=F32)

        @pl.when(k == 0)
        def _():
            acc[...] = part()

        @pl.when(k > 0)
        def _():
            acc[...] += part()

        @pl.when(k == nk - 1)
        def _():
            epilogue(i, acc, in_refs, out_refs)

    return _call(name, body, grid, in_specs, out_specs, out_shape, [pltpu.VMEM((tm, tn), F32)], [a, b, *[x for x, _ in ins]], side)


def _row(v):
    return v.reshape(1, -1)


def _mm_bias(name, a, b, dims, bias, tm, tn, tk, side=None):
    M = a.shape[0]
    N = b.shape[0] if dims == "nt" else b.shape[1]

    def epilogue(i, acc, ins, outs):
        def strip(r0):
            rows = pl.ds(r0, 128)
            outs[0][rows, :] = acc[rows, :] + ins[0][...]

        _for_strips(acc.shape[0], 128, strip)

    return _mm(name, a, b, dims, tm, tn, tk, ins=[(_row(bias), "row")], outs=[((M, N), F32, "tile")], epilogue=epilogue, side=side)[0]


def _mm_relu2(name, a, b, dims, bias, tm, tn, tk, side=None):
    M = a.shape[0]
    N = b.shape[0] if dims == "nt" else b.shape[1]

    def epilogue(i, acc, ins, outs):
        def strip(r0):
            rows = pl.ds(r0, 128)
            r = jnp.maximum(acc[rows, :] + ins[0][...], 0.0)
            outs[0][rows, :] = r.astype(BF16)
            outs[1][rows, :] = (r * r).astype(BF16)

        _for_strips(acc.shape[0], 128, strip)

    return _mm(
        name, a, b, dims, tm, tn, tk, ins=[(_row(bias), "row")],
        outs=[((M, N), BF16, "tile"), ((M, N), BF16, "tile")], epilogue=epilogue, side=side,
    )


def _mm_lagged(name, a, b, tm, tk, *, ins, outs, strip_fn, init_fn=None, side=None):
    M, K = a.shape
    N = b.shape[1]
    tm, tk = min(tm, M), min(tk, K)
    assert M % tm == 0 and K % tk == 0, (name, M, K, tm, tk)
    nm, nk = M // tm, K // tk
    assert nk >= 2, (name, nk)
    parts = nk // 2
    rows_p = tm // parts
    assert rows_p % LN_STRIP == 0, (name, rows_p)

    def part_index(i, k):
        return jnp.maximum((i - 1) * parts + jnp.minimum(k, parts - 1), 0)

    def spec(kind):
        if kind == "tile":
            return pl.BlockSpec((rows_p, N), lambda i, k: (part_index(i, k), 0))
        if kind == "row":
            return pl.BlockSpec((1, N), lambda i, k: (0, 0))
        assert kind == "col", kind
        return pl.BlockSpec((rows_p, 1), lambda i, k: (part_index(i, k), 0))

    in_specs = [
        pl.BlockSpec((tm, tk), lambda i, k: (jnp.minimum(i, nm - 1), k)),
        pl.BlockSpec((tk, N), lambda i, k: (jnp.where(i < nm, k, nk - 1), 0)),
    ] + [spec(kind) for _, kind in ins]
    n_in, n_out = len(ins), len(outs)

    def body(*refs):
        a_ref, b_ref = refs[:2]
        in_refs = refs[2 : 2 + n_in]
        out_refs = refs[2 + n_in : 2 + n_in + n_out]
        acc, fin = refs[2 + n_in + n_out :]
        i, k = pl.program_id(0), pl.program_id(1)

        def part():
            return jnp.dot(a_ref[...], b_ref[...], preferred_element_type=F32)

        def epilogue_part():
            base = k * rows_p
            for s in range(rows_p // LN_STRIP):
                acc_rows = fin[pl.ds(pl.multiple_of(base + s * LN_STRIP, LN_STRIP), LN_STRIP), :]
                strip_fn(acc_rows, in_refs, out_refs, pl.ds(s * LN_STRIP, LN_STRIP))

        has_dot = i < nm
        has_epilogue = jnp.logical_and(i > 0, k < parts)
        no_epilogue = jnp.logical_not(has_epilogue)
        last = k == nk - 1
        both = jnp.logical_and(has_dot, has_epilogue)
        alone = jnp.logical_and(has_dot, no_epilogue)

        if init_fn is not None:
            @pl.when(jnp.logical_and(i == 1, k == 0))
            def _():
                init_fn(out_refs)

        @pl.when(jnp.logical_and(alone, k == 0))
        def _():
            acc[...] = part()

        @pl.when(jnp.logical_and(alone, jnp.logical_and(k > 0, jnp.logical_not(last))))
        def _():
            acc[...] += part()

        @pl.when(jnp.logical_and(has_dot, last))
        def _():
            fin[...] = acc[...] + part()

        @pl.when(jnp.logical_and(both, k == 0))
        def _():
            acc[...] = part()
            epilogue_part()

        if parts > 1:
            @pl.when(jnp.logical_and(both, k > 0))
            def _():
                acc[...] += part()
                epilogue_part()

        @pl.when(jnp.logical_and(jnp.logical_not(has_dot), has_epilogue))
        def _():
            epilogue_part()

    return _call(
        name, body, (nm + 1, nk), in_specs, [spec(kind) for _, _, kind in outs], [S(shape, dtype) for shape, dtype, _ in outs],
        [pltpu.VMEM((tm, N), F32), pltpu.VMEM((tm, N), F32)], [a, b, *[x for x, _ in ins]], side,
    )


def _mm_ln(name, a, b, bias, res, g, beta, tm, tk, side=None):
    M = a.shape[0]
    N = b.shape[1]
    rxh, rg, rb = res

    def strip(acc_rows, ins, outs, rows):
        bias_r, rxh_r, rg_r, rb_r, g_r, beta_r = ins
        xhat_o, rstd_o, xbf_o = outs
        resid = rxh_r[rows, :] * rg_r[...] + rb_r[...]
        r = ALPHA * resid + (acc_rows + bias_r[...])
        xhat, rs = _ln_stats(r)
        xhat_o[rows, :] = xhat
        rstd_o[rows, :] = rs
        xbf_o[rows, :] = (xhat * g_r[...] + beta_r[...]).astype(BF16)

    return _mm_lagged(
        name, a, b, tm, tk,
        ins=[(_row(bias), "row"), (rxh, "tile"), (_row(rg), "row"), (_row(rb), "row"), (_row(g), "row"), (_row(beta), "row")],
        outs=[((M, N), F32, "tile"), ((M, 1), F32, "col"), ((M, N), BF16, "tile")],
        strip_fn=strip, side=side,
    )


def _ln_bwd_strip(dyv, xhat, rs, g, dr_o, drbf_o, dg_o, db_o, dsum_o, rows):
    dr = _ln_bwd(dyv, xhat, rs, g)
    dr_o[rows, :] = dr
    drbf_o[rows, :] = dr.astype(BF16)
    dg_o[...] += _row_sum(dyv * xhat)
    db_o[...] += _row_sum(dyv)
    dsum_o[...] += _row_sum(dr)


def _mm_ln_bwd(name, a, b, resgrad, xhat, rstd, g, tm, tk, side=None):
    M = a.shape[0]
    N = b.shape[1]

    def init(outs):
        for o in outs[2:]:
            o[...] = jnp.zeros_like(o)

    def strip(acc_rows, ins, outs, rows):
        rg_r, xh_r, rs_r, g_r = ins
        dyv = acc_rows + ALPHA * rg_r[rows, :]
        _ln_bwd_strip(dyv, xh_r[rows, :], rs_r[rows, :], g_r[...], *outs, rows)

    return _mm_lagged(
        name, a, b, tm, tk,
        ins=[(resgrad, "tile"), (xhat, "tile"), (rstd, "col"), (_row(g), "row")],
        outs=[((M, N), F32, "tile"), ((M, N), BF16, "tile"), ((1, N), F32, "row"), ((1, N), F32, "row"), ((1, N), F32, "row")],
        strip_fn=strip, init_fn=init, side=side,
    )


def _mm_dh(name, a, b, act, tm, tn, tk, side=None):
    M = a.shape[0]
    N = b.shape[0]

    def epilogue(i, acc, ins, outs):
        @pl.when(i == 0)
        def _():
            outs[1][...] = jnp.zeros_like(outs[1])

        def strip(r0):
            rows = pl.ds(r0, 128)
            d = acc[rows, :] * (2.0 * ins[0][rows, :].astype(F32))
            outs[0][rows, :] = d.astype(BF16)
            outs[1][...] += _row_sum(d)

        _for_strips(acc.shape[0], 128, strip)

    return _mm(
        name, a, b, "nt", tm, tn, tk, ins=[(act, "tile")],
        outs=[((M, N), BF16, "tile"), ((1, N), F32, "row")], epilogue=epilogue, j_outer=True, side=side,
    )


def _mm_plain(name, a, b, dims, tm, tn, tk, side=None):
    M = a.shape[0]
    N = b.shape[0] if dims == "nt" else b.shape[1]

    def epilogue(i, acc, ins, outs):
        def strip(r0):
            rows = pl.ds(r0, 128)
            outs[0][rows, :] = acc[rows, :]

        _for_strips(acc.shape[0], 128, strip)

    return _mm(name, a, b, dims, tm, tn, tk, outs=[((M, N), F32, "tile")], epilogue=epilogue, side=side)[0]


def _mm_res(name, a, b, res, tm, tk, side=None):
    def strip(acc_rows, ins, outs, rows):
        outs[0][rows, :] = acc_rows + ALPHA * ins[0][rows, :]

    return _mm_lagged(
        name, a, b, tm, tk, ins=[(res, "tile")], outs=[((a.shape[0], b.shape[1]), F32, "tile")], strip_fn=strip, side=side,
    )[0]


def _mm_wgrad(name, a, b, tm, tk, side=None):
    M = a.shape[1]
    N = b.shape[1]

    def epilogue(i, acc, ins, outs):
        def strip(r0):
            rows = pl.ds(r0, 128)
            outs[0][rows, :] = acc[rows, :].astype(BF16)

        _for_strips(acc.shape[0], 128, strip)

    return _mm(name, a, b, "tn", tm, N, tk, outs=[((M, N), BF16, "tile")], epilogue=epilogue, side=side)[0]


def _loss_top(xhat, rstd, g, beta, target, tm):
    T, D = xhat.shape
    tm = min(tm, T)
    nt = T // tm

    def body(xh_r, rs_r, g_r, b_r, t_r, dr_o, drbf_o, dg_o, db_o, dsum_o, loss_o, sq_acc):
        i = pl.program_id(0)

        @pl.when(i == 0)
        def _():
            dg_o[...] = jnp.zeros_like(dg_o)
            db_o[...] = jnp.zeros_like(db_o)
            dsum_o[...] = jnp.zeros_like(dsum_o)
            sq_acc[...] = jnp.zeros_like(sq_acc)

        def strip(r0):
            rows = pl.ds(r0, LN_STRIP)
            xh = xh_r[rows, :]
            err = (xh * g_r[...] + b_r[...]) - t_r[rows, :]
            sq_acc[...] += _row_sum(err * err)
            _ln_bwd_strip(err * (1.0 / D), xh, rs_r[rows, :], g_r[...], dr_o, drbf_o, dg_o, db_o, dsum_o, rows)

        _for_strips(tm, LN_STRIP, strip, unroll=LN_UNROLL)

        @pl.when(i == nt - 1)
        def _():
            total = jnp.sum(sq_acc[...], axis=-1, keepdims=True) * (0.5 / D)
            loss_o[...] = jnp.broadcast_to(total, loss_o.shape)

    tile = pl.BlockSpec((tm, D), lambda i: (i, 0))
    row = pl.BlockSpec((1, D), lambda i: (0, 0))
    return pl.pallas_call(
        body,
        name="loss_top",
        grid=(nt,),
        in_specs=[tile, pl.BlockSpec((tm, 1), lambda i: (i, 0)), row, row, tile],
        out_specs=[tile, tile, row, row, row, pl.BlockSpec((1, 128), lambda i: (0, 0))],
        out_shape=[S((T, D), F32), S((T, D), BF16), S((1, D), F32), S((1, D), F32), S((1, D), F32), S((1, 128), F32)],
        scratch_shapes=[pltpu.VMEM((1, D), F32)],
        compiler_params=_cparams(1),
    )(xhat, rstd, _row(g), _row(beta), target)


def _cols(ref, c):
    return ref[:, c[0] : c[1]]


def _causal_window_sum(e, w):
    s, sh = e, 1
    while sh < w:
        s = s + pltpu.roll(s, sh, axis=0)
        sh *= 2
    return s


def _anticausal_window_sum(d, w):
    n = d.shape[0]
    r, sh = d, 1
    while sh < w:
        r = r + pltpu.roll(r, n - sh, axis=0)
        sh *= 2
    return r


def _with_halo(halo_ref, main_ref, c, keep):
    return jnp.concatenate([_cols(halo_ref, c) * keep, _cols(main_ref, c)], axis=0)


def _pool_counts(tile_index, R, w):
    pos = lax.broadcasted_iota(jnp.int32, (R, 1), 0) + tile_index * R
    return jnp.minimum(pos + 1, w).astype(F32)


def _sgu_mix(wm_ref, vnb):
    return jnp.concatenate(
        [
            jnp.dot(wm_ref[h * GROUP : (h + 1) * GROUP, :], vnb[:, h * GROUP : (h + 1) * GROUP], preferred_element_type=F32)
            for h in range(SGU_HEADS)
        ],
        axis=1,
    )


CONV_HALVES = (slice(0, CONV_WIDTH // 2), slice(CONV_WIDTH // 2, CONV_WIDTH))
TAP_STRIP = 32
TAP_GROUP = 4


def _build_shifts(shf, src, cols, rows):
    n = rows - 8
    for r in range(1, 8):
        shf[r - 1, pl.ds(0, n), :] = src[pl.ds(r, n), cols]


def _shifted(shf, src, cols, offset, start, size):
    q, r = divmod(offset, 8)
    rows = pl.ds(pl.multiple_of(start + 8 * q, 8), size)
    return src[rows, cols] if r == 0 else shf[r - 1, rows, :]


def _conv_taps(shf, src, cw8, cols, offsets, n_rows, out, bias=None):
    width = cols.stop - cols.start

    def strip(s, carry):
        r0 = pl.multiple_of(s * TAP_STRIP, TAP_STRIP)
        acc = jnp.zeros((TAP_STRIP, width), F32)
        for k, o in enumerate(offsets):
            wk = cw8[pl.ds(8 * k, 8), cols]
            acc = acc + _shifted(shf, src, cols, o, r0, TAP_STRIP) * jnp.concatenate([wk] * (TAP_STRIP // 8), axis=0)
        if bias is not None:
            acc = acc + bias[:, cols]
        out[pl.ds(r0, TAP_STRIP), cols] = acc
        return carry

    lax.fori_loop(0, n_rows // TAP_STRIP, strip, 0)


def _conv_weight_grad(shf, src, dsrc, d_first, cols, offsets, n_rows, dcw):
    width = cols.stop - cols.start
    for k0 in range(0, len(offsets), TAP_GROUP):
        group = offsets[k0 : k0 + TAP_GROUP]

        def strip(s, accs, group=group):
            r0 = pl.multiple_of(s * TAP_STRIP, TAP_STRIP)
            d = dsrc[pl.ds(pl.multiple_of(d_first + r0, 8), TAP_STRIP), cols]
            out = []
            for acc8, o in zip(accs, group):
                p = _shifted(shf, src, cols, o, r0, TAP_STRIP) * d
                for j in range(TAP_STRIP // 8):
                    acc8 = acc8 + p[8 * j : 8 * j + 8, :]
                out.append(acc8)
            return tuple(out)

        accs = lax.fori_loop(0, n_rows // TAP_STRIP, strip, tuple(jnp.zeros((8, width), F32) for _ in group))
        for j, acc8 in enumerate(accs):
            dcw[pl.ds(k0 + j, 1), cols] += _row_sum(acc8)


def _mixer_params(p):
    return [p["wp"], p["ps"], p["lg"], p["lb"], p["wm"], p["wmt"], p["bsf"], p["cw8"], p["cb"], p["cg"], p["cbeta"]]


def _whole(x):
    return pl.BlockSpec(x.shape, lambda i: (0,) * x.ndim)


def _mixer_fwd(name, proj, p, R, side=None):
    T = proj.shape[0]
    R = min(R, T)
    E = R + HALO
    nt = T // R
    hb = R // HALO
    tap_offsets = [HALO - (CONV_KERNEL - 1) + k for k in range(CONV_KERNEL)]

    def body(pm, ph, wp, ps, lg, lb, wm, wmt, bsf, cw8, cb, cg, cbeta, out, hbuf, shf, convbuf):
        i = pl.program_id(0)
        keep = (i > 0).astype(F32)
        a_ext = _with_halo(ph, pm, C_POOL, keep)
        for gi, w in enumerate(POOL_WINDOWS):
            cs = slice(gi * GROUP, (gi + 1) * GROUP)
            e = a_ext[:, cs]
            s = _causal_window_sum(e, w)
            pooled = s[HALO:, :] / _pool_counts(i, R, w) - e[HALO:, :]
            z = jnp.dot(pooled.astype(BF16), wp[cs, :], preferred_element_type=F32)
            out[:, cs] = (z * ps[:, cs]).astype(BF16)
        u, _ = _gelu(_cols(pm, C_U))
        v, _ = _gelu(_cols(pm, C_V))
        vhat, _ = _ln_stats(v)
        vn = vhat * lg[...] + lb[...]
        for c in range(R // GROUP):
            rs = slice(c * GROUP, (c + 1) * GROUP)
            mixed = _sgu_mix(wm, vn[rs, :].astype(BF16)) + bsf[...]
            out[rs, M_SGU[0] : M_SGU[1]] = (u[rs, :] * mixed).astype(BF16)
        hbuf[...] = _with_halo(ph, pm, C_CA, keep) * jax.nn.sigmoid(_with_halo(ph, pm, C_CG, keep))
        for cols in CONV_HALVES:
            _build_shifts(shf, hbuf, cols, E)
            _conv_taps(shf, hbuf, cw8, cols, tap_offsets, R, convbuf, bias=cb)
        chat, _ = _ln_stats(convbuf[...])
        cn = chat * cg[...] + cbeta[...]
        out[:, M_CONV[0] : M_CONV[1]] = (cn * jax.nn.sigmoid(cn)).astype(BF16)

    params = _mixer_params(p)
    in_specs = [
        pl.BlockSpec((R, IN_WIDTH), lambda i: (i, 0)),
        pl.BlockSpec((HALO, IN_WIDTH), lambda i: (jnp.maximum(i * hb - 1, 0), 0)),
    ] + [_whole(x) for x in params]
    scratch = [pltpu.VMEM((E, CONV_WIDTH), F32), pltpu.VMEM((7, E, CONV_WIDTH // 2), F32), pltpu.VMEM((R, CONV_WIDTH), F32)]
    return _call(
        name, body, (nt,), in_specs, [pl.BlockSpec((R, D_MODEL), lambda i: (i, 0))], [S((T, D_MODEL), BF16)],
        scratch, [proj, proj, *params], side,
    )[0]


def _mixer_bwd(name, proj, dmix, p, R):
    T = proj.shape[0]
    R = min(R, T)
    E = R + HALO
    nt = T // R
    hb = R // HALO
    tap_offsets = [HALO - (CONV_KERNEL - 1) + k for k in range(CONV_KERNEL)]
    back_offsets = [HALO - o for o in tap_offsets]

    def body(pm, ph, dm, wp, ps, lg, lb, wm, wmt, bsf, cw8, cb, cg, cbeta,
             dproj, dwp, dps, dlg, dlb, dwm, dbs, dcw, dcb, dcg, dcbeta, dbin,
             hbuf, dbuf, carry_p, carry_c, dbs_acc, shf, convbuf, dhcbuf):
        step = pl.program_id(0)
        ti = nt - 1 - step
        keep = (ti > 0).astype(F32)

        @pl.when(step == 0)
        def _():
            for r in (dwp, dps, dlg, dlb, dwm, dcw, dcb, dcg, dcbeta, dbin, carry_p, carry_c, dbs_acc):
                r[...] = jnp.zeros_like(r)

        def tail(carry):
            return jnp.concatenate([jnp.zeros((R - HALO, carry.shape[1]), F32), carry], axis=0)

        def head(x):
            return jnp.concatenate([jnp.zeros((HALO, x.shape[1]), F32), x], axis=0)

        a_ext = _with_halo(ph, pm, C_POOL, keep)
        carry_in = carry_p[...]
        for gi, w in enumerate(POOL_WINDOWS):
            cs = slice(gi * GROUP, (gi + 1) * GROUP)
            e = a_ext[:, cs]
            s = _causal_window_sum(e, w)
            cnt = _pool_counts(ti, R, w)
            pooled_b = (s[HALO:, :] / cnt - e[HALO:, :]).astype(BF16)
            wg = wp[cs, :]
            z = jnp.dot(pooled_b, wg, preferred_element_type=F32)
            dya = dm[:, cs]
            dps[:, cs] += _row_sum(dya * z)
            dz_b = (dya * ps[:, cs]).astype(BF16)
            dwp[cs, :] += lax.dot_general(pooled_b, dz_b, (((0,), (0,)), ((), ())), preferred_element_type=F32)
            dpooled = lax.dot_general(dz_b, wg, (((1,), (1,)), ((), ())), preferred_element_type=F32)
            da_ext = _anticausal_window_sum(head(dpooled / cnt), w) - head(dpooled)
            carry_p[:, cs] = da_ext[:HALO, :]
            d_a = da_ext[HALO:, :] + tail(carry_in[:, cs])
            dbin[:, cs] += _row_sum(d_a)
            dproj[:, cs] = d_a.astype(BF16)

        pu = _cols(pm, C_U)
        pv = _cols(pm, C_V)
        u, thu = _gelu(pu)
        v, thv = _gelu(pv)
        vhat, vrs = _ln_stats(v)
        vn = vhat * lg[...] + lb[...]
        dyb = dm[:, M_SGU[0] : M_SGU[1]]
        du_parts, dvn_parts = [], []
        for c in range(R // GROUP):
            rs = slice(c * GROUP, (c + 1) * GROUP)
            vnb = vn[rs, :].astype(BF16)
            mixed = _sgu_mix(wm, vnb) + bsf[...]
            du_parts.append(dyb[rs, :] * mixed)
            dmixed = dyb[rs, :] * u[rs, :]
            dbs_acc[...] += dmixed
            dmb = dmixed.astype(BF16)
            dvn_h = []
            for h in range(SGU_HEADS):
                hs = slice(h * GROUP, (h + 1) * GROUP)
                dwm[hs, :] += lax.dot_general(dmb[:, hs], vnb[:, hs], (((1,), (1,)), ((), ())), preferred_element_type=F32)
                dvn_h.append(jnp.dot(wmt[hs, :], dmb[:, hs], preferred_element_type=F32))
            dvn_parts.append(jnp.concatenate(dvn_h, axis=1))
        du = jnp.concatenate(du_parts, axis=0) if len(du_parts) > 1 else du_parts[0]
        dvn = jnp.concatenate(dvn_parts, axis=0) if len(dvn_parts) > 1 else dvn_parts[0]
        dlg[...] += _row_sum(dvn * vhat)
        dlb[...] += _row_sum(dvn)
        d_pu = du * _gelu_grad(pu, thu)
        d_pv = _ln_bwd(dvn, vhat, vrs, lg[...]) * _gelu_grad(pv, thv)
        dbin[:, C_U[0] : C_U[1]] += _row_sum(d_pu)
        dbin[:, C_V[0] : C_V[1]] += _row_sum(d_pv)
        dproj[:, C_U[0] : C_U[1]] = d_pu.astype(BF16)
        dproj[:, C_V[0] : C_V[1]] = d_pv.astype(BF16)

        sg_ext = jax.nn.sigmoid(_with_halo(ph, pm, C_CG, keep))
        ca_ext = _with_halo(ph, pm, C_CA, keep)
        hbuf[...] = ca_ext * sg_ext
        for cols in CONV_HALVES:
            _build_shifts(shf, hbuf, cols, E)
            _conv_taps(shf, hbuf, cw8, cols, tap_offsets, R, convbuf, bias=cb)
        chat, crs = _ln_stats(convbuf[...])
        cn = chat * cg[...] + cbeta[...]
        sc = jax.nn.sigmoid(cn)
        dcn = dm[:, M_CONV[0] : M_CONV[1]] * (sc * (1.0 + cn * (1.0 - sc)))
        dcg[...] += _row_sum(dcn * chat)
        dcbeta[...] += _row_sum(dcn)
        dconv = _ln_bwd(dcn, chat, crs, cg[...])
        dcb[...] += _row_sum(dconv)
        dbuf[pl.ds(0, HALO), :] = jnp.zeros((HALO, CONV_WIDTH), F32)
        dbuf[pl.ds(HALO, R), :] = dconv
        dbuf[pl.ds(HALO + R, HALO), :] = jnp.zeros((HALO, CONV_WIDTH), F32)
        for cols in CONV_HALVES:
            _build_shifts(shf, hbuf, cols, E)
            _conv_weight_grad(shf, hbuf, dbuf, HALO, cols, tap_offsets, R, dcw)
            _build_shifts(shf, dbuf, cols, E + HALO)
            _conv_taps(shf, dbuf, cw8, cols, back_offsets, E, dhcbuf)
        dhc_main = dhcbuf[pl.ds(HALO, R), :] + tail(carry_c[...])
        carry_c[...] = dhcbuf[pl.ds(0, HALO), :]
        sg = sg_ext[HALO:, :]
        d_ca = dhc_main * sg
        d_cg = dhc_main * ca_ext[HALO:, :] * (sg * (1.0 - sg))
        dbin[:, C_CA[0] : C_CA[1]] += _row_sum(d_ca)
        dbin[:, C_CG[0] : C_CG[1]] += _row_sum(d_cg)
        dproj[:, C_CA[0] : C_CA[1]] = d_ca.astype(BF16)
        dproj[:, C_CG[0] : C_CG[1]] = d_cg.astype(BF16)

        @pl.when(step == nt - 1)
        def _():
            row = lax.broadcasted_iota(jnp.int32, (GROUP, GROUP), 0)
            col = lax.broadcasted_iota(jnp.int32, (GROUP, GROUP), 1)
            dbs[...] = jnp.zeros_like(dbs)
            for h in range(SGU_HEADS):
                hs = slice(h * GROUP, (h + 1) * GROUP)
                dwm[hs, :] = jnp.where(row >= col, dwm[hs, :], 0.0)
                dbs[pl.ds(h, 1), :] = _row_sum(dbs_acc[:, hs].T)

    params = _mixer_params(p)
    accs = [
        S((POOL_WIDTH, GROUP), F32), S((1, POOL_WIDTH), F32), S((1, SGU_WIDTH), F32), S((1, SGU_WIDTH), F32),
        S((SGU_WIDTH, GROUP), F32), S((8, GROUP), F32), S((32, CONV_WIDTH), F32), S((1, CONV_WIDTH), F32),
        S((1, CONV_WIDTH), F32), S((1, CONV_WIDTH), F32), S((1, IN_WIDTH), F32),
    ]
    return pl.pallas_call(
        body,
        name=name,
        grid=(nt,),
        in_specs=[
            pl.BlockSpec((R, IN_WIDTH), lambda i: (nt - 1 - i, 0)),
            pl.BlockSpec((HALO, IN_WIDTH), lambda i: (jnp.maximum((nt - 1 - i) * hb - 1, 0), 0)),
            pl.BlockSpec((R, D_MODEL), lambda i: (nt - 1 - i, 0)),
        ]
        + [_whole(x) for x in params],
        out_specs=[pl.BlockSpec((R, IN_WIDTH), lambda i: (nt - 1 - i, 0))] + [_whole(x) for x in accs],
        out_shape=[S((T, IN_WIDTH), BF16)] + accs,
        scratch_shapes=[
            pltpu.VMEM((E, CONV_WIDTH), F32), pltpu.VMEM((E + HALO, CONV_WIDTH), F32),
            pltpu.VMEM((HALO, POOL_WIDTH), F32), pltpu.VMEM((HALO, CONV_WIDTH), F32), pltpu.VMEM((GROUP, SGU_WIDTH), F32),
            pltpu.VMEM((7, E + HALO, CONV_WIDTH // 2), F32), pltpu.VMEM((R, CONV_WIDTH), F32), pltpu.VMEM((E, CONV_WIDTH), F32),
        ],
        compiler_params=_cparams(1),
    )(proj, proj, dmix, *params)


def _all_gather(xs):
    n = len(xs)

    def body(*refs):
        x_refs, o_refs = refs[:n], refs[n : 2 * n]
        send_sems, recv_sems, local_sems = refs[2 * n :]
        x, y, c = _place()
        me, sibling = (x, y, c), (x, y, 1 - c)
        chips = [(1 - x, y), (x, 1 - y), (1 - x, 1 - y)]

        def copy(a, k, block, to, src=None):
            dst = o_refs[a].at[_lin(block)]
            return pltpu.make_async_remote_copy(
                src_ref=dst if src is None else src, dst_ref=dst, send_sem=send_sems.at[a, k], recv_sem=recv_sems.at[a, k],
                device_id=to, device_id_type=MESH,
            )

        mine = [pltpu.make_async_copy(x_refs[a], o_refs[a].at[_lin(me)], local_sems.at[a]) for a in range(n)]
        for m in mine:
            m.start()
        first = []
        for a in range(n):
            first.append(copy(a, 0, me, sibling, src=x_refs[a]))
            first += [copy(a, 1 + j, me, (*chip, c), src=x_refs[a]) for j, chip in enumerate(chips)]
        for cp in first:
            cp.start()
        passed = []
        for a in range(n):
            for j, chip in enumerate(chips):
                copy(a, 1 + j, (*chip, c), me).wait_recv()
                fwd = copy(a, 4 + j, (*chip, c), sibling)
                fwd.start()
                passed.append(fwd)
        for a in range(n):
            copy(a, 0, sibling, me).wait_recv()
            for j, chip in enumerate(chips):
                copy(a, 4 + j, (*chip, 1 - c), me).wait_recv()
        for cp in first + passed:
            cp.wait_send()
        for m in mine:
            m.wait()

    return pl.pallas_call(
        body,
        name="all_gather_weights",
        in_specs=[_ANY] * n,
        out_specs=[_ANY] * n,
        out_shape=[S((N_DEV, *x.shape), x.dtype) for x in xs],
        scratch_shapes=[pltpu.SemaphoreType.DMA((n, 7)), pltpu.SemaphoreType.DMA((n, 7)), pltpu.SemaphoreType.DMA((n,))],
    )(*xs)


def _row_tile(rows, want):
    return next(t for t in range(min(rows, want) // 8 * 8, 0, -8) if rows % t == 0)


def _sum_slots(name, slots):
    _, rows, cols = slots.shape
    tr = _row_tile(rows, (4 << 20) // (N_DEV * cols * slots.dtype.itemsize))

    def body(s_ref, o_ref):
        total = s_ref[0].astype(F32)
        for d in range(1, N_DEV):
            total = total + s_ref[d].astype(F32)
        o_ref[...] = total

    return pl.pallas_call(
        body,
        name=name,
        grid=(rows // tr,),
        in_specs=[pl.BlockSpec((N_DEV, tr, cols), lambda i: (0, i, 0))],
        out_specs=pl.BlockSpec((tr, cols), lambda i: (i, 0)),
        out_shape=S((rows, cols), F32),
        compiler_params=_cparams(1),
    )(slots)


def _adamw(name, w, g, m, v):
    rows, cols = w.shape
    tr = rows if rows * cols * 4 <= (2 << 20) else _row_tile(rows, 1 << ((1 << 18) // cols).bit_length() - 1)

    def body(w_ref, g_ref, m_ref, v_ref, d_ref, nm_ref, nv_ref):
        gv = g_ref[...]
        nm = ADAM_B1 * m_ref[...] + (1.0 - ADAM_B1) * gv
        nv = ADAM_B2 * v_ref[...] + (1.0 - ADAM_B2) * (gv * gv)
        m_hat = nm / (1.0 - ADAM_B1**ADAM_STEP)
        v_hat = nv / (1.0 - ADAM_B2**ADAM_STEP)
        d_ref[...] = -ADAM_LR * (m_hat / (jnp.sqrt(v_hat) + ADAM_EPS) + ADAM_WD * w_ref[...])
        nm_ref[...] = nm
        nv_ref[...] = nv

    blk = pl.BlockSpec((tr, cols), lambda i: (i, 0))
    return pl.pallas_call(
        body,
        name=name,
        grid=(rows // tr,),
        in_specs=[blk] * 4,
        out_specs=[blk] * 3,
        out_shape=[S((rows, cols), F32)] * 3,
        compiler_params=_cparams(1),
    )(w, g, m, v)


_BIG = ("w_in", "w_out", "w_ff1", "w_ff2")
_TRANSPOSED = ("w_in", "w_ff1")
_SMALL = ("b_in", "w_pool", "pool_scale", "sgu_ln_g", "sgu_ln_b", "sgu_w", "sgu_b", "conv_b", "conv_ln_g", "conv_ln_b",
          "b_out", "ln1_g", "ln1_b", "b_ff1", "b_ff2", "ln2_g", "ln2_b")
_WEIGHTS = ("w_in", "b_in", "w_pool", "pool_scale", "sgu_ln_g", "sgu_ln_b", "sgu_w", "sgu_b", "conv_w", "conv_b", "conv_ln_g",
            "conv_ln_b", "w_out", "b_out", "ln1_g", "ln1_b", "w_ff1", "b_ff1", "w_ff2", "b_ff2", "ln2_g", "ln2_b")


def _pack(arrays):
    parts = []
    for a in arrays:
        rows = a.reshape(-1, 128)
        parts.append(jnp.pad(rows, ((0, -rows.shape[0] % 8), (0, 0))))
    return jnp.concatenate(parts, axis=0)


def _unpack(flat, like):
    out, at = [], 0
    for a in like:
        n = a.size // 128
        out.append(flat[at : at + n].reshape(a.shape))
        at += n + (-n % 8)
    return out


def _packed_rows(arrays):
    return sum(a.size // 128 + (-(a.size // 128) % 8) for a in arrays)


def kernel(x, w_in, b_in, w_pool, pool_scale, sgu_ln_g, sgu_ln_b, sgu_w, sgu_b, conv_w, conv_b, conv_ln_g, conv_ln_b, w_out, b_out, ln1_g, ln1_b, w_ff1, b_ff1, w_ff2, b_ff2, ln2_g, ln2_b, loss_target, m_w_in, m_b_in, m_w_pool, m_pool_scale, m_sgu_ln_g, m_sgu_ln_b, m_sgu_w, m_sgu_b, m_conv_w, m_conv_b, m_conv_ln_g, m_conv_ln_b, m_w_out, m_b_out, m_ln1_g, m_ln1_b, m_w_ff1, m_b_ff1, m_w_ff2, m_b_ff2, m_ln2_g, m_ln2_b, v_w_in, v_b_in, v_w_pool, v_pool_scale, v_sgu_ln_g, v_sgu_ln_b, v_sgu_w, v_sgu_b, v_conv_w, v_conv_b, v_conv_ln_g, v_conv_ln_b, v_w_out, v_b_out, v_ln1_g, v_ln1_b, v_w_ff1, v_b_ff1, v_w_ff2, v_b_ff2, v_ln2_g, v_ln2_b):
    w = dict(w_in=w_in, b_in=b_in, w_pool=w_pool, pool_scale=pool_scale, sgu_ln_g=sgu_ln_g, sgu_ln_b=sgu_ln_b, sgu_w=sgu_w,
             sgu_b=sgu_b, conv_w=conv_w, conv_b=conv_b, conv_ln_g=conv_ln_g, conv_ln_b=conv_ln_b, w_out=w_out, b_out=b_out,
             ln1_g=ln1_g, ln1_b=ln1_b, w_ff1=w_ff1, b_ff1=b_ff1, w_ff2=w_ff2, b_ff2=b_ff2, ln2_g=ln2_g, ln2_b=ln2_b)
    mom = dict(w_in=m_w_in, b_in=m_b_in, w_pool=m_w_pool, pool_scale=m_pool_scale, sgu_ln_g=m_sgu_ln_g, sgu_ln_b=m_sgu_ln_b,
               sgu_w=m_sgu_w, sgu_b=m_sgu_b, conv_w=m_conv_w, conv_b=m_conv_b, conv_ln_g=m_conv_ln_g, conv_ln_b=m_conv_ln_b,
               w_out=m_w_out, b_out=m_b_out, ln1_g=m_ln1_g, ln1_b=m_ln1_b, w_ff1=m_w_ff1, b_ff1=m_b_ff1, w_ff2=m_w_ff2,
               b_ff2=m_b_ff2, ln2_g=m_ln2_g, ln2_b=m_ln2_b)
    var = dict(w_in=v_w_in, b_in=v_b_in, w_pool=v_w_pool, pool_scale=v_pool_scale, sgu_ln_g=v_sgu_ln_g, sgu_ln_b=v_sgu_ln_b,
               sgu_w=v_sgu_w, sgu_b=v_sgu_b, conv_w=v_conv_w, conv_b=v_conv_b, conv_ln_g=v_conv_ln_g, conv_ln_b=v_conv_ln_b,
               w_out=v_w_out, b_out=v_b_out, ln1_g=v_ln1_g, ln1_b=v_ln1_b, w_ff1=v_w_ff1, b_ff1=v_b_ff1, w_ff2=v_w_ff2,
               b_ff2=v_b_ff2, ln2_g=v_ln2_g, ln2_b=v_ln2_b)
    T = x.shape[1]
    x0 = x.reshape(T, D_MODEL)
    target = loss_target.reshape(T, D_MODEL)
    me_lin = _lin(_place())

    shard = [
        {name: (w[name][l].T if name in _TRANSPOSED else w[name][l]).astype(BF16) for name in _BIG} for l in range(DEPTH)
    ]
    conv_shard = jnp.pad(conv_w, ((0, 0), (0, 1), (0, 128 - conv_w.shape[2]))).reshape(DEPTH * 32, 128)

    def rows_of(g):
        return g.reshape(N_DEV * g.shape[1], g.shape[2])

    first = _all_gather([shard[0]["w_in"], shard[0]["w_out"], conv_shard])
    full = [{} for _ in range(DEPTH)]
    full[0]["w_in"], full[0]["w_out"] = rows_of(first[0]), rows_of(first[1])
    conv_cols = conv_w.shape[2]
    conv_full = first[2].reshape(N_DEV, DEPTH, 32, 128)[:, :, :CONV_KERNEL, :conv_cols]
    conv_full = conv_full.transpose(1, 2, 0, 3).reshape(DEPTH, CONV_KERNEL, N_DEV * conv_cols)

    tril = jnp.tril(jnp.ones((GROUP, GROUP), F32))
    prm = []
    for l in range(DEPTH):
        wm = sgu_w[l] * tril
        prm.append(dict(
            wp=w_pool[l].reshape(POOL_WIDTH, GROUP).astype(BF16), ps=_row(pool_scale[l]), lg=_row(sgu_ln_g[l]), lb=_row(sgu_ln_b[l]),
            wm=wm.reshape(SGU_WIDTH, GROUP).astype(BF16), wmt=wm.transpose(0, 2, 1).reshape(SGU_WIDTH, GROUP).astype(BF16),
            bsf=jnp.repeat(sgu_b[l].T, GROUP, axis=1), cw8=jnp.repeat(jnp.pad(conv_full[l], ((0, 1), (0, 0))), 8, axis=0), cb=_row(conv_b[l]),
            cg=_row(conv_ln_g[l]), cbeta=_row(conv_ln_b[l]),
        ))

    saved = []
    res = (x0, jnp.ones((D_MODEL,), F32), jnp.zeros((D_MODEL,), F32))
    xbf = x0.astype(BF16)
    n_ff = shard[0]["w_ff1"].shape[0]
    thirds = [(0, n_ff // 48 * 16), (n_ff // 48 * 16, n_ff // 48 * 16), (n_ff // 48 * 32, n_ff - n_ff // 48 * 32)]

    def ff1_part(l, k, into):
        return _Exchange([("gather", shard[l]["w_ff1"], thirds[k], into)])

    for l in range(DEPTH):
        f = full[l]
        ex = ff1_part(l, 0, None)
        proj = _mm_bias(f"proj{l}", xbf, f["w_in"], "nt", b_in[l], 1024, 896, 2048, side=ex)
        ex = ff1_part(l, 1, ex.results[0])
        mixed = _mixer_fwd(f"mixer_fwd{l}", proj, prm[l], 256, side=ex)
        ex = ff1_part(l, 2, ex.results[0])
        xh1, rs1, x1bf = _mm_ln(f"out_ln1_{l}", mixed, f["w_out"], b_out[l], res, ln1_g[l], ln1_b[l], 512, 512, side=ex)
        f["w_ff1"] = rows_of(ex.results[0])
        ex = _Exchange([("gather", shard[l]["w_ff2"])])
        act, hsq = _mm_relu2(f"ff1_{l}", x1bf, f["w_ff1"], "nt", b_ff1[l], 1024, 1024, 2048, side=ex)
        f["w_ff2"] = rows_of(ex.results[0])
        ex = _Exchange([("gather", shard[l + 1]["w_in"]), ("gather", shard[l + 1]["w_out"])]) if l + 1 < DEPTH else None
        xh2, rs2, x2bf = _mm_ln(
            f"ff2_ln2_{l}", hsq, f["w_ff2"], b_ff2[l], (xh1, ln1_g[l], ln1_b[l]), ln2_g[l], ln2_b[l], 1024, 1024, side=ex)
        if ex is not None:
            full[l + 1]["w_in"], full[l + 1]["w_out"] = rows_of(ex.results[0]), rows_of(ex.results[1])
        saved.append(dict(xin=xbf, proj=proj, mixed=mixed, xh1=xh1, rs1=rs1, x1bf=x1bf, act=act, hsq=hsq, xh2=xh2, rs2=rs2))
        res = (xh2, ln2_g[l], ln2_b[l])
        xbf = x2bf

    top = saved[-1]
    dr2, dr2bf, g_ln2g, g_ln2b, g_bff2, loss_row = _loss_top(top["xh2"], top["rs2"], ln2_g[-1], ln2_b[-1], target, 256)
    loss = lax.psum(loss_row[0, 0], ("x", "y", "c"))
    slots = [{} for _ in range(DEPTH)]
    gsm = [{} for _ in range(DEPTH)]
    grad_x = small_slots = None

    def stacked_small():
        st = {name: jnp.stack([gsm[gl][name].reshape(w[name].shape[1:]) for gl in range(DEPTH)]) for name in _SMALL}
        conv_g = jnp.pad(jnp.stack([gsm[gl]["conv_w"] for gl in range(DEPTH)]), ((0, 0), (0, 1), (0, 0)))
        return [st[name] for name in _SMALL] + [conv_g]

    for l in reversed(range(DEPTH)):
        f, sv = full[l], saved[l]
        gsm[l].update(ln2_g=g_ln2g, ln2_b=g_ln2b, b_ff2=g_bff2)
        gw = _mm_wgrad(f"gw_ff2_{l}", sv["hsq"], dr2bf, 1024, 2048)
        ex = _Exchange([("slices", gw)])
        dhpre, g_bff1 = _mm_dh(f"dff1_{l}", dr2bf, f["w_ff2"], sv["act"], 1024, 1024, 2048, side=ex)
        slots[l]["w_ff2"] = ex.results[0]
        gsm[l]["b_ff1"] = g_bff1
        gw = _mm_wgrad(f"gw_ff1_{l}", dhpre, sv["x1bf"], 1024, 2048)
        ex = _Exchange([("slices", gw)])
        dr1, dr1bf, g_ln1g, g_ln1b, g_bout = _mm_ln_bwd(
            f"dx1_ln1_{l}", dhpre, f["w_ff1"], dr2, sv["xh1"], sv["rs1"], ln1_g[l], 1024, 1024, side=ex)
        slots[l]["w_ff1"] = ex.results[0]
        gsm[l].update(ln1_g=g_ln1g, ln1_b=g_ln1b, b_out=g_bout)
        gw = _mm_wgrad(f"gw_out_{l}", sv["mixed"], dr1bf, 1024, 2048)
        ex = _Exchange([("slices", gw)])
        dmix = _mm_plain(f"dmixed{l}", dr1bf, f["w_out"], "nt", 1024, 1024, 2048, side=ex)
        slots[l]["w_out"] = ex.results[0]
        (dproj, g_wp, g_ps, g_lg, g_lb, g_wm, g_bs, g_cw, g_cb, g_cg, g_cbeta, g_bin) = _mixer_bwd(
            f"mixer_bwd{l}", sv["proj"], dmix, prm[l], 256)
        gsm[l].update(b_in=g_bin, w_pool=g_wp, pool_scale=g_ps, sgu_ln_g=g_lg, sgu_ln_b=g_lb, sgu_w=g_wm, sgu_b=g_bs[:SGU_HEADS],
                      conv_w=g_cw[:CONV_KERNEL], conv_b=g_cb, conv_ln_g=g_cg, conv_ln_b=g_cbeta)
        if l > 0:
            gw = _mm_wgrad(f"gw_in_{l}", dproj, sv["xin"], 896, 2048)
            below = saved[l - 1]
            ex = _Exchange([("slices", gw)])
            dr2, dr2bf, g_ln2g, g_ln2b, g_bff2 = _mm_ln_bwd(
                f"dx_ln2_{l}", dproj, f["w_in"], dr1, below["xh2"], below["rs2"], ln2_g[l - 1], 512, 896, side=ex)
        else:
            small_like = stacked_small()
            ex = _Exchange([("gather", _pack(small_like))])
            gw = _mm_wgrad(f"gw_in_{l}", dproj, sv["xin"], 896, 2048, side=ex)
            small_slots = ex.results[0]
            ex = _Exchange([("slices", gw)])
            grad_x = _mm_res("dx0", dproj, f["w_in"], dr1, 512, 896, side=ex)
        slots[l]["w_in"] = ex.results[0]

    grads, deltas, new_m, new_v = {}, {}, {}, {}
    for name in _BIG:
        per_layer = []
        for l in range(DEPTH):
            g = _sum_slots(f"sum_{name}_{l}", slots[l][name])
            per_layer.append(g.T if name in _TRANSPOSED else g)
        g = jnp.stack(per_layer)
        shape = w[name].shape
        two_d = (shape[0] * shape[1], shape[2])
        d, nm, nv = _adamw(f"adamw_{name}", w[name].reshape(two_d), g.reshape(two_d), mom[name].reshape(two_d), var[name].reshape(two_d))
        grads[name], deltas[name], new_m[name], new_v[name] = g, d.reshape(shape), nm.reshape(shape), nv.reshape(shape)

    total = _sum_slots("sum_small", small_slots)
    small_g = _unpack(total, small_like)
    like = [w[name] for name in _SMALL]
    d, nm, nv = _adamw("adamw_small", _pack(like), total[: _packed_rows(like)],
                       _pack([mom[name] for name in _SMALL]), _pack([var[name] for name in _SMALL]))
    for name, gg, dd, mm_, vv in zip(_SMALL, small_g, _unpack(d, like), _unpack(nm, like), _unpack(nv, like)):
        grads[name], deltas[name], new_m[name], new_v[name] = gg, dd, mm_, vv
    conv_g = lax.dynamic_slice_in_dim(small_g[-1][:, :CONV_KERNEL, :], me_lin * conv_cols, conv_cols, axis=2)
    flat = (DEPTH * CONV_KERNEL, conv_cols)
    d, nm, nv = _adamw("adamw_conv_w", conv_w.reshape(flat), conv_g.reshape(flat), m_conv_w.reshape(flat), v_conv_w.reshape(flat))
    grads["conv_w"], deltas["conv_w"], new_m["conv_w"], new_v["conv_w"] = conv_g, d.reshape(conv_w.shape), nm.reshape(conv_w.shape), nv.reshape(conv_w.shape)

    return (loss, grad_x.reshape(x.shape), *[grads[n] for n in _WEIGHTS], *[deltas[n] for n in _WEIGHTS],
            *[new_m[n] for n in _WEIGHTS], *[new_v[n] for n in _WEIGHTS])
```

```python
import functools

import jax
import jax.numpy as jnp
from jax import lax
from jax.experimental import pallas as pl
from jax.experimental.pallas import tpu as pltpu

F32, BF16 = jnp.float32, jnp.bfloat16
S = jax.ShapeDtypeStruct

DEPTH = 2
D_MODEL = 2048
POOL_WINDOWS = (2, 4, 8, 16)
POOL_WIDTH = 512
GROUP = 128
SGU_WIDTH = 768
SGU_HEADS = 6
CONV_WIDTH = 768
CONV_KERNEL = 31
IN_WIDTH = 3584
D_FF = 8192
ALPHA = (2 * DEPTH) ** 0.25
LN_EPS = 1e-5
ADAM_LR, ADAM_B1, ADAM_B2, ADAM_EPS, ADAM_WD, ADAM_STEP = 0.001, 0.9, 0.999, 1e-08, 0.01, 10

N_DEV = 8
LN_STRIP = 32
LN_UNROLL = 4
HALO = 32
VMEM_LIMIT = 56 << 20
MESH = pl.DeviceIdType.MESH

C_POOL = (0, 512)
C_U = (512, 1280)
C_V = (1280, 2048)
C_CA = (2048, 2816)
C_CG = (2816, 3584)
M_POOL = (0, 512)
M_SGU = (512, 1280)
M_CONV = (1280, 2048)


def _cparams(n_axes):
    return pltpu.CompilerParams(dimension_semantics=("arbitrary",) * n_axes, vmem_limit_bytes=VMEM_LIMIT)


def _for_strips(rows, strip, fn, unroll=1):
    n = rows // strip
    if n == 1:
        fn(0)
        return

    def step(s, carry):
        fn(pl.multiple_of(s * strip, strip))
        return carry

    lax.fori_loop(0, n, step, 0, unroll=unroll)


def _row_sum(x):
    return jnp.sum(x, axis=0, keepdims=True)


def _ln_stats(r):
    mu = jnp.mean(r, axis=-1, keepdims=True)
    xc = r - mu
    var = jnp.mean(xc * xc, axis=-1, keepdims=True)
    rs = lax.rsqrt(var + LN_EPS)
    return xc * rs, rs


def _ln_bwd(dy, xhat, rs, g):
    gy = dy * g
    m1 = jnp.mean(gy, axis=-1, keepdims=True)
    m2 = jnp.mean(gy * xhat, axis=-1, keepdims=True)
    return rs * (gy - m1 - xhat * m2)


_GELU_C = 0.7978845608028654


def _gelu(x):
    th = jnp.tanh(_GELU_C * (x + 0.044715 * (x * x * x)))
    return 0.5 * x * (1.0 + th), th


def _gelu_grad(x, th):
    return 0.5 * (1.0 + th) + 0.5 * x * (1.0 - th * th) * (_GELU_C * (1.0 + 3.0 * 0.044715 * (x * x)))


def _place():
    x, y, c = lax.axis_index("x"), lax.axis_index("y"), lax.axis_index("c")
    return x, y, c


def _lin(p):
    return 4 * p[0] + 2 * p[1] + p[2]


def _flip(p, r):
    return tuple(1 - v if (r >> (2 - ax)) & 1 else v for ax, v in enumerate(p))


_ANY = pl.BlockSpec(memory_space=pl.ANY)


class _Exchange:
    def __init__(self, items):
        self.kinds = [item[0] for item in items]
        self.srcs = [item[1] for item in items]
        self.rows = [item[2] if len(item) > 2 else None for item in items]
        handed_on = [item[3] if len(item) > 3 else None for item in items]
        self.out_shape = [
            S((N_DEV, *x.shape), x.dtype) if kind == "gather" else S((N_DEV, x.shape[0] // N_DEV, x.shape[1]), x.dtype)
            for kind, x in zip(self.kinds, self.srcs)
        ]
        n = len(items)
        self.ins = self.srcs + [b for b in handed_on if b is not None]
        self.aliases = {}
        for a, b in enumerate(handed_on):
            if b is not None:
                self.aliases[n + len(self.aliases)] = a
        self.scratch = [pltpu.SemaphoreType.DMA((n, 7)), pltpu.SemaphoreType.DMA((n, 7)), pltpu.SemaphoreType.DMA((n,))]
        self.results = None

    def _src(self, in_refs, a, dest):
        if self.kinds[a] == "gather":
            return in_refs[a] if self.rows[a] is None else in_refs[a].at[pl.ds(*self.rows[a])]
        rows = self.srcs[a].shape[0] // N_DEV
        return in_refs[a].at[pl.ds(pl.multiple_of(_lin(dest) * rows, 8), rows)]

    def _dst(self, out_refs, a, slot):
        return out_refs[a].at[slot] if self.rows[a] is None else out_refs[a].at[slot, pl.ds(*self.rows[a])]

    def _copies(self, in_refs, out_refs, sems, with_arrivals):
        send_sems, recv_sems, local_sems = sems
        me = _place()
        local, sends, arrivals = [], [], []
        for a in range(len(self.srcs)):
            local.append(pltpu.make_async_copy(self._src(in_refs, a, me), self._dst(out_refs, a, _lin(me)), local_sems.at[a]))
            for r in range(1, N_DEV):
                peer = _flip(me, r)
                for slot, group in ((_lin(me), sends), (_lin(peer), arrivals)):
                    if group is sends or with_arrivals:
                        group.append(pltpu.make_async_remote_copy(
                            src_ref=self._src(in_refs, a, peer), dst_ref=self._dst(out_refs, a, slot),
                            send_sem=send_sems.at[a, r - 1], recv_sem=recv_sems.at[a, r - 1], device_id=peer, device_id_type=MESH,
                        ))
        return local, sends, arrivals

    def start(self, in_refs, out_refs, sems):
        local, sends, _ = self._copies(in_refs, out_refs, sems, False)
        for cp in local + sends:
            cp.start()

    def wait(self, in_refs, out_refs, sems):
        local, sends, arrivals = self._copies(in_refs, out_refs, sems, True)
        for cp in arrivals:
            cp.wait_recv()
        for cp in sends:
            cp.wait_send()
        for cp in local:
            cp.wait()


def _call(name, body, grid, in_specs, out_specs, out_shape, scratch, args, side=None):
    in_specs, out_specs, out_shape, scratch = list(in_specs), list(out_specs), list(out_shape), list(scratch)
    if side is None:
        return pl.pallas_call(
            body, name=name, grid=grid, in_specs=in_specs, out_specs=out_specs, out_shape=out_shape, scratch_shapes=scratch,
            compiler_params=_cparams(len(grid)),
        )(*args)
    n_in, n_out, n_scr = len(in_specs), len(out_specs), len(scratch)
    s_in, s_out = len(side.ins), len(side.out_shape)

    def wrapped(*refs):
        at = 0
        parts = []
        for n in (n_in, s_in, n_out, s_out, n_scr, 3):
            parts.append(refs[at : at + n])
            at += n
        ins, side_ins, outs, side_outs, scr, sems = parts
        pids = [pl.program_id(d) for d in range(len(grid))]
        first = functools.reduce(jnp.logical_and, [p == 0 for p in pids])
        last = functools.reduce(jnp.logical_and, [p == g - 1 for p, g in zip(pids, grid)])

        @pl.when(first)
        def _():
            side.start(side_ins, side_outs, sems)

        body(*ins, *outs, *scr)

        @pl.when(last)
        def _():
            side.wait(side_ins, side_outs, sems)

    res = pl.pallas_call(
        wrapped, name=name, grid=grid, in_specs=in_specs + [_ANY] * s_in, out_specs=out_specs + [_ANY] * s_out,
        out_shape=out_shape + side.out_shape, scratch_shapes=scratch + side.scratch, compiler_params=_cparams(len(grid)),
        input_output_aliases={n_in + i: n_out + o for i, o in side.aliases.items()},
    )(*args, *side.ins)
    side.results = list(res[n_out:])
    return list(res[:n_out])


_CONTRACT = {"nn": ((1,), (0,)), "nt": ((1,), (1,)), "tn": ((0,), (0,))}


def _mm(name, a, b, dims, tm, tn, tk, *, ins=(), outs, epilogue, j_outer=False, side=None):
    if dims == "tn":
        K, M = a.shape
    else:
        M, K = a.shape
    N = b.shape[0] if dims == "nt" else b.shape[1]
    tm, tn, tk = min(tm, M), min(tn, N), min(tk, K)
    assert M % tm == 0 and N % tn == 0 and K % tk == 0, (name, M, N, K, tm, tn, tk)
    nm, nn, nk = M // tm, N // tn, K // tk
    if j_outer:
        grid = (nn, nm, nk)
        ij = lambda g0, g1: (g1, g0)
    else:
        grid = (nm, nn, nk)
        ij = lambda g0, g1: (g0, g1)

    def amap(g0, g1, k):
        i, _ = ij(g0, g1)
        return (k, i) if dims == "tn" else (i, k)

    def bmap(g0, g1, k):
        _, j = ij(g0, g1)
        return (j, k) if dims == "nt" else (k, j)

    def spec(kind):
        if kind == "tile":
            return pl.BlockSpec((tm, tn), lambda g0, g1, k: ij(g0, g1))
        if kind == "row":
            return pl.BlockSpec((1, tn), lambda g0, g1, k: (0, ij(g0, g1)[1]))
        assert kind == "col", kind
        return pl.BlockSpec((tm, 1), lambda g0, g1, k: (ij(g0, g1)[0], 0))

    in_specs = [
        pl.BlockSpec((tk, tm) if dims == "tn" else (tm, tk), amap),
        pl.BlockSpec((tn, tk) if dims == "nt" else (tk, tn), bmap),
    ] + [spec(kind) for _, kind in ins]
    out_specs = [spec(kind) for _, _, kind in outs]
    out_shape = [S(shape, dtype) for shape, dtype, _ in outs]
    n_in, n_out = len(ins), len(outs)
    contract = (_CONTRACT[dims], ((), ()))

    def body(*refs):
        a_ref, b_ref = refs[:2]
        in_refs = refs[2 : 2 + n_in]
        out_refs = refs[2 + n_in : 2 + n_in + n_out]
        acc = refs[2 + n_in + n_out]
        i, _ = ij(pl.program_id(0), pl.program_id(1))
        k = pl.program_id(2)

        def part():
            return lax.dot_general(a_ref[...], b_ref[...], contract, preferred_element_type=F32)

        @pl.when(k == 0)
        def _():
            acc[...] = part()

        @pl.when(k > 0)
        def _():
            acc[...] += part()

        @pl.when(k == nk - 1)
        def _():
            epilogue(i, acc, in_refs, out_refs)

    return _call(name, body, grid, in_specs, out_specs, out_shape, [pltpu.VMEM((tm, tn), F32)], [a, b, *[x for x, _ in ins]], side)


def _row(v):
    return v.reshape(1, -1)


def _mm_bias(name, a, b, dims, bias, tm, tn, tk, side=None):
    M = a.shape[0]
    N = b.shape[0] if dims == "nt" else b.shape[1]

    def epilogue(i, acc, ins, outs):
        def strip(r0):
            rows = pl.ds(r0, 128)
            outs[0][rows, :] = acc[rows, :] + ins[0][...]

        _for_strips(acc.shape[0], 128, strip)

    return _mm(name, a, b, dims, tm, tn, tk, ins=[(_row(bias), "row")], outs=[((M, N), F32, "tile")], epilogue=epilogue, side=side)[0]


def _mm_relu2(name, a, b, dims, bias, tm, tn, tk, side=None):
    M = a.shape[0]
    N = b.shape[0] if dims == "nt" else b.shape[1]

    def epilogue(i, acc, ins, outs):
        def strip(r0):
            rows = pl.ds(r0, 128)
            r = jnp.maximum(acc[rows, :] + ins[0][...], 0.0)
            outs[0][rows, :] = r.astype(BF16)
            outs[1][rows, :] = (r * r).astype(BF16)

        _for_strips(acc.shape[0], 128, strip)

    return _mm(
        name, a, b, dims, tm, tn, tk, ins=[(_row(bias), "row")],
        outs=[((M, N), BF16, "tile"), ((M, N), BF16, "tile")], epilogue=epilogue, side=side,
    )


def _mm_lagged(name, a, b, tm, tk, *, ins, outs, strip_fn, init_fn=None, side=None):
    M, K = a.shape
    N = b.shape[1]
    tm, tk = min(tm, M), min(tk, K)
    assert M % tm == 0 and K % tk == 0, (name, M, K, tm, tk)
    nm, nk = M // tm, K // tk
    assert nk >= 2, (name, nk)
    parts = 1 << ((nk - 1).bit_length() - 1)
    rows_p = tm // parts
    assert rows_p % LN_STRIP == 0, (name, rows_p)

    def part_index(i, k):
        return jnp.maximum((i - 1) * parts + jnp.minimum(k, parts - 1), 0)

    def spec(kind):
        if kind == "tile":
            return pl.BlockSpec((rows_p, N), lambda i, k: (part_index(i, k), 0))
        if kind == "row":
            return pl.BlockSpec((1, N), lambda i, k: (0, 0))
        assert kind == "col", kind
        return pl.BlockSpec((rows_p, 1), lambda i, k: (part_index(i, k), 0))

    in_specs = [
        pl.BlockSpec((tm, tk), lambda i, k: (jnp.minimum(i, nm - 1), k)),
        pl.BlockSpec((tk, N), lambda i, k: (jnp.where(i < nm, k, nk - 1), 0)),
    ] + [spec(kind) for _, kind in ins]
    n_in, n_out = len(ins), len(outs)

    def body(*refs):
        a_ref, b_ref = refs[:2]
        in_refs = refs[2 : 2 + n_in]
        out_refs = refs[2 + n_in : 2 + n_in + n_out]
        acc, fin = refs[2 + n_in + n_out :]
        i, k = pl.program_id(0), pl.program_id(1)

        def part():
            return jnp.dot(a_ref[...], b_ref[...], preferred_element_type=F32)

        def epilogue_part():
            base = k * rows_p
            for s in range(rows_p // LN_STRIP):
                acc_rows = fin[pl.ds(pl.multiple_of(base + s * LN_STRIP, LN_STRIP), LN_STRIP), :]
                strip_fn(acc_rows, in_refs, out_refs, pl.ds(s * LN_STRIP, LN_STRIP))

        has_dot = i < nm
        has_epilogue = jnp.logical_and(i > 0, k < parts)
        no_epilogue = jnp.logical_not(has_epilogue)
        last = k == nk - 1
        both = jnp.logical_and(has_dot, has_epilogue)
        alone = jnp.logical_and(has_dot, no_epilogue)

        if init_fn is not None:
            @pl.when(jnp.logical_and(i == 1, k == 0))
            def _():
                init_fn(out_refs)

        @pl.when(jnp.logical_and(alone, k == 0))
        def _():
            acc[...] = part()

        @pl.when(jnp.logical_and(alone, jnp.logical_and(k > 0, jnp.logical_not(last))))
        def _():
            acc[...] += part()

        @pl.when(jnp.logical_and(has_dot, last))
        def _():
            fin[...] = acc[...] + part()

        @pl.when(jnp.logical_and(both, k == 0))
        def _():
            acc[...] = part()
            epilogue_part()

        if parts > 1:
            @pl.when(jnp.logical_and(both, k > 0))
            def _():
                acc[...] += part()
                epilogue_part()

        @pl.when(jnp.logical_and(jnp.logical_not(has_dot), has_epilogue))
        def _():
            epilogue_part()

    return _call(
        name, body, (nm + 1, nk), in_specs, [spec(kind) for _, _, kind in outs], [S(shape, dtype) for shape, dtype, _ in outs],
        [pltpu.VMEM((tm, N), F32), pltpu.VMEM((tm, N), F32)], [a, b, *[x for x, _ in ins]], side,
    )


def _mm_ln(name, a, b, bias, res, g, beta, tm, tk, side=None):
    M = a.shape[0]
    N = b.shape[1]
    rxh, rg, rb = res

    def strip(acc_rows, ins, outs, rows):
        bias_r, rxh_r, rg_r, rb_r, g_r, beta_r = ins
        xhat_o, rstd_o, xbf_o = outs
        resid = rxh_r[rows, :] * rg_r[...] + rb_r[...]
        r = ALPHA * resid + (acc_rows + bias_r[...])
        xhat, rs = _ln_stats(r)
        xhat_o[rows, :] = xhat
        rstd_o[rows, :] = rs
        xbf_o[rows, :] = (xhat * g_r[...] + beta_r[...]).astype(BF16)

    return _mm_lagged(
        name, a, b, tm, tk,
        ins=[(_row(bias), "row"), (rxh, "tile"), (_row(rg), "row"), (_row(rb), "row"), (_row(g), "row"), (_row(beta), "row")],
        outs=[((M, N), F32, "tile"), ((M, 1), F32, "col"), ((M, N), BF16, "tile")],
        strip_fn=strip, side=side,
    )


def _ln_bwd_strip(dyv, xhat, rs, g, dr_o, drbf_o, dg_o, db_o, dsum_o, rows):
    dr = _ln_bwd(dyv, xhat, rs, g)
    dr_o[rows, :] = dr
    drbf_o[rows, :] = dr.astype(BF16)
    dg_o[...] += _row_sum(dyv * xhat)
    db_o[...] += _row_sum(dyv)
    dsum_o[...] += _row_sum(dr)


def _mm_ln_bwd(name, a, b, resgrad, xhat, rstd, g, tm, tk, side=None):
    M = a.shape[0]
    N = b.shape[1]

    def init(outs):
        for o in outs[2:]:
            o[...] = jnp.zeros_like(o)

    def strip(acc_rows, ins, outs, rows):
        rg_r, xh_r, rs_r, g_r = ins
        dyv = acc_rows + ALPHA * rg_r[rows, :]
        _ln_bwd_strip(dyv, xh_r[rows, :], rs_r[rows, :], g_r[...], *outs, rows)

    return _mm_lagged(
        name, a, b, tm, tk,
        ins=[(resgrad, "tile"), (xhat, "tile"), (rstd, "col"), (_row(g), "row")],
        outs=[((M, N), F32, "tile"), ((M, N), BF16, "tile"), ((1, N), F32, "row"), ((1, N), F32, "row"), ((1, N), F32, "row")],
        strip_fn=strip, init_fn=init, side=side,
    )


def _mm_dh(name, a, b, act, tm, tn, tk, side=None):
    M = a.shape[0]
    N = b.shape[0]

    def epilogue(i, acc, ins, outs):
        @pl.when(i == 0)
        def _():
            outs[1][...] = jnp.zeros_like(outs[1])

        def strip(r0):
            rows = pl.ds(r0, 128)
            d = acc[rows, :] * (2.0 * ins[0][rows, :].astype(F32))
            outs[0][rows, :] = d.astype(BF16)
            outs[1][...] += _row_sum(d)

        _for_strips(acc.shape[0], 128, strip)

    return _mm(
        name, a, b, "nt", tm, tn, tk, ins=[(act, "tile")],
        outs=[((M, N), BF16, "tile"), ((1, N), F32, "row")], epilogue=epilogue, j_outer=True, side=side,
    )


def _mm_plain(name, a, b, dims, tm, tn, tk, side=None):
    M = a.shape[0]
    N = b.shape[0] if dims == "nt" else b.shape[1]

    def epilogue(i, acc, ins, outs):
        def strip(r0):
            rows = pl.ds(r0, 128)
            outs[0][rows, :] = acc[rows, :]

        _for_strips(acc.shape[0], 128, strip)

    return _mm(name, a, b, dims, tm, tn, tk, outs=[((M, N), F32, "tile")], epilogue=epilogue, side=side)[0]


def _mm_res(name, a, b, res, tm, tk, side=None):
    def strip(acc_rows, ins, outs, rows):
        outs[0][rows, :] = acc_rows + ALPHA * ins[0][rows, :]

    return _mm_lagged(
        name, a, b, tm, tk, ins=[(res, "tile")], outs=[((a.shape[0], b.shape[1]), F32, "tile")], strip_fn=strip, side=side,
    )[0]


def _mm_wgrad(name, a, b, tm, tk, side=None):
    M = a.shape[1]
    N = b.shape[1]

    def epilogue(i, acc, ins, outs):
        def strip(r0):
            rows = pl.ds(r0, 128)
            outs[0][rows, :] = acc[rows, :].astype(BF16)

        _for_strips(acc.shape[0], 128, strip)

    return _mm(name, a, b, "tn", tm, N, tk, outs=[((M, N), BF16, "tile")], epilogue=epilogue, side=side)[0]


def _loss_top(xhat, rstd, g, beta, target, tm):
    T, D = xhat.shape
    tm = min(tm, T)
    nt = T // tm

    def body(xh_r, rs_r, g_r, b_r, t_r, dr_o, drbf_o, dg_o, db_o, dsum_o, loss_o, sq_acc):
        i = pl.program_id(0)

        @pl.when(i == 0)
        def _():
            dg_o[...] = jnp.zeros_like(dg_o)
            db_o[...] = jnp.zeros_like(db_o)
            dsum_o[...] = jnp.zeros_like(dsum_o)
            sq_acc[...] = jnp.zeros_like(sq_acc)

        def strip(r0):
            rows = pl.ds(r0, LN_STRIP)
            xh = xh_r[rows, :]
            err = (xh * g_r[...] + b_r[...]) - t_r[rows, :]
            sq_acc[...] += _row_sum(err * err)
            _ln_bwd_strip(err * (1.0 / D), xh, rs_r[rows, :], g_r[...], dr_o, drbf_o, dg_o, db_o, dsum_o, rows)

        _for_strips(tm, LN_STRIP, strip, unroll=LN_UNROLL)

        @pl.when(i == nt - 1)
        def _():
            total = jnp.sum(sq_acc[...], axis=-1, keepdims=True) * (0.5 / D)
            loss_o[...] = jnp.broadcast_to(total, loss_o.shape)

    tile = pl.BlockSpec((tm, D), lambda i: (i, 0))
    row = pl.BlockSpec((1, D), lambda i: (0, 0))
    return pl.pallas_call(
        body,
        name="loss_top",
        grid=(nt,),
        in_specs=[tile, pl.BlockSpec((tm, 1), lambda i: (i, 0)), row, row, tile],
        out_specs=[tile, tile, row, row, row, pl.BlockSpec((1, 128), lambda i: (0, 0))],
        out_shape=[S((T, D), F32), S((T, D), BF16), S((1, D), F32), S((1, D), F32), S((1, D), F32), S((1, 128), F32)],
        scratch_shapes=[pltpu.VMEM((1, D), F32)],
        compiler_params=_cparams(1),
    )(xhat, rstd, _row(g), _row(beta), target)


def _cols(ref, c):
    return ref[:, c[0] : c[1]]


def _causal_window_sum(e, w):
    s, sh = e, 1
    while sh < w:
        s = s + pltpu.roll(s, sh, axis=0)
        sh *= 2
    return s


def _anticausal_window_sum(d, w):
    n = d.shape[0]
    r, sh = d, 1
    while sh < w:
        r = r + pltpu.roll(r, n - sh, axis=0)
        sh *= 2
    return r


def _with_halo(halo_ref, main_ref, c, keep):
    return jnp.concatenate([_cols(halo_ref, c) * keep, _cols(main_ref, c)], axis=0)


def _pool_counts(tile_index, R, w):
    pos = lax.broadcasted_iota(jnp.int32, (R, 1), 0) + tile_index * R
    return jnp.minimum(pos + 1, w).astype(F32)


def _sgu_mix(wm_ref, vnb):
    return jnp.concatenate(
        [
            jnp.dot(wm_ref[h * GROUP : (h + 1) * GROUP, :], vnb[:, h * GROUP : (h + 1) * GROUP], preferred_element_type=F32)
            for h in range(SGU_HEADS)
        ],
        axis=1,
    )


CONV_HALVES = (slice(0, CONV_WIDTH // 2), slice(CONV_WIDTH // 2, CONV_WIDTH))
TAP_STRIP = 32
TAP_GROUP = 4


def _build_shifts(shf, src, cols, rows):
    n = rows - 8
    for r in range(1, 8):
        shf[r - 1, pl.ds(0, n), :] = src[pl.ds(r, n), cols]


def _shifted(shf, src, cols, offset, start, size):
    q, r = divmod(offset, 8)
    rows = pl.ds(pl.multiple_of(start + 8 * q, 8), size)
    return src[rows, cols] if r == 0 else shf[r - 1, rows, :]


def _conv_taps(shf, src, cw8, cols, offsets, n_rows, out, bias=None):
    width = cols.stop - cols.start

    def strip(s, carry):
        r0 = pl.multiple_of(s * TAP_STRIP, TAP_STRIP)
        acc = jnp.zeros((TAP_STRIP, width), F32)
        for k, o in enumerate(offsets):
            wk = cw8[pl.ds(8 * k, 8), cols]
            acc = acc + _shifted(shf, src, cols, o, r0, TAP_STRIP) * jnp.concatenate([wk] * (TAP_STRIP // 8), axis=0)
        if bias is not None:
            acc = acc + bias[:, cols]
        out[pl.ds(r0, TAP_STRIP), cols] = acc
        return carry

    lax.fori_loop(0, n_rows // TAP_STRIP, strip, 0)


def _conv_weight_grad(shf, src, dsrc, d_first, cols, offsets, n_rows, dcw):
    width = cols.stop - cols.start
    for k0 in range(0, len(offsets), TAP_GROUP):
        group = offsets[k0 : k0 + TAP_GROUP]

        def strip(s, accs, group=group):
            r0 = pl.multiple_of(s * TAP_STRIP, TAP_STRIP)
            d = dsrc[pl.ds(pl.multiple_of(d_first + r0, 8), TAP_STRIP), cols]
            out = []
            for acc8, o in zip(accs, group):
                p = _shifted(shf, src, cols, o, r0, TAP_STRIP) * d
                for j in range(TAP_STRIP // 8):
                    acc8 = acc8 + p[8 * j : 8 * j + 8, :]
                out.append(acc8)
            return tuple(out)

        accs = lax.fori_loop(0, n_rows // TAP_STRIP, strip, tuple(jnp.zeros((8, width), F32) for _ in group))
        for j, acc8 in enumerate(accs):
            dcw[pl.ds(k0 + j, 1), cols] += _row_sum(acc8)


def _mixer_params(p):
    return [p["wp"], p["ps"], p["lg"], p["lb"], p["wm"], p["wmt"], p["bsf"], p["cw8"], p["cb"], p["cg"], p["cbeta"]]


def _whole(x):
    return pl.BlockSpec(x.shape, lambda i: (0,) * x.ndim)


def _mixer_fwd(name, proj, p, R, side=None):
    T = proj.shape[0]
    R = min(R, T)
    E = R + HALO
    nt = T // R
    hb = R // HALO
    tap_offsets = [HALO - (CONV_KERNEL - 1) + k for k in range(CONV_KERNEL)]

    def body(pm, ph, wp, ps, lg, lb, wm, wmt, bsf, cw8, cb, cg, cbeta, out, hbuf, shf, convbuf):
        i = pl.program_id(0)
        keep = (i > 0).astype(F32)
        a_ext = _with_halo(ph, pm, C_POOL, keep)
        for gi, w in enumerate(POOL_WINDOWS):
            cs = slice(gi * GROUP, (gi + 1) * GROUP)
            e = a_ext[:, cs]
            s = _causal_window_sum(e, w)
            pooled = s[HALO:, :] / _pool_counts(i, R, w) - e[HALO:, :]
            z = jnp.dot(pooled.astype(BF16), wp[cs, :], preferred_element_type=F32)
            out[:, cs] = (z * ps[:, cs]).astype(BF16)
        u, _ = _gelu(_cols(pm, C_U))
        v, _ = _gelu(_cols(pm, C_V))
        vhat, _ = _ln_stats(v)
        vn = vhat * lg[...] + lb[...]
        for c in range(R // GROUP):
            rs = slice(c * GROUP, (c + 1) * GROUP)
            mixed = _sgu_mix(wm, vn[rs, :].astype(BF16)) + bsf[...]
            out[rs, M_SGU[0] : M_SGU[1]] = (u[rs, :] * mixed).astype(BF16)
        hbuf[...] = _with_halo(ph, pm, C_CA, keep) * jax.nn.sigmoid(_with_halo(ph, pm, C_CG, keep))
        for cols in CONV_HALVES:
            _build_shifts(shf, hbuf, cols, E)
            _conv_taps(shf, hbuf, cw8, cols, tap_offsets, R, convbuf, bias=cb)
        chat, _ = _ln_stats(convbuf[...])
        cn = chat * cg[...] + cbeta[...]
        out[:, M_CONV[0] : M_CONV[1]] = (cn * jax.nn.sigmoid(cn)).astype(BF16)

    params = _mixer_params(p)
    in_specs = [
        pl.BlockSpec((R, IN_WIDTH), lambda i: (i, 0)),
        pl.BlockSpec((HALO, IN_WIDTH), lambda i: (jnp.maximum(i * hb - 1, 0), 0)),
    ] + [_whole(x) for x in params]
    scratch = [pltpu.VMEM((E, CONV_WIDTH), F32), pltpu.VMEM((7, E, CONV_WIDTH // 2), F32), pltpu.VMEM((R, CONV_WIDTH), F32)]
    return _call(
        name, body, (nt,), in_specs, [pl.BlockSpec((R, D_MODEL), lambda i: (i, 0))], [S((T, D_MODEL), BF16)],
        scratch, [proj, proj, *params], side,
    )[0]


def _mixer_bwd(name, proj, dmix, p, R, side=None):
    T = proj.shape[0]
    R = min(R, T)
    E = R + HALO
    nt = T // R
    hb = R // HALO
    tap_offsets = [HALO - (CONV_KERNEL - 1) + k for k in range(CONV_KERNEL)]
    back_offsets = [HALO - o for o in tap_offsets]

    def body(pm, ph, dm, wp, ps, lg, lb, wm, wmt, bsf, cw8, cb, cg, cbeta,
             dproj, dwp, dps, dlg, dlb, dwm, dbs, dcw, dcb, dcg, dcbeta, dbin,
             hbuf, dbuf, carry_p, carry_c, dbs_acc, shf, convbuf, dhcbuf):
        step = pl.program_id(0)
        ti = nt - 1 - step
        keep = (ti > 0).astype(F32)

        @pl.when(step == 0)
        def _():
            for r in (dwp, dps, dlg, dlb, dwm, dcw, dcb, dcg, dcbeta, dbin, carry_p, carry_c, dbs_acc):
                r[...] = jnp.zeros_like(r)

        def tail(carry):
            return jnp.concatenate([jnp.zeros((R - HALO, carry.shape[1]), F32), carry], axis=0)

        def head(x):
            return jnp.concatenate([jnp.zeros((HALO, x.shape[1]), F32), x], axis=0)

        a_ext = _with_halo(ph, pm, C_POOL, keep)
        carry_in = carry_p[...]
        for gi, w in enumerate(POOL_WINDOWS):
            cs = slice(gi * GROUP, (gi + 1) * GROUP)
            e = a_ext[:, cs]
            s = _causal_window_sum(e, w)
            cnt = _pool_counts(ti, R, w)
            pooled_b = (s[HALO:, :] / cnt - e[HALO:, :]).astype(BF16)
            wg = wp[cs, :]
            z = jnp.dot(pooled_b, wg, preferred_element_type=F32)
            dya = dm[:, cs]
            dps[:, cs] += _row_sum(dya * z)
            dz_b = (dya * ps[:, cs]).astype(BF16)
            dwp[cs, :] += lax.dot_general(pooled_b, dz_b, (((0,), (0,)), ((), ())), preferred_element_type=F32)
            dpooled = lax.dot_general(dz_b, wg, (((1,), (1,)), ((), ())), preferred_element_type=F32)
            da_ext = _anticausal_window_sum(head(dpooled / cnt), w) - head(dpooled)
            carry_p[:, cs] = da_ext[:HALO, :]
            d_a = da_ext[HALO:, :] + tail(carry_in[:, cs])
            dbin[:, cs] += _row_sum(d_a)
            dproj[:, cs] = d_a.astype(BF16)

        pu = _cols(pm, C_U)
        pv = _cols(pm, C_V)
        u, thu = _gelu(pu)
        v, thv = _gelu(pv)
        vhat, vrs = _ln_stats(v)
        vn = vhat * lg[...] + lb[...]
        dyb = dm[:, M_SGU[0] : M_SGU[1]]
        du_parts, dvn_parts = [], []
        for c in range(R // GROUP):
            rs = slice(c * GROUP, (c + 1) * GROUP)
            vnb = vn[rs, :].astype(BF16)
            mixed = _sgu_mix(wm, vnb) + bsf[...]
            du_parts.append(dyb[rs, :] * mixed)
            dmixed = dyb[rs, :] * u[rs, :]
            dbs_acc[...] += dmixed
            dmb = dmixed.astype(BF16)
            dvn_h = []
            for h in range(SGU_HEADS):
                hs = slice(h * GROUP, (h + 1) * GROUP)
                dwm[hs, :] += lax.dot_general(dmb[:, hs], vnb[:, hs], (((1,), (1,)), ((), ())), preferred_element_type=F32)
                dvn_h.append(jnp.dot(wmt[hs, :], dmb[:, hs], preferred_element_type=F32))
            dvn_parts.append(jnp.concatenate(dvn_h, axis=1))
        du = jnp.concatenate(du_parts, axis=0) if len(du_parts) > 1 else du_parts[0]
        dvn = jnp.concatenate(dvn_parts, axis=0) if len(dvn_parts) > 1 else dvn_parts[0]
        dlg[...] += _row_sum(dvn * vhat)
        dlb[...] += _row_sum(dvn)
        d_pu = du * _gelu_grad(pu, thu)
        d_pv = _ln_bwd(dvn, vhat, vrs, lg[...]) * _gelu_grad(pv, thv)
        dbin[:, C_U[0] : C_U[1]] += _row_sum(d_pu)
        dbin[:, C_V[0] : C_V[1]] += _row_sum(d_pv)
        dproj[:, C_U[0] : C_U[1]] = d_pu.astype(BF16)
        dproj[:, C_V[0] : C_V[1]] = d_pv.astype(BF16)

        sg_ext = jax.nn.sigmoid(_with_halo(ph, pm, C_CG, keep))
        ca_ext = _with_halo(ph, pm, C_CA, keep)
        hbuf[...] = ca_ext * sg_ext
        for cols in CONV_HALVES:
            _build_shifts(shf, hbuf, cols, E)
            _conv_taps(shf, hbuf, cw8, cols, tap_offsets, R, convbuf, bias=cb)
        chat, crs = _ln_stats(convbuf[...])
        cn = chat * cg[...] + cbeta[...]
        sc = jax.nn.sigmoid(cn)
        dcn = dm[:, M_CONV[0] : M_CONV[1]] * (sc * (1.0 + cn * (1.0 - sc)))
        dcg[...] += _row_sum(dcn * chat)
        dcbeta[...] += _row_sum(dcn)
        dconv = _ln_bwd(dcn, chat, crs, cg[...])
        dcb[...] += _row_sum(dconv)
        dbuf[pl.ds(0, HALO), :] = jnp.zeros((HALO, CONV_WIDTH), F32)
        dbuf[pl.ds(HALO, R), :] = dconv
        dbuf[pl.ds(HALO + R, HALO), :] = jnp.zeros((HALO, CONV_WIDTH), F32)
        for cols in CONV_HALVES:
            _build_shifts(shf, hbuf, cols, E)
            _conv_weight_grad(shf, hbuf, dbuf, HALO, cols, tap_offsets, R, dcw)
            _build_shifts(shf, dbuf, cols, E + HALO)
            _conv_taps(shf, dbuf, cw8, cols, back_offsets, E, dhcbuf)
        dhc_main = dhcbuf[pl.ds(HALO, R), :] + tail(carry_c[...])
        carry_c[...] = dhcbuf[pl.ds(0, HALO), :]
        sg = sg_ext[HALO:, :]
        d_ca = dhc_main * sg
        d_cg = dhc_main * ca_ext[HALO:, :] * (sg * (1.0 - sg))
        dbin[:, C_CA[0] : C_CA[1]] += _row_sum(d_ca)
        dbin[:, C_CG[0] : C_CG[1]] += _row_sum(d_cg)
        dproj[:, C_CA[0] : C_CA[1]] = d_ca.astype(BF16)
        dproj[:, C_CG[0] : C_CG[1]] = d_cg.astype(BF16)

        @pl.when(step == nt - 1)
        def _():
            row = lax.broadcasted_iota(jnp.int32, (GROUP, GROUP), 0)
            col = lax.broadcasted_iota(jnp.int32, (GROUP, GROUP), 1)
            dbs[...] = jnp.zeros_like(dbs)
            for h in range(SGU_HEADS):
                hs = slice(h * GROUP, (h + 1) * GROUP)
                dwm[hs, :] = jnp.where(row >= col, dwm[hs, :], 0.0)
                dbs[pl.ds(h, 1), :] = _row_sum(dbs_acc[:, hs].T)

    params = _mixer_params(p)
    accs = [
        S((POOL_WIDTH, GROUP), F32), S((1, POOL_WIDTH), F32), S((1, SGU_WIDTH), F32), S((1, SGU_WIDTH), F32),
        S((SGU_WIDTH, GROUP), F32), S((8, GROUP), F32), S((32, CONV_WIDTH), F32), S((1, CONV_WIDTH), F32),
        S((1, CONV_WIDTH), F32), S((1, CONV_WIDTH), F32), S((1, IN_WIDTH), F32),
    ]
    in_specs = [
        pl.BlockSpec((R, IN_WIDTH), lambda i: (nt - 1 - i, 0)),
        pl.BlockSpec((HALO, IN_WIDTH), lambda i: (jnp.maximum((nt - 1 - i) * hb - 1, 0), 0)),
        pl.BlockSpec((R, D_MODEL), lambda i: (nt - 1 - i, 0)),
    ] + [_whole(x) for x in params]
    scratch = [
        pltpu.VMEM((E, CONV_WIDTH), F32), pltpu.VMEM((E + HALO, CONV_WIDTH), F32),
        pltpu.VMEM((HALO, POOL_WIDTH), F32), pltpu.VMEM((HALO, CONV_WIDTH), F32), pltpu.VMEM((GROUP, SGU_WIDTH), F32),
        pltpu.VMEM((7, E + HALO, CONV_WIDTH // 2), F32), pltpu.VMEM((R, CONV_WIDTH), F32), pltpu.VMEM((E, CONV_WIDTH), F32),
    ]
    return _call(
        name, body, (nt,), in_specs, [pl.BlockSpec((R, IN_WIDTH), lambda i: (nt - 1 - i, 0))] + [_whole(x) for x in accs],
        [S((T, IN_WIDTH), BF16)] + accs, scratch, [proj, proj, dmix, *params], side,
    )


def _all_gather(xs):
    n = len(xs)

    def body(*refs):
        x_refs, o_refs = refs[:n], refs[n : 2 * n]
        send_sems, recv_sems, local_sems = refs[2 * n :]
        x, y, c = _place()
        me, sibling = (x, y, c), (x, y, 1 - c)
        chips = [(1 - x, y), (x, 1 - y), (1 - x, 1 - y)]

        def copy(a, k, block, to, src=None):
            dst = o_refs[a].at[_lin(block)]
            return pltpu.make_async_remote_copy(
                src_ref=dst if src is None else src, dst_ref=dst, send_sem=send_sems.at[a, k], recv_sem=recv_sems.at[a, k],
                device_id=to, device_id_type=MESH,
            )

        mine = [pltpu.make_async_copy(x_refs[a], o_refs[a].at[_lin(me)], local_sems.at[a]) for a in range(n)]
        for m in mine:
            m.start()
        first = []
        for a in range(n):
            first.append(copy(a, 0, me, sibling, src=x_refs[a]))
            first += [copy(a, 1 + j, me, (*chip, c), src=x_refs[a]) for j, chip in enumerate(chips)]
        for cp in first:
            cp.start()
        passed = []
        for a in range(n):
            for j, chip in enumerate(chips):
                copy(a, 1 + j, (*chip, c), me).wait_recv()
                fwd = copy(a, 4 + j, (*chip, c), sibling)
                fwd.start()
                passed.append(fwd)
        for a in range(n):
            copy(a, 0, sibling, me).wait_recv()
            for j, chip in enumerate(chips):
                copy(a, 4 + j, (*chip, 1 - c), me).wait_recv()
        for cp in first + passed:
            cp.wait_send()
        for m in mine:
            m.wait()

    return pl.pallas_call(
        body,
        name="all_gather_weights",
        in_specs=[_ANY] * n,
        out_specs=[_ANY] * n,
        out_shape=[S((N_DEV, *x.shape), x.dtype) for x in xs],
        scratch_shapes=[pltpu.SemaphoreType.DMA((n, 7)), pltpu.SemaphoreType.DMA((n, 7)), pltpu.SemaphoreType.DMA((n,))],
    )(*xs)


def _row_tile(rows, want):
    return next(t for t in range(min(rows, want) // 8 * 8, 0, -8) if rows % t == 0)


def _sum_slots(name, slots):
    _, rows, cols = slots.shape
    tr = _row_tile(rows, (4 << 20) // (N_DEV * cols * slots.dtype.itemsize))

    def body(s_ref, o_ref):
        total = s_ref[0].astype(F32)
        for d in range(1, N_DEV):
            total = total + s_ref[d].astype(F32)
        o_ref[...] = total

    return pl.pallas_call(
        body,
        name=name,
        grid=(rows // tr,),
        in_specs=[pl.BlockSpec((N_DEV, tr, cols), lambda i: (0, i, 0))],
        out_specs=pl.BlockSpec((tr, cols), lambda i: (i, 0)),
        out_shape=S((rows, cols), F32),
        compiler_params=_cparams(1),
    )(slots)


def _adamw(name, w, g, m, v):
    rows, cols = w.shape
    tr = rows if rows * cols * 4 <= (2 << 20) else _row_tile(rows, 1 << ((1 << 18) // cols).bit_length() - 1)

    def body(w_ref, g_ref, m_ref, v_ref, d_ref, nm_ref, nv_ref):
        gv = g_ref[...]
        nm = ADAM_B1 * m_ref[...] + (1.0 - ADAM_B1) * gv
        nv = ADAM_B2 * v_ref[...] + (1.0 - ADAM_B2) * (gv * gv)
        m_hat = nm / (1.0 - ADAM_B1**ADAM_STEP)
        v_hat = nv / (1.0 - ADAM_B2**ADAM_STEP)
        d_ref[...] = -ADAM_LR * (m_hat / (jnp.sqrt(v_hat) + ADAM_EPS) + ADAM_WD * w_ref[...])
        nm_ref[...] = nm
        nv_ref[...] = nv

    blk = pl.BlockSpec((tr, cols), lambda i: (i, 0))
    return pl.pallas_call(
        body,
        name=name,
        grid=(rows // tr,),
        in_specs=[blk] * 4,
        out_specs=[blk] * 3,
        out_shape=[S((rows, cols), F32)] * 3,
        compiler_params=_cparams(1),
    )(w, g, m, v)


_BIG = ("w_in", "w_out", "w_ff1", "w_ff2")
_TRANSPOSED = ("w_in", "w_ff1")
_SMALL = ("b_in", "w_pool", "pool_scale", "sgu_ln_g", "sgu_ln_b", "sgu_w", "sgu_b", "conv_b", "conv_ln_g", "conv_ln_b",
          "b_out", "ln1_g", "ln1_b", "b_ff1", "b_ff2", "ln2_g", "ln2_b")
_WEIGHTS = ("w_in", "b_in", "w_pool", "pool_scale", "sgu_ln_g", "sgu_ln_b", "sgu_w", "sgu_b", "conv_w", "conv_b", "conv_ln_g",
            "conv_ln_b", "w_out", "b_out", "ln1_g", "ln1_b", "w_ff1", "b_ff1", "w_ff2", "b_ff2", "ln2_g", "ln2_b")


def _pack(arrays):
    parts = []
    for a in arrays:
        rows = a.reshape(-1, 128)
        parts.append(jnp.pad(rows, ((0, -rows.shape[0] % 8), (0, 0))))
    return jnp.concatenate(parts, axis=0)


def _unpack(flat, like):
    out, at = [], 0
    for a in like:
        n = a.size // 128
        out.append(flat[at : at + n].reshape(a.shape))
        at += n + (-n % 8)
    return out


def _packed_rows(arrays):
    return sum(a.size // 128 + (-(a.size // 128) % 8) for a in arrays)


def kernel(x, w_in, b_in, w_pool, pool_scale, sgu_ln_g, sgu_ln_b, sgu_w, sgu_b, conv_w, conv_b, conv_ln_g, conv_ln_b, w_out, b_out, ln1_g, ln1_b, w_ff1, b_ff1, w_ff2, b_ff2, ln2_g, ln2_b, loss_target, m_w_in, m_b_in, m_w_pool, m_pool_scale, m_sgu_ln_g, m_sgu_ln_b, m_sgu_w, m_sgu_b, m_conv_w, m_conv_b, m_conv_ln_g, m_conv_ln_b, m_w_out, m_b_out, m_ln1_g, m_ln1_b, m_w_ff1, m_b_ff1, m_w_ff2, m_b_ff2, m_ln2_g, m_ln2_b, v_w_in, v_b_in, v_w_pool, v_pool_scale, v_sgu_ln_g, v_sgu_ln_b, v_sgu_w, v_sgu_b, v_conv_w, v_conv_b, v_conv_ln_g, v_conv_ln_b, v_w_out, v_b_out, v_ln1_g, v_ln1_b, v_w_ff1, v_b_ff1, v_w_ff2, v_b_ff2, v_ln2_g, v_ln2_b):
    w = dict(w_in=w_in, b_in=b_in, w_pool=w_pool, pool_scale=pool_scale, sgu_ln_g=sgu_ln_g, sgu_ln_b=sgu_ln_b, sgu_w=sgu_w,
             sgu_b=sgu_b, conv_w=conv_w, conv_b=conv_b, conv_ln_g=conv_ln_g, conv_ln_b=conv_ln_b, w_out=w_out, b_out=b_out,
             ln1_g=ln1_g, ln1_b=ln1_b, w_ff1=w_ff1, b_ff1=b_ff1, w_ff2=w_ff2, b_ff2=b_ff2, ln2_g=ln2_g, ln2_b=ln2_b)
    mom = dict(w_in=m_w_in, b_in=m_b_in, w_pool=m_w_pool, pool_scale=m_pool_scale, sgu_ln_g=m_sgu_ln_g, sgu_ln_b=m_sgu_ln_b,
               sgu_w=m_sgu_w, sgu_b=m_sgu_b, conv_w=m_conv_w, conv_b=m_conv_b, conv_ln_g=m_conv_ln_g, conv_ln_b=m_conv_ln_b,
               w_out=m_w_out, b_out=m_b_out, ln1_g=m_ln1_g, ln1_b=m_ln1_b, w_ff1=m_w_ff1, b_ff1=m_b_ff1, w_ff2=m_w_ff2,
               b_ff2=m_b_ff2, ln2_g=m_ln2_g, ln2_b=m_ln2_b)
    var = dict(w_in=v_w_in, b_in=v_b_in, w_pool=v_w_pool, pool_scale=v_pool_scale, sgu_ln_g=v_sgu_ln_g, sgu_ln_b=v_sgu_ln_b,
               sgu_w=v_sgu_w, sgu_b=v_sgu_b, conv_w=v_conv_w, conv_b=v_conv_b, conv_ln_g=v_conv_ln_g, conv_ln_b=v_conv_ln_b,
               w_out=v_w_out, b_out=v_b_out, ln1_g=v_ln1_g, ln1_b=v_ln1_b, w_ff1=v_w_ff1, b_ff1=v_b_ff1, w_ff2=v_w_ff2,
               b_ff2=v_b_ff2, ln2_g=v_ln2_g, ln2_b=v_ln2_b)
    T = x.shape[1]
    x0 = x.reshape(T, D_MODEL)
    target = loss_target.reshape(T, D_MODEL)
    me_lin = _lin(_place())

    shard = [
        {name: (w[name][l].T if name in _TRANSPOSED else w[name][l]).astype(BF16) for name in _BIG} for l in range(DEPTH)
    ]
    conv_shard = jnp.pad(conv_w, ((0, 0), (0, 1), (0, 128 - conv_w.shape[2]))).reshape(DEPTH * 32, 128)

    def rows_of(g):
        return g.reshape(N_DEV * g.shape[1], g.shape[2])

    first = _all_gather([shard[0]["w_in"], shard[0]["w_out"], conv_shard])
    full = [{} for _ in range(DEPTH)]
    full[0]["w_in"], full[0]["w_out"] = rows_of(first[0]), rows_of(first[1])
    conv_cols = conv_w.shape[2]
    conv_full = first[2].reshape(N_DEV, DEPTH, 32, 128)[:, :, :CONV_KERNEL, :conv_cols]
    conv_full = conv_full.transpose(1, 2, 0, 3).reshape(DEPTH, CONV_KERNEL, N_DEV * conv_cols)

    tril = jnp.tril(jnp.ones((GROUP, GROUP), F32))
    prm = []
    for l in range(DEPTH):
        wm = sgu_w[l] * tril
        prm.append(dict(
            wp=w_pool[l].reshape(POOL_WIDTH, GROUP).astype(BF16), ps=_row(pool_scale[l]), lg=_row(sgu_ln_g[l]), lb=_row(sgu_ln_b[l]),
            wm=wm.reshape(SGU_WIDTH, GROUP).astype(BF16), wmt=wm.transpose(0, 2, 1).reshape(SGU_WIDTH, GROUP).astype(BF16),
            bsf=jnp.repeat(sgu_b[l].T, GROUP, axis=1), cw8=jnp.repeat(jnp.pad(conv_full[l], ((0, 1), (0, 0))), 8, axis=0), cb=_row(conv_b[l]),
            cg=_row(conv_ln_g[l]), cbeta=_row(conv_ln_b[l]),
        ))

    saved = []
    res = (x0, jnp.ones((D_MODEL,), F32), jnp.zeros((D_MODEL,), F32))
    xbf = x0.astype(BF16)
    u = shard[0]["w_ff1"].shape[0] // 8

    for l in range(DEPTH):
        f = full[l]
        s_ff1, s_ff2 = shard[l]["w_ff1"], shard[l]["w_ff2"]
        ex = _Exchange([("gather", s_ff1, (0, 3 * u), None)])
        proj = _mm_bias(f"proj{l}", xbf, f["w_in"], "nt", b_in[l], 1024, 896, 2048, side=ex)
        ex = _Exchange([("gather", s_ff1, (3 * u, 3 * u), ex.results[0])])
        mixed = _mixer_fwd(f"mixer_fwd{l}", proj, prm[l], 256, side=ex)
        ex = _Exchange([("gather", s_ff1, (6 * u, 2 * u), ex.results[0]), ("gather", s_ff2, (0, u), None)])
        xh1, rs1, x1bf = _mm_ln(f"out_ln1_{l}", mixed, f["w_out"], b_out[l], res, ln1_g[l], ln1_b[l], 1024, 512, side=ex)
        f["w_ff1"] = rows_of(ex.results[0])
        ex = _Exchange([("gather", s_ff2, (u, 7 * u), ex.results[1])])
        act, hsq = _mm_relu2(f"ff1_{l}", x1bf, f["w_ff1"], "nt", b_ff1[l], 1024, 1024, 2048, side=ex)
        f["w_ff2"] = rows_of(ex.results[0])
        ex = _Exchange([("gather", shard[l + 1]["w_in"]), ("gather", shard[l + 1]["w_out"])]) if l + 1 < DEPTH else None
        xh2, rs2, x2bf = _mm_ln(
            f"ff2_ln2_{l}", hsq, f["w_ff2"], b_ff2[l], (xh1, ln1_g[l], ln1_b[l]), ln2_g[l], ln2_b[l], 1024, 1024, side=ex)
        if ex is not None:
            full[l + 1]["w_in"], full[l + 1]["w_out"] = rows_of(ex.results[0]), rows_of(ex.results[1])
        saved.append(dict(xin=xbf, proj=proj, mixed=mixed, xh1=xh1, rs1=rs1, x1bf=x1bf, act=act, hsq=hsq, xh2=xh2, rs2=rs2))
        res = (xh2, ln2_g[l], ln2_b[l])
        xbf = x2bf

    top = saved[-1]
    dr2, dr2bf, g_ln2g, g_ln2b, g_bff2, loss_row = _loss_top(top["xh2"], top["rs2"], ln2_g[-1], ln2_b[-1], target, 256)
    loss = lax.psum(loss_row[0, 0], ("x", "y", "c"))
    slots = [{} for _ in range(DEPTH)]
    gsm = [{} for _ in range(DEPTH)]
    grad_x = small_slots = None

    def stacked_small():
        st = {name: jnp.stack([gsm[gl][name].reshape(w[name].shape[1:]) for gl in range(DEPTH)]) for name in _SMALL}
        conv_g = jnp.pad(jnp.stack([gsm[gl]["conv_w"] for gl in range(DEPTH)]), ((0, 0), (0, 1), (0, 0)))
        return [st[name] for name in _SMALL] + [conv_g]

    for l in reversed(range(DEPTH)):
        f, sv = full[l], saved[l]
        gsm[l].update(ln2_g=g_ln2g, ln2_b=g_ln2b, b_ff2=g_bff2)
        gw = _mm_wgrad(f"gw_ff2_{l}", sv["hsq"], dr2bf, 1024, 2048)
        ex = _Exchange([("slices", gw)])
        dhpre, g_bff1 = _mm_dh(f"dff1_{l}", dr2bf, f["w_ff2"], sv["act"], 1024, 1024, 2048, side=ex)
        slots[l]["w_ff2"] = ex.results[0]
        gsm[l]["b_ff1"] = g_bff1
        gw = _mm_wgrad(f"gw_ff1_{l}", dhpre, sv["x1bf"], 1024, 2048)
        ex = _Exchange([("slices", gw)])
        dr1, dr1bf, g_ln1g, g_ln1b, g_bout = _mm_ln_bwd(
            f"dx1_ln1_{l}", dhpre, f["w_ff1"], dr2, sv["xh1"], sv["rs1"], ln1_g[l], 1024, 1024, side=ex)
        slots[l]["w_ff1"] = ex.results[0]
        gsm[l].update(ln1_g=g_ln1g, ln1_b=g_ln1b, b_out=g_bout)
        gw = _mm_wgrad(f"gw_out_{l}", sv["mixed"], dr1bf, 1024, 2048)
        dmix = _mm_plain(f"dmixed{l}", dr1bf, f["w_out"], "nt", 1024, 1024, 2048)
        ex = _Exchange([("slices", gw)])
        (dproj, g_wp, g_ps, g_lg, g_lb, g_wm, g_bs, g_cw, g_cb, g_cg, g_cbeta, g_bin) = _mixer_bwd(
            f"mixer_bwd{l}", sv["proj"], dmix, prm[l], 256, side=ex)
        slots[l]["w_out"] = ex.results[0]
        gsm[l].update(b_in=g_bin, w_pool=g_wp, pool_scale=g_ps, sgu_ln_g=g_lg, sgu_ln_b=g_lb, sgu_w=g_wm, sgu_b=g_bs[:SGU_HEADS],
                      conv_w=g_cw[:CONV_KERNEL], conv_b=g_cb, conv_ln_g=g_cg, conv_ln_b=g_cbeta)
        if l > 0:
            gw = _mm_wgrad(f"gw_in_{l}", dproj, sv["xin"], 896, 2048)
            below = saved[l - 1]
            ex = _Exchange([("slices", gw)])
            dr2, dr2bf, g_ln2g, g_ln2b, g_bff2 = _mm_ln_bwd(
                f"dx_ln2_{l}", dproj, f["w_in"], dr1, below["xh2"], below["rs2"], ln2_g[l - 1], 1024, 512, side=ex)
        else:
            small_like = stacked_small()
            ex = _Exchange([("gather", _pack(small_like))])
            gw = _mm_wgrad(f"gw_in_{l}", dproj, sv["xin"], 896, 2048, side=ex)
            small_slots = ex.results[0]
            ex = _Exchange([("slices", gw)])
            grad_x = _mm_res("dx0", dproj, f["w_in"], dr1, 1024, 512, side=ex)
        slots[l]["w_in"] = ex.results[0]

    grads, deltas, new_m, new_v = {}, {}, {}, {}
    for name in _BIG:
        per_layer = []
        for l in range(DEPTH):
            g = _sum_slots(f"sum_{name}_{l}", slots[l][name])
            per_layer.append(g.T if name in _TRANSPOSED else g)
        g = jnp.stack(per_layer)
        shape = w[name].shape
        two_d = (shape[0] * shape[1], shape[2])
        d, nm, nv = _adamw(f"adamw_{name}", w[name].reshape(two_d), g.reshape(two_d), mom[name].reshape(two_d), var[name].reshape(two_d))
        grads[name], deltas[name], new_m[name], new_v[name] = g, d.reshape(shape), nm.reshape(shape), nv.reshape(shape)

    total = _sum_slots("sum_small", small_slots)
    small_g = _unpack(total, small_like)
    like = [w[name] for name in _SMALL]
    d, nm, nv = _adamw("adamw_small", _pack(like), total[: _packed_rows(like)],
                       _pack([mom[name] for name in _SMALL]), _pack([var[name] for name in _SMALL]))
    for name, gg, dd, mm_, vv in zip(_SMALL, small_g, _unpack(d, like), _unpack(nm, like), _unpack(nv, like)):
        grads[name], deltas[name], new_m[name], new_v[name] = gg, dd, mm_, vv
    conv_g = lax.dynamic_slice_in_dim(small_g[-1][:, :CONV_KERNEL, :], me_lin * conv_cols, conv_cols, axis=2)
    flat = (DEPTH * CONV_KERNEL, conv_cols)
    d, nm, nv = _adamw("adamw_conv_w", conv_w.reshape(flat), conv_g.reshape(flat), m_conv_w.reshape(flat), v_conv_w.reshape(flat))
    grads["conv_w"], deltas["conv_w"], new_m["conv_w"], new_v["conv_w"] = conv_g, d.reshape(conv_w.shape), nm.reshape(conv_w.shape), nv.reshape(conv_w.shape)

    return (loss, grad_x.reshape(x.shape), *[grads[n] for n in _WEIGHTS], *[deltas[n] for n in _WEIGHTS],
            *[new_m[n] for n in _WEIGHTS], *[new_v[n] for n in _WEIGHTS])
```

```python
import functools

import jax
import jax.numpy as jnp
from jax import lax
from jax.experimental import pallas as pl
from jax.experimental.pallas import tpu as pltpu

F32, BF16 = jnp.float32, jnp.bfloat16
S = jax.ShapeDtypeStruct

DEPTH = 2
D_MODEL = 2048
POOL_WINDOWS = (2, 4, 8, 16)
POOL_WIDTH = 512
GROUP = 128
SGU_WIDTH = 768
SGU_HEADS = 6
CONV_WIDTH = 768
CONV_KERNEL = 31
IN_WIDTH = 3584
D_FF = 8192
ALPHA = (2 * DEPTH) ** 0.25
LN_EPS = 1e-5
ADAM_LR, ADAM_B1, ADAM_B2, ADAM_EPS, ADAM_WD, ADAM_STEP = 0.001, 0.9, 0.999, 1e-08, 0.01, 10

N_DEV = 8
LN_STRIP = 32
LN_UNROLL = 4
HALO = 32
VMEM_LIMIT = 56 << 20
MESH = pl.DeviceIdType.MESH

C_POOL = (0, 512)
C_U = (512, 1280)
C_V = (1280, 2048)
C_CA = (2048, 2816)
C_CG = (2816, 3584)
M_POOL = (0, 512)
M_SGU = (512, 1280)
M_CONV = (1280, 2048)


def _cparams(n_axes):
    return pltpu.CompilerParams(dimension_semantics=("arbitrary",) * n_axes, vmem_limit_bytes=VMEM_LIMIT)


def _for_strips(rows, strip, fn, unroll=1, straight=False):
    n = rows // strip
    if n == 1 or straight:
        for s in range(n):
            fn(s * strip)
        return

    def step(s, carry):
        fn(pl.multiple_of(s * strip, strip))
        return carry

    lax.fori_loop(0, n, step, 0, unroll=unroll)


def _row_sum(x):
    return jnp.sum(x, axis=0, keepdims=True)


def _ln_stats(r):
    mu = jnp.mean(r, axis=-1, keepdims=True)
    xc = r - mu
    var = jnp.mean(xc * xc, axis=-1, keepdims=True)
    rs = lax.rsqrt(var + LN_EPS)
    return xc * rs, rs


def _ln_bwd(dy, xhat, rs, g):
    gy = dy * g
    m1 = jnp.mean(gy, axis=-1, keepdims=True)
    m2 = jnp.mean(gy * xhat, axis=-1, keepdims=True)
    return rs * (gy - m1 - xhat * m2)


_GELU_C = 0.7978845608028654


def _gelu(x):
    th = jnp.tanh(_GELU_C * (x + 0.044715 * (x * x * x)))
    return 0.5 * x * (1.0 + th), th


def _gelu_grad(x, th):
    return 0.5 * (1.0 + th) + 0.5 * x * (1.0 - th * th) * (_GELU_C * (1.0 + 3.0 * 0.044715 * (x * x)))


def _place():
    x, y, c = lax.axis_index("x"), lax.axis_index("y"), lax.axis_index("c")
    return x, y, c


def _lin(p):
    return 4 * p[0] + 2 * p[1] + p[2]


def _flip(p, r):
    return tuple(1 - v if (r >> (2 - ax)) & 1 else v for ax, v in enumerate(p))


_ANY = pl.BlockSpec(memory_space=pl.ANY)


class _Exchange:
    def __init__(self, items):
        self.kinds = [item[0] for item in items]
        self.srcs = [item[1] for item in items]
        self.rows = [item[2] if len(item) > 2 else None for item in items]
        handed_on = [item[3] if len(item) > 3 else None for item in items]
        self.out_shape = [
            S((N_DEV, *x.shape), x.dtype) if kind == "gather" else S((N_DEV, x.shape[0] // N_DEV, x.shape[1]), x.dtype)
            for kind, x in zip(self.kinds, self.srcs)
        ]
        n = len(items)
        self.ins = self.srcs + [b for b in handed_on if b is not None]
        self.aliases = {}
        for a, b in enumerate(handed_on):
            if b is not None:
                self.aliases[n + len(self.aliases)] = a
        self.scratch = [pltpu.SemaphoreType.DMA((n, 7)), pltpu.SemaphoreType.DMA((n, 7)), pltpu.SemaphoreType.DMA((n,))]
        self.results = None

    def _src(self, in_refs, a, dest):
        if self.kinds[a] == "gather":
            return in_refs[a] if self.rows[a] is None else in_refs[a].at[pl.ds(*self.rows[a])]
        rows = self.srcs[a].shape[0] // N_DEV
        return in_refs[a].at[pl.ds(pl.multiple_of(_lin(dest) * rows, 8), rows)]

    def _dst(self, out_refs, a, slot):
        return out_refs[a].at[slot] if self.rows[a] is None else out_refs[a].at[slot, pl.ds(*self.rows[a])]

    def _copies(self, in_refs, out_refs, sems, with_arrivals):
        send_sems, recv_sems, local_sems = sems
        me = _place()
        local, sends, arrivals = [], [], []
        for a in range(len(self.srcs)):
            local.append(pltpu.make_async_copy(self._src(in_refs, a, me), self._dst(out_refs, a, _lin(me)), local_sems.at[a]))
            for r in range(1, N_DEV):
                peer = _flip(me, r)
                for slot, group in ((_lin(me), sends), (_lin(peer), arrivals)):
                    if group is sends or with_arrivals:
                        group.append(pltpu.make_async_remote_copy(
                            src_ref=self._src(in_refs, a, peer), dst_ref=self._dst(out_refs, a, slot),
                            send_sem=send_sems.at[a, r - 1], recv_sem=recv_sems.at[a, r - 1], device_id=peer, device_id_type=MESH,
                        ))
        return local, sends, arrivals

    def start(self, in_refs, out_refs, sems):
        local, sends, _ = self._copies(in_refs, out_refs, sems, False)
        for cp in local + sends:
            cp.start()

    def wait(self, in_refs, out_refs, sems):
        local, sends, arrivals = self._copies(in_refs, out_refs, sems, True)
        for cp in arrivals:
            cp.wait_recv()
        for cp in sends:
            cp.wait_send()
        for cp in local:
            cp.wait()


def _call(name, body, grid, in_specs, out_specs, out_shape, scratch, args, side=None):
    in_specs, out_specs, out_shape, scratch = list(in_specs), list(out_specs), list(out_shape), list(scratch)
    if side is None:
        return pl.pallas_call(
            body, name=name, grid=grid, in_specs=in_specs, out_specs=out_specs, out_shape=out_shape, scratch_shapes=scratch,
            compiler_params=_cparams(len(grid)),
        )(*args)
    n_in, n_out, n_scr = len(in_specs), len(out_specs), len(scratch)
    s_in, s_out = len(side.ins), len(side.out_shape)

    def wrapped(*refs):
        at = 0
        parts = []
        for n in (n_in, s_in, n_out, s_out, n_scr, 3):
            parts.append(refs[at : at + n])
            at += n
        ins, side_ins, outs, side_outs, scr, sems = parts
        pids = [pl.program_id(d) for d in range(len(grid))]
        first = functools.reduce(jnp.logical_and, [p == 0 for p in pids])
        last = functools.reduce(jnp.logical_and, [p == g - 1 for p, g in zip(pids, grid)])

        @pl.when(first)
        def _():
            side.start(side_ins, side_outs, sems)

        body(*ins, *outs, *scr)

        @pl.when(last)
        def _():
            side.wait(side_ins, side_outs, sems)

    res = pl.pallas_call(
        wrapped, name=name, grid=grid, in_specs=in_specs + [_ANY] * s_in, out_specs=out_specs + [_ANY] * s_out,
        out_shape=out_shape + side.out_shape, scratch_shapes=scratch + side.scratch, compiler_params=_cparams(len(grid)),
        input_output_aliases={n_in + i: n_out + o for i, o in side.aliases.items()},
    )(*args, *side.ins)
    side.results = list(res[n_out:])
    return list(res[:n_out])


_CONTRACT = {"nn": ((1,), (0,)), "nt": ((1,), (1,)), "tn": ((0,), (0,))}


def _mm(name, a, b, dims, tm, tn, tk, *, ins=(), outs, epilogue, j_outer=False, side=None):
    if dims == "tn":
        K, M = a.shape
    else:
        M, K = a.shape
    N = b.shape[0] if dims == "nt" else b.shape[1]
    tm, tn, tk = min(tm, M), min(tn, N), min(tk, K)
    assert M % tm == 0 and N % tn == 0 and K % tk == 0, (name, M, N, K, tm, tn, tk)
    nm, nn, nk = M // tm, N // tn, K // tk
    if j_outer:
        grid = (nn, nm, nk)
        ij = lambda g0, g1: (g1, g0)
    else:
        grid = (nm, nn, nk)
        ij = lambda g0, g1: (g0, g1)

    def amap(g0, g1, k):
        i, _ = ij(g0, g1)
        return (k, i) if dims == "tn" else (i, k)

    def bmap(g0, g1, k):
        _, j = ij(g0, g1)
        return (j, k) if dims == "nt" else (k, j)

    def spec(kind):
        if kind == "tile":
            return pl.BlockSpec((tm, tn), lambda g0, g1, k: ij(g0, g1))
        if kind == "row":
            return pl.BlockSpec((1, tn), lambda g0, g1, k: (0, ij(g0, g1)[1]))
        assert kind == "col", kind
        return pl.BlockSpec((tm, 1), lambda g0, g1, k: (ij(g0, g1)[0], 0))

    in_specs = [
        pl.BlockSpec((tk, tm) if dims == "tn" else (tm, tk), amap),
        pl.BlockSpec((tn, tk) if dims == "nt" else (tk, tn), bmap),
    ] + [spec(kind) for _, kind in ins]
    out_specs = [spec(kind) for _, _, kind in outs]
    out_shape = [S(shape, dtype) for shape, dtype, _ in outs]
    n_in, n_out = len(ins), len(outs)
    contract = (_CONTRACT[dims], ((), ()))

    def body(*refs):
        a_ref, b_ref = refs[:2]
        in_refs = refs[2 : 2 + n_in]
        out_refs = refs[2 + n_in : 2 + n_in + n_out]
        acc = refs[2 + n_in + n_out]
        i, _ = ij(pl.program_id(0), pl.program_id(1))
        k = pl.program_id(2)

        def part():
            return lax.dot_general(a_ref[...], b_ref[...], contract, preferred_element_type=F32)

        @pl.when(k == 0)
        def _():
            acc[...] = part()

        @pl.when(k > 0)
        def _():
            acc[...] += part()

        @pl.when(k == nk - 1)
        def _():
            epilogue(i, acc, in_refs, out_refs)

    return _call(name, body, grid, in_specs, out_specs, out_shape, [pltpu.VMEM((tm, tn), F32)], [a, b, *[x for x, _ in ins]], side)


def _mm_one_step(name, a, b, dims, tm, tn, *, ins=(), outs, epilogue, init=None, j_outer=False, side=None):
    M, K = a.shape
    N = b.shape[0] if dims == "nt" else b.shape[1]
    tm, tn = min(tm, M), min(tn, N)
    assert dims in ("nn", "nt") and M % tm == 0 and N % tn == 0, (name, dims, M, N, tm, tn)
    nm, nn = M // tm, N // tn
    steps = nm * nn

    def tile_of(s):
        return (s % nm, s // nm) if j_outer else (s // nn, s % nn)

    def current(s):
        return tile_of(jnp.minimum(s, steps - 1))

    def finished(s):
        return tile_of(jnp.maximum(s - 1, 0))

    def spec(kind):
        if kind == "tile":
            return pl.BlockSpec((tm, tn), lambda s: finished(s))
        if kind == "row":
            return pl.BlockSpec((1, tn), lambda s: (0, finished(s)[1]))
        assert kind == "col", kind
        return pl.BlockSpec((tm, 1), lambda s: (finished(s)[0], 0))

    in_specs = [
        pl.BlockSpec((tm, K), lambda s: (current(s)[0], 0)),
        pl.BlockSpec((tn, K), lambda s: (current(s)[1], 0)) if dims == "nt" else pl.BlockSpec((K, tn), lambda s: (0, current(s)[1])),
    ] + [spec(kind) for _, kind in ins]
    n_in, n_out = len(ins), len(outs)
    contract = (_CONTRACT[dims], ((), ()))

    def body(*refs):
        a_ref, b_ref = refs[:2]
        in_refs = refs[2 : 2 + n_in]
        out_refs = refs[2 + n_in : 2 + n_in + n_out]
        accs = refs[2 + n_in + n_out :]
        s = pl.program_id(0)
        i_done = finished(s)[0]
        even = s % 2 == 0
        middle = jnp.logical_and(s > 0, s < steps)

        chunks = [slice(c, c + COL_CHUNK) for c in range(0, tn, COL_CHUNK)] if tn % COL_CHUNK == 0 else [slice(0, tn)]

        def product(acc, cols):
            rhs = b_ref[cols, :] if dims == "nt" else b_ref[:, cols]
            acc[:, cols] = lax.dot_general(a_ref[...], rhs, contract, preferred_element_type=F32)

        if init is not None:
            @pl.when(s > 0)
            def _():
                init(i_done, out_refs)

        @pl.when(s == 0)
        def _():
            for cols in chunks:
                product(accs[0], cols)

        for parity in (0, 1):
            @pl.when(jnp.logical_and(middle, even if parity == 0 else jnp.logical_not(even)))
            def _(parity=parity):
                for cols in chunks:
                    product(accs[parity], cols)
                    epilogue(i_done, accs[1 - parity], in_refs, out_refs, cols)

        @pl.when(s == steps)
        def _():
            for cols in chunks:
                epilogue(i_done, accs[(steps - 1) % 2], in_refs, out_refs, cols)

    return _call(
        name, body, (steps + 1,), in_specs, [spec(kind) for _, _, kind in outs], [S(shape, dtype) for shape, dtype, _ in outs],
        [pltpu.VMEM((tm, tn), F32), pltpu.VMEM((tm, tn), F32)], [a, b, *[x for x, _ in ins]], side,
    )


def _row(v):
    return v.reshape(1, -1)


def _mm_bias(name, a, b, dims, bias, tm, tn, tk, side=None):
    M = a.shape[0]
    N = b.shape[0] if dims == "nt" else b.shape[1]

    def epilogue(i, acc, ins, outs, cols):
        def strip(r0):
            rows = pl.ds(r0, 128)
            outs[0][rows, cols] = acc[rows, cols] + ins[0][:, cols]

        _for_strips(acc.shape[0], 128, strip, straight=True)

    return _mm_one_step(
        name, a, b, dims, tm, tn, ins=[(_row(bias), "row")], outs=[((M, N), F32, "tile")], epilogue=epilogue, side=side)[0]


def _mm_relu2(name, a, b, dims, bias, tm, tn, tk, side=None):
    M = a.shape[0]
    N = b.shape[0] if dims == "nt" else b.shape[1]

    def epilogue(i, acc, ins, outs, cols):
        def strip(r0):
            rows = pl.ds(r0, 128)
            r = jnp.maximum(acc[rows, cols] + ins[0][:, cols], 0.0)
            outs[0][rows, cols] = r.astype(BF16)
            outs[1][rows, cols] = (r * r).astype(BF16)

        _for_strips(acc.shape[0], 128, strip, straight=True)

    return _mm_one_step(
        name, a, b, dims, tm, tn, ins=[(_row(bias), "row")],
        outs=[((M, N), BF16, "tile"), ((M, N), BF16, "tile")], epilogue=epilogue, side=side,
    )


def _mm_lagged(name, a, b, tm, tk, *, ins, outs, strip_fn, init_fn=None, side=None):
    M, K = a.shape
    N = b.shape[1]
    tm, tk = min(tm, M), min(tk, K)
    assert M % tm == 0 and K % tk == 0, (name, M, K, tm, tk)
    nm, nk = M // tm, K // tk
    assert nk >= 2, (name, nk)
    parts = 1 << ((nk - 1).bit_length() - 1)
    rows_p = tm // parts
    assert rows_p % LN_STRIP == 0, (name, rows_p)

    def part_index(i, k):
        return jnp.maximum((i - 1) * parts + jnp.minimum(k, parts - 1), 0)

    def spec(kind):
        if kind == "tile":
            return pl.BlockSpec((rows_p, N), lambda i, k: (part_index(i, k), 0))
        if kind == "row":
            return pl.BlockSpec((1, N), lambda i, k: (0, 0))
        assert kind == "col", kind
        return pl.BlockSpec((rows_p, 1), lambda i, k: (part_index(i, k), 0))

    in_specs = [
        pl.BlockSpec((tm, tk), lambda i, k: (jnp.minimum(i, nm - 1), k)),
        pl.BlockSpec((tk, N), lambda i, k: (jnp.where(i < nm, k, nk - 1), 0)),
    ] + [spec(kind) for _, kind in ins]
    n_in, n_out = len(ins), len(outs)

    def body(*refs):
        a_ref, b_ref = refs[:2]
        in_refs = refs[2 : 2 + n_in]
        out_refs = refs[2 + n_in : 2 + n_in + n_out]
        acc, fin = refs[2 + n_in + n_out :]
        i, k = pl.program_id(0), pl.program_id(1)

        def part():
            return jnp.dot(a_ref[...], b_ref[...], preferred_element_type=F32)

        def epilogue_part():
            base = k * rows_p
            for s in range(rows_p // LN_STRIP):
                acc_rows = fin[pl.ds(pl.multiple_of(base + s * LN_STRIP, LN_STRIP), LN_STRIP), :]
                strip_fn(acc_rows, in_refs, out_refs, pl.ds(s * LN_STRIP, LN_STRIP))

        has_dot = i < nm
        has_epilogue = jnp.logical_and(i > 0, k < parts)
        no_epilogue = jnp.logical_not(has_epilogue)
        last = k == nk - 1
        both = jnp.logical_and(has_dot, has_epilogue)
        alone = jnp.logical_and(has_dot, no_epilogue)

        if init_fn is not None:
            @pl.when(jnp.logical_and(i == 1, k == 0))
            def _():
                init_fn(out_refs)

        @pl.when(jnp.logical_and(alone, k == 0))
        def _():
            acc[...] = part()

        @pl.when(jnp.logical_and(alone, jnp.logical_and(k > 0, jnp.logical_not(last))))
        def _():
            acc[...] += part()

        @pl.when(jnp.logical_and(has_dot, last))
        def _():
            fin[...] = acc[...] + part()

        @pl.when(jnp.logical_and(both, k == 0))
        def _():
            acc[...] = part()
            epilogue_part()

        if parts > 1:
            @pl.when(jnp.logical_and(both, k > 0))
            def _():
                acc[...] += part()
                epilogue_part()

        @pl.when(jnp.logical_and(jnp.logical_not(has_dot), has_epilogue))
        def _():
            epilogue_part()

    return _call(
        name, body, (nm + 1, nk), in_specs, [spec(kind) for _, _, kind in outs], [S(shape, dtype) for shape, dtype, _ in outs],
        [pltpu.VMEM((tm, N), F32), pltpu.VMEM((tm, N), F32)], [a, b, *[x for x, _ in ins]], side,
    )


def _mm_ln(name, a, b, bias, res, g, beta, tm, tk, side=None):
    M = a.shape[0]
    N = b.shape[1]
    rxh, rg, rb = res

    def strip(acc_rows, ins, outs, rows):
        bias_r, rxh_r, rg_r, rb_r, g_r, beta_r = ins
        xhat_o, rstd_o, xbf_o = outs
        resid = rxh_r[rows, :] * rg_r[...] + rb_r[...]
        r = ALPHA * resid + (acc_rows + bias_r[...])
        xhat, rs = _ln_stats(r)
        xhat_o[rows, :] = xhat
        rstd_o[rows, :] = rs
        xbf_o[rows, :] = (xhat * g_r[...] + beta_r[...]).astype(BF16)

    return _mm_lagged(
        name, a, b, tm, tk,
        ins=[(_row(bias), "row"), (rxh, "tile"), (_row(rg), "row"), (_row(rb), "row"), (_row(g), "row"), (_row(beta), "row")],
        outs=[((M, N), F32, "tile"), ((M, 1), F32, "col"), ((M, N), BF16, "tile")],
        strip_fn=strip, side=side,
    )


def _ln_bwd_strip(dyv, xhat, rs, g, dr_o, drbf_o, dg_o, db_o, dsum_o, rows):
    dr = _ln_bwd(dyv, xhat, rs, g)
    dr_o[rows, :] = dr
    drbf_o[rows, :] = dr.astype(BF16)
    dg_o[...] += _row_sum(dyv * xhat)
    db_o[...] += _row_sum(dyv)
    dsum_o[...] += _row_sum(dr)


def _mm_ln_bwd(name, a, b, resgrad, xhat, rstd, g, tm, tk, side=None):
    M = a.shape[0]
    N = b.shape[1]

    def init(outs):
        for o in outs[2:]:
            o[...] = jnp.zeros_like(o)

    def strip(acc_rows, ins, outs, rows):
        rg_r, xh_r, rs_r, g_r = ins
        dyv = acc_rows + ALPHA * rg_r[rows, :]
        _ln_bwd_strip(dyv, xh_r[rows, :], rs_r[rows, :], g_r[...], *outs, rows)

    return _mm_lagged(
        name, a, b, tm, tk,
        ins=[(resgrad, "tile"), (xhat, "tile"), (rstd, "col"), (_row(g), "row")],
        outs=[((M, N), F32, "tile"), ((M, N), BF16, "tile"), ((1, N), F32, "row"), ((1, N), F32, "row"), ((1, N), F32, "row")],
        strip_fn=strip, init_fn=init, side=side,
    )


def _mm_dh(name, a, b, act, tm, tn, tk, side=None):
    M = a.shape[0]
    N = b.shape[0]

    def init(i, outs):
        @pl.when(i == 0)
        def _():
            outs[1][...] = jnp.zeros_like(outs[1])

    def epilogue(i, acc, ins, outs, cols):
        def strip(r0):
            rows = pl.ds(r0, 128)
            d = acc[rows, cols] * (2.0 * ins[0][rows, cols].astype(F32))
            outs[0][rows, cols] = d.astype(BF16)
            outs[1][:, cols] += _row_sum(d)

        _for_strips(acc.shape[0], 128, strip, straight=True)

    return _mm_one_step(
        name, a, b, "nt", tm, tn, ins=[(act, "tile")],
        outs=[((M, N), BF16, "tile"), ((1, N), F32, "row")], epilogue=epilogue, init=init, j_outer=True, side=side,
    )


def _mm_plain(name, a, b, dims, tm, tn, tk, side=None):
    M = a.shape[0]
    N = b.shape[0] if dims == "nt" else b.shape[1]

    def epilogue(i, acc, ins, outs, cols):
        def strip(r0):
            rows = pl.ds(r0, 128)
            outs[0][rows, cols] = acc[rows, cols]

        _for_strips(acc.shape[0], 128, strip, straight=True)

    return _mm_one_step(name, a, b, dims, tm, tn, outs=[((M, N), F32, "tile")], epilogue=epilogue, side=side)[0]


def _mm_res(name, a, b, res, tm, tk, side=None):
    def strip(acc_rows, ins, outs, rows):
        outs[0][rows, :] = acc_rows + ALPHA * ins[0][rows, :]

    return _mm_lagged(
        name, a, b, tm, tk, ins=[(res, "tile")], outs=[((a.shape[0], b.shape[1]), F32, "tile")], strip_fn=strip, side=side,
    )[0]


def _mm_wgrad(name, a, b, tm, tk, side=None):
    M = a.shape[1]
    N = b.shape[1]

    def epilogue(i, acc, ins, outs):
        def strip(r0):
            rows = pl.ds(r0, 128)
            outs[0][rows, :] = acc[rows, :].astype(BF16)

        _for_strips(acc.shape[0], 128, strip)

    return _mm(name, a, b, "tn", tm, N, tk, outs=[((M, N), BF16, "tile")], epilogue=epilogue, side=side)[0]


def _loss_top(xhat, rstd, g, beta, target, tm):
    T, D = xhat.shape
    tm = min(tm, T)
    nt = T // tm

    def body(xh_r, rs_r, g_r, b_r, t_r, dr_o, drbf_o, dg_o, db_o, dsum_o, loss_o, sq_acc):
        i = pl.program_id(0)

        @pl.when(i == 0)
        def _():
            dg_o[...] = jnp.zeros_like(dg_o)
            db_o[...] = jnp.zeros_like(db_o)
            dsum_o[...] = jnp.zeros_like(dsum_o)
            sq_acc[...] = jnp.zeros_like(sq_acc)

        def strip(r0):
            rows = pl.ds(r0, LN_STRIP)
            xh = xh_r[rows, :]
            err = (xh * g_r[...] + b_r[...]) - t_r[rows, :]
            sq_acc[...] += _row_sum(err * err)
            _ln_bwd_strip(err * (1.0 / D), xh, rs_r[rows, :], g_r[...], dr_o, drbf_o, dg_o, db_o, dsum_o, rows)

        _for_strips(tm, LN_STRIP, strip, unroll=LN_UNROLL)

        @pl.when(i == nt - 1)
        def _():
            total = jnp.sum(sq_acc[...], axis=-1, keepdims=True) * (0.5 / D)
            loss_o[...] = jnp.broadcast_to(total, loss_o.shape)

    tile = pl.BlockSpec((tm, D), lambda i: (i, 0))
    row = pl.BlockSpec((1, D), lambda i: (0, 0))
    return pl.pallas_call(
        body,
        name="loss_top",
        grid=(nt,),
        in_specs=[tile, pl.BlockSpec((tm, 1), lambda i: (i, 0)), row, row, tile],
        out_specs=[tile, tile, row, row, row, pl.BlockSpec((1, 128), lambda i: (0, 0))],
        out_shape=[S((T, D), F32), S((T, D), BF16), S((1, D), F32), S((1, D), F32), S((1, D), F32), S((1, 128), F32)],
        scratch_shapes=[pltpu.VMEM((1, D), F32)],
        compiler_params=_cparams(1),
    )(xhat, rstd, _row(g), _row(beta), target)


def _cols(ref, c):
    return ref[:, c[0] : c[1]]


def _causal_window_sum(e, w):
    s, sh = e, 1
    while sh < w:
        s = s + pltpu.roll(s, sh, axis=0)
        sh *= 2
    return s


def _anticausal_window_sum(d, w):
    n = d.shape[0]
    r, sh = d, 1
    while sh < w:
        r = r + pltpu.roll(r, n - sh, axis=0)
        sh *= 2
    return r


def _with_halo(halo_ref, main_ref, c, keep):
    return jnp.concatenate([_cols(halo_ref, c) * keep, _cols(main_ref, c)], axis=0)


def _pool_counts(tile_index, R, w):
    pos = lax.broadcasted_iota(jnp.int32, (R, 1), 0) + tile_index * R
    return jnp.minimum(pos + 1, w).astype(F32)


def _sgu_mix(wm_ref, vnb):
    return jnp.concatenate(
        [
            jnp.dot(wm_ref[h * GROUP : (h + 1) * GROUP, :], vnb[:, h * GROUP : (h + 1) * GROUP], preferred_element_type=F32)
            for h in range(SGU_HEADS)
        ],
        axis=1,
    )


CONV_HALVES = (slice(0, CONV_WIDTH // 2), slice(CONV_WIDTH // 2, CONV_WIDTH))
TAP_STRIP = 32
COL_CHUNK = 256
TAP_GROUP = 4


def _build_shifts(shf, src, cols, rows):
    n = rows - 8
    for r in range(1, 8):
        shf[r - 1, pl.ds(0, n), :] = src[pl.ds(r, n), cols]


def _shifted(shf, src, cols, offset, start, size):
    q, r = divmod(offset, 8)
    rows = pl.ds(pl.multiple_of(start + 8 * q, 8), size)
    return src[rows, cols] if r == 0 else shf[r - 1, rows, :]


def _conv_taps(shf, src, cw8, cols, offsets, n_rows, out, bias=None):
    width = cols.stop - cols.start

    def strip(s, carry):
        r0 = pl.multiple_of(s * TAP_STRIP, TAP_STRIP)
        acc = jnp.zeros((TAP_STRIP, width), F32)
        for k, o in enumerate(offsets):
            wk = cw8[pl.ds(8 * k, 8), cols]
            acc = acc + _shifted(shf, src, cols, o, r0, TAP_STRIP) * jnp.concatenate([wk] * (TAP_STRIP // 8), axis=0)
        if bias is not None:
            acc = acc + bias[:, cols]
        out[pl.ds(r0, TAP_STRIP), cols] = acc
        return carry

    lax.fori_loop(0, n_rows // TAP_STRIP, strip, 0)


def _conv_weight_grad(shf, src, dsrc, d_first, cols, offsets, n_rows, dcw):
    width = cols.stop - cols.start
    for k0 in range(0, len(offsets), TAP_GROUP):
        group = offsets[k0 : k0 + TAP_GROUP]

        def strip(s, accs, group=group):
            r0 = pl.multiple_of(s * TAP_STRIP, TAP_STRIP)
            d = dsrc[pl.ds(pl.multiple_of(d_first + r0, 8), TAP_STRIP), cols]
            out = []
            for acc8, o in zip(accs, group):
                p = _shifted(shf, src, cols, o, r0, TAP_STRIP) * d
                for j in range(TAP_STRIP // 8):
                    acc8 = acc8 + p[8 * j : 8 * j + 8, :]
                out.append(acc8)
            return tuple(out)

        accs = lax.fori_loop(0, n_rows // TAP_STRIP, strip, tuple(jnp.zeros((8, width), F32) for _ in group))
        for j, acc8 in enumerate(accs):
            dcw[pl.ds(k0 + j, 1), cols] += _row_sum(acc8)


def _mixer_params(p):
    return [p["wp"], p["ps"], p["lg"], p["lb"], p["wm"], p["wmt"], p["bsf"], p["cw8"], p["cb"], p["cg"], p["cbeta"]]


def _whole(x):
    return pl.BlockSpec(x.shape, lambda i: (0,) * x.ndim)


def _mixer_fwd(name, proj, p, R, side=None):
    T = proj.shape[0]
    R = min(R, T)
    E = R + HALO
    nt = T // R
    hb = R // HALO
    tap_offsets = [HALO - (CONV_KERNEL - 1) + k for k in range(CONV_KERNEL)]

    def body(pm, ph, wp, ps, lg, lb, wm, wmt, bsf, cw8, cb, cg, cbeta, out, hbuf, shf, convbuf):
        i = pl.program_id(0)
        keep = (i > 0).astype(F32)
        a_ext = _with_halo(ph, pm, C_POOL, keep)
        for gi, w in enumerate(POOL_WINDOWS):
            cs = slice(gi * GROUP, (gi + 1) * GROUP)
            e = a_ext[:, cs]
            s = _causal_window_sum(e, w)
            pooled = s[HALO:, :] / _pool_counts(i, R, w) - e[HALO:, :]
            z = jnp.dot(pooled.astype(BF16), wp[cs, :], preferred_element_type=F32)
            out[:, cs] = (z * ps[:, cs]).astype(BF16)
        u, _ = _gelu(_cols(pm, C_U))
        v, _ = _gelu(_cols(pm, C_V))
        vhat, _ = _ln_stats(v)
        vn = vhat * lg[...] + lb[...]
        for c in range(R // GROUP):
            rs = slice(c * GROUP, (c + 1) * GROUP)
            mixed = _sgu_mix(wm, vn[rs, :].astype(BF16)) + bsf[...]
            out[rs, M_SGU[0] : M_SGU[1]] = (u[rs, :] * mixed).astype(BF16)
        hbuf[...] = _with_halo(ph, pm, C_CA, keep) * jax.nn.sigmoid(_with_halo(ph, pm, C_CG, keep))
        for cols in CONV_HALVES:
            _build_shifts(shf, hbuf, cols, E)
            _conv_taps(shf, hbuf, cw8, cols, tap_offsets, R, convbuf, bias=cb)
        chat, _ = _ln_stats(convbuf[...])
        cn = chat * cg[...] + cbeta[...]
        out[:, M_CONV[0] : M_CONV[1]] = (cn * jax.nn.sigmoid(cn)).astype(BF16)

    params = _mixer_params(p)
    in_specs = [
        pl.BlockSpec((R, IN_WIDTH), lambda i: (i, 0)),
        pl.BlockSpec((HALO, IN_WIDTH), lambda i: (jnp.maximum(i * hb - 1, 0), 0)),
    ] + [_whole(x) for x in params]
    scratch = [pltpu.VMEM((E, CONV_WIDTH), F32), pltpu.VMEM((7, E, CONV_WIDTH // 2), F32), pltpu.VMEM((R, CONV_WIDTH), F32)]
    return _call(
        name, body, (nt,), in_specs, [pl.BlockSpec((R, D_MODEL), lambda i: (i, 0))], [S((T, D_MODEL), BF16)],
        scratch, [proj, proj, *params], side,
    )[0]


def _mixer_bwd(name, proj, dmix, p, R, side=None):
    T = proj.shape[0]
    R = min(R, T)
    E = R + HALO
    nt = T // R
    hb = R // HALO
    tap_offsets = [HALO - (CONV_KERNEL - 1) + k for k in range(CONV_KERNEL)]
    back_offsets = [HALO - o for o in tap_offsets]

    def body(pm, ph, dm, wp, ps, lg, lb, wm, wmt, bsf, cw8, cb, cg, cbeta,
             dproj, dwp, dps, dlg, dlb, dwm, dbs, dcw, dcb, dcg, dcbeta, dbin,
             hbuf, dbuf, carry_p, carry_c, dbs_acc, shf, convbuf, dhcbuf):
        step = pl.program_id(0)
        ti = nt - 1 - step
        keep = (ti > 0).astype(F32)

        @pl.when(step == 0)
        def _():
            for r in (dwp, dps, dlg, dlb, dwm, dcw, dcb, dcg, dcbeta, dbin, carry_p, carry_c, dbs_acc):
                r[...] = jnp.zeros_like(r)

        def tail(carry):
            return jnp.concatenate([jnp.zeros((R - HALO, carry.shape[1]), F32), carry], axis=0)

        def head(x):
            return jnp.concatenate([jnp.zeros((HALO, x.shape[1]), F32), x], axis=0)

        a_ext = _with_halo(ph, pm, C_POOL, keep)
        carry_in = carry_p[...]
        for gi, w in enumerate(POOL_WINDOWS):
            cs = slice(gi * GROUP, (gi + 1) * GROUP)
            e = a_ext[:, cs]
            s = _causal_window_sum(e, w)
            cnt = _pool_counts(ti, R, w)
            pooled_b = (s[HALO:, :] / cnt - e[HALO:, :]).astype(BF16)
            wg = wp[cs, :]
            z = jnp.dot(pooled_b, wg, preferred_element_type=F32)
            dya = dm[:, cs]
            dps[:, cs] += _row_sum(dya * z)
            dz_b = (dya * ps[:, cs]).astype(BF16)
            dwp[cs, :] += lax.dot_general(pooled_b, dz_b, (((0,), (0,)), ((), ())), preferred_element_type=F32)
            dpooled = lax.dot_general(dz_b, wg, (((1,), (1,)), ((), ())), preferred_element_type=F32)
            da_ext = _anticausal_window_sum(head(dpooled / cnt), w) - head(dpooled)
            carry_p[:, cs] = da_ext[:HALO, :]
            d_a = da_ext[HALO:, :] + tail(carry_in[:, cs])
            dbin[:, cs] += _row_sum(d_a)
            dproj[:, cs] = d_a.astype(BF16)

        pu = _cols(pm, C_U)
        pv = _cols(pm, C_V)
        u, thu = _gelu(pu)
        v, thv = _gelu(pv)
        vhat, vrs = _ln_stats(v)
        vn = vhat * lg[...] + lb[...]
        dyb = dm[:, M_SGU[0] : M_SGU[1]]
        du_parts, dvn_parts = [], []
        for c in range(R // GROUP):
            rs = slice(c * GROUP, (c + 1) * GROUP)
            vnb = vn[rs, :].astype(BF16)
            mixed = _sgu_mix(wm, vnb) + bsf[...]
            du_parts.append(dyb[rs, :] * mixed)
            dmixed = dyb[rs, :] * u[rs, :]
            dbs_acc[...] += dmixed
            dmb = dmixed.astype(BF16)
            dvn_h = []
            for h in range(SGU_HEADS):
                hs = slice(h * GROUP, (h + 1) * GROUP)
                dwm[hs, :] += lax.dot_general(dmb[:, hs], vnb[:, hs], (((1,), (1,)), ((), ())), preferred_element_type=F32)
                dvn_h.append(jnp.dot(wmt[hs, :], dmb[:, hs], preferred_element_type=F32))
            dvn_parts.append(jnp.concatenate(dvn_h, axis=1))
        du = jnp.concatenate(du_parts, axis=0) if len(du_parts) > 1 else du_parts[0]
        dvn = jnp.concatenate(dvn_parts, axis=0) if len(dvn_parts) > 1 else dvn_parts[0]
        dlg[...] += _row_sum(dvn * vhat)
        dlb[...] += _row_sum(dvn)
        d_pu = du * _gelu_grad(pu, thu)
        d_pv = _ln_bwd(dvn, vhat, vrs, lg[...]) * _gelu_grad(pv, thv)
        dbin[:, C_U[0] : C_U[1]] += _row_sum(d_pu)
        dbin[:, C_V[0] : C_V[1]] += _row_sum(d_pv)
        dproj[:, C_U[0] : C_U[1]] = d_pu.astype(BF16)
        dproj[:, C_V[0] : C_V[1]] = d_pv.astype(BF16)

        sg_ext = jax.nn.sigmoid(_with_halo(ph, pm, C_CG, keep))
        ca_ext = _with_halo(ph, pm, C_CA, keep)
        hbuf[...] = ca_ext * sg_ext
        for cols in CONV_HALVES:
            _build_shifts(shf, hbuf, cols, E)
            _conv_taps(shf, hbuf, cw8, cols, tap_offsets, R, convbuf, bias=cb)
        chat, crs = _ln_stats(convbuf[...])
        cn = chat * cg[...] + cbeta[...]
        sc = jax.nn.sigmoid(cn)
        dcn = dm[:, M_CONV[0] : M_CONV[1]] * (sc * (1.0 + cn * (1.0 - sc)))
        dcg[...] += _row_sum(dcn * chat)
        dcbeta[...] += _row_sum(dcn)
        dconv = _ln_bwd(dcn, chat, crs, cg[...])
        dcb[...] += _row_sum(dconv)
        dbuf[pl.ds(0, HALO), :] = jnp.zeros((HALO, CONV_WIDTH), F32)
        dbuf[pl.ds(HALO, R), :] = dconv
        dbuf[pl.ds(HALO + R, HALO), :] = jnp.zeros((HALO, CONV_WIDTH), F32)
        for cols in CONV_HALVES:
            _build_shifts(shf, hbuf, cols, E)
            _conv_weight_grad(shf, hbuf, dbuf, HALO, cols, tap_offsets, R, dcw)
            _build_shifts(shf, dbuf, cols, E + HALO)
            _conv_taps(shf, dbuf, cw8, cols, back_offsets, E, dhcbuf)
        dhc_main = dhcbuf[pl.ds(HALO, R), :] + tail(carry_c[...])
        carry_c[...] = dhcbuf[pl.ds(0, HALO), :]
        sg = sg_ext[HALO:, :]
        d_ca = dhc_main * sg
        d_cg = dhc_main * ca_ext[HALO:, :] * (sg * (1.0 - sg))
        dbin[:, C_CA[0] : C_CA[1]] += _row_sum(d_ca)
        dbin[:, C_CG[0] : C_CG[1]] += _row_sum(d_cg)
        dproj[:, C_CA[0] : C_CA[1]] = d_ca.astype(BF16)
        dproj[:, C_CG[0] : C_CG[1]] = d_cg.astype(BF16)

        @pl.when(step == nt - 1)
        def _():
            row = lax.broadcasted_iota(jnp.int32, (GROUP, GROUP), 0)
            col = lax.broadcasted_iota(jnp.int32, (GROUP, GROUP), 1)
            dbs[...] = jnp.zeros_like(dbs)
            for h in range(SGU_HEADS):
                hs = slice(h * GROUP, (h + 1) * GROUP)
                dwm[hs, :] = jnp.where(row >= col, dwm[hs, :], 0.0)
                dbs[pl.ds(h, 1), :] = _row_sum(dbs_acc[:, hs].T)

    params = _mixer_params(p)
    accs = [
        S((POOL_WIDTH, GROUP), F32), S((1, POOL_WIDTH), F32), S((1, SGU_WIDTH), F32), S((1, SGU_WIDTH), F32),
        S((SGU_WIDTH, GROUP), F32), S((8, GROUP), F32), S((32, CONV_WIDTH), F32), S((1, CONV_WIDTH), F32),
        S((1, CONV_WIDTH), F32), S((1, CONV_WIDTH), F32), S((1, IN_WIDTH), F32),
    ]
    in_specs = [
        pl.BlockSpec((R, IN_WIDTH), lambda i: (nt - 1 - i, 0)),
        pl.BlockSpec((HALO, IN_WIDTH), lambda i: (jnp.maximum((nt - 1 - i) * hb - 1, 0), 0)),
        pl.BlockSpec((R, D_MODEL), lambda i: (nt - 1 - i, 0)),
    ] + [_whole(x) for x in params]
    scratch = [
        pltpu.VMEM((E, CONV_WIDTH), F32), pltpu.VMEM((E + HALO, CONV_WIDTH), F32),
        pltpu.VMEM((HALO, POOL_WIDTH), F32), pltpu.VMEM((HALO, CONV_WIDTH), F32), pltpu.VMEM((GROUP, SGU_WIDTH), F32),
        pltpu.VMEM((7, E + HALO, CONV_WIDTH // 2), F32), pltpu.VMEM((R, CONV_WIDTH), F32), pltpu.VMEM((E, CONV_WIDTH), F32),
    ]
    return _call(
        name, body, (nt,), in_specs, [pl.BlockSpec((R, IN_WIDTH), lambda i: (nt - 1 - i, 0))] + [_whole(x) for x in accs],
        [S((T, IN_WIDTH), BF16)] + accs, scratch, [proj, proj, dmix, *params], side,
    )


def _all_gather(xs):
    n = len(xs)

    def body(*refs):
        x_refs, o_refs = refs[:n], refs[n : 2 * n]
        send_sems, recv_sems, local_sems = refs[2 * n :]
        x, y, c = _place()
        me, sibling = (x, y, c), (x, y, 1 - c)
        chips = [(1 - x, y), (x, 1 - y), (1 - x, 1 - y)]

        def copy(a, k, block, to, src=None):
            dst = o_refs[a].at[_lin(block)]
            return pltpu.make_async_remote_copy(
                src_ref=dst if src is None else src, dst_ref=dst, send_sem=send_sems.at[a, k], recv_sem=recv_sems.at[a, k],
                device_id=to, device_id_type=MESH,
            )

        mine = [pltpu.make_async_copy(x_refs[a], o_refs[a].at[_lin(me)], local_sems.at[a]) for a in range(n)]
        for m in mine:
            m.start()
        first = []
        for a in range(n):
            first.append(copy(a, 0, me, sibling, src=x_refs[a]))
            first += [copy(a, 1 + j, me, (*chip, c), src=x_refs[a]) for j, chip in enumerate(chips)]
        for cp in first:
            cp.start()
        passed = []
        for a in range(n):
            for j, chip in enumerate(chips):
                copy(a, 1 + j, (*chip, c), me).wait_recv()
                fwd = copy(a, 4 + j, (*chip, c), sibling)
                fwd.start()
                passed.append(fwd)
        for a in range(n):
            copy(a, 0, sibling, me).wait_recv()
            for j, chip in enumerate(chips):
                copy(a, 4 + j, (*chip, 1 - c), me).wait_recv()
        for cp in first + passed:
            cp.wait_send()
        for m in mine:
            m.wait()

    return pl.pallas_call(
        body,
        name="all_gather_weights",
        in_specs=[_ANY] * n,
        out_specs=[_ANY] * n,
        out_shape=[S((N_DEV, *x.shape), x.dtype) for x in xs],
        scratch_shapes=[pltpu.SemaphoreType.DMA((n, 7)), pltpu.SemaphoreType.DMA((n, 7)), pltpu.SemaphoreType.DMA((n,))],
    )(*xs)


def _row_tile(rows, want):
    return next(t for t in range(min(rows, want) // 8 * 8, 0, -8) if rows % t == 0)


def _sum_slots(name, slots):
    _, rows, cols = slots.shape
    tr = _row_tile(rows, (4 << 20) // (N_DEV * cols * slots.dtype.itemsize))

    def body(s_ref, o_ref):
        total = s_ref[0].astype(F32)
        for d in range(1, N_DEV):
            total = total + s_ref[d].astype(F32)
        o_ref[...] = total

    return pl.pallas_call(
        body,
        name=name,
        grid=(rows // tr,),
        in_specs=[pl.BlockSpec((N_DEV, tr, cols), lambda i: (0, i, 0))],
        out_specs=pl.BlockSpec((tr, cols), lambda i: (i, 0)),
        out_shape=S((rows, cols), F32),
        compiler_params=_cparams(1),
    )(slots)


def _adamw(name, w, g, m, v):
    rows, cols = w.shape
    tr = rows if rows * cols * 4 <= (2 << 20) else _row_tile(rows, 1 << ((1 << 18) // cols).bit_length() - 1)

    def body(w_ref, g_ref, m_ref, v_ref, d_ref, nm_ref, nv_ref):
        gv = g_ref[...]
        nm = ADAM_B1 * m_ref[...] + (1.0 - ADAM_B1) * gv
        nv = ADAM_B2 * v_ref[...] + (1.0 - ADAM_B2) * (gv * gv)
        m_hat = nm / (1.0 - ADAM_B1**ADAM_STEP)
        v_hat = nv / (1.0 - ADAM_B2**ADAM_STEP)
        d_ref[...] = -ADAM_LR * (m_hat / (jnp.sqrt(v_hat) + ADAM_EPS) + ADAM_WD * w_ref[...])
        nm_ref[...] = nm
        nv_ref[...] = nv

    blk = pl.BlockSpec((tr, cols), lambda i: (i, 0))
    return pl.pallas_call(
        body,
        name=name,
        grid=(rows // tr,),
        in_specs=[blk] * 4,
        out_specs=[blk] * 3,
        out_shape=[S((rows, cols), F32)] * 3,
        compiler_params=_cparams(1),
    )(w, g, m, v)


_BIG = ("w_in", "w_out", "w_ff1", "w_ff2")
_TRANSPOSED = ("w_in", "w_ff1")
_SMALL = ("b_in", "w_pool", "pool_scale", "sgu_ln_g", "sgu_ln_b", "sgu_w", "sgu_b", "conv_b", "conv_ln_g", "conv_ln_b",
          "b_out", "ln1_g", "ln1_b", "b_ff1", "b_ff2", "ln2_g", "ln2_b")
_WEIGHTS = ("w_in", "b_in", "w_pool", "pool_scale", "sgu_ln_g", "sgu_ln_b", "sgu_w", "sgu_b", "conv_w", "conv_b", "conv_ln_g",
            "conv_ln_b", "w_out", "b_out", "ln1_g", "ln1_b", "w_ff1", "b_ff1", "w_ff2", "b_ff2", "ln2_g", "ln2_b")


def _pack(arrays):
    parts = []
    for a in arrays:
        rows = a.reshape(-1, 128)
        parts.append(jnp.pad(rows, ((0, -rows.shape[0] % 8), (0, 0))))
    return jnp.concatenate(parts, axis=0)


def _unpack(flat, like):
    out, at = [], 0
    for a in like:
        n = a.size // 128
        out.append(flat[at : at + n].reshape(a.shape))
        at += n + (-n % 8)
    return out


def _packed_rows(arrays):
    return sum(a.size // 128 + (-(a.size // 128) % 8) for a in arrays)


def kernel(x, w_in, b_in, w_pool, pool_scale, sgu_ln_g, sgu_ln_b, sgu_w, sgu_b, conv_w, conv_b, conv_ln_g, conv_ln_b, w_out, b_out, ln1_g, ln1_b, w_ff1, b_ff1, w_ff2, b_ff2, ln2_g, ln2_b, loss_target, m_w_in, m_b_in, m_w_pool, m_pool_scale, m_sgu_ln_g, m_sgu_ln_b, m_sgu_w, m_sgu_b, m_conv_w, m_conv_b, m_conv_ln_g, m_conv_ln_b, m_w_out, m_b_out, m_ln1_g, m_ln1_b, m_w_ff1, m_b_ff1, m_w_ff2, m_b_ff2, m_ln2_g, m_ln2_b, v_w_in, v_b_in, v_w_pool, v_pool_scale, v_sgu_ln_g, v_sgu_ln_b, v_sgu_w, v_sgu_b, v_conv_w, v_conv_b, v_conv_ln_g, v_conv_ln_b, v_w_out, v_b_out, v_ln1_g, v_ln1_b, v_w_ff1, v_b_ff1, v_w_ff2, v_b_ff2, v_ln2_g, v_ln2_b):
    w = dict(w_in=w_in, b_in=b_in, w_pool=w_pool, pool_scale=pool_scale, sgu_ln_g=sgu_ln_g, sgu_ln_b=sgu_ln_b, sgu_w=sgu_w,
             sgu_b=sgu_b, conv_w=conv_w, conv_b=conv_b, conv_ln_g=conv_ln_g, conv_ln_b=conv_ln_b, w_out=w_out, b_out=b_out,
             ln1_g=ln1_g, ln1_b=ln1_b, w_ff1=w_ff1, b_ff1=b_ff1, w_ff2=w_ff2, b_ff2=b_ff2, ln2_g=ln2_g, ln2_b=ln2_b)
    mom = dict(w_in=m_w_in, b_in=m_b_in, w_pool=m_w_pool, pool_scale=m_pool_scale, sgu_ln_g=m_sgu_ln_g, sgu_ln_b=m_sgu_ln_b,
               sgu_w=m_sgu_w, sgu_b=m_sgu_b, conv_w=m_conv_w, conv_b=m_conv_b, conv_ln_g=m_conv_ln_g, conv_ln_b=m_conv_ln_b,
               w_out=m_w_out, b_out=m_b_out, ln1_g=m_ln1_g, ln1_b=m_ln1_b, w_ff1=m_w_ff1, b_ff1=m_b_ff1, w_ff2=m_w_ff2,
               b_ff2=m_b_ff2, ln2_g=m_ln2_g, ln2_b=m_ln2_b)
    var = dict(w_in=v_w_in, b_in=v_b_in, w_pool=v_w_pool, pool_scale=v_pool_scale, sgu_ln_g=v_sgu_ln_g, sgu_ln_b=v_sgu_ln_b,
               sgu_w=v_sgu_w, sgu_b=v_sgu_b, conv_w=v_conv_w, conv_b=v_conv_b, conv_ln_g=v_conv_ln_g, conv_ln_b=v_conv_ln_b,
               w_out=v_w_out, b_out=v_b_out, ln1_g=v_ln1_g, ln1_b=v_ln1_b, w_ff1=v_w_ff1, b_ff1=v_b_ff1, w_ff2=v_w_ff2,
               b_ff2=v_b_ff2, ln2_g=v_ln2_g, ln2_b=v_ln2_b)
    T = x.shape[1]
    x0 = x.reshape(T, D_MODEL)
    target = loss_target.reshape(T, D_MODEL)
    me_lin = _lin(_place())

    shard = [
        {name: (w[name][l].T if name in _TRANSPOSED else w[name][l]).astype(BF16) for name in _BIG} for l in range(DEPTH)
    ]
    conv_shard = jnp.pad(conv_w, ((0, 0), (0, 1), (0, 128 - conv_w.shape[2]))).reshape(DEPTH * 32, 128)

    def rows_of(g):
        return g.reshape(N_DEV * g.shape[1], g.shape[2])

    first = _all_gather([shard[0]["w_in"], shard[0]["w_out"], conv_shard])
    full = [{} for _ in range(DEPTH)]
    full[0]["w_in"], full[0]["w_out"] = rows_of(first[0]), rows_of(first[1])
    conv_cols = conv_w.shape[2]
    conv_full = first[2].reshape(N_DEV, DEPTH, 32, 128)[:, :, :CONV_KERNEL, :conv_cols]
    conv_full = conv_full.transpose(1, 2, 0, 3).reshape(DEPTH, CONV_KERNEL, N_DEV * conv_cols)

    tril = jnp.tril(jnp.ones((GROUP, GROUP), F32))
    prm = []
    for l in range(DEPTH):
        wm = sgu_w[l] * tril
        prm.append(dict(
            wp=w_pool[l].reshape(POOL_WIDTH, GROUP).astype(BF16), ps=_row(pool_scale[l]), lg=_row(sgu_ln_g[l]), lb=_row(sgu_ln_b[l]),
            wm=wm.reshape(SGU_WIDTH, GROUP).astype(BF16), wmt=wm.transpose(0, 2, 1).reshape(SGU_WIDTH, GROUP).astype(BF16),
            bsf=jnp.repeat(sgu_b[l].T, GROUP, axis=1), cw8=jnp.repeat(jnp.pad(conv_full[l], ((0, 1), (0, 0))), 8, axis=0), cb=_row(conv_b[l]),
            cg=_row(conv_ln_g[l]), cbeta=_row(conv_ln_b[l]),
        ))

    saved = []
    res = (x0, jnp.ones((D_MODEL,), F32), jnp.zeros((D_MODEL,), F32))
    xbf = x0.astype(BF16)
    u = shard[0]["w_ff1"].shape[0] // 8

    for l in range(DEPTH):
        f = full[l]
        s_ff1, s_ff2 = shard[l]["w_ff1"], shard[l]["w_ff2"]
        ex = _Exchange([("gather", s_ff1, (0, 3 * u), None)])
        proj = _mm_bias(f"proj{l}", xbf, f["w_in"], "nt", b_in[l], 1024, 896, 2048, side=ex)
        ex = _Exchange([("gather", s_ff1, (3 * u, 3 * u), ex.results[0])])
        mixed = _mixer_fwd(f"mixer_fwd{l}", proj, prm[l], 256, side=ex)
        ex = _Exchange([("gather", s_ff1, (6 * u, 2 * u), ex.results[0]), ("gather", s_ff2, (0, u), None)])
        xh1, rs1, x1bf = _mm_ln(f"out_ln1_{l}", mixed, f["w_out"], b_out[l], res, ln1_g[l], ln1_b[l], 1024, 512, side=ex)
        f["w_ff1"] = rows_of(ex.results[0])
        ex = _Exchange([("gather", s_ff2, (u, 7 * u), ex.results[1])])
        act, hsq = _mm_relu2(f"ff1_{l}", x1bf, f["w_ff1"], "nt", b_ff1[l], 1024, 1024, 2048, side=ex)
        f["w_ff2"] = rows_of(ex.results[0])
        ex = _Exchange([("gather", shard[l + 1]["w_in"]), ("gather", shard[l + 1]["w_out"])]) if l + 1 < DEPTH else None
        xh2, rs2, x2bf = _mm_ln(
            f"ff2_ln2_{l}", hsq, f["w_ff2"], b_ff2[l], (xh1, ln1_g[l], ln1_b[l]), ln2_g[l], ln2_b[l], 1024, 1024, side=ex)
        if ex is not None:
            full[l + 1]["w_in"], full[l + 1]["w_out"] = rows_of(ex.results[0]), rows_of(ex.results[1])
        saved.append(dict(xin=xbf, proj=proj, mixed=mixed, xh1=xh1, rs1=rs1, x1bf=x1bf, act=act, hsq=hsq, xh2=xh2, rs2=rs2))
        res = (xh2, ln2_g[l], ln2_b[l])
        xbf = x2bf

    top = saved[-1]
    dr2, dr2bf, g_ln2g, g_ln2b, g_bff2, loss_row = _loss_top(top["xh2"], top["rs2"], ln2_g[-1], ln2_b[-1], target, 256)
    loss = lax.psum(loss_row[0, 0], ("x", "y", "c"))
    slots = [{} for _ in range(DEPTH)]
    gsm = [{} for _ in range(DEPTH)]
    grad_x = small_slots = None

    def stacked_small():
        st = {name: jnp.stack([gsm[gl][name].reshape(w[name].shape[1:]) for gl in range(DEPTH)]) for name in _SMALL}
        conv_g = jnp.pad(jnp.stack([gsm[gl]["conv_w"] for gl in range(DEPTH)]), ((0, 0), (0, 1), (0, 0)))
        return [st[name] for name in _SMALL] + [conv_g]

    for l in reversed(range(DEPTH)):
        f, sv = full[l], saved[l]
        gsm[l].update(ln2_g=g_ln2g, ln2_b=g_ln2b, b_ff2=g_bff2)
        gw = _mm_wgrad(f"gw_ff2_{l}", sv["hsq"], dr2bf, 1024, 2048)
        ex = _Exchange([("slices", gw)])
        dhpre, g_bff1 = _mm_dh(f"dff1_{l}", dr2bf, f["w_ff2"], sv["act"], 1024, 1024, 2048, side=ex)
        slots[l]["w_ff2"] = ex.results[0]
        gsm[l]["b_ff1"] = g_bff1
        gw = _mm_wgrad(f"gw_ff1_{l}", dhpre, sv["x1bf"], 1024, 2048)
        ex = _Exchange([("slices", gw)])
        dr1, dr1bf, g_ln1g, g_ln1b, g_bout = _mm_ln_bwd(
            f"dx1_ln1_{l}", dhpre, f["w_ff1"], dr2, sv["xh1"], sv["rs1"], ln1_g[l], 1024, 1024, side=ex)
        slots[l]["w_ff1"] = ex.results[0]
        gsm[l].update(ln1_g=g_ln1g, ln1_b=g_ln1b, b_out=g_bout)
        gw = _mm_wgrad(f"gw_out_{l}", sv["mixed"], dr1bf, 1024, 2048)
        dmix = _mm_plain(f"dmixed{l}", dr1bf, f["w_out"], "nt", 1024, 1024, 2048)
        ex = _Exchange([("slices", gw)])
        (dproj, g_wp, g_ps, g_lg, g_lb, g_wm, g_bs, g_cw, g_cb, g_cg, g_cbeta, g_bin) = _mixer_bwd(
            f"mixer_bwd{l}", sv["proj"], dmix, prm[l], 256, side=ex)
        slots[l]["w_out"] = ex.results[0]
        gsm[l].update(b_in=g_bin, w_pool=g_wp, pool_scale=g_ps, sgu_ln_g=g_lg, sgu_ln_b=g_lb, sgu_w=g_wm, sgu_b=g_bs[:SGU_HEADS],
                      conv_w=g_cw[:CONV_KERNEL], conv_b=g_cb, conv_ln_g=g_cg, conv_ln_b=g_cbeta)
        if l > 0:
            gw = _mm_wgrad(f"gw_in_{l}", dproj, sv["xin"], 896, 2048)
            below = saved[l - 1]
            ex = _Exchange([("slices", gw)])
            dr2, dr2bf, g_ln2g, g_ln2b, g_bff2 = _mm_ln_bwd(
                f"dx_ln2_{l}", dproj, f["w_in"], dr1, below["xh2"], below["rs2"], ln2_g[l - 1], 1024, 512, side=ex)
        else:
            small_like = stacked_small()
            ex = _Exchange([("gather", _pack(small_like))])
            gw = _mm_wgrad(f"gw_in_{l}", dproj, sv["xin"], 896, 2048, side=ex)
            small_slots = ex.results[0]
            ex = _Exchange([("slices", gw)])
            grad_x = _mm_res("dx0", dproj, f["w_in"], dr1, 1024, 512, side=ex)
        slots[l]["w_in"] = ex.results[0]

    grads, deltas, new_m, new_v = {}, {}, {}, {}
    for name in _BIG:
        per_layer = []
        for l in range(DEPTH):
            g = _sum_slots(f"sum_{name}_{l}", slots[l][name])
            per_layer.append(g.T if name in _TRANSPOSED else g)
        g = jnp.stack(per_layer)
        shape = w[name].shape
        two_d = (shape[0] * shape[1], shape[2])
        d, nm, nv = _adamw(f"adamw_{name}", w[name].reshape(two_d), g.reshape(two_d), mom[name].reshape(two_d), var[name].reshape(two_d))
        grads[name], deltas[name], new_m[name], new_v[name] = g, d.reshape(shape), nm.reshape(shape), nv.reshape(shape)

    total = _sum_slots("sum_small", small_slots)
    small_g = _unpack(total, small_like)
    like = [w[name] for name in _SMALL]
    d, nm, nv = _adamw("adamw_small", _pack(like), total[: _packed_rows(like)],
                       _pack([mom[name] for name in _SMALL]), _pack([var[name] for name in _SMALL]))
    for name, gg, dd, mm_, vv in zip(_SMALL, small_g, _unpack(d, like), _unpack(nm, like), _unpack(nv, like)):
        grads[name], deltas[name], new_m[name], new_v[name] = gg, dd, mm_, vv
    conv_g = lax.dynamic_slice_in_dim(small_g[-1][:, :CONV_KERNEL, :], me_lin * conv_cols, conv_cols, axis=2)
    flat = (DEPTH * CONV_KERNEL, conv_cols)
    d, nm, nv = _adamw("adamw_conv_w", conv_w.reshape(flat), conv_g.reshape(flat), m_conv_w.reshape(flat), v_conv_w.reshape(flat))
    grads["conv_w"], deltas["conv_w"], new_m["conv_w"], new_v["conv_w"] = conv_g, d.reshape(conv_w.shape), nm.reshape(conv_w.shape), nv.reshape(conv_w.shape)

    return (loss, grad_x.reshape(x.shape), *[grads[n] for n in _WEIGHTS], *[deltas[n] for n in _WEIGHTS],
            *[new_m[n] for n in _WEIGHTS], *[new_v[n] for n in _WEIGHTS])
```

```python
import functools

import jax
import jax.numpy as jnp
from jax import lax
from jax.experimental import pallas as pl
from jax.experimental.pallas import tpu as pltpu

F32, BF16 = jnp.float32, jnp.bfloat16
S = jax.ShapeDtypeStruct

DEPTH = 2
D_MODEL = 2048
POOL_WINDOWS = (2, 4, 8, 16)
POOL_WIDTH = 512
GROUP = 128
SGU_WIDTH = 768
SGU_HEADS = 6
CONV_WIDTH = 768
CONV_KERNEL = 31
IN_WIDTH = 3584
D_FF = 8192
ALPHA = (2 * DEPTH) ** 0.25
LN_EPS = 1e-5
ADAM_LR, ADAM_B1, ADAM_B2, ADAM_EPS, ADAM_WD, ADAM_STEP = 0.001, 0.9, 0.999, 1e-08, 0.01, 10

N_DEV = 8
LN_STRIP = 32
LN_UNROLL = 4
HALO = 32
VMEM_LIMIT = 56 << 20
MESH = pl.DeviceIdType.MESH

C_POOL = (0, 512)
C_U = (512, 1280)
C_V = (1280, 2048)
C_CA = (2048, 2816)
C_CG = (2816, 3584)
M_POOL = (0, 512)
M_SGU = (512, 1280)
M_CONV = (1280, 2048)


def _cparams(n_axes):
    return pltpu.CompilerParams(dimension_semantics=("arbitrary",) * n_axes, vmem_limit_bytes=VMEM_LIMIT)


def _for_strips(rows, strip, fn, unroll=1):
    n = rows // strip
    if n == 1:
        fn(0)
        return

    def step(s, carry):
        fn(pl.multiple_of(s * strip, strip))
        return carry

    lax.fori_loop(0, n, step, 0, unroll=unroll)


def _row_sum(x):
    return jnp.sum(x, axis=0, keepdims=True)


def _ln_stats(r):
    mu = jnp.mean(r, axis=-1, keepdims=True)
    xc = r - mu
    var = jnp.mean(xc * xc, axis=-1, keepdims=True)
    rs = lax.rsqrt(var + LN_EPS)
    return xc * rs, rs


def _ln_bwd(dy, xhat, rs, g):
    gy = dy * g
    m1 = jnp.mean(gy, axis=-1, keepdims=True)
    m2 = jnp.mean(gy * xhat, axis=-1, keepdims=True)
    return rs * (gy - m1 - xhat * m2)


_GELU_C = 0.7978845608028654


def _gelu(x):
    th = jnp.tanh(_GELU_C * (x + 0.044715 * (x * x * x)))
    return 0.5 * x * (1.0 + th), th


def _gelu_grad(x, th):
    return 0.5 * (1.0 + th) + 0.5 * x * (1.0 - th * th) * (_GELU_C * (1.0 + 3.0 * 0.044715 * (x * x)))


def _place():
    x, y, c = lax.axis_index("x"), lax.axis_index("y"), lax.axis_index("c")
    return x, y, c


def _lin(p):
    return 4 * p[0] + 2 * p[1] + p[2]


def _flip(p, r):
    return tuple(1 - v if (r >> (2 - ax)) & 1 else v for ax, v in enumerate(p))


_ANY = pl.BlockSpec(memory_space=pl.ANY)


class _Exchange:
    def __init__(self, items):
        self.kinds = [item[0] for item in items]
        self.srcs = [item[1] for item in items]
        self.rows = [item[2] if len(item) > 2 else None for item in items]
        handed_on = [item[3] if len(item) > 3 else None for item in items]
        self.out_shape = [
            S((N_DEV, *x.shape), x.dtype) if kind == "gather" else S((N_DEV, x.shape[0] // N_DEV, x.shape[1]), x.dtype)
            for kind, x in zip(self.kinds, self.srcs)
        ]
        n = len(items)
        self.ins = self.srcs + [b for b in handed_on if b is not None]
        self.aliases = {}
        for a, b in enumerate(handed_on):
            if b is not None:
                self.aliases[n + len(self.aliases)] = a
        self.scratch = [pltpu.SemaphoreType.DMA((n, 7)), pltpu.SemaphoreType.DMA((n, 7)), pltpu.SemaphoreType.DMA((n,))]
        self.results = None

    def _src(self, in_refs, a, dest):
        if self.kinds[a] == "gather":
            return in_refs[a] if self.rows[a] is None else in_refs[a].at[pl.ds(*self.rows[a])]
        rows = self.srcs[a].shape[0] // N_DEV
        return in_refs[a].at[pl.ds(pl.multiple_of(_lin(dest) * rows, 8), rows)]

    def _dst(self, out_refs, a, slot):
        return out_refs[a].at[slot] if self.rows[a] is None else out_refs[a].at[slot, pl.ds(*self.rows[a])]

    def _copies(self, in_refs, out_refs, sems, with_arrivals):
        send_sems, recv_sems, local_sems = sems
        me = _place()
        local, sends, arrivals = [], [], []
        for a in range(len(self.srcs)):
            local.append(pltpu.make_async_copy(self._src(in_refs, a, me), self._dst(out_refs, a, _lin(me)), local_sems.at[a]))
            for r in range(1, N_DEV):
                peer = _flip(me, r)
                for slot, group in ((_lin(me), sends), (_lin(peer), arrivals)):
                    if group is sends or with_arrivals:
                        group.append(pltpu.make_async_remote_copy(
                            src_ref=self._src(in_refs, a, peer), dst_ref=self._dst(out_refs, a, slot),
                            send_sem=send_sems.at[a, r - 1], recv_sem=recv_sems.at[a, r - 1], device_id=peer, device_id_type=MESH,
                        ))
        return local, sends, arrivals

    def start(self, in_refs, out_refs, sems):
        local, sends, _ = self._copies(in_refs, out_refs, sems, False)
        for cp in local + sends:
            cp.start()

    def wait(self, in_refs, out_refs, sems):
        local, sends, arrivals = self._copies(in_refs, out_refs, sems, True)
        for cp in arrivals:
            cp.wait_recv()
        for cp in sends:
            cp.wait_send()
        for cp in local:
            cp.wait()


def _call(name, body, grid, in_specs, out_specs, out_shape, scratch, args, side=None):
    in_specs, out_specs, out_shape, scratch = list(in_specs), list(out_specs), list(out_shape), list(scratch)
    if side is None:
        return pl.pallas_call(
            body, name=name, grid=grid, in_specs=in_specs, out_specs=out_specs, out_shape=out_shape, scratch_shapes=scratch,
            compiler_params=_cparams(len(grid)),
        )(*args)
    n_in, n_out, n_scr = len(in_specs), len(out_specs), len(scratch)
    s_in, s_out = len(side.ins), len(side.out_shape)

    def wrapped(*refs):
        at = 0
        parts = []
        for n in (n_in, s_in, n_out, s_out, n_scr, 3):
            parts.append(refs[at : at + n])
            at += n
        ins, side_ins, outs, side_outs, scr, sems = parts
        pids = [pl.program_id(d) for d in range(len(grid))]
        first = functools.reduce(jnp.logical_and, [p == 0 for p in pids])
        last = functools.reduce(jnp.logical_and, [p == g - 1 for p, g in zip(pids, grid)])

        @pl.when(first)
        def _():
            side.start(side_ins, side_outs, sems)

        body(*ins, *outs, *scr)

        @pl.when(last)
        def _():
            side.wait(side_ins, side_outs, sems)

    res = pl.pallas_call(
        wrapped, name=name, grid=grid, in_specs=in_specs + [_ANY] * s_in, out_specs=out_specs + [_ANY] * s_out,
        out_shape=out_shape + side.out_shape, scratch_shapes=scratch + side.scratch, compiler_params=_cparams(len(grid)),
        input_output_aliases={n_in + i: n_out + o for i, o in side.aliases.items()},
    )(*args, *side.ins)
    side.results = list(res[n_out:])
    return list(res[:n_out])


_CONTRACT = {"nn": ((1,), (0,)), "nt": ((1,), (1,)), "tn": ((0,), (0,))}


def _mm(name, a, b, dims, tm, tn, tk, *, ins=(), outs, epilogue, j_outer=False, side=None):
    if dims == "tn":
        K, M = a.shape
    else:
        M, K = a.shape
    N = b.shape[0] if dims == "nt" else b.shape[1]
    tm, tn, tk = min(tm, M), min(tn, N), min(tk, K)
    assert M % tm == 0 and N % tn == 0 and K % tk == 0, (name, M, N, K, tm, tn, tk)
    nm, nn, nk = M // tm, N // tn, K // tk
    if j_outer:
        grid = (nn, nm, nk)
        ij = lambda g0, g1: (g1, g0)
    else:
        grid = (nm, nn, nk)
        ij = lambda g0, g1: (g0, g1)

    def amap(g0, g1, k):
        i, _ = ij(g0, g1)
        return (k, i) if dims == "tn" else (i, k)

    def bmap(g0, g1, k):
        _, j = ij(g0, g1)
        return (j, k) if dims == "nt" else (k, j)

    def spec(kind):
        if kind == "tile":
            return pl.BlockSpec((tm, tn), lambda g0, g1, k: ij(g0, g1))
        if kind == "row":
            return pl.BlockSpec((1, tn), lambda g0, g1, k: (0, ij(g0, g1)[1]))
        assert kind == "col", kind
        return pl.BlockSpec((tm, 1), lambda g0, g1, k: (ij(g0, g1)[0], 0))

    in_specs = [
        pl.BlockSpec((tk, tm) if dims == "tn" else (tm, tk), amap),
        pl.BlockSpec((tn, tk) if dims == "nt" else (tk, tn), bmap),
    ] + [spec(kind) for _, kind in ins]
    out_specs = [spec(kind) for _, _, kind in outs]
    out_shape = [S(shape, dtype) for shape, dtype, _ in outs]
    n_in, n_out = len(ins), len(outs)
    contract = (_CONTRACT[dims], ((), ()))

    def body(*refs):
        a_ref, b_ref = refs[:2]
        in_refs = refs[2 : 2 + n_in]
        out_refs = refs[2 + n_in : 2 + n_in + n_out]
        acc = refs[2 + n_in + n_out]
        i, _ = ij(pl.program_id(0), pl.program_id(1))
        k = pl.program_id(2)

        def part():
            return lax.dot_general(a_ref[...], b_ref[...], contract, preferred_element_type=F32)

        @pl.when(k == 0)
        def _():
            acc[...] = part()

        @pl.when(k > 0)
        def _():
            acc[...] += part()

        @pl.when(k == nk - 1)
        def _():
            epilogue(i, acc, in_refs, out_refs)

    return _call(name, body, grid, in_specs, out_specs, out_shape, [pltpu.VMEM((tm, tn), F32)], [a, b, *[x for x, _ in ins]], side)


def _row(v):
    return v.reshape(1, -1)


def _mm_bias(name, a, b, dims, bias, tm, tn, tk, side=None):
    M = a.shape[0]
    N = b.shape[0] if dims == "nt" else b.shape[1]

    def epilogue(i, acc, ins, outs):
        def strip(r0):
            rows = pl.ds(r0, 128)
            outs[0][rows, :] = acc[rows, :] + ins[0][...]

        _for_strips(acc.shape[0], 128, strip)

    return _mm(name, a, b, dims, tm, tn, tk, ins=[(_row(bias), "row")], outs=[((M, N), F32, "tile")], epilogue=epilogue, side=side)[0]


def _mm_relu2(name, a, b, dims, bias, tm, tn, tk, side=None):
    M = a.shape[0]
    N = b.shape[0] if dims == "nt" else b.shape[1]

    def epilogue(i, acc, ins, outs):
        def strip(r0):
            rows = pl.ds(r0, 128)
            r = jnp.maximum(acc[rows, :] + ins[0][...], 0.0)
            outs[0][rows, :] = r.astype(BF16)
            outs[1][rows, :] = (r * r).astype(BF16)

        _for_strips(acc.shape[0], 128, strip)

    return _mm(
        name, a, b, dims, tm, tn, tk, ins=[(_row(bias), "row")],
        outs=[((M, N), BF16, "tile"), ((M, N), BF16, "tile")], epilogue=epilogue, side=side,
    )


def _mm_lagged(name, a, b, tm, tk, *, ins, outs, strip_fn, init_fn=None, side=None):
    M, K = a.shape
    N = b.shape[1]
    tm, tk = min(tm, M), min(tk, K)
    assert M % tm == 0 and K % tk == 0, (name, M, K, tm, tk)
    nm, nk = M // tm, K // tk
    assert nk >= 2, (name, nk)
    parts = 1 << ((nk - 1).bit_length() - 1)
    rows_p = tm // parts
    assert rows_p % LN_STRIP == 0, (name, rows_p)

    def part_index(i, k):
        return jnp.maximum((i - 1) * parts + jnp.minimum(k, parts - 1), 0)

    def spec(kind):
        if kind == "tile":
            return pl.BlockSpec((rows_p, N), lambda i, k: (part_index(i, k), 0))
        if kind == "row":
            return pl.BlockSpec((1, N), lambda i, k: (0, 0))
        assert kind == "col", kind
        return pl.BlockSpec((rows_p, 1), lambda i, k: (part_index(i, k), 0))

    in_specs = [
        pl.BlockSpec((tm, tk), lambda i, k: (jnp.minimum(i, nm - 1), k)),
        pl.BlockSpec((tk, N), lambda i, k: (jnp.where(i < nm, k, nk - 1), 0)),
    ] + [spec(kind) for _, kind in ins]
    n_in, n_out = len(ins), len(outs)

    def body(*refs):
        a_ref, b_ref = refs[:2]
        in_refs = refs[2 : 2 + n_in]
        out_refs = refs[2 + n_in : 2 + n_in + n_out]
        acc, fin = refs[2 + n_in + n_out :]
        i, k = pl.program_id(0), pl.program_id(1)

        def part():
            return jnp.dot(a_ref[...], b_ref[...], preferred_element_type=F32)

        def epilogue_part():
            base = k * rows_p
            for s in range(rows_p // LN_STRIP):
                acc_rows = fin[pl.ds(pl.multiple_of(base + s * LN_STRIP, LN_STRIP), LN_STRIP), :]
                strip_fn(acc_rows, in_refs, out_refs, pl.ds(s * LN_STRIP, LN_STRIP))

        has_dot = i < nm
        has_epilogue = jnp.logical_and(i > 0, k < parts)
        no_epilogue = jnp.logical_not(has_epilogue)
        last = k == nk - 1
        both = jnp.logical_and(has_dot, has_epilogue)
        alone = jnp.logical_and(has_dot, no_epilogue)

        if init_fn is not None:
            @pl.when(jnp.logical_and(i == 1, k == 0))
            def _():
                init_fn(out_refs)

        @pl.when(jnp.logical_and(alone, k == 0))
        def _():
            acc[...] = part()

        @pl.when(jnp.logical_and(alone, jnp.logical_and(k > 0, jnp.logical_not(last))))
        def _():
            acc[...] += part()

        @pl.when(jnp.logical_and(has_dot, last))
        def _():
            fin[...] = acc[...] + part()

        @pl.when(jnp.logical_and(both, k == 0))
        def _():
            acc[...] = part()
            epilogue_part()

        if parts > 1:
            @pl.when(jnp.logical_and(both, k > 0))
            def _():
                acc[...] += part()
                epilogue_part()

        @pl.when(jnp.logical_and(jnp.logical_not(has_dot), has_epilogue))
        def _():
            epilogue_part()

    return _call(
        name, body, (nm + 1, nk), in_specs, [spec(kind) for _, _, kind in outs], [S(shape, dtype) for shape, dtype, _ in outs],
        [pltpu.VMEM((tm, N), F32), pltpu.VMEM((tm, N), F32)], [a, b, *[x for x, _ in ins]], side,
    )


def _mm_ln(name, a, b, bias, res, g, beta, tm, tk, side=None):
    M = a.shape[0]
    N = b.shape[1]
    rxh, rg, rb = res

    def strip(acc_rows, ins, outs, rows):
        bias_r, rxh_r, rg_r, rb_r, g_r, beta_r = ins
        xhat_o, rstd_o, xbf_o = outs
        resid = rxh_r[rows, :] * rg_r[...] + rb_r[...]
        r = ALPHA * resid + (acc_rows + bias_r[...])
        xhat, rs = _ln_stats(r)
        xhat_o[rows, :] = xhat
        rstd_o[rows, :] = rs
        xbf_o[rows, :] = (xhat * g_r[...] + beta_r[...]).astype(BF16)

    return _mm_lagged(
        name, a, b, tm, tk,
        ins=[(_row(bias), "row"), (rxh, "tile"), (_row(rg), "row"), (_row(rb), "row"), (_row(g), "row"), (_row(beta), "row")],
        outs=[((M, N), F32, "tile"), ((M, 1), F32, "col"), ((M, N), BF16, "tile")],
        strip_fn=strip, side=side,
    )


def _ln_bwd_strip(dyv, xhat, rs, g, dr_o, drbf_o, dg_o, db_o, dsum_o, rows):
    dr = _ln_bwd(dyv, xhat, rs, g)
    dr_o[rows, :] = dr
    drbf_o[rows, :] = dr.astype(BF16)
    dg_o[...] += _row_sum(dyv * xhat)
    db_o[...] += _row_sum(dyv)
    dsum_o[...] += _row_sum(dr)


def _mm_ln_bwd(name, a, b, resgrad, xhat, rstd, g, tm, tk, side=None):
    M = a.shape[0]
    N = b.shape[1]

    def init(outs):
        for o in outs[2:]:
            o[...] = jnp.zeros_like(o)

    def strip(acc_rows, ins, outs, rows):
        rg_r, xh_r, rs_r, g_r = ins
        dyv = acc_rows + ALPHA * rg_r[rows, :]
        _ln_bwd_strip(dyv, xh_r[rows, :], rs_r[rows, :], g_r[...], *outs, rows)

    return _mm_lagged(
        name, a, b, tm, tk,
        ins=[(resgrad, "tile"), (xhat, "tile"), (rstd, "col"), (_row(g), "row")],
        outs=[((M, N), F32, "tile"), ((M, N), BF16, "tile"), ((1, N), F32, "row"), ((1, N), F32, "row"), ((1, N), F32, "row")],
        strip_fn=strip, init_fn=init, side=side,
    )


def _mm_dh(name, a, b, act, tm, tn, tk, side=None):
    M = a.shape[0]
    N = b.shape[0]

    def epilogue(i, acc, ins, outs):
        @pl.when(i == 0)
        def _():
            outs[1][...] = jnp.zeros_like(outs[1])

        def strip(r0):
            rows = pl.ds(r0, 128)
            d = acc[rows, :] * (2.0 * ins[0][rows, :].astype(F32))
            outs[0][rows, :] = d.astype(BF16)
            outs[1][...] += _row_sum(d)

        _for_strips(acc.shape[0], 128, strip)

    return _mm(
        name, a, b, "nt", tm, tn, tk, ins=[(act, "tile")],
        outs=[((M, N), BF16, "tile"), ((1, N), F32, "row")], epilogue=epilogue, j_outer=True, side=side,
    )


def _mm_plain(name, a, b, dims, tm, tn, tk, side=None):
    M = a.shape[0]
    N = b.shape[0] if dims == "nt" else b.shape[1]

    def epilogue(i, acc, ins, outs):
        def strip(r0):
            rows = pl.ds(r0, 128)
            outs[0][rows, :] = acc[rows, :]

        _for_strips(acc.shape[0], 128, strip)

    return _mm(name, a, b, dims, tm, tn, tk, outs=[((M, N), F32, "tile")], epilogue=epilogue, side=side)[0]


def _mm_res(name, a, b, res, tm, tk, side=None):
    def strip(acc_rows, ins, outs, rows):
        outs[0][rows, :] = acc_rows + ALPHA * ins[0][rows, :]

    return _mm_lagged(
        name, a, b, tm, tk, ins=[(res, "tile")], outs=[((a.shape[0], b.shape[1]), F32, "tile")], strip_fn=strip, side=side,
    )[0]


def _mm_wgrad(name, a, b, tm, tk, side=None):
    M = a.shape[1]
    N = b.shape[1]

    def epilogue(i, acc, ins, outs):
        def strip(r0):
            rows = pl.ds(r0, 128)
            outs[0][rows, :] = acc[rows, :].astype(BF16)

        _for_strips(acc.shape[0], 128, strip)

    return _mm(name, a, b, "tn", tm, N, tk, outs=[((M, N), BF16, "tile")], epilogue=epilogue, side=side)[0]


def _loss_top(xhat, rstd, g, beta, target, tm):
    T, D = xhat.shape
    tm = min(tm, T)
    nt = T // tm

    def body(xh_r, rs_r, g_r, b_r, t_r, dr_o, drbf_o, dg_o, db_o, dsum_o, loss_o, sq_acc):
        i = pl.program_id(0)

        @pl.when(i == 0)
        def _():
            dg_o[...] = jnp.zeros_like(dg_o)
            db_o[...] = jnp.zeros_like(db_o)
            dsum_o[...] = jnp.zeros_like(dsum_o)
            sq_acc[...] = jnp.zeros_like(sq_acc)

        def strip(r0):
            rows = pl.ds(r0, LN_STRIP)
            xh = xh_r[rows, :]
            err = (xh * g_r[...] + b_r[...]) - t_r[rows, :]
            sq_acc[...] += _row_sum(err * err)
            _ln_bwd_strip(err * (1.0 / D), xh, rs_r[rows, :], g_r[...], dr_o, drbf_o, dg_o, db_o, dsum_o, rows)

        _for_strips(tm, LN_STRIP, strip, unroll=LN_UNROLL)

        @pl.when(i == nt - 1)
        def _():
            total = jnp.sum(sq_acc[...], axis=-1, keepdims=True) * (0.5 / D)
            loss_o[...] = jnp.broadcast_to(total, loss_o.shape)

    tile = pl.BlockSpec((tm, D), lambda i: (i, 0))
    row = pl.BlockSpec((1, D), lambda i: (0, 0))
    return pl.pallas_call(
        body,
        name="loss_top",
        grid=(nt,),
        in_specs=[tile, pl.BlockSpec((tm, 1), lambda i: (i, 0)), row, row, tile],
        out_specs=[tile, tile, row, row, row, pl.BlockSpec((1, 128), lambda i: (0, 0))],
        out_shape=[S((T, D), F32), S((T, D), BF16), S((1, D), F32), S((1, D), F32), S((1, D), F32), S((1, 128), F32)],
        scratch_shapes=[pltpu.VMEM((1, D), F32)],
        compiler_params=_cparams(1),
    )(xhat, rstd, _row(g), _row(beta), target)


def _cols(ref, c):
    return ref[:, c[0] : c[1]]


def _causal_window_sum(e, w):
    s, sh = e, 1
    while sh < w:
        s = s + pltpu.roll(s, sh, axis=0)
        sh *= 2
    return s


def _anticausal_window_sum(d, w):
    n = d.shape[0]
    r, sh = d, 1
    while sh < w:
        r = r + pltpu.roll(r, n - sh, axis=0)
        sh *= 2
    return r


def _with_halo(halo_ref, main_ref, c, keep):
    return jnp.concatenate([_cols(halo_ref, c) * keep, _cols(main_ref, c)], axis=0)


def _pool_counts(tile_index, R, w):
    pos = lax.broadcasted_iota(jnp.int32, (R, 1), 0) + tile_index * R
    return jnp.minimum(pos + 1, w).astype(F32)


def _sgu_mix(wm_ref, vnb):
    return jnp.concatenate(
        [
            jnp.dot(wm_ref[h * GROUP : (h + 1) * GROUP, :], vnb[:, h * GROUP : (h + 1) * GROUP], preferred_element_type=F32)
            for h in range(SGU_HEADS)
        ],
        axis=1,
    )


CONV_HALVES = (slice(0, CONV_WIDTH // 2), slice(CONV_WIDTH // 2, CONV_WIDTH))
TAP_STRIP = 32
TAP_GROUP = 4


def _build_shifts(shf, src, cols, rows):
    n = rows - 8
    for r in range(1, 8):
        shf[r - 1, pl.ds(0, n), :] = src[pl.ds(r, n), cols]


def _shifted(shf, src, cols, offset, start, size):
    q, r = divmod(offset, 8)
    rows = pl.ds(pl.multiple_of(start + 8 * q, 8), size)
    return src[rows, cols] if r == 0 else shf[r - 1, rows, :]


def _conv_taps(shf, src, cw8, cols, offsets, n_rows, out, bias=None):
    width = cols.stop - cols.start

    def strip(s, carry):
        r0 = pl.multiple_of(s * TAP_STRIP, TAP_STRIP)
        acc = jnp.zeros((TAP_STRIP, width), F32)
        for k, o in enumerate(offsets):
            wk = cw8[pl.ds(8 * k, 8), cols]
            acc = acc + _shifted(shf, src, cols, o, r0, TAP_STRIP) * jnp.concatenate([wk] * (TAP_STRIP // 8), axis=0)
        if bias is not None:
            acc = acc + bias[:, cols]
        out[pl.ds(r0, TAP_STRIP), cols] = acc
        return carry

    lax.fori_loop(0, n_rows // TAP_STRIP, strip, 0)


def _conv_weight_grad(shf, src, dsrc, d_first, cols, offsets, n_rows, dcw):
    width = cols.stop - cols.start
    for k0 in range(0, len(offsets), TAP_GROUP):
        group = offsets[k0 : k0 + TAP_GROUP]

        def strip(s, accs, group=group):
            r0 = pl.multiple_of(s * TAP_STRIP, TAP_STRIP)
            d = dsrc[pl.ds(pl.multiple_of(d_first + r0, 8), TAP_STRIP), cols]
            out = []
            for acc8, o in zip(accs, group):
                p = _shifted(shf, src, cols, o, r0, TAP_STRIP) * d
                for j in range(TAP_STRIP // 8):
                    acc8 = acc8 + p[8 * j : 8 * j + 8, :]
                out.append(acc8)
            return tuple(out)

        accs = lax.fori_loop(0, n_rows // TAP_STRIP, strip, tuple(jnp.zeros((8, width), F32) for _ in group))
        for j, acc8 in enumerate(accs):
            dcw[pl.ds(k0 + j, 1), cols] += _row_sum(acc8)


def _mixer_params(p):
    return [p["wp"], p["ps"], p["lg"], p["lb"], p["wm"], p["wmt"], p["bsf"], p["cw8"], p["cb"], p["cg"], p["cbeta"]]


def _whole(x):
    return pl.BlockSpec(x.shape, lambda i: (0,) * x.ndim)


def _mixer_fwd(name, proj, p, R, side=None):
    T = proj.shape[0]
    R = min(R, T)
    E = R + HALO
    nt = T // R
    hb = R // HALO
    tap_offsets = [HALO - (CONV_KERNEL - 1) + k for k in range(CONV_KERNEL)]

    def body(pm, ph, wp, ps, lg, lb, wm, wmt, bsf, cw8, cb, cg, cbeta, out, hbuf, shf, convbuf):
        i = pl.program_id(0)
        keep = (i > 0).astype(F32)
        a_ext = _with_halo(ph, pm, C_POOL, keep)
        for gi, w in enumerate(POOL_WINDOWS):
            cs = slice(gi * GROUP, (gi + 1) * GROUP)
            e = a_ext[:, cs]
            s = _causal_window_sum(e, w)
            pooled = s[HALO:, :] / _pool_counts(i, R, w) - e[HALO:, :]
            z = jnp.dot(pooled.astype(BF16), wp[cs, :], preferred_element_type=F32)
            out[:, cs] = (z * ps[:, cs]).astype(BF16)
        u, _ = _gelu(_cols(pm, C_U))
        v, _ = _gelu(_cols(pm, C_V))
        vhat, _ = _ln_stats(v)
        vn = vhat * lg[...] + lb[...]
        for c in range(R // GROUP):
            rs = slice(c * GROUP, (c + 1) * GROUP)
            mixed = _sgu_mix(wm, vn[rs, :].astype(BF16)) + bsf[...]
            out[rs, M_SGU[0] : M_SGU[1]] = (u[rs, :] * mixed).astype(BF16)
        hbuf[...] = _with_halo(ph, pm, C_CA, keep) * jax.nn.sigmoid(_with_halo(ph, pm, C_CG, keep))
        for cols in CONV_HALVES:
            _build_shifts(shf, hbuf, cols, E)
            _conv_taps(shf, hbuf, cw8, cols, tap_offsets, R, convbuf, bias=cb)
        chat, _ = _ln_stats(convbuf[...])
        cn = chat * cg[...] + cbeta[...]
        out[:, M_CONV[0] : M_CONV[1]] = (cn * jax.nn.sigmoid(cn)).astype(BF16)

    params = _mixer_params(p)
    in_specs = [
        pl.BlockSpec((R, IN_WIDTH), lambda i: (i, 0)),
        pl.BlockSpec((HALO, IN_WIDTH), lambda i: (jnp.maximum(i * hb - 1, 0), 0)),
    ] + [_whole(x) for x in params]
    scratch = [pltpu.VMEM((E, CONV_WIDTH), F32), pltpu.VMEM((7, E, CONV_WIDTH // 2), F32), pltpu.VMEM((R, CONV_WIDTH), F32)]
    return _call(
        name, body, (nt,), in_specs, [pl.BlockSpec((R, D_MODEL), lambda i: (i, 0))], [S((T, D_MODEL), BF16)],
        scratch, [proj, proj, *params], side,
    )[0]


def _mixer_bwd(name, proj, dmix, p, R, side=None):
    T = proj.shape[0]
    R = min(R, T)
    E = R + HALO
    nt = T // R
    hb = R // HALO
    tap_offsets = [HALO - (CONV_KERNEL - 1) + k for k in range(CONV_KERNEL)]
    back_offsets = [HALO - o for o in tap_offsets]

    def body(pm, ph, dm, wp, ps, lg, lb, wm, wmt, bsf, cw8, cb, cg, cbeta,
             dproj, dwp, dps, dlg, dlb, dwm, dbs, dcw, dcb, dcg, dcbeta, dbin,
             hbuf, dbuf, carry_p, carry_c, dbs_acc, shf, convbuf, dhcbuf):
        step = pl.program_id(0)
        ti = nt - 1 - step
        keep = (ti > 0).astype(F32)

        @pl.when(step == 0)
        def _():
            for r in (dwp, dps, dlg, dlb, dwm, dcw, dcb, dcg, dcbeta, dbin, carry_p, carry_c, dbs_acc):
                r[...] = jnp.zeros_like(r)

        def tail(carry):
            return jnp.concatenate([jnp.zeros((R - HALO, carry.shape[1]), F32), carry], axis=0)

        def head(x):
            return jnp.concatenate([jnp.zeros((HALO, x.shape[1]), F32), x], axis=0)

        a_ext = _with_halo(ph, pm, C_POOL, keep)
        carry_in = carry_p[...]
        for gi, w in enumerate(POOL_WINDOWS):
            cs = slice(gi * GROUP, (gi + 1) * GROUP)
            e = a_ext[:, cs]
            s = _causal_window_sum(e, w)
            cnt = _pool_counts(ti, R, w)
            pooled_b = (s[HALO:, :] / cnt - e[HALO:, :]).astype(BF16)
            wg = wp[cs, :]
            z = jnp.dot(pooled_b, wg, preferred_element_type=F32)
            dya = dm[:, cs]
            dps[:, cs] += _row_sum(dya * z)
            dz_b = (dya * ps[:, cs]).astype(BF16)
            dwp[cs, :] += lax.dot_general(pooled_b, dz_b, (((0,), (0,)), ((), ())), preferred_element_type=F32)
            dpooled = lax.dot_general(dz_b, wg, (((1,), (1,)), ((), ())), preferred_element_type=F32)
            da_ext = _anticausal_window_sum(head(dpooled / cnt), w) - head(dpooled)
            carry_p[:, cs] = da_ext[:HALO, :]
            d_a = da_ext[HALO:, :] + tail(carry_in[:, cs])
            dbin[:, cs] += _row_sum(d_a)
            dproj[:, cs] = d_a.astype(BF16)

        pu = _cols(pm, C_U)
        pv = _cols(pm, C_V)
        u, thu = _gelu(pu)
        v, thv = _gelu(pv)
        vhat, vrs = _ln_stats(v)
        vn = vhat * lg[...] + lb[...]
        dyb = dm[:, M_SGU[0] : M_SGU[1]]
        du_parts, dvn_parts = [], []
        for c in range(R // GROUP):
            rs = slice(c * GROUP, (c + 1) * GROUP)
            vnb = vn[rs, :].astype(BF16)
            mixed = _sgu_mix(wm, vnb) + bsf[...]
            du_parts.append(dyb[rs, :] * mixed)
            dmixed = dyb[rs, :] * u[rs, :]
            dbs_acc[...] += dmixed
            dmb = dmixed.astype(BF16)
            dvn_h = []
            for h in range(SGU_HEADS):
                hs = slice(h * GROUP, (h + 1) * GROUP)
                dwm[hs, :] += lax.dot_general(dmb[:, hs], vnb[:, hs], (((1,), (1,)), ((), ())), preferred_element_type=F32)
                dvn_h.append(jnp.dot(wmt[hs, :], dmb[:, hs], preferred_element_type=F32))
            dvn_parts.append(jnp.concatenate(dvn_h, axis=1))
        du = jnp.concatenate(du_parts, axis=0) if len(du_parts) > 1 else du_parts[0]
        dvn = jnp.concatenate(dvn_parts, axis=0) if len(dvn_parts) > 1 else dvn_parts[0]
        dlg[...] += _row_sum(dvn * vhat)
        dlb[...] += _row_sum(dvn)
        d_pu = du * _gelu_grad(pu, thu)
        d_pv = _ln_bwd(dvn, vhat, vrs, lg[...]) * _gelu_grad(pv, thv)
        dbin[:, C_U[0] : C_U[1]] += _row_sum(d_pu)
        dbin[:, C_V[0] : C_V[1]] += _row_sum(d_pv)
        dproj[:, C_U[0] : C_U[1]] = d_pu.astype(BF16)
        dproj[:, C_V[0] : C_V[1]] = d_pv.astype(BF16)

        sg_ext = jax.nn.sigmoid(_with_halo(ph, pm, C_CG, keep))
        ca_ext = _with_halo(ph, pm, C_CA, keep)
        hbuf[...] = ca_ext * sg_ext
        for cols in CONV_HALVES:
            _build_shifts(shf, hbuf, cols, E)
            _conv_taps(shf, hbuf, cw8, cols, tap_offsets, R, convbuf, bias=cb)
        chat, crs = _ln_stats(convbuf[...])
        cn = chat * cg[...] + cbeta[...]
        sc = jax.nn.sigmoid(cn)
        dcn = dm[:, M_CONV[0] : M_CONV[1]] * (sc * (1.0 + cn * (1.0 - sc)))
        dcg[...] += _row_sum(dcn * chat)
        dcbeta[...] += _row_sum(dcn)
        dconv = _ln_bwd(dcn, chat, crs, cg[...])
        dcb[...] += _row_sum(dconv)
        dbuf[pl.ds(0, HALO), :] = jnp.zeros((HALO, CONV_WIDTH), F32)
        dbuf[pl.ds(HALO, R), :] = dconv
        dbuf[pl.ds(HALO + R, HALO), :] = jnp.zeros((HALO, CONV_WIDTH), F32)
        for cols in CONV_HALVES:
            _build_shifts(shf, hbuf, cols, E)
            _conv_weight_grad(shf, hbuf, dbuf, HALO, cols, tap_offsets, R, dcw)
            _build_shifts(shf, dbuf, cols, E + HALO)
            _conv_taps(shf, dbuf, cw8, cols, back_offsets, E, dhcbuf)
        dhc_main = dhcbuf[pl.ds(HALO, R), :] + tail(carry_c[...])
        carry_c[...] = dhcbuf[pl.ds(0, HALO), :]
        sg = sg_ext[HALO:, :]
        d_ca = dhc_main * sg
        d_cg = dhc_main * ca_ext[HALO:, :] * (sg * (1.0 - sg))
        dbin[:, C_CA[0] : C_CA[1]] += _row_sum(d_ca)
        dbin[:, C_CG[0] : C_CG[1]] += _row_sum(d_cg)
        dproj[:, C_CA[0] : C_CA[1]] = d_ca.astype(BF16)
        dproj[:, C_CG[0] : C_CG[1]] = d_cg.astype(BF16)

        @pl.when(step == nt - 1)
        def _():
            row = lax.broadcasted_iota(jnp.int32, (GROUP, GROUP), 0)
            col = lax.broadcasted_iota(jnp.int32, (GROUP, GROUP), 1)
            dbs[...] = jnp.zeros_like(dbs)
            for h in range(SGU_HEADS):
                hs = slice(h * GROUP, (h + 1) * GROUP)
                dwm[hs, :] = jnp.where(row >= col, dwm[hs, :], 0.0)
                dbs[pl.ds(h, 1), :] = _row_sum(dbs_acc[:, hs].T)

    params = _mixer_params(p)
    accs = [
        S((POOL_WIDTH, GROUP), F32), S((1, POOL_WIDTH), F32), S((1, SGU_WIDTH), F32), S((1, SGU_WIDTH), F32),
        S((SGU_WIDTH, GROUP), F32), S((8, GROUP), F32), S((32, CONV_WIDTH), F32), S((1, CONV_WIDTH), F32),
        S((1, CONV_WIDTH), F32), S((1, CONV_WIDTH), F32), S((1, IN_WIDTH), F32),
    ]
    in_specs = [
        pl.BlockSpec((R, IN_WIDTH), lambda i: (nt - 1 - i, 0)),
        pl.BlockSpec((HALO, IN_WIDTH), lambda i: (jnp.maximum((nt - 1 - i) * hb - 1, 0), 0)),
        pl.BlockSpec((R, D_MODEL), lambda i: (nt - 1 - i, 0)),
    ] + [_whole(x) for x in params]
    scratch = [
        pltpu.VMEM((E, CONV_WIDTH), F32), pltpu.VMEM((E + HALO, CONV_WIDTH), F32),
        pltpu.VMEM((HALO, POOL_WIDTH), F32), pltpu.VMEM((HALO, CONV_WIDTH), F32), pltpu.VMEM((GROUP, SGU_WIDTH), F32),
        pltpu.VMEM((7, E + HALO, CONV_WIDTH // 2), F32), pltpu.VMEM((R, CONV_WIDTH), F32), pltpu.VMEM((E, CONV_WIDTH), F32),
    ]
    return _call(
        name, body, (nt,), in_specs, [pl.BlockSpec((R, IN_WIDTH), lambda i: (nt - 1 - i, 0))] + [_whole(x) for x in accs],
        [S((T, IN_WIDTH), BF16)] + accs, scratch, [proj, proj, dmix, *params], side,
    )


def _all_gather(xs):
    n = len(xs)

    def body(*refs):
        x_refs, o_refs = refs[:n], refs[n : 2 * n]
        send_sems, recv_sems, local_sems = refs[2 * n :]
        x, y, c = _place()
        me, sibling = (x, y, c), (x, y, 1 - c)
        chips = [(1 - x, y), (x, 1 - y), (1 - x, 1 - y)]

        def copy(a, k, block, to, src=None):
            dst = o_refs[a].at[_lin(block)]
            return pltpu.make_async_remote_copy(
                src_ref=dst if src is None else src, dst_ref=dst, send_sem=send_sems.at[a, k], recv_sem=recv_sems.at[a, k],
                device_id=to, device_id_type=MESH,
            )

        mine = [pltpu.make_async_copy(x_refs[a], o_refs[a].at[_lin(me)], local_sems.at[a]) for a in range(n)]
        for m in mine:
            m.start()
        first = []
        for a in range(n):
            first.append(copy(a, 0, me, sibling, src=x_refs[a]))
            first += [copy(a, 1 + j, me, (*chip, c), src=x_refs[a]) for j, chip in enumerate(chips)]
        for cp in first:
            cp.start()
        passed = []
        for a in range(n):
            for j, chip in enumerate(chips):
                copy(a, 1 + j, (*chip, c), me).wait_recv()
                fwd = copy(a, 4 + j, (*chip, c), sibling)
                fwd.start()
                passed.append(fwd)
        for a in range(n):
            copy(a, 0, sibling, me).wait_recv()
            for j, chip in enumerate(chips):
                copy(a, 4 + j, (*chip, 1 - c), me).wait_recv()
        for cp in first + passed:
            cp.wait_send()
        for m in mine:
            m.wait()

    return pl.pallas_call(
        body,
        name="all_gather_weights",
        in_specs=[_ANY] * n,
        out_specs=[_ANY] * n,
        out_shape=[S((N_DEV, *x.shape), x.dtype) for x in xs],
        scratch_shapes=[pltpu.SemaphoreType.DMA((n, 7)), pltpu.SemaphoreType.DMA((n, 7)), pltpu.SemaphoreType.DMA((n,))],
    )(*xs)


def _row_tile(rows, want):
    return next(t for t in range(min(rows, want) // 8 * 8, 0, -8) if rows % t == 0)


def _sum_slots(name, slots):
    _, rows, cols = slots.shape
    tr = _row_tile(rows, (4 << 20) // (N_DEV * cols * slots.dtype.itemsize))

    def body(s_ref, o_ref):
        total = s_ref[0].astype(F32)
        for d in range(1, N_DEV):
            total = total + s_ref[d].astype(F32)
        o_ref[...] = total

    return pl.pallas_call(
        body,
        name=name,
        grid=(rows // tr,),
        in_specs=[pl.BlockSpec((N_DEV, tr, cols), lambda i: (0, i, 0))],
        out_specs=pl.BlockSpec((tr, cols), lambda i: (i, 0)),
        out_shape=S((rows, cols), F32),
        compiler_params=_cparams(1),
    )(slots)


def _adamw(name, w, g, m, v):
    rows, cols = w.shape
    tr = rows if rows * cols * 4 <= (2 << 20) else _row_tile(rows, 1 << ((1 << 18) // cols).bit_length() - 1)

    def body(w_ref, g_ref, m_ref, v_ref, d_ref, nm_ref, nv_ref):
        gv = g_ref[...]
        nm = ADAM_B1 * m_ref[...] + (1.0 - ADAM_B1) * gv
        nv = ADAM_B2 * v_ref[...] + (1.0 - ADAM_B2) * (gv * gv)
        m_hat = nm / (1.0 - ADAM_B1**ADAM_STEP)
        v_hat = nv / (1.0 - ADAM_B2**ADAM_STEP)
        d_ref[...] = -ADAM_LR * (m_hat / (jnp.sqrt(v_hat) + ADAM_EPS) + ADAM_WD * w_ref[...])
        nm_ref[...] = nm
        nv_ref[...] = nv

    blk = pl.BlockSpec((tr, cols), lambda i: (i, 0))
    return pl.pallas_call(
        body,
        name=name,
        grid=(rows // tr,),
        in_specs=[blk] * 4,
        out_specs=[blk] * 3,
        out_shape=[S((rows, cols), F32)] * 3,
        compiler_params=_cparams(1),
    )(w, g, m, v)


_BIG = ("w_in", "w_out", "w_ff1", "w_ff2")
_TRANSPOSED = ("w_in", "w_ff1")
_SMALL = ("b_in", "w_pool", "pool_scale", "sgu_ln_g", "sgu_ln_b", "sgu_w", "sgu_b", "conv_b", "conv_ln_g", "conv_ln_b",
          "b_out", "ln1_g", "ln1_b", "b_ff1", "b_ff2", "ln2_g", "ln2_b")
_WEIGHTS = ("w_in", "b_in", "w_pool", "pool_scale", "sgu_ln_g", "sgu_ln_b", "sgu_w", "sgu_b", "conv_w", "conv_b", "conv_ln_g",
            "conv_ln_b", "w_out", "b_out", "ln1_g", "ln1_b", "w_ff1", "b_ff1", "w_ff2", "b_ff2", "ln2_g", "ln2_b")


def _pack(arrays):
    parts = []
    for a in arrays:
        rows = a.reshape(-1, 128)
        parts.append(jnp.pad(rows, ((0, -rows.shape[0] % 8), (0, 0))))
    return jnp.concatenate(parts, axis=0)


def _unpack(flat, like):
    out, at = [], 0
    for a in like:
        n = a.size // 128
        out.append(flat[at : at + n].reshape(a.shape))
        at += n + (-n % 8)
    return out


def _packed_rows(arrays):
    return sum(a.size // 128 + (-(a.size // 128) % 8) for a in arrays)


def kernel(x, w_in, b_in, w_pool, pool_scale, sgu_ln_g, sgu_ln_b, sgu_w, sgu_b, conv_w, conv_b, conv_ln_g, conv_ln_b, w_out, b_out, ln1_g, ln1_b, w_ff1, b_ff1, w_ff2, b_ff2, ln2_g, ln2_b, loss_target, m_w_in, m_b_in, m_w_pool, m_pool_scale, m_sgu_ln_g, m_sgu_ln_b, m_sgu_w, m_sgu_b, m_conv_w, m_conv_b, m_conv_ln_g, m_conv_ln_b, m_w_out, m_b_out, m_ln1_g, m_ln1_b, m_w_ff1, m_b_ff1, m_w_ff2, m_b_ff2, m_ln2_g, m_ln2_b, v_w_in, v_b_in, v_w_pool, v_pool_scale, v_sgu_ln_g, v_sgu_ln_b, v_sgu_w, v_sgu_b, v_conv_w, v_conv_b, v_conv_ln_g, v_conv_ln_b, v_w_out, v_b_out, v_ln1_g, v_ln1_b, v_w_ff1, v_b_ff1, v_w_ff2, v_b_ff2, v_ln2_g, v_ln2_b):
    w = dict(w_in=w_in, b_in=b_in, w_pool=w_pool, pool_scale=pool_scale, sgu_ln_g=sgu_ln_g, sgu_ln_b=sgu_ln_b, sgu_w=sgu_w,
             sgu_b=sgu_b, conv_w=conv_w, conv_b=conv_b, conv_ln_g=conv_ln_g, conv_ln_b=conv_ln_b, w_out=w_out, b_out=b_out,
             ln1_g=ln1_g, ln1_b=ln1_b, w_ff1=w_ff1, b_ff1=b_ff1, w_ff2=w_ff2, b_ff2=b_ff2, ln2_g=ln2_g, ln2_b=ln2_b)
    mom = dict(w_in=m_w_in, b_in=m_b_in, w_pool=m_w_pool, pool_scale=m_pool_scale, sgu_ln_g=m_sgu_ln_g, sgu_ln_b=m_sgu_ln_b,
               sgu_w=m_sgu_w, sgu_b=m_sgu_b, conv_w=m_conv_w, conv_b=m_conv_b, conv_ln_g=m_conv_ln_g, conv_ln_b=m_conv_ln_b,
               w_out=m_w_out, b_out=m_b_out, ln1_g=m_ln1_g, ln1_b=m_ln1_b, w_ff1=m_w_ff1, b_ff1=m_b_ff1, w_ff2=m_w_ff2,
               b_ff2=m_b_ff2, ln2_g=m_ln2_g, ln2_b=m_ln2_b)
    var = dict(w_in=v_w_in, b_in=v_b_in, w_pool=v_w_pool, pool_scale=v_pool_scale, sgu_ln_g=v_sgu_ln_g, sgu_ln_b=v_sgu_ln_b,
               sgu_w=v_sgu_w, sgu_b=v_sgu_b, conv_w=v_conv_w, conv_b=v_conv_b, conv_ln_g=v_conv_ln_g, conv_ln_b=v_conv_ln_b,
               w_out=v_w_out, b_out=v_b_out, ln1_g=v_ln1_g, ln1_b=v_ln1_b, w_ff1=v_w_ff1, b_ff1=v_b_ff1, w_ff2=v_w_ff2,
               b_ff2=v_b_ff2, ln2_g=v_ln2_g, ln2_b=v_ln2_b)
    T = x.shape[1]
    x0 = x.reshape(T, D_MODEL)
    target = loss_target.reshape(T, D_MODEL)
    me_lin = _lin(_place())

    shard = [
        {name: (w[name][l].T if name in _TRANSPOSED else w[name][l]).astype(BF16) for name in _BIG} for l in range(DEPTH)
    ]
    conv_shard = jnp.pad(conv_w, ((0, 0), (0, 1), (0, 128 - conv_w.shape[2]))).reshape(DEPTH * 32, 128)

    def rows_of(g):
        return g.reshape(N_DEV * g.shape[1], g.shape[2])

    first = _all_gather([shard[0]["w_in"], shard[0]["w_out"], conv_shard])
    full = [{} for _ in range(DEPTH)]
    full[0]["w_in"], full[0]["w_out"] = rows_of(first[0]), rows_of(first[1])
    conv_cols = conv_w.shape[2]
    conv_full = first[2].reshape(N_DEV, DEPTH, 32, 128)[:, :, :CONV_KERNEL, :conv_cols]
    conv_full = conv_full.transpose(1, 2, 0, 3).reshape(DEPTH, CONV_KERNEL, N_DEV * conv_cols)

    tril = jnp.tril(jnp.ones((GROUP, GROUP), F32))
    prm = []
    for l in range(DEPTH):
        wm = sgu_w[l] * tril
        prm.append(dict(
            wp=w_pool[l].reshape(POOL_WIDTH, GROUP).astype(BF16), ps=_row(pool_scale[l]), lg=_row(sgu_ln_g[l]), lb=_row(sgu_ln_b[l]),
            wm=wm.reshape(SGU_WIDTH, GROUP).astype(BF16), wmt=wm.transpose(0, 2, 1).reshape(SGU_WIDTH, GROUP).astype(BF16),
            bsf=jnp.repeat(sgu_b[l].T, GROUP, axis=1), cw8=jnp.repeat(jnp.pad(conv_full[l], ((0, 1), (0, 0))), 8, axis=0), cb=_row(conv_b[l]),
            cg=_row(conv_ln_g[l]), cbeta=_row(conv_ln_b[l]),
        ))

    saved = []
    res = (x0, jnp.ones((D_MODEL,), F32), jnp.zeros((D_MODEL,), F32))
    xbf = x0.astype(BF16)
    u = shard[0]["w_ff1"].shape[0] // 8

    for l in range(DEPTH):
        f = full[l]
        s_ff1, s_ff2 = shard[l]["w_ff1"], shard[l]["w_ff2"]
        ex = _Exchange([("gather", s_ff1, (0, 3 * u), None)])
        proj = _mm_bias(f"proj{l}", xbf, f["w_in"], "nt", b_in[l], 1024, 896, 2048, side=ex)
        ex = _Exchange([("gather", s_ff1, (3 * u, 3 * u), ex.results[0])])
        mixed = _mixer_fwd(f"mixer_fwd{l}", proj, prm[l], 512, side=ex)
        ex = _Exchange([("gather", s_ff1, (6 * u, 2 * u), ex.results[0]), ("gather", s_ff2, (0, u), None)])
        xh1, rs1, x1bf = _mm_ln(f"out_ln1_{l}", mixed, f["w_out"], b_out[l], res, ln1_g[l], ln1_b[l], 1024, 512, side=ex)
        f["w_ff1"] = rows_of(ex.results[0])
        ex = _Exchange([("gather", s_ff2, (u, 7 * u), ex.results[1])])
        act, hsq = _mm_relu2(f"ff1_{l}", x1bf, f["w_ff1"], "nt", b_ff1[l], 1024, 1024, 2048, side=ex)
        f["w_ff2"] = rows_of(ex.results[0])
        ex = _Exchange([("gather", shard[l + 1]["w_in"]), ("gather", shard[l + 1]["w_out"])]) if l + 1 < DEPTH else None
        xh2, rs2, x2bf = _mm_ln(
            f"ff2_ln2_{l}", hsq, f["w_ff2"], b_ff2[l], (xh1, ln1_g[l], ln1_b[l]), ln2_g[l], ln2_b[l], 1024, 1024, side=ex)
        if ex is not None:
            full[l + 1]["w_in"], full[l + 1]["w_out"] = rows_of(ex.results[0]), rows_of(ex.results[1])
        saved.append(dict(xin=xbf, proj=proj, mixed=mixed, xh1=xh1, rs1=rs1, x1bf=x1bf, act=act, hsq=hsq, xh2=xh2, rs2=rs2))
        res = (xh2, ln2_g[l], ln2_b[l])
        xbf = x2bf

    top = saved[-1]
    dr2, dr2bf, g_ln2g, g_ln2b, g_bff2, loss_row = _loss_top(top["xh2"], top["rs2"], ln2_g[-1], ln2_b[-1], target, 256)
    loss = lax.psum(loss_row[0, 0], ("x", "y", "c"))
    slots = [{} for _ in range(DEPTH)]
    gsm = [{} for _ in range(DEPTH)]
    grad_x = small_slots = None

    def stacked_small():
        st = {name: jnp.stack([gsm[gl][name].reshape(w[name].shape[1:]) for gl in range(DEPTH)]) for name in _SMALL}
        conv_g = jnp.pad(jnp.stack([gsm[gl]["conv_w"] for gl in range(DEPTH)]), ((0, 0), (0, 1), (0, 0)))
        return [st[name] for name in _SMALL] + [conv_g]

    for l in reversed(range(DEPTH)):
        f, sv = full[l], saved[l]
        gsm[l].update(ln2_g=g_ln2g, ln2_b=g_ln2b, b_ff2=g_bff2)
        gw = _mm_wgrad(f"gw_ff2_{l}", sv["hsq"], dr2bf, 1024, 2048)
        ex = _Exchange([("slices", gw)])
        dhpre, g_bff1 = _mm_dh(f"dff1_{l}", dr2bf, f["w_ff2"], sv["act"], 1024, 1024, 2048, side=ex)
        slots[l]["w_ff2"] = ex.results[0]
        gsm[l]["b_ff1"] = g_bff1
        gw = _mm_wgrad(f"gw_ff1_{l}", dhpre, sv["x1bf"], 1024, 2048)
        ex = _Exchange([("slices", gw)])
        dr1, dr1bf, g_ln1g, g_ln1b, g_bout = _mm_ln_bwd(
            f"dx1_ln1_{l}", dhpre, f["w_ff1"], dr2, sv["xh1"], sv["rs1"], ln1_g[l], 1024, 1024, side=ex)
        slots[l]["w_ff1"] = ex.results[0]
        gsm[l].update(ln1_g=g_ln1g, ln1_b=g_ln1b, b_out=g_bout)
        gw = _mm_wgrad(f"gw_out_{l}", sv["mixed"], dr1bf, 1024, 2048)
        dmix = _mm_plain(f"dmixed{l}", dr1bf, f["w_out"], "nt", 1024, 1024, 2048)
        ex = _Exchange([("slices", gw)])
        (dproj, g_wp, g_ps, g_lg, g_lb, g_wm, g_bs, g_cw, g_cb, g_cg, g_cbeta, g_bin) = _mixer_bwd(
            f"mixer_bwd{l}", sv["proj"], dmix, prm[l], 512, side=ex)
        slots[l]["w_out"] = ex.results[0]
        gsm[l].update(b_in=g_bin, w_pool=g_wp, pool_scale=g_ps, sgu_ln_g=g_lg, sgu_ln_b=g_lb, sgu_w=g_wm, sgu_b=g_bs[:SGU_HEADS],
                      conv_w=g_cw[:CONV_KERNEL], conv_b=g_cb, conv_ln_g=g_cg, conv_ln_b=g_cbeta)
        if l > 0:
            gw = _mm_wgrad(f"gw_in_{l}", dproj, sv["xin"], 896, 2048)
            below = saved[l - 1]
            ex = _Exchange([("slices", gw)])
            dr2, dr2bf, g_ln2g, g_ln2b, g_bff2 = _mm_ln_bwd(
                f"dx_ln2_{l}", dproj, f["w_in"], dr1, below["xh2"], below["rs2"], ln2_g[l - 1], 1024, 512, side=ex)
        else:
            small_like = stacked_small()
            ex = _Exchange([("gather", _pack(small_like))])
            gw = _mm_wgrad(f"gw_in_{l}", dproj, sv["xin"], 896, 2048, side=ex)
            small_slots = ex.results[0]
            ex = _Exchange([("slices", gw)])
            grad_x = _mm_res("dx0", dproj, f["w_in"], dr1, 1024, 512, side=ex)
        slots[l]["w_in"] = ex.results[0]

    grads, deltas, new_m, new_v = {}, {}, {}, {}
    for name in _BIG:
        per_layer = []
        for l in range(DEPTH):
            g = _sum_slots(f"sum_{name}_{l}", slots[l][name])
            per_layer.append(g.T if name in _TRANSPOSED else g)
        g = jnp.stack(per_layer)
        shape = w[name].shape
        two_d = (shape[0] * shape[1], shape[2])
        d, nm, nv = _adamw(f"adamw_{name}", w[name].reshape(two_d), g.reshape(two_d), mom[name].reshape(two_d), var[name].reshape(two_d))
        grads[name], deltas[name], new_m[name], new_v[name] = g, d.reshape(shape), nm.reshape(shape), nv.reshape(shape)

    total = _sum_slots("sum_small", small_slots)
    small_g = _unpack(total, small_like)
    like = [w[name] for name in _SMALL]
    d, nm, nv = _adamw("adamw_small", _pack(like), total[: _packed_rows(like)],
                       _pack([mom[name] for name in _SMALL]), _pack([var[name] for name in _SMALL]))
    for name, gg, dd, mm_, vv in zip(_SMALL, small_g, _unpack(d, like), _unpack(nm, like), _unpack(nv, like)):
        grads[name], deltas[name], new_m[name], new_v[name] = gg, dd, mm_, vv
    conv_g = lax.dynamic_slice_in_dim(small_g[-1][:, :CONV_KERNEL, :], me_lin * conv_cols, conv_cols, axis=2)
    flat = (DEPTH * CONV_KERNEL, conv_cols)
    d, nm, nv = _adamw("adamw_conv_w", conv_w.reshape(flat), conv_g.reshape(flat), m_conv_w.reshape(flat), v_conv_w.reshape(flat))
    grads["conv_w"], deltas["conv_w"], new_m["conv_w"], new_v["conv_w"] = conv_g, d.reshape(conv_w.shape), nm.reshape(conv_w.shape), nv.reshape(conv_w.shape)

    return (loss, grad_x.reshape(x.shape), *[grads[n] for n in _WEIGHTS], *[deltas[n] for n in _WEIGHTS],
            *[new_m[n] for n in _WEIGHTS], *[new_v[n] for n in _WEIGHTS])
```

```python
import functools

import jax
import jax.numpy as jnp
from jax import lax
from jax.experimental import pallas as pl
from jax.experimental.pallas import tpu as pltpu

F32, BF16 = jnp.float32, jnp.bfloat16
S = jax.ShapeDtypeStruct

DEPTH = 2
D_MODEL = 2048
POOL_WINDOWS = (2, 4, 8, 16)
POOL_WIDTH = 512
GROUP = 128
SGU_WIDTH = 768
SGU_HEADS = 6
CONV_WIDTH = 768
CONV_KERNEL = 31
IN_WIDTH = 3584
D_FF = 8192
ALPHA = (2 * DEPTH) ** 0.25
LN_EPS = 1e-5
ADAM_LR, ADAM_B1, ADAM_B2, ADAM_EPS, ADAM_WD, ADAM_STEP = 0.001, 0.9, 0.999, 1e-08, 0.01, 10

N_DEV = 8
LN_STRIP = 32
LN_UNROLL = 4
HALO = 32
VMEM_LIMIT = 56 << 20
MESH = pl.DeviceIdType.MESH

C_POOL = (0, 512)
C_U = (512, 1280)
C_V = (1280, 2048)
C_CA = (2048, 2816)
C_CG = (2816, 3584)
M_POOL = (0, 512)
M_SGU = (512, 1280)
M_CONV = (1280, 2048)


def _cparams(n_axes):
    return pltpu.CompilerParams(dimension_semantics=("arbitrary",) * n_axes, vmem_limit_bytes=VMEM_LIMIT)


def _for_strips(rows, strip, fn, unroll=1):
    n = rows // strip
    if n == 1:
        fn(0)
        return

    def step(s, carry):
        fn(pl.multiple_of(s * strip, strip))
        return carry

    lax.fori_loop(0, n, step, 0, unroll=unroll)


def _row_sum(x):
    return jnp.sum(x, axis=0, keepdims=True)


def _ln_stats(r):
    mu = jnp.mean(r, axis=-1, keepdims=True)
    xc = r - mu
    var = jnp.mean(xc * xc, axis=-1, keepdims=True)
    rs = lax.rsqrt(var + LN_EPS)
    return xc * rs, rs


def _ln_bwd(dy, xhat, rs, g):
    gy = dy * g
    m1 = jnp.mean(gy, axis=-1, keepdims=True)
    m2 = jnp.mean(gy * xhat, axis=-1, keepdims=True)
    return rs * (gy - m1 - xhat * m2)


_GELU_C = 0.7978845608028654


def _gelu(x):
    th = jnp.tanh(_GELU_C * (x + 0.044715 * (x * x * x)))
    return 0.5 * x * (1.0 + th), th


def _gelu_grad(x, th):
    return 0.5 * (1.0 + th) + 0.5 * x * (1.0 - th * th) * (_GELU_C * (1.0 + 3.0 * 0.044715 * (x * x)))


def _place():
    x, y, c = lax.axis_index("x"), lax.axis_index("y"), lax.axis_index("c")
    return x, y, c


def _lin(p):
    return 4 * p[0] + 2 * p[1] + p[2]


def _flip(p, r):
    return tuple(1 - v if (r >> (2 - ax)) & 1 else v for ax, v in enumerate(p))


_ANY = pl.BlockSpec(memory_space=pl.ANY)


class _Exchange:
    def __init__(self, items):
        self.kinds = [item[0] for item in items]
        self.srcs = [item[1] for item in items]
        self.rows = [item[2] if len(item) > 2 else None for item in items]
        handed_on = [item[3] if len(item) > 3 else None for item in items]
        self.out_shape = [
            S((N_DEV, *x.shape), x.dtype) if kind == "gather" else S((N_DEV, x.shape[0] // N_DEV, x.shape[1]), x.dtype)
            for kind, x in zip(self.kinds, self.srcs)
        ]
        n = len(items)
        self.ins = self.srcs + [b for b in handed_on if b is not None]
        self.aliases = {}
        for a, b in enumerate(handed_on):
            if b is not None:
                self.aliases[n + len(self.aliases)] = a
        self.scratch = [pltpu.SemaphoreType.DMA((n, 7)), pltpu.SemaphoreType.DMA((n, 7)), pltpu.SemaphoreType.DMA((n,))]
        self.results = None

    def _src(self, in_refs, a, dest):
        if self.kinds[a] == "gather":
            return in_refs[a] if self.rows[a] is None else in_refs[a].at[pl.ds(*self.rows[a])]
        rows = self.srcs[a].shape[0] // N_DEV
        return in_refs[a].at[pl.ds(pl.multiple_of(_lin(dest) * rows, 8), rows)]

    def _dst(self, out_refs, a, slot):
        return out_refs[a].at[slot] if self.rows[a] is None else out_refs[a].at[slot, pl.ds(*self.rows[a])]

    def _copies(self, in_refs, out_refs, sems, with_arrivals):
        send_sems, recv_sems, local_sems = sems
        me = _place()
        local, sends, arrivals = [], [], []
        for a in range(len(self.srcs)):
            local.append(pltpu.make_async_copy(self._src(in_refs, a, me), self._dst(out_refs, a, _lin(me)), local_sems.at[a]))
            for r in range(1, N_DEV):
                peer = _flip(me, r)
                for slot, group in ((_lin(me), sends), (_lin(peer), arrivals)):
                    if group is sends or with_arrivals:
                        group.append(pltpu.make_async_remote_copy(
                            src_ref=self._src(in_refs, a, peer), dst_ref=self._dst(out_refs, a, slot),
                            send_sem=send_sems.at[a, r - 1], recv_sem=recv_sems.at[a, r - 1], device_id=peer, device_id_type=MESH,
                        ))
        return local, sends, arrivals

    def start(self, in_refs, out_refs, sems):
        local, sends, _ = self._copies(in_refs, out_refs, sems, False)
        for cp in local + sends:
            cp.start()

    def wait(self, in_refs, out_refs, sems):
        local, sends, arrivals = self._copies(in_refs, out_refs, sems, True)
        for cp in arrivals:
            cp.wait_recv()
        for cp in sends:
            cp.wait_send()
        for cp in local:
            cp.wait()


def _call(name, body, grid, in_specs, out_specs, out_shape, scratch, args, side=None):
    in_specs, out_specs, out_shape, scratch = list(in_specs), list(out_specs), list(out_shape), list(scratch)
    if side is None:
        return pl.pallas_call(
            body, name=name, grid=grid, in_specs=in_specs, out_specs=out_specs, out_shape=out_shape, scratch_shapes=scratch,
            compiler_params=_cparams(len(grid)),
        )(*args)
    n_in, n_out, n_scr = len(in_specs), len(out_specs), len(scratch)
    s_in, s_out = len(side.ins), len(side.out_shape)

    def wrapped(*refs):
        at = 0
        parts = []
        for n in (n_in, s_in, n_out, s_out, n_scr, 3):
            parts.append(refs[at : at + n])
            at += n
        ins, side_ins, outs, side_outs, scr, sems = parts
        pids = [pl.program_id(d) for d in range(len(grid))]
        first = functools.reduce(jnp.logical_and, [p == 0 for p in pids])
        last = functools.reduce(jnp.logical_and, [p == g - 1 for p, g in zip(pids, grid)])

        @pl.when(first)
        def _():
            side.start(side_ins, side_outs, sems)

        body(*ins, *outs, *scr)

        @pl.when(last)
        def _():
            side.wait(side_ins, side_outs, sems)

    res = pl.pallas_call(
        wrapped, name=name, grid=grid, in_specs=in_specs + [_ANY] * s_in, out_specs=out_specs + [_ANY] * s_out,
        out_shape=out_shape + side.out_shape, scratch_shapes=scratch + side.scratch, compiler_params=_cparams(len(grid)),
        input_output_aliases={n_in + i: n_out + o for i, o in side.aliases.items()},
    )(*args, *side.ins)
    side.results = list(res[n_out:])
    return list(res[:n_out])


_CONTRACT = {"nn": ((1,), (0,)), "nt": ((1,), (1,)), "tn": ((0,), (0,))}


def _mm(name, a, b, dims, tm, tn, tk, *, ins=(), outs, epilogue, j_outer=False, side=None):
    if dims == "tn":
        K, M = a.shape
    else:
        M, K = a.shape
    N = b.shape[0] if dims == "nt" else b.shape[1]
    tm, tn, tk = min(tm, M), min(tn, N), min(tk, K)
    assert M % tm == 0 and N % tn == 0 and K % tk == 0, (name, M, N, K, tm, tn, tk)
    nm, nn, nk = M // tm, N // tn, K // tk
    if j_outer:
        grid = (nn, nm, nk)
        ij = lambda g0, g1: (g1, g0)
    else:
        grid = (nm, nn, nk)
        ij = lambda g0, g1: (g0, g1)

    def amap(g0, g1, k):
        i, _ = ij(g0, g1)
        return (k, i) if dims == "tn" else (i, k)

    def bmap(g0, g1, k):
        _, j = ij(g0, g1)
        return (j, k) if dims == "nt" else (k, j)

    def spec(kind):
        if kind == "tile":
            return pl.BlockSpec((tm, tn), lambda g0, g1, k: ij(g0, g1))
        if kind == "row":
            return pl.BlockSpec((1, tn), lambda g0, g1, k: (0, ij(g0, g1)[1]))
        assert kind == "col", kind
        return pl.BlockSpec((tm, 1), lambda g0, g1, k: (ij(g0, g1)[0], 0))

    in_specs = [
        pl.BlockSpec((tk, tm) if dims == "tn" else (tm, tk), amap),
        pl.BlockSpec((tn, tk) if dims == "nt" else (tk, tn), bmap),
    ] + [spec(kind) for _, kind in ins]
    out_specs = [spec(kind) for _, _, kind in outs]
    out_shape = [S(shape, dtype) for shape, dtype, _ in outs]
    n_in, n_out = len(ins), len(outs)
    contract = (_CONTRACT[dims], ((), ()))

    def body(*refs):
        a_ref, b_ref = refs[:2]
        in_refs = refs[2 : 2 + n_in]
        out_refs = refs[2 + n_in : 2 + n_in + n_out]
        acc = refs[2 + n_in + n_out]
        i, _ = ij(pl.program_id(0), pl.program_id(1))
        k = pl.program_id(2)

        def part():
            return lax.dot_general(a_ref[...], b_ref[...], contract, preferred_element_type=F32)

        @pl.when(k == 0)
        def _():
            acc[...] = part()

        @pl.when(k > 0)
        def _():
            acc[...] += part()

        @pl.when(k == nk - 1)
        def _():
            epilogue(i, acc, in_refs, out_refs)

    return _call(name, body, grid, in_specs, out_specs, out_shape, [pltpu.VMEM((tm, tn), F32)], [a, b, *[x for x, _ in ins]], side)


def _row(v):
    return v.reshape(1, -1)


def _mm_bias(name, a, b, dims, bias, tm, tn, tk, side=None):
    M = a.shape[0]
    N = b.shape[0] if dims == "nt" else b.shape[1]

    def epilogue(i, acc, ins, outs):
        def strip(r0):
            rows = pl.ds(r0, 128)
            outs[0][rows, :] = acc[rows, :] + ins[0][...]

        _for_strips(acc.shape[0], 128, strip)

    return _mm(name, a, b, dims, tm, tn, tk, ins=[(_row(bias), "row")], outs=[((M, N), F32, "tile")], epilogue=epilogue, side=side)[0]


def _mm_relu2(name, a, b, dims, bias, tm, tn, tk, side=None):
    M = a.shape[0]
    N = b.shape[0] if dims == "nt" else b.shape[1]

    def epilogue(i, acc, ins, outs):
        def strip(r0):
            rows = pl.ds(r0, 128)
            r = jnp.maximum(acc[rows, :] + ins[0][...], 0.0)
            outs[0][rows, :] = r.astype(BF16)
            outs[1][rows, :] = (r * r).astype(BF16)

        _for_strips(acc.shape[0], 128, strip)

    return _mm(
        name, a, b, dims, tm, tn, tk, ins=[(_row(bias), "row")],
        outs=[((M, N), BF16, "tile"), ((M, N), BF16, "tile")], epilogue=epilogue, side=side,
    )


def _mm_lagged(name, a, b, tm, tk, *, ins, outs, strip_fn, init_fn=None, side=None):
    M, K = a.shape
    N = b.shape[1]
    tm, tk = min(tm, M), min(tk, K)
    assert M % tm == 0 and K % tk == 0, (name, M, K, tm, tk)
    nm, nk = M // tm, K // tk
    assert nk >= 2, (name, nk)
    parts = 1 << ((nk - 1).bit_length() - 1)
    rows_p = tm // parts
    assert rows_p % LN_STRIP == 0, (name, rows_p)

    def part_index(i, k):
        return jnp.maximum((i - 1) * parts + jnp.minimum(k, parts - 1), 0)

    def spec(kind):
        if kind == "tile":
            return pl.BlockSpec((rows_p, N), lambda i, k: (part_index(i, k), 0))
        if kind == "row":
            return pl.BlockSpec((1, N), lambda i, k: (0, 0))
        assert kind == "col", kind
        return pl.BlockSpec((rows_p, 1), lambda i, k: (part_index(i, k), 0))

    in_specs = [
        pl.BlockSpec((tm, tk), lambda i, k: (jnp.minimum(i, nm - 1), k)),
        pl.BlockSpec((tk, N), lambda i, k: (jnp.where(i < nm, k, nk - 1), 0)),
    ] + [spec(kind) for _, kind in ins]
    n_in, n_out = len(ins), len(outs)

    def body(*refs):
        a_ref, b_ref = refs[:2]
        in_refs = refs[2 : 2 + n_in]
        out_refs = refs[2 + n_in : 2 + n_in + n_out]
        acc, fin = refs[2 + n_in + n_out :]
        i, k = pl.program_id(0), pl.program_id(1)

        def part():
            return jnp.dot(a_ref[...], b_ref[...], preferred_element_type=F32)

        def epilogue_part():
            base = k * rows_p
            for s in range(rows_p // LN_STRIP):
                acc_rows = fin[pl.ds(pl.multiple_of(base + s * LN_STRIP, LN_STRIP), LN_STRIP), :]
                strip_fn(acc_rows, in_refs, out_refs, pl.ds(s * LN_STRIP, LN_STRIP))

        has_dot = i < nm
        has_epilogue = jnp.logical_and(i > 0, k < parts)
        no_epilogue = jnp.logical_not(has_epilogue)
        last = k == nk - 1
        both = jnp.logical_and(has_dot, has_epilogue)
        alone = jnp.logical_and(has_dot, no_epilogue)

        if init_fn is not None:
            @pl.when(jnp.logical_and(i == 1, k == 0))
            def _():
                init_fn(out_refs)

        @pl.when(jnp.logical_and(alone, k == 0))
        def _():
            acc[...] = part()

        @pl.when(jnp.logical_and(alone, jnp.logical_and(k > 0, jnp.logical_not(last))))
        def _():
            acc[...] += part()

        @pl.when(jnp.logical_and(has_dot, last))
        def _():
            fin[...] = acc[...] + part()

        @pl.when(jnp.logical_and(both, k == 0))
        def _():
            acc[...] = part()
            epilogue_part()

        if parts > 1:
            @pl.when(jnp.logical_and(both, k > 0))
            def _():
                acc[...] += part()
                epilogue_part()

        @pl.when(jnp.logical_and(jnp.logical_not(has_dot), has_epilogue))
        def _():
            epilogue_part()

    return _call(
        name, body, (nm + 1, nk), in_specs, [spec(kind) for _, _, kind in outs], [S(shape, dtype) for shape, dtype, _ in outs],
        [pltpu.VMEM((tm, N), F32), pltpu.VMEM((tm, N), F32)], [a, b, *[x for x, _ in ins]], side,
    )


def _mm_ln(name, a, b, bias, res, g, beta, tm, tk, side=None):
    M = a.shape[0]
    N = b.shape[1]
    rxh, rg, rb = res

    def strip(acc_rows, ins, outs, rows):
        bias_r, rxh_r, rg_r, rb_r, g_r, beta_r = ins
        xhat_o, rstd_o, xbf_o = outs
        resid = rxh_r[rows, :] * rg_r[...] + rb_r[...]
        r = ALPHA * resid + (acc_rows + bias_r[...])
        xhat, rs = _ln_stats(r)
        xhat_o[rows, :] = xhat
        rstd_o[rows, :] = rs
        xbf_o[rows, :] = (xhat * g_r[...] + beta_r[...]).astype(BF16)

    return _mm_lagged(
        name, a, b, tm, tk,
        ins=[(_row(bias), "row"), (rxh, "tile"), (_row(rg), "row"), (_row(rb), "row"), (_row(g), "row"), (_row(beta), "row")],
        outs=[((M, N), F32, "tile"), ((M, 1), F32, "col"), ((M, N), BF16, "tile")],
        strip_fn=strip, side=side,
    )


def _ln_bwd_strip(dyv, xhat, rs, g, dr_o, drbf_o, dg_o, db_o, dsum_o, rows):
    dr = _ln_bwd(dyv, xhat, rs, g)
    dr_o[rows, :] = dr
    drbf_o[rows, :] = dr.astype(BF16)
    dg_o[...] += _row_sum(dyv * xhat)
    db_o[...] += _row_sum(dyv)
    dsum_o[...] += _row_sum(dr)


def _mm_ln_bwd(name, a, b, resgrad, xhat, rstd, g, tm, tk, side=None):
    M = a.shape[0]
    N = b.shape[1]

    def init(outs):
        for o in outs[2:]:
            o[...] = jnp.zeros_like(o)

    def strip(acc_rows, ins, outs, rows):
        rg_r, xh_r, rs_r, g_r = ins
        dyv = acc_rows + ALPHA * rg_r[rows, :]
        _ln_bwd_strip(dyv, xh_r[rows, :], rs_r[rows, :], g_r[...], *outs, rows)

    return _mm_lagged(
        name, a, b, tm, tk,
        ins=[(resgrad, "tile"), (xhat, "tile"), (rstd, "col"), (_row(g), "row")],
        outs=[((M, N), F32, "tile"), ((M, N), BF16, "tile"), ((1, N), F32, "row"), ((1, N), F32, "row"), ((1, N), F32, "row")],
        strip_fn=strip, init_fn=init, side=side,
    )


def _mm_dh(name, a, b, act, tm, tn, tk, side=None):
    M = a.shape[0]
    N = b.shape[0]

    def epilogue(i, acc, ins, outs):
        @pl.when(i == 0)
        def _():
            outs[1][...] = jnp.zeros_like(outs[1])

        def strip(r0):
            rows = pl.ds(r0, 128)
            d = acc[rows, :] * (2.0 * ins[0][rows, :].astype(F32))
            outs[0][rows, :] = d.astype(BF16)
            outs[1][...] += _row_sum(d)

        _for_strips(acc.shape[0], 128, strip)

    return _mm(
        name, a, b, "nt", tm, tn, tk, ins=[(act, "tile")],
        outs=[((M, N), BF16, "tile"), ((1, N), F32, "row")], epilogue=epilogue, j_outer=True, side=side,
    )


def _mm_plain(name, a, b, dims, tm, tn, tk, side=None):
    M = a.shape[0]
    N = b.shape[0] if dims == "nt" else b.shape[1]

    def epilogue(i, acc, ins, outs):
        def strip(r0):
            rows = pl.ds(r0, 128)
            outs[0][rows, :] = acc[rows, :]

        _for_strips(acc.shape[0], 128, strip)

    return _mm(name, a, b, dims, tm, tn, tk, outs=[((M, N), F32, "tile")], epilogue=epilogue, side=side)[0]


def _mm_res(name, a, b, res, tm, tk, side=None):
    def strip(acc_rows, ins, outs, rows):
        outs[0][rows, :] = acc_rows + ALPHA * ins[0][rows, :]

    return _mm_lagged(
        name, a, b, tm, tk, ins=[(res, "tile")], outs=[((a.shape[0], b.shape[1]), F32, "tile")], strip_fn=strip, side=side,
    )[0]


def _mm_wgrad(name, a, b, tm, tk, side=None):
    M = a.shape[1]
    N = b.shape[1]

    def epilogue(i, acc, ins, outs):
        def strip(r0):
            rows = pl.ds(r0, 128)
            outs[0][rows, :] = acc[rows, :].astype(BF16)

        _for_strips(acc.shape[0], 128, strip)

    return _mm(name, a, b, "tn", tm, N, tk, outs=[((M, N), BF16, "tile")], epilogue=epilogue, side=side)[0]


def _loss_top(xhat, rstd, g, beta, target, tm):
    T, D = xhat.shape
    tm = min(tm, T)
    nt = T // tm

    def body(xh_r, rs_r, g_r, b_r, t_r, dr_o, drbf_o, dg_o, db_o, dsum_o, loss_o, sq_acc):
        i = pl.program_id(0)

        @pl.when(i == 0)
        def _():
            dg_o[...] = jnp.zeros_like(dg_o)
            db_o[...] = jnp.zeros_like(db_o)
            dsum_o[...] = jnp.zeros_like(dsum_o)
            sq_acc[...] = jnp.zeros_like(sq_acc)

        def strip(r0):
            rows = pl.ds(r0, LN_STRIP)
            xh = xh_r[rows, :]
            err = (xh * g_r[...] + b_r[...]) - t_r[rows, :]
            sq_acc[...] += _row_sum(err * err)
            _ln_bwd_strip(err * (1.0 / D), xh, rs_r[rows, :], g_r[...], dr_o, drbf_o, dg_o, db_o, dsum_o, rows)

        _for_strips(tm, LN_STRIP, strip, unroll=LN_UNROLL)

        @pl.when(i == nt - 1)
        def _():
            total = jnp.sum(sq_acc[...], axis=-1, keepdims=True) * (0.5 / D)
            loss_o[...] = jnp.broadcast_to(total, loss_o.shape)

    tile = pl.BlockSpec((tm, D), lambda i: (i, 0))
    row = pl.BlockSpec((1, D), lambda i: (0, 0))
    return pl.pallas_call(
        body,
        name="loss_top",
        grid=(nt,),
        in_specs=[tile, pl.BlockSpec((tm, 1), lambda i: (i, 0)), row, row, tile],
        out_specs=[tile, tile, row, row, row, pl.BlockSpec((1, 128), lambda i: (0, 0))],
        out_shape=[S((T, D), F32), S((T, D), BF16), S((1, D), F32), S((1, D), F32), S((1, D), F32), S((1, 128), F32)],
        scratch_shapes=[pltpu.VMEM((1, D), F32)],
        compiler_params=_cparams(1),
    )(xhat, rstd, _row(g), _row(beta), target)


def _cols(ref, c):
    return ref[:, c[0] : c[1]]


def _causal_window_sum(e, w):
    s, sh = e, 1
    while sh < w:
        s = s + pltpu.roll(s, sh, axis=0)
        sh *= 2
    return s


def _anticausal_window_sum(d, w):
    n = d.shape[0]
    r, sh = d, 1
    while sh < w:
        r = r + pltpu.roll(r, n - sh, axis=0)
        sh *= 2
    return r


def _with_halo(halo_ref, main_ref, c, keep):
    return jnp.concatenate([_cols(halo_ref, c) * keep, _cols(main_ref, c)], axis=0)


def _pool_counts(tile_index, R, w):
    pos = lax.broadcasted_iota(jnp.int32, (R, 1), 0) + tile_index * R
    return jnp.minimum(pos + 1, w).astype(F32)


def _sgu_mix(wm_ref, vnb):
    return jnp.concatenate(
        [
            jnp.dot(wm_ref[h * GROUP : (h + 1) * GROUP, :], vnb[:, h * GROUP : (h + 1) * GROUP], preferred_element_type=F32)
            for h in range(SGU_HEADS)
        ],
        axis=1,
    )


CONV_HALVES = (slice(0, CONV_WIDTH // 2), slice(CONV_WIDTH // 2, CONV_WIDTH))
TAP_STRIP = 32
TAP_GROUP = 4


def _build_shifts(shf, src, cols, rows):
    n = rows - 8
    for r in range(1, 8):
        shf[r - 1, pl.ds(0, n), :] = src[pl.ds(r, n), cols]


def _shifted(shf, src, cols, offset, start, size):
    q, r = divmod(offset, 8)
    rows = pl.ds(pl.multiple_of(start + 8 * q, 8), size)
    return src[rows, cols] if r == 0 else shf[r - 1, rows, :]


def _conv_taps(shf, src, cw8, cols, offsets, n_rows, out, bias=None):
    width = cols.stop - cols.start

    def strip(s, carry):
        r0 = pl.multiple_of(s * TAP_STRIP, TAP_STRIP)
        acc = jnp.zeros((TAP_STRIP, width), F32)
        for k, o in enumerate(offsets):
            wk = cw8[pl.ds(8 * k, 8), cols]
            acc = acc + _shifted(shf, src, cols, o, r0, TAP_STRIP) * jnp.concatenate([wk] * (TAP_STRIP // 8), axis=0)
        if bias is not None:
            acc = acc + bias[:, cols]
        out[pl.ds(r0, TAP_STRIP), cols] = acc
        return carry

    lax.fori_loop(0, n_rows // TAP_STRIP, strip, 0)


def _conv_weight_grad(shf, src, dsrc, d_first, cols, offsets, n_rows, dcw):
    width = cols.stop - cols.start
    for k0 in range(0, len(offsets), TAP_GROUP):
        group = offsets[k0 : k0 + TAP_GROUP]

        def strip(s, accs, group=group):
            r0 = pl.multiple_of(s * TAP_STRIP, TAP_STRIP)
            d = dsrc[pl.ds(pl.multiple_of(d_first + r0, 8), TAP_STRIP), cols]
            out = []
            for acc8, o in zip(accs, group):
                p = _shifted(shf, src, cols, o, r0, TAP_STRIP) * d
                for j in range(TAP_STRIP // 8):
                    acc8 = acc8 + p[8 * j : 8 * j + 8, :]
                out.append(acc8)
            return tuple(out)

        accs = lax.fori_loop(0, n_rows // TAP_STRIP, strip, tuple(jnp.zeros((8, width), F32) for _ in group))
        for j, acc8 in enumerate(accs):
            dcw[pl.ds(k0 + j, 1), cols] += _row_sum(acc8)


def _mixer_params(p):
    return [p["wp"], p["ps"], p["lg"], p["lb"], p["wm"], p["wmt"], p["bsf"], p["cw8"], p["cb"], p["cg"], p["cbeta"]]


def _whole(x):
    return pl.BlockSpec(x.shape, lambda i: (0,) * x.ndim)


def _mixer_fwd(name, proj, p, R, side=None):
    T = proj.shape[0]
    R = min(R, T)
    E = R + HALO
    nt = T // R
    hb = R // HALO
    tap_offsets = [HALO - (CONV_KERNEL - 1) + k for k in range(CONV_KERNEL)]

    def body(pm, ph, wp, ps, lg, lb, wm, wmt, bsf, cw8, cb, cg, cbeta, out, hbuf, shf, convbuf):
        i = pl.program_id(0)
        keep = (i > 0).astype(F32)
        a_ext = _with_halo(ph, pm, C_POOL, keep)
        for gi, w in enumerate(POOL_WINDOWS):
            cs = slice(gi * GROUP, (gi + 1) * GROUP)
            e = a_ext[:, cs]
            s = _causal_window_sum(e, w)
            pooled = s[HALO:, :] / _pool_counts(i, R, w) - e[HALO:, :]
            z = jnp.dot(pooled.astype(BF16), wp[cs, :], preferred_element_type=F32)
            out[:, cs] = (z * ps[:, cs]).astype(BF16)
        u, _ = _gelu(_cols(pm, C_U))
        v, _ = _gelu(_cols(pm, C_V))
        vhat, _ = _ln_stats(v)
        vn = vhat * lg[...] + lb[...]
        for c in range(R // GROUP):
            rs = slice(c * GROUP, (c + 1) * GROUP)
            mixed = _sgu_mix(wm, vn[rs, :].astype(BF16)) + bsf[...]
            out[rs, M_SGU[0] : M_SGU[1]] = (u[rs, :] * mixed).astype(BF16)
        hbuf[...] = _with_halo(ph, pm, C_CA, keep) * jax.nn.sigmoid(_with_halo(ph, pm, C_CG, keep))
        for cols in CONV_HALVES:
            _build_shifts(shf, hbuf, cols, E)
            _conv_taps(shf, hbuf, cw8, cols, tap_offsets, R, convbuf, bias=cb)
        chat, _ = _ln_stats(convbuf[...])
        cn = chat * cg[...] + cbeta[...]
        out[:, M_CONV[0] : M_CONV[1]] = (cn * jax.nn.sigmoid(cn)).astype(BF16)

    params = _mixer_params(p)
    in_specs = [
        pl.BlockSpec((R, IN_WIDTH), lambda i: (i, 0)),
        pl.BlockSpec((HALO, IN_WIDTH), lambda i: (jnp.maximum(i * hb - 1, 0), 0)),
    ] + [_whole(x) for x in params]
    scratch = [pltpu.VMEM((E, CONV_WIDTH), F32), pltpu.VMEM((7, E, CONV_WIDTH // 2), F32), pltpu.VMEM((R, CONV_WIDTH), F32)]
    return _call(
        name, body, (nt,), in_specs, [pl.BlockSpec((R, D_MODEL), lambda i: (i, 0))], [S((T, D_MODEL), BF16)],
        scratch, [proj, proj, *params], side,
    )[0]


def _mixer_bwd(name, proj, dmix, p, R, side=None):
    T = proj.shape[0]
    R = min(R, T)
    E = R + HALO
    nt = T // R
    hb = R // HALO
    tap_offsets = [HALO - (CONV_KERNEL - 1) + k for k in range(CONV_KERNEL)]
    back_offsets = [HALO - o for o in tap_offsets]

    def body(pm, ph, dm, wp, ps, lg, lb, wm, wmt, bsf, cw8, cb, cg, cbeta,
             dproj, dwp, dps, dlg, dlb, dwm, dbs, dcw, dcb, dcg, dcbeta, dbin,
             hbuf, dbuf, carry_p, carry_c, dbs_acc, shf, convbuf, dhcbuf):
        step = pl.program_id(0)
        ti = nt - 1 - step
        keep = (ti > 0).astype(F32)

        @pl.when(step == 0)
        def _():
            for r in (dwp, dps, dlg, dlb, dwm, dcw, dcb, dcg, dcbeta, dbin, carry_p, carry_c, dbs_acc):
                r[...] = jnp.zeros_like(r)

        def tail(carry):
            return jnp.concatenate([jnp.zeros((R - HALO, carry.shape[1]), F32), carry], axis=0)

        def head(x):
            return jnp.concatenate([jnp.zeros((HALO, x.shape[1]), F32), x], axis=0)

        a_ext = _with_halo(ph, pm, C_POOL, keep)
        carry_in = carry_p[...]
        for gi, w in enumerate(POOL_WINDOWS):
            cs = slice(gi * GROUP, (gi + 1) * GROUP)
            e = a_ext[:, cs]
            s = _causal_window_sum(e, w)
            cnt = _pool_counts(ti, R, w)
            pooled_b = (s[HALO:, :] / cnt - e[HALO:, :]).astype(BF16)
            wg = wp[cs, :]
            z = jnp.dot(pooled_b, wg, preferred_element_type=F32)
            dya = dm[:, cs]
            dps[:, cs] += _row_sum(dya * z)
            dz_b = (dya * ps[:, cs]).astype(BF16)
            dwp[cs, :] += lax.dot_general(pooled_b, dz_b, (((0,), (0,)), ((), ())), preferred_element_type=F32)
            dpooled = lax.dot_general(dz_b, wg, (((1,), (1,)), ((), ())), preferred_element_type=F32)
            da_ext = _anticausal_window_sum(head(dpooled / cnt), w) - head(dpooled)
            carry_p[:, cs] = da_ext[:HALO, :]
            d_a = da_ext[HALO:, :] + tail(carry_in[:, cs])
            dbin[:, cs] += _row_sum(d_a)
            dproj[:, cs] = d_a.astype(BF16)

        pu = _cols(pm, C_U)
        pv = _cols(pm, C_V)
        u, thu = _gelu(pu)
        v, thv = _gelu(pv)
        vhat, vrs = _ln_stats(v)
        vn = vhat * lg[...] + lb[...]
        dyb = dm[:, M_SGU[0] : M_SGU[1]]
        du_parts, dvn_parts = [], []
        for c in range(R // GROUP):
            rs = slice(c * GROUP, (c + 1) * GROUP)
            vnb = vn[rs, :].astype(BF16)
            mixed = _sgu_mix(wm, vnb) + bsf[...]
            du_parts.append(dyb[rs, :] * mixed)
            dmixed = dyb[rs, :] * u[rs, :]
            dbs_acc[...] += dmixed
            dmb = dmixed.astype(BF16)
            dvn_h = []
            for h in range(SGU_HEADS):
                hs = slice(h * GROUP, (h + 1) * GROUP)
                dwm[hs, :] += lax.dot_general(dmb[:, hs], vnb[:, hs], (((1,), (1,)), ((), ())), preferred_element_type=F32)
                dvn_h.append(jnp.dot(wmt[hs, :], dmb[:, hs], preferred_element_type=F32))
            dvn_parts.append(jnp.concatenate(dvn_h, axis=1))
        du = jnp.concatenate(du_parts, axis=0) if len(du_parts) > 1 else du_parts[0]
        dvn = jnp.concatenate(dvn_parts, axis=0) if len(dvn_parts) > 1 else dvn_parts[0]
        dlg[...] += _row_sum(dvn * vhat)
        dlb[...] += _row_sum(dvn)
        d_pu = du * _gelu_grad(pu, thu)
        d_pv = _ln_bwd(dvn, vhat, vrs, lg[...]) * _gelu_grad(pv, thv)
        dbin[:, C_U[0] : C_U[1]] += _row_sum(d_pu)
        dbin[:, C_V[0] : C_V[1]] += _row_sum(d_pv)
        dproj[:, C_U[0] : C_U[1]] = d_pu.astype(BF16)
        dproj[:, C_V[0] : C_V[1]] = d_pv.astype(BF16)

        sg_ext = jax.nn.sigmoid(_with_halo(ph, pm, C_CG, keep))
        ca_ext = _with_halo(ph, pm, C_CA, keep)
        hbuf[...] = ca_ext * sg_ext
        for cols in CONV_HALVES:
            _build_shifts(shf, hbuf, cols, E)
            _conv_taps(shf, hbuf, cw8, cols, tap_offsets, R, convbuf, bias=cb)
        chat, crs = _ln_stats(convbuf[...])
        cn = chat * cg[...] + cbeta[...]
        sc = jax.nn.sigmoid(cn)
        dcn = dm[:, M_CONV[0] : M_CONV[1]] * (sc * (1.0 + cn * (1.0 - sc)))
        dcg[...] += _row_sum(dcn * chat)
        dcbeta[...] += _row_sum(dcn)
        dconv = _ln_bwd(dcn, chat, crs, cg[...])
        dcb[...] += _row_sum(dconv)
        dbuf[pl.ds(0, HALO), :] = jnp.zeros((HALO, CONV_WIDTH), F32)
        dbuf[pl.ds(HALO, R), :] = dconv
        dbuf[pl.ds(HALO + R, HALO), :] = jnp.zeros((HALO, CONV_WIDTH), F32)
        for cols in CONV_HALVES:
            _build_shifts(shf, hbuf, cols, E)
            _conv_weight_grad(shf, hbuf, dbuf, HALO, cols, tap_offsets, R, dcw)
            _build_shifts(shf, dbuf, cols, E + HALO)
            _conv_taps(shf, dbuf, cw8, cols, back_offsets, E, dhcbuf)
        dhc_main = dhcbuf[pl.ds(HALO, R), :] + tail(carry_c[...])
        carry_c[...] = dhcbuf[pl.ds(0, HALO), :]
        sg = sg_ext[HALO:, :]
        d_ca = dhc_main * sg
        d_cg = dhc_main * ca_ext[HALO:, :] * (sg * (1.0 - sg))
        dbin[:, C_CA[0] : C_CA[1]] += _row_sum(d_ca)
        dbin[:, C_CG[0] : C_CG[1]] += _row_sum(d_cg)
        dproj[:, C_CA[0] : C_CA[1]] = d_ca.astype(BF16)
        dproj[:, C_CG[0] : C_CG[1]] = d_cg.astype(BF16)

        @pl.when(step == nt - 1)
        def _():
            row = lax.broadcasted_iota(jnp.int32, (GROUP, GROUP), 0)
            col = lax.broadcasted_iota(jnp.int32, (GROUP, GROUP), 1)
            dbs[...] = jnp.zeros_like(dbs)
            for h in range(SGU_HEADS):
                hs = slice(h * GROUP, (h + 1) * GROUP)
                dwm[hs, :] = jnp.where(row >= col, dwm[hs, :], 0.0)
                dbs[pl.ds(h, 1), :] = _row_sum(dbs_acc[:, hs].T)

    params = _mixer_params(p)
    accs = [
        S((POOL_WIDTH, GROUP), F32), S((1, POOL_WIDTH), F32), S((1, SGU_WIDTH), F32), S((1, SGU_WIDTH), F32),
        S((SGU_WIDTH, GROUP), F32), S((8, GROUP), F32), S((32, CONV_WIDTH), F32), S((1, CONV_WIDTH), F32),
        S((1, CONV_WIDTH), F32), S((1, CONV_WIDTH), F32), S((1, IN_WIDTH), F32),
    ]
    in_specs = [
        pl.BlockSpec((R, IN_WIDTH), lambda i: (nt - 1 - i, 0)),
        pl.BlockSpec((HALO, IN_WIDTH), lambda i: (jnp.maximum((nt - 1 - i) * hb - 1, 0), 0)),
        pl.BlockSpec((R, D_MODEL), lambda i: (nt - 1 - i, 0)),
    ] + [_whole(x) for x in params]
    scratch = [
        pltpu.VMEM((E, CONV_WIDTH), F32), pltpu.VMEM((E + HALO, CONV_WIDTH), F32),
        pltpu.VMEM((HALO, POOL_WIDTH), F32), pltpu.VMEM((HALO, CONV_WIDTH), F32), pltpu.VMEM((GROUP, SGU_WIDTH), F32),
        pltpu.VMEM((7, E + HALO, CONV_WIDTH // 2), F32), pltpu.VMEM((R, CONV_WIDTH), F32), pltpu.VMEM((E, CONV_WIDTH), F32),
    ]
    return _call(
        name, body, (nt,), in_specs, [pl.BlockSpec((R, IN_WIDTH), lambda i: (nt - 1 - i, 0))] + [_whole(x) for x in accs],
        [S((T, IN_WIDTH), BF16)] + accs, scratch, [proj, proj, dmix, *params], side,
    )


def _all_gather(xs):
    n = len(xs)

    def body(*refs):
        x_refs, o_refs = refs[:n], refs[n : 2 * n]
        send_sems, recv_sems, local_sems = refs[2 * n :]
        x, y, c = _place()
        me, sibling = (x, y, c), (x, y, 1 - c)
        chips = [(1 - x, y), (x, 1 - y), (1 - x, 1 - y)]

        def copy(a, k, block, to, src=None):
            dst = o_refs[a].at[_lin(block)]
            return pltpu.make_async_remote_copy(
                src_ref=dst if src is None else src, dst_ref=dst, send_sem=send_sems.at[a, k], recv_sem=recv_sems.at[a, k],
                device_id=to, device_id_type=MESH,
            )

        mine = [pltpu.make_async_copy(x_refs[a], o_refs[a].at[_lin(me)], local_sems.at[a]) for a in range(n)]
        for m in mine:
            m.start()
        first = []
        for a in range(n):
            first.append(copy(a, 0, me, sibling, src=x_refs[a]))
            first += [copy(a, 1 + j, me, (*chip, c), src=x_refs[a]) for j, chip in enumerate(chips)]
        for cp in first:
            cp.start()
        passed = []
        for a in range(n):
            for j, chip in enumerate(chips):
                copy(a, 1 + j, (*chip, c), me).wait_recv()
                fwd = copy(a, 4 + j, (*chip, c), sibling)
                fwd.start()
                passed.append(fwd)
        for a in range(n):
            copy(a, 0, sibling, me).wait_recv()
            for j, chip in enumerate(chips):
                copy(a, 4 + j, (*chip, 1 - c), me).wait_recv()
        for cp in first + passed:
            cp.wait_send()
        for m in mine:
            m.wait()

    return pl.pallas_call(
        body,
        name="all_gather_weights",
        in_specs=[_ANY] * n,
        out_specs=[_ANY] * n,
        out_shape=[S((N_DEV, *x.shape), x.dtype) for x in xs],
        scratch_shapes=[pltpu.SemaphoreType.DMA((n, 7)), pltpu.SemaphoreType.DMA((n, 7)), pltpu.SemaphoreType.DMA((n,))],
    )(*xs)


def _row_tile(rows, want):
    return next(t for t in range(min(rows, want) // 8 * 8, 0, -8) if rows % t == 0)


def _sum_slots(name, slots):
    _, rows, cols = slots.shape
    tr = _row_tile(rows, (4 << 20) // (N_DEV * cols * slots.dtype.itemsize))

    def body(s_ref, o_ref):
        total = s_ref[0].astype(F32)
        for d in range(1, N_DEV):
            total = total + s_ref[d].astype(F32)
        o_ref[...] = total

    return pl.pallas_call(
        body,
        name=name,
        grid=(rows // tr,),
        in_specs=[pl.BlockSpec((N_DEV, tr, cols), lambda i: (0, i, 0))],
        out_specs=pl.BlockSpec((tr, cols), lambda i: (i, 0)),
        out_shape=S((rows, cols), F32),
        compiler_params=_cparams(1),
    )(slots)


def _adamw(name, w, g, m, v):
    rows, cols = w.shape
    tr = rows if rows * cols * 4 <= (2 << 20) else _row_tile(rows, 1 << ((1 << 18) // cols).bit_length() - 1)

    def body(w_ref, g_ref, m_ref, v_ref, d_ref, nm_ref, nv_ref):
        gv = g_ref[...]
        nm = ADAM_B1 * m_ref[...] + (1.0 - ADAM_B1) * gv
        nv = ADAM_B2 * v_ref[...] + (1.0 - ADAM_B2) * (gv * gv)
        m_hat = nm / (1.0 - ADAM_B1**ADAM_STEP)
        v_hat = nv / (1.0 - ADAM_B2**ADAM_STEP)
        d_ref[...] = -ADAM_LR * (m_hat / (jnp.sqrt(v_hat) + ADAM_EPS) + ADAM_WD * w_ref[...])
        nm_ref[...] = nm
        nv_ref[...] = nv

    blk = pl.BlockSpec((tr, cols), lambda i: (i, 0))
    return pl.pallas_call(
        body,
        name=name,
        grid=(rows // tr,),
        in_specs=[blk] * 4,
        out_specs=[blk] * 3,
        out_shape=[S((rows, cols), F32)] * 3,
        compiler_params=_cparams(1),
    )(w, g, m, v)


_BIG = ("w_in", "w_out", "w_ff1", "w_ff2")
_TRANSPOSED = ("w_in", "w_ff1")
_SMALL = ("b_in", "w_pool", "pool_scale", "sgu_ln_g", "sgu_ln_b", "sgu_w", "sgu_b", "conv_b", "conv_ln_g", "conv_ln_b",
          "b_out", "ln1_g", "ln1_b", "b_ff1", "b_ff2", "ln2_g", "ln2_b")
_WEIGHTS = ("w_in", "b_in", "w_pool", "pool_scale", "sgu_ln_g", "sgu_ln_b", "sgu_w", "sgu_b", "conv_w", "conv_b", "conv_ln_g",
            "conv_ln_b", "w_out", "b_out", "ln1_g", "ln1_b", "w_ff1", "b_ff1", "w_ff2", "b_ff2", "ln2_g", "ln2_b")


def _pack(arrays):
    parts = []
    for a in arrays:
        rows = a.reshape(-1, 128)
        parts.append(jnp.pad(rows, ((0, -rows.shape[0] % 8), (0, 0))))
    return jnp.concatenate(parts, axis=0)


def _unpack(flat, like):
    out, at = [], 0
    for a in like:
        n = a.size // 128
        out.append(flat[at : at + n].reshape(a.shape))
        at += n + (-n % 8)
    return out


def _packed_rows(arrays):
    return sum(a.size // 128 + (-(a.size // 128) % 8) for a in arrays)


def kernel(x, w_in, b_in, w_pool, pool_scale, sgu_ln_g, sgu_ln_b, sgu_w, sgu_b, conv_w, conv_b, conv_ln_g, conv_ln_b, w_out, b_out, ln1_g, ln1_b, w_ff1, b_ff1, w_ff2, b_ff2, ln2_g, ln2_b, loss_target, m_w_in, m_b_in, m_w_pool, m_pool_scale, m_sgu_ln_g, m_sgu_ln_b, m_sgu_w, m_sgu_b, m_conv_w, m_conv_b, m_conv_ln_g, m_conv_ln_b, m_w_out, m_b_out, m_ln1_g, m_ln1_b, m_w_ff1, m_b_ff1, m_w_ff2, m_b_ff2, m_ln2_g, m_ln2_b, v_w_in, v_b_in, v_w_pool, v_pool_scale, v_sgu_ln_g, v_sgu_ln_b, v_sgu_w, v_sgu_b, v_conv_w, v_conv_b, v_conv_ln_g, v_conv_ln_b, v_w_out, v_b_out, v_ln1_g, v_ln1_b, v_w_ff1, v_b_ff1, v_w_ff2, v_b_ff2, v_ln2_g, v_ln2_b):
    w = dict(w_in=w_in, b_in=b_in, w_pool=w_pool, pool_scale=pool_scale, sgu_ln_g=sgu_ln_g, sgu_ln_b=sgu_ln_b, sgu_w=sgu_w,
             sgu_b=sgu_b, conv_w=conv_w, conv_b=conv_b, conv_ln_g=conv_ln_g, conv_ln_b=conv_ln_b, w_out=w_out, b_out=b_out,
             ln1_g=ln1_g, ln1_b=ln1_b, w_ff1=w_ff1, b_ff1=b_ff1, w_ff2=w_ff2, b_ff2=b_ff2, ln2_g=ln2_g, ln2_b=ln2_b)
    mom = dict(w_in=m_w_in, b_in=m_b_in, w_pool=m_w_pool, pool_scale=m_pool_scale, sgu_ln_g=m_sgu_ln_g, sgu_ln_b=m_sgu_ln_b,
               sgu_w=m_sgu_w, sgu_b=m_sgu_b, conv_w=m_conv_w, conv_b=m_conv_b, conv_ln_g=m_conv_ln_g, conv_ln_b=m_conv_ln_b,
               w_out=m_w_out, b_out=m_b_out, ln1_g=m_ln1_g, ln1_b=m_ln1_b, w_ff1=m_w_ff1, b_ff1=m_b_ff1, w_ff2=m_w_ff2,
               b_ff2=m_b_ff2, ln2_g=m_ln2_g, ln2_b=m_ln2_b)
    var = dict(w_in=v_w_in, b_in=v_b_in, w_pool=v_w_pool, pool_scale=v_pool_scale, sgu_ln_g=v_sgu_ln_g, sgu_ln_b=v_sgu_ln_b,
               sgu_w=v_sgu_w, sgu_b=v_sgu_b, conv_w=v_conv_w, conv_b=v_conv_b, conv_ln_g=v_conv_ln_g, conv_ln_b=v_conv_ln_b,
               w_out=v_w_out, b_out=v_b_out, ln1_g=v_ln1_g, ln1_b=v_ln1_b, w_ff1=v_w_ff1, b_ff1=v_b_ff1, w_ff2=v_w_ff2,
               b_ff2=v_b_ff2, ln2_g=v_ln2_g, ln2_b=v_ln2_b)
    T = x.shape[1]
    x0 = x.reshape(T, D_MODEL)
    target = loss_target.reshape(T, D_MODEL)
    me_lin = _lin(_place())

    shard = [
        {name: (w[name][l].T if name in _TRANSPOSED else w[name][l]).astype(BF16) for name in _BIG} for l in range(DEPTH)
    ]
    conv_shard = jnp.pad(conv_w, ((0, 0), (0, 1), (0, 128 - conv_w.shape[2]))).reshape(DEPTH * 32, 128)

    def rows_of(g):
        return g.reshape(N_DEV * g.shape[1], g.shape[2])

    first = _all_gather([shard[0]["w_in"], shard[0]["w_out"], conv_shard])
    full = [{} for _ in range(DEPTH)]
    full[0]["w_in"], full[0]["w_out"] = rows_of(first[0]), rows_of(first[1])
    conv_cols = conv_w.shape[2]
    conv_full = first[2].reshape(N_DEV, DEPTH, 32, 128)[:, :, :CONV_KERNEL, :conv_cols]
    conv_full = conv_full.transpose(1, 2, 0, 3).reshape(DEPTH, CONV_KERNEL, N_DEV * conv_cols)

    tril = jnp.tril(jnp.ones((GROUP, GROUP), F32))
    prm = []
    for l in range(DEPTH):
        wm = sgu_w[l] * tril
        prm.append(dict(
            wp=w_pool[l].reshape(POOL_WIDTH, GROUP).astype(BF16), ps=_row(pool_scale[l]), lg=_row(sgu_ln_g[l]), lb=_row(sgu_ln_b[l]),
            wm=wm.reshape(SGU_WIDTH, GROUP).astype(BF16), wmt=wm.transpose(0, 2, 1).reshape(SGU_WIDTH, GROUP).astype(BF16),
            bsf=jnp.repeat(sgu_b[l].T, GROUP, axis=1), cw8=jnp.repeat(jnp.pad(conv_full[l], ((0, 1), (0, 0))), 8, axis=0), cb=_row(conv_b[l]),
            cg=_row(conv_ln_g[l]), cbeta=_row(conv_ln_b[l]),
        ))

    saved = []
    res = (x0, jnp.ones((D_MODEL,), F32), jnp.zeros((D_MODEL,), F32))
    xbf = x0.astype(BF16)
    u = shard[0]["w_ff1"].shape[0] // 8

    for l in range(DEPTH):
        f = full[l]
        s_ff1, s_ff2 = shard[l]["w_ff1"], shard[l]["w_ff2"]
        ex = _Exchange([("gather", s_ff1, (0, 3 * u), None)])
        proj = _mm_bias(f"proj{l}", xbf, f["w_in"], "nt", b_in[l], 2048, 896, 2048, side=ex)
        ex = _Exchange([("gather", s_ff1, (3 * u, 3 * u), ex.results[0])])
        mixed = _mixer_fwd(f"mixer_fwd{l}", proj, prm[l], 512, side=ex)
        ex = _Exchange([("gather", s_ff1, (6 * u, 2 * u), ex.results[0]), ("gather", s_ff2, (0, u), None)])
        xh1, rs1, x1bf = _mm_ln(f"out_ln1_{l}", mixed, f["w_out"], b_out[l], res, ln1_g[l], ln1_b[l], 1024, 512, side=ex)
        f["w_ff1"] = rows_of(ex.results[0])
        ex = _Exchange([("gather", s_ff2, (u, 7 * u), ex.results[1])])
        act, hsq = _mm_relu2(f"ff1_{l}", x1bf, f["w_ff1"], "nt", b_ff1[l], 1024, 1024, 2048, side=ex)
        f["w_ff2"] = rows_of(ex.results[0])
        ex = _Exchange([("gather", shard[l + 1]["w_in"]), ("gather", shard[l + 1]["w_out"])]) if l + 1 < DEPTH else None
        xh2, rs2, x2bf = _mm_ln(
            f"ff2_ln2_{l}", hsq, f["w_ff2"], b_ff2[l], (xh1, ln1_g[l], ln1_b[l]), ln2_g[l], ln2_b[l], 1024, 1024, side=ex)
        if ex is not None:
            full[l + 1]["w_in"], full[l + 1]["w_out"] = rows_of(ex.results[0]), rows_of(ex.results[1])
        saved.append(dict(xin=xbf, proj=proj, mixed=mixed, xh1=xh1, rs1=rs1, x1bf=x1bf, act=act, hsq=hsq, xh2=xh2, rs2=rs2))
        res = (xh2, ln2_g[l], ln2_b[l])
        xbf = x2bf

    top = saved[-1]
    dr2, dr2bf, g_ln2g, g_ln2b, g_bff2, loss_row = _loss_top(top["xh2"], top["rs2"], ln2_g[-1], ln2_b[-1], target, 256)
    loss = lax.psum(loss_row[0, 0], ("x", "y", "c"))
    slots = [{} for _ in range(DEPTH)]
    gsm = [{} for _ in range(DEPTH)]
    grad_x = small_slots = None

    def stacked_small():
        st = {name: jnp.stack([gsm[gl][name].reshape(w[name].shape[1:]) for gl in range(DEPTH)]) for name in _SMALL}
        conv_g = jnp.pad(jnp.stack([gsm[gl]["conv_w"] for gl in range(DEPTH)]), ((0, 0), (0, 1), (0, 0)))
        return [st[name] for name in _SMALL] + [conv_g]

    for l in reversed(range(DEPTH)):
        f, sv = full[l], saved[l]
        gsm[l].update(ln2_g=g_ln2g, ln2_b=g_ln2b, b_ff2=g_bff2)
        gw = _mm_wgrad(f"gw_ff2_{l}", sv["hsq"], dr2bf, 1024, 2048)
        ex = _Exchange([("slices", gw)])
        dhpre, g_bff1 = _mm_dh(f"dff1_{l}", dr2bf, f["w_ff2"], sv["act"], 1024, 1024, 2048, side=ex)
        slots[l]["w_ff2"] = ex.results[0]
        gsm[l]["b_ff1"] = g_bff1
        gw = _mm_wgrad(f"gw_ff1_{l}", dhpre, sv["x1bf"], 1024, 2048)
        ex = _Exchange([("slices", gw)])
        dr1, dr1bf, g_ln1g, g_ln1b, g_bout = _mm_ln_bwd(
            f"dx1_ln1_{l}", dhpre, f["w_ff1"], dr2, sv["xh1"], sv["rs1"], ln1_g[l], 1024, 1024, side=ex)
        slots[l]["w_ff1"] = ex.results[0]
        gsm[l].update(ln1_g=g_ln1g, ln1_b=g_ln1b, b_out=g_bout)
        gw = _mm_wgrad(f"gw_out_{l}", sv["mixed"], dr1bf, 1024, 2048)
        dmix = _mm_plain(f"dmixed{l}", dr1bf, f["w_out"], "nt", 1024, 2048, 2048)
        ex = _Exchange([("slices", gw)])
        (dproj, g_wp, g_ps, g_lg, g_lb, g_wm, g_bs, g_cw, g_cb, g_cg, g_cbeta, g_bin) = _mixer_bwd(
            f"mixer_bwd{l}", sv["proj"], dmix, prm[l], 512, side=ex)
        slots[l]["w_out"] = ex.results[0]
        gsm[l].update(b_in=g_bin, w_pool=g_wp, pool_scale=g_ps, sgu_ln_g=g_lg, sgu_ln_b=g_lb, sgu_w=g_wm, sgu_b=g_bs[:SGU_HEADS],
                      conv_w=g_cw[:CONV_KERNEL], conv_b=g_cb, conv_ln_g=g_cg, conv_ln_b=g_cbeta)
        if l > 0:
            gw = _mm_wgrad(f"gw_in_{l}", dproj, sv["xin"], 896, 2048)
            below = saved[l - 1]
            ex = _Exchange([("slices", gw)])
            dr2, dr2bf, g_ln2g, g_ln2b, g_bff2 = _mm_ln_bwd(
                f"dx_ln2_{l}", dproj, f["w_in"], dr1, below["xh2"], below["rs2"], ln2_g[l - 1], 1024, 512, side=ex)
        else:
            small_like = stacked_small()
            ex = _Exchange([("gather", _pack(small_like))])
            gw = _mm_wgrad(f"gw_in_{l}", dproj, sv["xin"], 896, 2048, side=ex)
            small_slots = ex.results[0]
            ex = _Exchange([("slices", gw)])
            grad_x = _mm_res("dx0", dproj, f["w_in"], dr1, 1024, 512, side=ex)
        slots[l]["w_in"] = ex.results[0]

    grads, deltas, new_m, new_v = {}, {}, {}, {}
    for name in _BIG:
        per_layer = []
        for l in range(DEPTH):
            g = _sum_slots(f"sum_{name}_{l}", slots[l][name])
            per_layer.append(g.T if name in _TRANSPOSED else g)
        g = jnp.stack(per_layer)
        shape = w[name].shape
        two_d = (shape[0] * shape[1], shape[2])
        d, nm, nv = _adamw(f"adamw_{name}", w[name].reshape(two_d), g.reshape(two_d), mom[name].reshape(two_d), var[name].reshape(two_d))
        grads[name], deltas[name], new_m[name], new_v[name] = g, d.reshape(shape), nm.reshape(shape), nv.reshape(shape)

    total = _sum_slots("sum_small", small_slots)
    small_g = _unpack(total, small_like)
    like = [w[name] for name in _SMALL]
    d, nm, nv = _adamw("adamw_small", _pack(like), total[: _packed_rows(like)],
                       _pack([mom[name] for name in _SMALL]), _pack([var[name] for name in _SMALL]))
    for name, gg, dd, mm_, vv in zip(_SMALL, small_g, _unpack(d, like), _unpack(nm, like), _unpack(nv, like)):
        grads[name], deltas[name], new_m[name], new_v[name] = gg, dd, mm_, vv
    conv_g = lax.dynamic_slice_in_dim(small_g[-1][:, :CONV_KERNEL, :], me_lin * conv_cols, conv_cols, axis=2)
    flat = (DEPTH * CONV_KERNEL, conv_cols)
    d, nm, nv = _adamw("adamw_conv_w", conv_w.reshape(flat), conv_g.reshape(flat), m_conv_w.reshape(flat), v_conv_w.reshape(flat))
    grads["conv_w"], deltas["conv_w"], new_m["conv_w"], new_v["conv_w"] = conv_g, d.reshape(conv_w.shape), nm.reshape(conv_w.shape), nv.reshape(conv_w.shape)

    return (loss, grad_x.reshape(x.shape), *[grads[n] for n in _WEIGHTS], *[deltas[n] for n in _WEIGHTS],
            *[new_m[n] for n in _WEIGHTS], *[new_v[n] for n in _WEIGHTS])
```

```python
import functools

import jax
import jax.numpy as jnp
from jax import lax
from jax.experimental import pallas as pl
from jax.experimental.pallas import tpu as pltpu

F32, BF16 = jnp.float32, jnp.bfloat16
S = jax.ShapeDtypeStruct

DEPTH = 2
D_MODEL = 2048
POOL_WINDOWS = (2, 4, 8, 16)
POOL_WIDTH = 512
GROUP = 128
SGU_WIDTH = 768
SGU_HEADS = 6
CONV_WIDTH = 768
CONV_KERNEL = 31
IN_WIDTH = 3584
D_FF = 8192
ALPHA = (2 * DEPTH) ** 0.25
LN_EPS = 1e-5
ADAM_LR, ADAM_B1, ADAM_B2, ADAM_EPS, ADAM_WD, ADAM_STEP = 0.001, 0.9, 0.999, 1e-08, 0.01, 10

N_DEV = 8
LN_STRIP = 32
LN_UNROLL = 4
HALO = 32
VMEM_LIMIT = 56 << 20
MESH = pl.DeviceIdType.MESH

C_POOL = (0, 512)
C_U = (512, 1280)
C_V = (1280, 2048)
C_CA = (2048, 2816)
C_CG = (2816, 3584)
M_POOL = (0, 512)
M_SGU = (512, 1280)
M_CONV = (1280, 2048)


def _cparams(n_axes):
    return pltpu.CompilerParams(dimension_semantics=("arbitrary",) * n_axes, vmem_limit_bytes=VMEM_LIMIT)


def _for_strips(rows, strip, fn, unroll=1):
    n = rows // strip
    if n == 1:
        fn(0)
        return

    def step(s, carry):
        fn(pl.multiple_of(s * strip, strip))
        return carry

    lax.fori_loop(0, n, step, 0, unroll=unroll)


def _row_sum(x):
    return jnp.sum(x, axis=0, keepdims=True)


def _ln_stats(r):
    mu = jnp.mean(r, axis=-1, keepdims=True)
    xc = r - mu
    var = jnp.mean(xc * xc, axis=-1, keepdims=True)
    rs = lax.rsqrt(var + LN_EPS)
    return xc * rs, rs


def _ln_bwd(dy, xhat, rs, g):
    gy = dy * g
    m1 = jnp.mean(gy, axis=-1, keepdims=True)
    m2 = jnp.mean(gy * xhat, axis=-1, keepdims=True)
    return rs * (gy - m1 - xhat * m2)


_GELU_C = 0.7978845608028654


def _gelu(x):
    th = jnp.tanh(_GELU_C * (x + 0.044715 * (x * x * x)))
    return 0.5 * x * (1.0 + th), th


def _gelu_grad(x, th):
    return 0.5 * (1.0 + th) + 0.5 * x * (1.0 - th * th) * (_GELU_C * (1.0 + 3.0 * 0.044715 * (x * x)))


def _place():
    x, y, c = lax.axis_index("x"), lax.axis_index("y"), lax.axis_index("c")
    return x, y, c


def _lin(p):
    return 4 * p[0] + 2 * p[1] + p[2]


def _flip(p, r):
    return tuple(1 - v if (r >> (2 - ax)) & 1 else v for ax, v in enumerate(p))


_ANY = pl.BlockSpec(memory_space=pl.ANY)


class _Exchange:
    def __init__(self, items):
        self.kinds = [item[0] for item in items]
        self.srcs = [item[1] for item in items]
        self.rows = [item[2] if len(item) > 2 else None for item in items]
        handed_on = [item[3] if len(item) > 3 else None for item in items]
        self.out_shape = [
            S((N_DEV, *x.shape), x.dtype) if kind == "gather" else S((N_DEV, x.shape[0] // N_DEV, x.shape[1]), x.dtype)
            for kind, x in zip(self.kinds, self.srcs)
        ]
        n = len(items)
        self.ins = self.srcs + [b for b in handed_on if b is not None]
        self.aliases = {}
        for a, b in enumerate(handed_on):
            if b is not None:
                self.aliases[n + len(self.aliases)] = a
        self.scratch = [pltpu.SemaphoreType.DMA((n, 7)), pltpu.SemaphoreType.DMA((n, 7)), pltpu.SemaphoreType.DMA((n,))]
        self.results = None

    def _src(self, in_refs, a, dest):
        if self.kinds[a] == "gather":
            return in_refs[a] if self.rows[a] is None else in_refs[a].at[pl.ds(*self.rows[a])]
        rows = self.srcs[a].shape[0] // N_DEV
        return in_refs[a].at[pl.ds(pl.multiple_of(_lin(dest) * rows, 8), rows)]

    def _dst(self, out_refs, a, slot):
        return out_refs[a].at[slot] if self.rows[a] is None else out_refs[a].at[slot, pl.ds(*self.rows[a])]

    def _copies(self, in_refs, out_refs, sems, with_arrivals):
        send_sems, recv_sems, local_sems = sems
        me = _place()
        local, sends, arrivals = [], [], []
        for a in range(len(self.srcs)):
            local.append(pltpu.make_async_copy(self._src(in_refs, a, me), self._dst(out_refs, a, _lin(me)), local_sems.at[a]))
            for r in range(1, N_DEV):
                peer = _flip(me, r)
                for slot, group in ((_lin(me), sends), (_lin(peer), arrivals)):
                    if group is sends or with_arrivals:
                        group.append(pltpu.make_async_remote_copy(
                            src_ref=self._src(in_refs, a, peer), dst_ref=self._dst(out_refs, a, slot),
                            send_sem=send_sems.at[a, r - 1], recv_sem=recv_sems.at[a, r - 1], device_id=peer, device_id_type=MESH,
                        ))
        return local, sends, arrivals

    def start(self, in_refs, out_refs, sems):
        local, sends, _ = self._copies(in_refs, out_refs, sems, False)
        for cp in local + sends:
            cp.start()

    def wait(self, in_refs, out_refs, sems):
        local, sends, arrivals = self._copies(in_refs, out_refs, sems, True)
        for cp in arrivals:
            cp.wait_recv()
        for cp in sends:
            cp.wait_send()
        for cp in local:
            cp.wait()


def _call(name, body, grid, in_specs, out_specs, out_shape, scratch, args, side=None):
    in_specs, out_specs, out_shape, scratch = list(in_specs), list(out_specs), list(out_shape), list(scratch)
    if side is None:
        return pl.pallas_call(
            body, name=name, grid=grid, in_specs=in_specs, out_specs=out_specs, out_shape=out_shape, scratch_shapes=scratch,
            compiler_params=_cparams(len(grid)),
        )(*args)
    n_in, n_out, n_scr = len(in_specs), len(out_specs), len(scratch)
    s_in, s_out = len(side.ins), len(side.out_shape)

    def wrapped(*refs):
        at = 0
        parts = []
        for n in (n_in, s_in, n_out, s_out, n_scr, 3):
            parts.append(refs[at : at + n])
            at += n
        ins, side_ins, outs, side_outs, scr, sems = parts
        pids = [pl.program_id(d) for d in range(len(grid))]
        first = functools.reduce(jnp.logical_and, [p == 0 for p in pids])
        last = functools.reduce(jnp.logical_and, [p == g - 1 for p, g in zip(pids, grid)])

        @pl.when(first)
        def _():
            side.start(side_ins, side_outs, sems)

        body(*ins, *outs, *scr)

        @pl.when(last)
        def _():
            side.wait(side_ins, side_outs, sems)

    res = pl.pallas_call(
        wrapped, name=name, grid=grid, in_specs=in_specs + [_ANY] * s_in, out_specs=out_specs + [_ANY] * s_out,
        out_shape=out_shape + side.out_shape, scratch_shapes=scratch + side.scratch, compiler_params=_cparams(len(grid)),
        input_output_aliases={n_in + i: n_out + o for i, o in side.aliases.items()},
    )(*args, *side.ins)
    side.results = list(res[n_out:])
    return list(res[:n_out])


_CONTRACT = {"nn": ((1,), (0,)), "nt": ((1,), (1,)), "tn": ((0,), (0,))}


def _mm(name, a, b, dims, tm, tn, tk, *, ins=(), outs, epilogue, j_outer=False, side=None):
    if dims == "tn":
        K, M = a.shape
    else:
        M, K = a.shape
    N = b.shape[0] if dims == "nt" else b.shape[1]
    tm, tn, tk = min(tm, M), min(tn, N), min(tk, K)
    assert M % tm == 0 and N % tn == 0 and K % tk == 0, (name, M, N, K, tm, tn, tk)
    nm, nn, nk = M // tm, N // tn, K // tk
    if j_outer:
        grid = (nn, nm, nk)
        ij = lambda g0, g1: (g1, g0)
    else:
        grid = (nm, nn, nk)
        ij = lambda g0, g1: (g0, g1)

    def amap(g0, g1, k):
        i, _ = ij(g0, g1)
        return (k, i) if dims == "tn" else (i, k)

    def bmap(g0, g1, k):
        _, j = ij(g0, g1)
        return (j, k) if dims == "nt" else (k, j)

    def spec(kind):
        if kind == "tile":
            return pl.BlockSpec((tm, tn), lambda g0, g1, k: ij(g0, g1))
        if kind == "row":
            return pl.BlockSpec((1, tn), lambda g0, g1, k: (0, ij(g0, g1)[1]))
        assert kind == "col", kind
        return pl.BlockSpec((tm, 1), lambda g0, g1, k: (ij(g0, g1)[0], 0))

    in_specs = [
        pl.BlockSpec((tk, tm) if dims == "tn" else (tm, tk), amap),
        pl.BlockSpec((tn, tk) if dims == "nt" else (tk, tn), bmap),
    ] + [spec(kind) for _, kind in ins]
    out_specs = [spec(kind) for _, _, kind in outs]
    out_shape = [S(shape, dtype) for shape, dtype, _ in outs]
    n_in, n_out = len(ins), len(outs)
    contract = (_CONTRACT[dims], ((), ()))

    def body(*refs):
        a_ref, b_ref = refs[:2]
        in_refs = refs[2 : 2 + n_in]
        out_refs = refs[2 + n_in : 2 + n_in + n_out]
        acc = refs[2 + n_in + n_out]
        i, _ = ij(pl.program_id(0), pl.program_id(1))
        k = pl.program_id(2)

        def part():
            return lax.dot_general(a_ref[...], b_ref[...], contract, preferred_element_type=F32)

        @pl.when(k == 0)
        def _():
            acc[...] = part()

        @pl.when(k > 0)
        def _():
            acc[...] += part()

        @pl.when(k == nk - 1)
        def _():
            epilogue(i, acc, in_refs, out_refs)

    return _call(name, body, grid, in_specs, out_specs, out_shape, [pltpu.VMEM((tm, tn), F32)], [a, b, *[x for x, _ in ins]], side)


def _row(v):
    return v.reshape(1, -1)


def _mm_bias(name, a, b, dims, bias, tm, tn, tk, side=None):
    M = a.shape[0]
    N = b.shape[0] if dims == "nt" else b.shape[1]

    def epilogue(i, acc, ins, outs):
        def strip(r0):
            rows = pl.ds(r0, 128)
            outs[0][rows, :] = acc[rows, :] + ins[0][...]

        _for_strips(acc.shape[0], 128, strip)

    return _mm(name, a, b, dims, tm, tn, tk, ins=[(_row(bias), "row")], outs=[((M, N), F32, "tile")], epilogue=epilogue, side=side)[0]


def _mm_relu2(name, a, b, dims, bias, tm, tn, tk, side=None):
    M = a.shape[0]
    N = b.shape[0] if dims == "nt" else b.shape[1]

    def epilogue(i, acc, ins, outs):
        def strip(r0):
            rows = pl.ds(r0, 128)
            r = jnp.maximum(acc[rows, :] + ins[0][...], 0.0)
            outs[0][rows, :] = r.astype(BF16)
            outs[1][rows, :] = (r * r).astype(BF16)

        _for_strips(acc.shape[0], 128, strip)

    return _mm(
        name, a, b, dims, tm, tn, tk, ins=[(_row(bias), "row")],
        outs=[((M, N), BF16, "tile"), ((M, N), BF16, "tile")], epilogue=epilogue, side=side,
    )


def _mm_lagged(name, a, b, tm, tk, *, ins, outs, strip_fn, init_fn=None, side=None):
    M, K = a.shape
    N = b.shape[1]
    tm, tk = min(tm, M), min(tk, K)
    assert M % tm == 0 and K % tk == 0, (name, M, K, tm, tk)
    nm, nk = M // tm, K // tk
    assert nk >= 2, (name, nk)
    parts = 1 << ((nk - 1).bit_length() - 1)
    rows_p = tm // parts
    assert rows_p % LN_STRIP == 0, (name, rows_p)

    def part_index(i, k):
        return jnp.maximum((i - 1) * parts + jnp.minimum(k, parts - 1), 0)

    def spec(kind):
        if kind == "tile":
            return pl.BlockSpec((rows_p, N), lambda i, k: (part_index(i, k), 0))
        if kind == "row":
            return pl.BlockSpec((1, N), lambda i, k: (0, 0))
        assert kind == "col", kind
        return pl.BlockSpec((rows_p, 1), lambda i, k: (part_index(i, k), 0))

    in_specs = [
        pl.BlockSpec((tm, tk), lambda i, k: (jnp.minimum(i, nm - 1), k)),
        pl.BlockSpec((tk, N), lambda i, k: (jnp.where(i < nm, k, nk - 1), 0)),
    ] + [spec(kind) for _, kind in ins]
    n_in, n_out = len(ins), len(outs)

    def body(*refs):
        a_ref, b_ref = refs[:2]
        in_refs = refs[2 : 2 + n_in]
        out_refs = refs[2 + n_in : 2 + n_in + n_out]
        acc, fin = refs[2 + n_in + n_out :]
        i, k = pl.program_id(0), pl.program_id(1)

        def part():
            return jnp.dot(a_ref[...], b_ref[...], preferred_element_type=F32)

        def epilogue_part():
            base = k * rows_p
            for s in range(rows_p // LN_STRIP):
                acc_rows = fin[pl.ds(pl.multiple_of(base + s * LN_STRIP, LN_STRIP), LN_STRIP), :]
                strip_fn(acc_rows, in_refs, out_refs, pl.ds(s * LN_STRIP, LN_STRIP))

        has_dot = i < nm
        has_epilogue = jnp.logical_and(i > 0, k < parts)
        no_epilogue = jnp.logical_not(has_epilogue)
        last = k == nk - 1
        both = jnp.logical_and(has_dot, has_epilogue)
        alone = jnp.logical_and(has_dot, no_epilogue)

        if init_fn is not None:
            @pl.when(jnp.logical_and(i == 1, k == 0))
            def _():
                init_fn(out_refs)

        @pl.when(jnp.logical_and(alone, k == 0))
        def _():
            acc[...] = part()

        @pl.when(jnp.logical_and(alone, jnp.logical_and(k > 0, jnp.logical_not(last))))
        def _():
            acc[...] += part()

        @pl.when(jnp.logical_and(has_dot, last))
        def _():
            fin[...] = acc[...] + part()

        @pl.when(jnp.logical_and(both, k == 0))
        def _():
            acc[...] = part()
            epilogue_part()

        if parts > 1:
            @pl.when(jnp.logical_and(both, k > 0))
            def _():
                acc[...] += part()
                epilogue_part()

        @pl.when(jnp.logical_and(jnp.logical_not(has_dot), has_epilogue))
        def _():
            epilogue_part()

    return _call(
        name, body, (nm + 1, nk), in_specs, [spec(kind) for _, _, kind in outs], [S(shape, dtype) for shape, dtype, _ in outs],
        [pltpu.VMEM((tm, N), F32), pltpu.VMEM((tm, N), F32)], [a, b, *[x for x, _ in ins]], side,
    )


def _mm_ln(name, a, b, bias, res, g, beta, tm, tk, side=None):
    M = a.shape[0]
    N = b.shape[1]
    rxh, rg, rb = res

    def strip(acc_rows, ins, outs, rows):
        bias_r, rxh_r, rg_r, rb_r, g_r, beta_r = ins
        xhat_o, rstd_o, xbf_o = outs
        resid = rxh_r[rows, :] * rg_r[...] + rb_r[...]
        r = ALPHA * resid + (acc_rows + bias_r[...])
        xhat, rs = _ln_stats(r)
        xhat_o[rows, :] = xhat
        rstd_o[rows, :] = rs
        xbf_o[rows, :] = (xhat * g_r[...] + beta_r[...]).astype(BF16)

    return _mm_lagged(
        name, a, b, tm, tk,
        ins=[(_row(bias), "row"), (rxh, "tile"), (_row(rg), "row"), (_row(rb), "row"), (_row(g), "row"), (_row(beta), "row")],
        outs=[((M, N), F32, "tile"), ((M, 1), F32, "col"), ((M, N), BF16, "tile")],
        strip_fn=strip, side=side,
    )


def _ln_bwd_strip(dyv, xhat, rs, g, dr_o, drbf_o, dg_o, db_o, dsum_o, rows):
    dr = _ln_bwd(dyv, xhat, rs, g)
    dr_o[rows, :] = dr
    drbf_o[rows, :] = dr.astype(BF16)
    dg_o[...] += _row_sum(dyv * xhat)
    db_o[...] += _row_sum(dyv)
    dsum_o[...] += _row_sum(dr)


def _mm_ln_bwd(name, a, b, resgrad, xhat, rstd, g, tm, tk, side=None):
    M = a.shape[0]
    N = b.shape[1]

    def init(outs):
        for o in outs[2:]:
            o[...] = jnp.zeros_like(o)

    def strip(acc_rows, ins, outs, rows):
        rg_r, xh_r, rs_r, g_r = ins
        dyv = acc_rows + ALPHA * rg_r[rows, :]
        _ln_bwd_strip(dyv, xh_r[rows, :], rs_r[rows, :], g_r[...], *outs, rows)

    return _mm_lagged(
        name, a, b, tm, tk,
        ins=[(resgrad, "tile"), (xhat, "tile"), (rstd, "col"), (_row(g), "row")],
        outs=[((M, N), F32, "tile"), ((M, N), BF16, "tile"), ((1, N), F32, "row"), ((1, N), F32, "row"), ((1, N), F32, "row")],
        strip_fn=strip, init_fn=init, side=side,
    )


def _mm_dh(name, a, b, act, tm, tn, tk, side=None):
    M = a.shape[0]
    N = b.shape[0]

    def epilogue(i, acc, ins, outs):
        @pl.when(i == 0)
        def _():
            outs[1][...] = jnp.zeros_like(outs[1])

        def strip(r0):
            rows = pl.ds(r0, 128)
            d = acc[rows, :] * (2.0 * ins[0][rows, :].astype(F32))
            outs[0][rows, :] = d.astype(BF16)
            outs[1][...] += _row_sum(d)

        _for_strips(acc.shape[0], 128, strip)

    return _mm(
        name, a, b, "nt", tm, tn, tk, ins=[(act, "tile")],
        outs=[((M, N), BF16, "tile"), ((1, N), F32, "row")], epilogue=epilogue, j_outer=True, side=side,
    )


def _mm_plain(name, a, b, dims, tm, tn, tk, side=None):
    M = a.shape[0]
    N = b.shape[0] if dims == "nt" else b.shape[1]

    def epilogue(i, acc, ins, outs):
        def strip(r0):
            rows = pl.ds(r0, 128)
            outs[0][rows, :] = acc[rows, :]

        _for_strips(acc.shape[0], 128, strip)

    return _mm(name, a, b, dims, tm, tn, tk, outs=[((M, N), F32, "tile")], epilogue=epilogue, side=side)[0]


def _mm_res(name, a, b, res, tm, tk, side=None):
    def strip(acc_rows, ins, outs, rows):
        outs[0][rows, :] = acc_rows + ALPHA * ins[0][rows, :]

    return _mm_lagged(
        name, a, b, tm, tk, ins=[(res, "tile")], outs=[((a.shape[0], b.shape[1]), F32, "tile")], strip_fn=strip, side=side,
    )[0]


def _mm_wgrad(name, a, b, tm, tk, side=None):
    M = a.shape[1]
    N = b.shape[1]

    def epilogue(i, acc, ins, outs):
        def strip(r0):
            rows = pl.ds(r0, 128)
            outs[0][rows, :] = acc[rows, :].astype(BF16)

        _for_strips(acc.shape[0], 128, strip)

    return _mm(name, a, b, "tn", tm, N, tk, outs=[((M, N), BF16, "tile")], epilogue=epilogue, side=side)[0]


def _loss_top(xhat, rstd, g, beta, target, tm):
    T, D = xhat.shape
    tm = min(tm, T)
    nt = T // tm

    def body(xh_r, rs_r, g_r, b_r, t_r, dr_o, drbf_o, dg_o, db_o, dsum_o, loss_o, sq_acc):
        i = pl.program_id(0)

        @pl.when(i == 0)
        def _():
            dg_o[...] = jnp.zeros_like(dg_o)
            db_o[...] = jnp.zeros_like(db_o)
            dsum_o[...] = jnp.zeros_like(dsum_o)
            sq_acc[...] = jnp.zeros_like(sq_acc)

        def strip(r0):
            rows = pl.ds(r0, LN_STRIP)
            xh = xh_r[rows, :]
            err = (xh * g_r[...] + b_r[...]) - t_r[rows, :]
            sq_acc[...] += _row_sum(err * err)
            _ln_bwd_strip(err * (1.0 / D), xh, rs_r[rows, :], g_r[...], dr_o, drbf_o, dg_o, db_o, dsum_o, rows)

        _for_strips(tm, LN_STRIP, strip, unroll=LN_UNROLL)

        @pl.when(i == nt - 1)
        def _():
            total = jnp.sum(sq_acc[...], axis=-1, keepdims=True) * (0.5 / D)
            loss_o[...] = jnp.broadcast_to(total, loss_o.shape)

    tile = pl.BlockSpec((tm, D), lambda i: (i, 0))
    row = pl.BlockSpec((1, D), lambda i: (0, 0))
    return pl.pallas_call(
        body,
        name="loss_top",
        grid=(nt,),
        in_specs=[tile, pl.BlockSpec((tm, 1), lambda i: (i, 0)), row, row, tile],
        out_specs=[tile, tile, row, row, row, pl.BlockSpec((1, 128), lambda i: (0, 0))],
        out_shape=[S((T, D), F32), S((T, D), BF16), S((1, D), F32), S((1, D), F32), S((1, D), F32), S((1, 128), F32)],
        scratch_shapes=[pltpu.VMEM((1, D), F32)],
        compiler_params=_cparams(1),
    )(xhat, rstd, _row(g), _row(beta), target)


def _cols(ref, c):
    return ref[:, c[0] : c[1]]


def _causal_window_sum(e, w):
    s, sh = e, 1
    while sh < w:
        s = s + pltpu.roll(s, sh, axis=0)
        sh *= 2
    return s


def _anticausal_window_sum(d, w):
    n = d.shape[0]
    r, sh = d, 1
    while sh < w:
        r = r + pltpu.roll(r, n - sh, axis=0)
        sh *= 2
    return r


def _with_halo(halo_ref, main_ref, c, keep):
    return jnp.concatenate([_cols(halo_ref, c) * keep, _cols(main_ref, c)], axis=0)


def _pool_counts(tile_index, R, w):
    pos = lax.broadcasted_iota(jnp.int32, (R, 1), 0) + tile_index * R
    return jnp.minimum(pos + 1, w).astype(F32)


def _sgu_mix(wm_ref, vnb):
    return jnp.concatenate(
        [
            jnp.dot(wm_ref[h * GROUP : (h + 1) * GROUP, :], vnb[:, h * GROUP : (h + 1) * GROUP], preferred_element_type=F32)
            for h in range(SGU_HEADS)
        ],
        axis=1,
    )


CONV_HALVES = (slice(0, CONV_WIDTH // 2), slice(CONV_WIDTH // 2, CONV_WIDTH))
TAP_STRIP = 32
TAP_GROUP = 4


def _build_shifts(shf, src, cols, rows):
    n = rows - 8
    for r in range(1, 8):
        shf[r - 1, pl.ds(0, n), :] = src[pl.ds(r, n), cols]


def _shifted(shf, src, cols, offset, start, size):
    q, r = divmod(offset, 8)
    rows = pl.ds(pl.multiple_of(start + 8 * q, 8), size)
    return src[rows, cols] if r == 0 else shf[r - 1, rows, :]


def _conv_taps(shf, src, cw8, cols, offsets, n_rows, out, bias=None):
    width = cols.stop - cols.start

    def strip(s, carry):
        r0 = pl.multiple_of(s * TAP_STRIP, TAP_STRIP)
        acc = jnp.zeros((TAP_STRIP, width), F32)
        for k, o in enumerate(offsets):
            wk = cw8[pl.ds(8 * k, 8), cols]
            acc = acc + _shifted(shf, src, cols, o, r0, TAP_STRIP) * jnp.concatenate([wk] * (TAP_STRIP // 8), axis=0)
        if bias is not None:
            acc = acc + bias[:, cols]
        out[pl.ds(r0, TAP_STRIP), cols] = acc
        return carry

    lax.fori_loop(0, n_rows // TAP_STRIP, strip, 0)


def _conv_weight_grad(shf, src, dsrc, d_first, cols, offsets, n_rows, dcw):
    width = cols.stop - cols.start
    for k0 in range(0, len(offsets), TAP_GROUP):
        group = offsets[k0 : k0 + TAP_GROUP]

        def strip(s, accs, group=group):
            r0 = pl.multiple_of(s * TAP_STRIP, TAP_STRIP)
            d = dsrc[pl.ds(pl.multiple_of(d_first + r0, 8), TAP_STRIP), cols]
            out = []
            for acc8, o in zip(accs, group):
                p = _shifted(shf, src, cols, o, r0, TAP_STRIP) * d
                for j in range(TAP_STRIP // 8):
                    acc8 = acc8 + p[8 * j : 8 * j + 8, :]
                out.append(acc8)
            return tuple(out)

        accs = lax.fori_loop(0, n_rows // TAP_STRIP, strip, tuple(jnp.zeros((8, width), F32) for _ in group))
        for j, acc8 in enumerate(accs):
            dcw[pl.ds(k0 + j, 1), cols] += _row_sum(acc8)


def _mixer_params(p):
    return [p["wp"], p["ps"], p["lg"], p["lb"], p["wm"], p["wmt"], p["bsf"], p["cw8"], p["cb"], p["cg"], p["cbeta"]]


def _whole(x):
    return pl.BlockSpec(x.shape, lambda i: (0,) * x.ndim)


def _mixer_fwd(name, proj, p, R, side=None):
    T = proj.shape[0]
    R = min(R, T)
    E = R + HALO
    nt = T // R
    hb = R // HALO
    tap_offsets = [HALO - (CONV_KERNEL - 1) + k for k in range(CONV_KERNEL)]

    def body(pm, ph, wp, ps, lg, lb, wm, wmt, bsf, cw8, cb, cg, cbeta, out, hbuf, shf, convbuf):
        i = pl.program_id(0)
        keep = (i > 0).astype(F32)
        a_ext = _with_halo(ph, pm, C_POOL, keep)
        for gi, w in enumerate(POOL_WINDOWS):
            cs = slice(gi * GROUP, (gi + 1) * GROUP)
            e = a_ext[:, cs]
            s = _causal_window_sum(e, w)
            pooled = s[HALO:, :] / _pool_counts(i, R, w) - e[HALO:, :]
            z = jnp.dot(pooled.astype(BF16), wp[cs, :], preferred_element_type=F32)
            out[:, cs] = (z * ps[:, cs]).astype(BF16)
        u, _ = _gelu(_cols(pm, C_U))
        v, _ = _gelu(_cols(pm, C_V))
        vhat, _ = _ln_stats(v)
        vn = vhat * lg[...] + lb[...]
        for c in range(R // GROUP):
            rs = slice(c * GROUP, (c + 1) * GROUP)
            mixed = _sgu_mix(wm, vn[rs, :].astype(BF16)) + bsf[...]
            out[rs, M_SGU[0] : M_SGU[1]] = (u[rs, :] * mixed).astype(BF16)
        hbuf[...] = _with_halo(ph, pm, C_CA, keep) * jax.nn.sigmoid(_with_halo(ph, pm, C_CG, keep))
        for cols in CONV_HALVES:
            _build_shifts(shf, hbuf, cols, E)
            _conv_taps(shf, hbuf, cw8, cols, tap_offsets, R, convbuf, bias=cb)
        chat, _ = _ln_stats(convbuf[...])
        cn = chat * cg[...] + cbeta[...]
        out[:, M_CONV[0] : M_CONV[1]] = (cn * jax.nn.sigmoid(cn)).astype(BF16)

    params = _mixer_params(p)
    in_specs = [
        pl.BlockSpec((R, IN_WIDTH), lambda i: (i, 0)),
        pl.BlockSpec((HALO, IN_WIDTH), lambda i: (jnp.maximum(i * hb - 1, 0), 0)),
    ] + [_whole(x) for x in params]
    scratch = [pltpu.VMEM((E, CONV_WIDTH), F32), pltpu.VMEM((7, E, CONV_WIDTH // 2), F32), pltpu.VMEM((R, CONV_WIDTH), F32)]
    return _call(
        name, body, (nt,), in_specs, [pl.BlockSpec((R, D_MODEL), lambda i: (i, 0))], [S((T, D_MODEL), BF16)],
        scratch, [proj, proj, *params], side,
    )[0]


def _mixer_bwd(name, proj, dmix, p, R, side=None):
    T = proj.shape[0]
    R = min(R, T)
    E = R + HALO
    nt = T // R
    hb = R // HALO
    tap_offsets = [HALO - (CONV_KERNEL - 1) + k for k in range(CONV_KERNEL)]
    back_offsets = [HALO - o for o in tap_offsets]

    def body(pm, ph, dm, wp, ps, lg, lb, wm, wmt, bsf, cw8, cb, cg, cbeta,
             dproj, dwp, dps, dlg, dlb, dwm, dbs, dcw, dcb, dcg, dcbeta, dbin,
             hbuf, dbuf, carry_p, carry_c, dbs_acc, shf, convbuf, dhcbuf):
        step = pl.program_id(0)
        ti = nt - 1 - step
        keep = (ti > 0).astype(F32)

        @pl.when(step == 0)
        def _():
            for r in (dwp, dps, dlg, dlb, dwm, dcw, dcb, dcg, dcbeta, dbin, carry_p, carry_c, dbs_acc):
                r[...] = jnp.zeros_like(r)

        def tail(carry):
            return jnp.concatenate([jnp.zeros((R - HALO, carry.shape[1]), F32), carry], axis=0)

        def head(x):
            return jnp.concatenate([jnp.zeros((HALO, x.shape[1]), F32), x], axis=0)

        a_ext = _with_halo(ph, pm, C_POOL, keep)
        carry_in = carry_p[...]
        for gi, w in enumerate(POOL_WINDOWS):
            cs = slice(gi * GROUP, (gi + 1) * GROUP)
            e = a_ext[:, cs]
            s = _causal_window_sum(e, w)
            cnt = _pool_counts(ti, R, w)
            pooled_b = (s[HALO:, :] / cnt - e[HALO:, :]).astype(BF16)
            wg = wp[cs, :]
            z = jnp.dot(pooled_b, wg, preferred_element_type=F32)
            dya = dm[:, cs]
            dps[:, cs] += _row_sum(dya * z)
            dz_b = (dya * ps[:, cs]).astype(BF16)
            dwp[cs, :] += lax.dot_general(pooled_b, dz_b, (((0,), (0,)), ((), ())), preferred_element_type=F32)
            dpooled = lax.dot_general(dz_b, wg, (((1,), (1,)), ((), ())), preferred_element_type=F32)
            da_ext = _anticausal_window_sum(head(dpooled / cnt), w) - head(dpooled)
            carry_p[:, cs] = da_ext[:HALO, :]
            d_a = da_ext[HALO:, :] + tail(carry_in[:, cs])
            dbin[:, cs] += _row_sum(d_a)
            dproj[:, cs] = d_a.astype(BF16)

        pu = _cols(pm, C_U)
        pv = _cols(pm, C_V)
        u, thu = _gelu(pu)
        v, thv = _gelu(pv)
        vhat, vrs = _ln_stats(v)
        vn = vhat * lg[...] + lb[...]
        dyb = dm[:, M_SGU[0] : M_SGU[1]]
        du_parts, dvn_parts = [], []
        for c in range(R // GROUP):
            rs = slice(c * GROUP, (c + 1) * GROUP)
            vnb = vn[rs, :].astype(BF16)
            mixed = _sgu_mix(wm, vnb) + bsf[...]
            du_parts.append(dyb[rs, :] * mixed)
            dmixed = dyb[rs, :] * u[rs, :]
            dbs_acc[...] += dmixed
            dmb = dmixed.astype(BF16)
            dvn_h = []
            for h in range(SGU_HEADS):
                hs = slice(h * GROUP, (h + 1) * GROUP)
                dwm[hs, :] += lax.dot_general(dmb[:, hs], vnb[:, hs], (((1,), (1,)), ((), ())), preferred_element_type=F32)
                dvn_h.append(jnp.dot(wmt[hs, :], dmb[:, hs], preferred_element_type=F32))
            dvn_parts.append(jnp.concatenate(dvn_h, axis=1))
        du = jnp.concatenate(du_parts, axis=0) if len(du_parts) > 1 else du_parts[0]
        dvn = jnp.concatenate(dvn_parts, axis=0) if len(dvn_parts) > 1 else dvn_parts[0]
        dlg[...] += _row_sum(dvn * vhat)
        dlb[...] += _row_sum(dvn)
        d_pu = du * _gelu_grad(pu, thu)
        d_pv = _ln_bwd(dvn, vhat, vrs, lg[...]) * _gelu_grad(pv, thv)
        dbin[:, C_U[0] : C_U[1]] += _row_sum(d_pu)
        dbin[:, C_V[0] : C_V[1]] += _row_sum(d_pv)
        dproj[:, C_U[0] : C_U[1]] = d_pu.astype(BF16)
        dproj[:, C_V[0] : C_V[1]] = d_pv.astype(BF16)

        sg_ext = jax.nn.sigmoid(_with_halo(ph, pm, C_CG, keep))
        ca_ext = _with_halo(ph, pm, C_CA, keep)
        hbuf[...] = ca_ext * sg_ext
        for cols in CONV_HALVES:
            _build_shifts(shf, hbuf, cols, E)
            _conv_taps(shf, hbuf, cw8, cols, tap_offsets, R, convbuf, bias=cb)
        chat, crs = _ln_stats(convbuf[...])
        cn = chat * cg[...] + cbeta[...]
        sc = jax.nn.sigmoid(cn)
        dcn = dm[:, M_CONV[0] : M_CONV[1]] * (sc * (1.0 + cn * (1.0 - sc)))
        dcg[...] += _row_sum(dcn * chat)
        dcbeta[...] += _row_sum(dcn)
        dconv = _ln_bwd(dcn, chat, crs, cg[...])
        dcb[...] += _row_sum(dconv)
        dbuf[pl.ds(0, HALO), :] = jnp.zeros((HALO, CONV_WIDTH), F32)
        dbuf[pl.ds(HALO, R), :] = dconv
        dbuf[pl.ds(HALO + R, HALO), :] = jnp.zeros((HALO, CONV_WIDTH), F32)
        for cols in CONV_HALVES:
            _build_shifts(shf, hbuf, cols, E)
            _conv_weight_grad(shf, hbuf, dbuf, HALO, cols, tap_offsets, R, dcw)
            _build_shifts(shf, dbuf, cols, E + HALO)
            _conv_taps(shf, dbuf, cw8, cols, back_offsets, E, dhcbuf)
        dhc_main = dhcbuf[pl.ds(HALO, R), :] + tail(carry_c[...])
        carry_c[...] = dhcbuf[pl.ds(0, HALO), :]
        sg = sg_ext[HALO:, :]
        d_ca = dhc_main * sg
        d_cg = dhc_main * ca_ext[HALO:, :] * (sg * (1.0 - sg))
        dbin[:, C_CA[0] : C_CA[1]] += _row_sum(d_ca)
        dbin[:, C_CG[0] : C_CG[1]] += _row_sum(d_cg)
        dproj[:, C_CA[0] : C_CA[1]] = d_ca.astype(BF16)
        dproj[:, C_CG[0] : C_CG[1]] = d_cg.astype(BF16)

        @pl.when(step == nt - 1)
        def _():
            row = lax.broadcasted_iota(jnp.int32, (GROUP, GROUP), 0)
            col = lax.broadcasted_iota(jnp.int32, (GROUP, GROUP), 1)
            dbs[...] = jnp.zeros_like(dbs)
            for h in range(SGU_HEADS):
                hs = slice(h * GROUP, (h + 1) * GROUP)
                dwm[hs, :] = jnp.where(row >= col, dwm[hs, :], 0.0)
                dbs[pl.ds(h, 1), :] = _row_sum(dbs_acc[:, hs].T)

    params = _mixer_params(p)
    accs = [
        S((POOL_WIDTH, GROUP), F32), S((1, POOL_WIDTH), F32), S((1, SGU_WIDTH), F32), S((1, SGU_WIDTH), F32),
        S((SGU_WIDTH, GROUP), F32), S((8, GROUP), F32), S((32, CONV_WIDTH), F32), S((1, CONV_WIDTH), F32),
        S((1, CONV_WIDTH), F32), S((1, CONV_WIDTH), F32), S((1, IN_WIDTH), F32),
    ]
    in_specs = [
        pl.BlockSpec((R, IN_WIDTH), lambda i: (nt - 1 - i, 0)),
        pl.BlockSpec((HALO, IN_WIDTH), lambda i: (jnp.maximum((nt - 1 - i) * hb - 1, 0), 0)),
        pl.BlockSpec((R, D_MODEL), lambda i: (nt - 1 - i, 0)),
    ] + [_whole(x) for x in params]
    scratch = [
        pltpu.VMEM((E, CONV_WIDTH), F32), pltpu.VMEM((E + HALO, CONV_WIDTH), F32),
        pltpu.VMEM((HALO, POOL_WIDTH), F32), pltpu.VMEM((HALO, CONV_WIDTH), F32), pltpu.VMEM((GROUP, SGU_WIDTH), F32),
        pltpu.VMEM((7, E + HALO, CONV_WIDTH // 2), F32), pltpu.VMEM((R, CONV_WIDTH), F32), pltpu.VMEM((E, CONV_WIDTH), F32),
    ]
    return _call(
        name, body, (nt,), in_specs, [pl.BlockSpec((R, IN_WIDTH), lambda i: (nt - 1 - i, 0))] + [_whole(x) for x in accs],
        [S((T, IN_WIDTH), BF16)] + accs, scratch, [proj, proj, dmix, *params], side,
    )


def _all_gather(xs):
    n = len(xs)

    def body(*refs):
        x_refs, o_refs = refs[:n], refs[n : 2 * n]
        send_sems, recv_sems, local_sems = refs[2 * n :]
        x, y, c = _place()
        me, sibling = (x, y, c), (x, y, 1 - c)
        chips = [(1 - x, y), (x, 1 - y), (1 - x, 1 - y)]

        def copy(a, k, block, to, src=None):
            dst = o_refs[a].at[_lin(block)]
            return pltpu.make_async_remote_copy(
                src_ref=dst if src is None else src, dst_ref=dst, send_sem=send_sems.at[a, k], recv_sem=recv_sems.at[a, k],
                device_id=to, device_id_type=MESH,
            )

        mine = [pltpu.make_async_copy(x_refs[a], o_refs[a].at[_lin(me)], local_sems.at[a]) for a in range(n)]
        for m in mine:
            m.start()
        first = []
        for a in range(n):
            first.append(copy(a, 0, me, sibling, src=x_refs[a]))
            first += [copy(a, 1 + j, me, (*chip, c), src=x_refs[a]) for j, chip in enumerate(chips)]
        for cp in first:
            cp.start()
        passed = []
        for a in range(n):
            for j, chip in enumerate(chips):
                copy(a, 1 + j, (*chip, c), me).wait_recv()
                fwd = copy(a, 4 + j, (*chip, c), sibling)
                fwd.start()
                passed.append(fwd)
        for a in range(n):
            copy(a, 0, sibling, me).wait_recv()
            for j, chip in enumerate(chips):
                copy(a, 4 + j, (*chip, 1 - c), me).wait_recv()
        for cp in first + passed:
            cp.wait_send()
        for m in mine:
            m.wait()

    return pl.pallas_call(
        body,
        name="all_gather_weights",
        in_specs=[_ANY] * n,
        out_specs=[_ANY] * n,
        out_shape=[S((N_DEV, *x.shape), x.dtype) for x in xs],
        scratch_shapes=[pltpu.SemaphoreType.DMA((n, 7)), pltpu.SemaphoreType.DMA((n, 7)), pltpu.SemaphoreType.DMA((n,))],
    )(*xs)


def _row_tile(rows, want):
    return next(t for t in range(min(rows, want) // 8 * 8, 0, -8) if rows % t == 0)


def _sum_slots(name, slots):
    _, rows, cols = slots.shape
    tr = _row_tile(rows, (4 << 20) // (N_DEV * cols * slots.dtype.itemsize))

    def body(s_ref, o_ref):
        total = s_ref[0].astype(F32)
        for d in range(1, N_DEV):
            total = total + s_ref[d].astype(F32)
        o_ref[...] = total

    return pl.pallas_call(
        body,
        name=name,
        grid=(rows // tr,),
        in_specs=[pl.BlockSpec((N_DEV, tr, cols), lambda i: (0, i, 0))],
        out_specs=pl.BlockSpec((tr, cols), lambda i: (i, 0)),
        out_shape=S((rows, cols), F32),
        compiler_params=_cparams(1),
    )(slots)


def _adamw(name, w, g, m, v):
    rows, cols = w.shape
    tr = rows if rows * cols * 4 <= (2 << 20) else _row_tile(rows, 1 << ((1 << 18) // cols).bit_length() - 1)

    def body(w_ref, g_ref, m_ref, v_ref, d_ref, nm_ref, nv_ref):
        gv = g_ref[...]
        nm = ADAM_B1 * m_ref[...] + (1.0 - ADAM_B1) * gv
        nv = ADAM_B2 * v_ref[...] + (1.0 - ADAM_B2) * (gv * gv)
        m_hat = nm / (1.0 - ADAM_B1**ADAM_STEP)
        v_hat = nv / (1.0 - ADAM_B2**ADAM_STEP)
        d_ref[...] = -ADAM_LR * (m_hat / (jnp.sqrt(v_hat) + ADAM_EPS) + ADAM_WD * w_ref[...])
        nm_ref[...] = nm
        nv_ref[...] = nv

    blk = pl.BlockSpec((tr, cols), lambda i: (i, 0))
    return pl.pallas_call(
        body,
        name=name,
        grid=(rows // tr,),
        in_specs=[blk] * 4,
        out_specs=[blk] * 3,
        out_shape=[S((rows, cols), F32)] * 3,
        compiler_params=_cparams(1),
    )(w, g, m, v)


_BIG = ("w_in", "w_out", "w_ff1", "w_ff2")
_TRANSPOSED = ("w_in", "w_ff1")
_SMALL = ("b_in", "w_pool", "pool_scale", "sgu_ln_g", "sgu_ln_b", "sgu_w", "sgu_b", "conv_b", "conv_ln_g", "conv_ln_b",
          "b_out", "ln1_g", "ln1_b", "b_ff1", "b_ff2", "ln2_g", "ln2_b")
_WEIGHTS = ("w_in", "b_in", "w_pool", "pool_scale", "sgu_ln_g", "sgu_ln_b", "sgu_w", "sgu_b", "conv_w", "conv_b", "conv_ln_g",
            "conv_ln_b", "w_out", "b_out", "ln1_g", "ln1_b", "w_ff1", "b_ff1", "w_ff2", "b_ff2", "ln2_g", "ln2_b")


def _pack(arrays):
    parts = []
    for a in arrays:
        rows = a.reshape(-1, 128)
        parts.append(jnp.pad(rows, ((0, -rows.shape[0] % 8), (0, 0))))
    return jnp.concatenate(parts, axis=0)


def _unpack(flat, like):
    out, at = [], 0
    for a in like:
        n = a.size // 128
        out.append(flat[at : at + n].reshape(a.shape))
        at += n + (-n % 8)
    return out


def _packed_rows(arrays):
    return sum(a.size // 128 + (-(a.size // 128) % 8) for a in arrays)


def kernel(x, w_in, b_in, w_pool, pool_scale, sgu_ln_g, sgu_ln_b, sgu_w, sgu_b, conv_w, conv_b, conv_ln_g, conv_ln_b, w_out, b_out, ln1_g, ln1_b, w_ff1, b_ff1, w_ff2, b_ff2, ln2_g, ln2_b, loss_target, m_w_in, m_b_in, m_w_pool, m_pool_scale, m_sgu_ln_g, m_sgu_ln_b, m_sgu_w, m_sgu_b, m_conv_w, m_conv_b, m_conv_ln_g, m_conv_ln_b, m_w_out, m_b_out, m_ln1_g, m_ln1_b, m_w_ff1, m_b_ff1, m_w_ff2, m_b_ff2, m_ln2_g, m_ln2_b, v_w_in, v_b_in, v_w_pool, v_pool_scale, v_sgu_ln_g, v_sgu_ln_b, v_sgu_w, v_sgu_b, v_conv_w, v_conv_b, v_conv_ln_g, v_conv_ln_b, v_w_out, v_b_out, v_ln1_g, v_ln1_b, v_w_ff1, v_b_ff1, v_w_ff2, v_b_ff2, v_ln2_g, v_ln2_b):
    w = dict(w_in=w_in, b_in=b_in, w_pool=w_pool, pool_scale=pool_scale, sgu_ln_g=sgu_ln_g, sgu_ln_b=sgu_ln_b, sgu_w=sgu_w,
             sgu_b=sgu_b, conv_w=conv_w, conv_b=conv_b, conv_ln_g=conv_ln_g, conv_ln_b=conv_ln_b, w_out=w_out, b_out=b_out,
             ln1_g=ln1_g, ln1_b=ln1_b, w_ff1=w_ff1, b_ff1=b_ff1, w_ff2=w_ff2, b_ff2=b_ff2, ln2_g=ln2_g, ln2_b=ln2_b)
    mom = dict(w_in=m_w_in, b_in=m_b_in, w_pool=m_w_pool, pool_scale=m_pool_scale, sgu_ln_g=m_sgu_ln_g, sgu_ln_b=m_sgu_ln_b,
               sgu_w=m_sgu_w, sgu_b=m_sgu_b, conv_w=m_conv_w, conv_b=m_conv_b, conv_ln_g=m_conv_ln_g, conv_ln_b=m_conv_ln_b,
               w_out=m_w_out, b_out=m_b_out, ln1_g=m_ln1_g, ln1_b=m_ln1_b, w_ff1=m_w_ff1, b_ff1=m_b_ff1, w_ff2=m_w_ff2,
               b_ff2=m_b_ff2, ln2_g=m_ln2_g, ln2_b=m_ln2_b)
    var = dict(w_in=v_w_in, b_in=v_b_in, w_pool=v_w_pool, pool_scale=v_pool_scale, sgu_ln_g=v_sgu_ln_g, sgu_ln_b=v_sgu_ln_b,
               sgu_w=v_sgu_w, sgu_b=v_sgu_b, conv_w=v_conv_w, conv_b=v_conv_b, conv_ln_g=v_conv_ln_g, conv_ln_b=v_conv_ln_b,
               w_out=v_w_out, b_out=v_b_out, ln1_g=v_ln1_g, ln1_b=v_ln1_b, w_ff1=v_w_ff1, b_ff1=v_b_ff1, w_ff2=v_w_ff2,
               b_ff2=v_b_ff2, ln2_g=v_ln2_g, ln2_b=v_ln2_b)
    T = x.shape[1]
    x0 = x.reshape(T, D_MODEL)
    target = loss_target.reshape(T, D_MODEL)
    me_lin = _lin(_place())

    shard = [
        {name: (w[name][l].T if name in _TRANSPOSED else w[name][l]).astype(BF16) for name in _BIG} for l in range(DEPTH)
    ]
    conv_shard = jnp.pad(conv_w, ((0, 0), (0, 1), (0, 128 - conv_w.shape[2]))).reshape(DEPTH * 32, 128)

    def rows_of(g):
        return g.reshape(N_DEV * g.shape[1], g.shape[2])

    first = _all_gather([shard[0]["w_in"], shard[0]["w_out"], conv_shard])
    full = [{} for _ in range(DEPTH)]
    full[0]["w_in"], full[0]["w_out"] = rows_of(first[0]), rows_of(first[1])
    conv_cols = conv_w.shape[2]
    conv_full = first[2].reshape(N_DEV, DEPTH, 32, 128)[:, :, :CONV_KERNEL, :conv_cols]
    conv_full = conv_full.transpose(1, 2, 0, 3).reshape(DEPTH, CONV_KERNEL, N_DEV * conv_cols)

    tril = jnp.tril(jnp.ones((GROUP, GROUP), F32))
    prm = []
    for l in range(DEPTH):
        wm = sgu_w[l] * tril
        prm.append(dict(
            wp=w_pool[l].reshape(POOL_WIDTH, GROUP).astype(BF16), ps=_row(pool_scale[l]), lg=_row(sgu_ln_g[l]), lb=_row(sgu_ln_b[l]),
            wm=wm.reshape(SGU_WIDTH, GROUP).astype(BF16), wmt=wm.transpose(0, 2, 1).reshape(SGU_WIDTH, GROUP).astype(BF16),
            bsf=jnp.repeat(sgu_b[l].T, GROUP, axis=1), cw8=jnp.repeat(jnp.pad(conv_full[l], ((0, 1), (0, 0))), 8, axis=0), cb=_row(conv_b[l]),
            cg=_row(conv_ln_g[l]), cbeta=_row(conv_ln_b[l]),
        ))

    saved = []
    res = (x0, jnp.ones((D_MODEL,), F32), jnp.zeros((D_MODEL,), F32))
    xbf = x0.astype(BF16)
    u = shard[0]["w_ff1"].shape[0] // 8

    for l in range(DEPTH):
        f = full[l]
        s_ff1, s_ff2 = shard[l]["w_ff1"], shard[l]["w_ff2"]
        ex = _Exchange([("gather", s_ff1, (0, 3 * u), None)])
        proj = _mm_bias(f"proj{l}", xbf, f["w_in"], "nt", b_in[l], 1024, 896, 2048, side=ex)
        ex = _Exchange([("gather", s_ff1, (3 * u, 3 * u), ex.results[0])])
        mixed = _mixer_fwd(f"mixer_fwd{l}", proj, prm[l], 512, side=ex)
        ex = _Exchange([("gather", s_ff1, (6 * u, 2 * u), ex.results[0]), ("gather", s_ff2, (0, u), None)])
        xh1, rs1, x1bf = _mm_ln(f"out_ln1_{l}", mixed, f["w_out"], b_out[l], res, ln1_g[l], ln1_b[l], 1024, 512, side=ex)
        f["w_ff1"] = rows_of(ex.results[0])
        ex = _Exchange([("gather", s_ff2, (u, 7 * u), ex.results[1])])
        act, hsq = _mm_relu2(f"ff1_{l}", x1bf, f["w_ff1"], "nt", b_ff1[l], 1024, 1024, 2048, side=ex)
        f["w_ff2"] = rows_of(ex.results[0])
        ex = _Exchange([("gather", shard[l + 1]["w_in"]), ("gather", shard[l + 1]["w_out"])]) if l + 1 < DEPTH else None
        xh2, rs2, x2bf = _mm_ln(
            f"ff2_ln2_{l}", hsq, f["w_ff2"], b_ff2[l], (xh1, ln1_g[l], ln1_b[l]), ln2_g[l], ln2_b[l], 1024, 1024, side=ex)
        if ex is not None:
            full[l + 1]["w_in"], full[l + 1]["w_out"] = rows_of(ex.results[0]), rows_of(ex.results[1])
        saved.append(dict(xin=xbf, proj=proj, mixed=mixed, xh1=xh1, rs1=rs1, x1bf=x1bf, act=act, hsq=hsq, xh2=xh2, rs2=rs2))
        res = (xh2, ln2_g[l], ln2_b[l])
        xbf = x2bf

    top = saved[-1]
    dr2, dr2bf, g_ln2g, g_ln2b, g_bff2, loss_row = _loss_top(top["xh2"], top["rs2"], ln2_g[-1], ln2_b[-1], target, 256)
    loss = lax.psum(loss_row[0, 0], ("x", "y", "c"))
    slots = [{} for _ in range(DEPTH)]
    gsm = [{} for _ in range(DEPTH)]
    grad_x = small_slots = None

    def stacked_small():
        st = {name: jnp.stack([gsm[gl][name].reshape(w[name].shape[1:]) for gl in range(DEPTH)]) for name in _SMALL}
        conv_g = jnp.pad(jnp.stack([gsm[gl]["conv_w"] for gl in range(DEPTH)]), ((0, 0), (0, 1), (0, 0)))
        return [st[name] for name in _SMALL] + [conv_g]

    for l in reversed(range(DEPTH)):
        f, sv = full[l], saved[l]
        gsm[l].update(ln2_g=g_ln2g, ln2_b=g_ln2b, b_ff2=g_bff2)
        gw_ff2 = _mm_wgrad(f"gw_ff2_{l}", sv["hsq"], dr2bf, 1024, 2048)
        dhpre, g_bff1 = _mm_dh(f"dff1_{l}", dr2bf, f["w_ff2"], sv["act"], 1024, 1024, 2048)
        gsm[l]["b_ff1"] = g_bff1
        gw_ff1 = _mm_wgrad(f"gw_ff1_{l}", dhpre, sv["x1bf"], 1024, 2048)
        ex = _Exchange([("slices", gw_ff2)])
        dr1, dr1bf, g_ln1g, g_ln1b, g_bout = _mm_ln_bwd(
            f"dx1_ln1_{l}", dhpre, f["w_ff1"], dr2, sv["xh1"], sv["rs1"], ln1_g[l], 1024, 1024, side=ex)
        slots[l]["w_ff2"] = ex.results[0]
        gsm[l].update(ln1_g=g_ln1g, ln1_b=g_ln1b, b_out=g_bout)
        gw = _mm_wgrad(f"gw_out_{l}", sv["mixed"], dr1bf, 1024, 2048)
        dmix = _mm_plain(f"dmixed{l}", dr1bf, f["w_out"], "nt", 1024, 1024, 2048)
        ex = _Exchange([("slices", gw_ff1), ("slices", gw)])
        (dproj, g_wp, g_ps, g_lg, g_lb, g_wm, g_bs, g_cw, g_cb, g_cg, g_cbeta, g_bin) = _mixer_bwd(
            f"mixer_bwd{l}", sv["proj"], dmix, prm[l], 512, side=ex)
        slots[l]["w_ff1"], slots[l]["w_out"] = ex.results
        gsm[l].update(b_in=g_bin, w_pool=g_wp, pool_scale=g_ps, sgu_ln_g=g_lg, sgu_ln_b=g_lb, sgu_w=g_wm, sgu_b=g_bs[:SGU_HEADS],
                      conv_w=g_cw[:CONV_KERNEL], conv_b=g_cb, conv_ln_g=g_cg, conv_ln_b=g_cbeta)
        if l > 0:
            gw = _mm_wgrad(f"gw_in_{l}", dproj, sv["xin"], 896, 2048)
            below = saved[l - 1]
            ex = _Exchange([("slices", gw)])
            dr2, dr2bf, g_ln2g, g_ln2b, g_bff2 = _mm_ln_bwd(
                f"dx_ln2_{l}", dproj, f["w_in"], dr1, below["xh2"], below["rs2"], ln2_g[l - 1], 1024, 512, side=ex)
        else:
            small_like = stacked_small()
            ex = _Exchange([("gather", _pack(small_like))])
            gw = _mm_wgrad(f"gw_in_{l}", dproj, sv["xin"], 896, 2048, side=ex)
            small_slots = ex.results[0]
            ex = _Exchange([("slices", gw)])
            grad_x = _mm_res("dx0", dproj, f["w_in"], dr1, 1024, 512, side=ex)
        slots[l]["w_in"] = ex.results[0]

    grads, deltas, new_m, new_v = {}, {}, {}, {}
    for name in _BIG:
        per_layer = []
        for l in range(DEPTH):
            g = _sum_slots(f"sum_{name}_{l}", slots[l][name])
            per_layer.append(g.T if name in _TRANSPOSED else g)
        g = jnp.stack(per_layer)
        shape = w[name].shape
        two_d = (shape[0] * shape[1], shape[2])
        d, nm, nv = _adamw(f"adamw_{name}", w[name].reshape(two_d), g.reshape(two_d), mom[name].reshape(two_d), var[name].reshape(two_d))
        grads[name], deltas[name], new_m[name], new_v[name] = g, d.reshape(shape), nm.reshape(shape), nv.reshape(shape)

    total = _sum_slots("sum_small", small_slots)
    small_g = _unpack(total, small_like)
    like = [w[name] for name in _SMALL]
    d, nm, nv = _adamw("adamw_small", _pack(like), total[: _packed_rows(like)],
                       _pack([mom[name] for name in _SMALL]), _pack([var[name] for name in _SMALL]))
    for name, gg, dd, mm_, vv in zip(_SMALL, small_g, _unpack(d, like), _unpack(nm, like), _unpack(nv, like)):
        grads[name], deltas[name], new_m[name], new_v[name] = gg, dd, mm_, vv
    conv_g = lax.dynamic_slice_in_dim(small_g[-1][:, :CONV_KERNEL, :], me_lin * conv_cols, conv_cols, axis=2)
    flat = (DEPTH * CONV_KERNEL, conv_cols)
    d, nm, nv = _adamw("adamw_conv_w", conv_w.reshape(flat), conv_g.reshape(flat), m_conv_w.reshape(flat), v_conv_w.reshape(flat))
    grads["conv_w"], deltas["conv_w"], new_m["conv_w"], new_v["conv_w"] = conv_g, d.reshape(conv_w.shape), nm.reshape(conv_w.shape), nv.reshape(conv_w.shape)

    return (loss, grad_x.reshape(x.shape), *[grads[n] for n in _WEIGHTS], *[deltas[n] for n in _WEIGHTS],
            *[new_m[n] for n in _WEIGHTS], *[new_v[n] for n in _WEIGHTS])
```
